```python
import math
import jax
import jax.numpy as jnp
from jax import lax
import numpy as np

D_MODEL = 1024
BATCH = 32
SEQ = 256
DEPTH = 2
DEC_BATCH = 2
DEC_SEQ = 1024
PAST_LEN = 512

GRID_W = 64
FN_WIDTH = D_MODEL // 4
FN_GROUPS = 4
FN_GROUP_DIM = FN_WIDTH // FN_GROUPS
DA_WIDTH = D_MODEL // 2
DA_HEADS = 4
DA_VDIM = DA_WIDTH // DA_HEADS
DA_HALF = DA_VDIM // 2
HG_WIDTH = D_MODEL // 4
HG_HEADS = 4
HG_DK = HG_WIDTH // HG_HEADS
HG_DV = HG_WIDTH // HG_HEADS
MIX_WIDTH = FN_WIDTH + DA_WIDTH + HG_WIDTH
PROJ_WIDTH = FN_WIDTH + 3 * DA_WIDTH + 5 * HG_WIDTH
CHUNK = 64
Q_BLOCK = 128
ROPE_THETA = 10000.0
N_EXPERTS = 64
TOP_K = 8
N_GROUPS = 8
TOPK_GROUPS = 4
D_EXPERT = 128
D_SHARED = 128
ROUTED_SCALE = 2.5
EPS = 1e-6

kernel_name = "hybrid_fnet_diffattn_hgrn2_moe_dit_step"


def _rmsnorm(x, g):
    xf = x.astype(jnp.float32)
    y = xf * lax.rsqrt(jnp.mean(xf * xf, axis=-1, keepdims=True) + EPS)
    return (y * g.astype(jnp.float32)).astype(x.dtype)


def _axial_rope(n_tokens):
    rows = n_tokens // GRID_W
    row = jnp.repeat(jnp.arange(rows, dtype=jnp.float32), GRID_W)
    col = jnp.tile(jnp.arange(GRID_W, dtype=jnp.float32), rows)
    axis_dim = DA_HALF // 2
    inv_freq = ROPE_THETA ** (-jnp.arange(0, axis_dim, 2, dtype=jnp.float32) / axis_dim)
    ang_r = row[:, None] * inv_freq[None, :]
    ang_c = col[:, None] * inv_freq[None, :]
    ang = jnp.concatenate([ang_r, ang_r, ang_c, ang_c], axis=-1)
    return jnp.cos(ang), jnp.sin(ang)


def _apply_rope(x, cos, sin):
    xf = x.astype(jnp.float32)
    x1, x2, x3, x4 = jnp.split(xf, 4, axis=-1)
    rot = jnp.concatenate([-x2, x1, -x4, x3], axis=-1)
    c = cos[None, :, None, None, :]
    s = sin[None, :, None, None, :]
    return (xf * c + rot * s).astype(x.dtype)


def _diff_attention(q, k, v, lam):
    b, lq = q.shape[0], q.shape[1]
    nb = lq // Q_BLOCK
    qb = q.reshape(b, nb, Q_BLOCK, DA_HEADS, 2, DA_HALF).swapaxes(0, 1)
    kf = k.astype(jnp.float32)
    vf = v.astype(jnp.float32)
    scale = DA_HALF ** -0.5

    def block(qblk):
        s = jnp.einsum('bqhmd,bkhmd->bmhqk', qblk.astype(jnp.float32), kf) * scale
        p = jax.nn.softmax(s, axis=-1)
        w = p[:, 0] - lam * p[:, 1]
        return jnp.einsum('bhqk,bkhd->bqhd', w, vf)

    o = lax.map(block, qb)
    return o.swapaxes(0, 1).reshape(b, lq, DA_HEADS, DA_VDIM)


def _fourier_mixer(u, w):
    b, l, _ = u.shape
    z = u.astype(jnp.float32).reshape(b, l, FN_GROUPS, FN_GROUP_DIM)
    z = jnp.fft.fft2(z, axes=(1, 3), norm='ortho').real
    return z.reshape(b, l, FN_WIDTH).astype(u.dtype) @ w


def _gla_chunk_scan(q, k, v, logf, s0):
    b, l, h, _ = q.shape
    n = l // CHUNK

    def to_chunks(t):
        return t.reshape(b, n, CHUNK, h, t.shape[-1]).transpose(1, 0, 3, 2, 4)

    causal = jnp.tril(jnp.ones((CHUNK, CHUNK), dtype=bool))[None, None, :, :, None]

    def step(s, inp):
        qc, kc, vc, gc = inp
        cum = jnp.cumsum(gc, axis=2)
        o_inter = jnp.einsum('bhtk,bhkv->bhtv', qc * jnp.exp(cum), s)
        dec = jnp.where(causal, cum[:, :, :, None, :] - cum[:, :, None, :, :], -jnp.inf)
        a = jnp.einsum('bhtk,bhsk,bhtsk->bhts', qc, kc, jnp.exp(dec))
        o = o_inter + jnp.einsum('bhts,bhsv->bhtv', a, vc)
        last = cum[:, :, -1:, :]
        s_new = jnp.exp(last[:, :, 0, :])[..., None] * s + jnp.einsum('bhsk,bhsv->bhkv', kc * jnp.exp(last - cum), vc)
        return s_new, o

    s_fin, o = lax.scan(step, s0, (to_chunks(q), to_chunks(k), to_chunks(v), to_chunks(logf)))
    o = o.transpose(1, 0, 3, 2, 4).reshape(b, l, h, v.shape[-1])
    return o, s_fin


def _hgrn2_mixer(hq, hi, hf_fwd, hf_bwd, hg, lb, norm_g, s0_fwd, s0_bwd):
    b, l, _ = hq.shape
    shp = (b, l, HG_HEADS, HG_DK)
    q = jax.nn.silu(hq.astype(jnp.float32)).reshape(shp) * (HG_DK ** -0.5)
    v = hi.astype(jnp.float32).reshape(b, l, HG_HEADS, HG_DV)
    outs = []
    finals = []
    for d, (fp, s0) in enumerate(((hf_fwd, s0_fwd), (hf_bwd, s0_bwd))):
        fpf = fp.astype(jnp.float32).reshape(shp)
        lbd = lb[d].astype(jnp.float32).reshape(HG_HEADS, HG_DK)
        logf = jnp.logaddexp(jnp.log(lbd), jnp.log1p(-lbd) + jax.nn.log_sigmoid(fpf))
        kk = (1.0 - lbd) * jax.nn.sigmoid(-fpf)
        if d == 1:
            qd, kd, vd, gd = (jnp.flip(t, axis=1) for t in (q, kk, v, logf))
        else:
            qd, kd, vd, gd = q, kk, v, logf
        o, s_fin = _gla_chunk_scan(qd, kd, vd, gd, s0.astype(jnp.float32))
        if d == 1:
            o = jnp.flip(o, axis=1)
        outs.append(o)
        finals.append(s_fin)
    o = _rmsnorm(outs[0] + outs[1], norm_g) * jax.nn.silu(hg.astype(jnp.float32).reshape(b, l, HG_HEADS, HG_DV))
    return o.reshape(b, l, HG_WIDTH).astype(hq.dtype), finals[0], finals[1]


def _moe(h, p):
    t = h.shape[0]
    scores = jax.nn.sigmoid((h @ p['w_router']).astype(jnp.float32))
    biased = scores + p['router_bias'].astype(jnp.float32)
    per_group = N_EXPERTS // N_GROUPS
    gscore = lax.top_k(biased.reshape(t, N_GROUPS, per_group), 2)[0].sum(-1)
    _, gidx = lax.top_k(gscore, TOPK_GROUPS)
    gmask = jnp.sum(jax.nn.one_hot(gidx, N_GROUPS, dtype=jnp.float32), axis=1) > 0.0
    emask = jnp.repeat(gmask, per_group, axis=1)
    _, eidx = lax.top_k(jnp.where(emask, biased, -jnp.inf), TOP_K)
    w = jnp.take_along_axis(scores, eidx, axis=1)
    w = w / jnp.sum(w, axis=-1, keepdims=True) * ROUTED_SCALE
    gates = jnp.sum(jax.nn.one_hot(eidx, N_EXPERTS, dtype=jnp.float32) * w[..., None], axis=1)
    hg = jnp.einsum('td,edf->tef', h, p['w_gate'])
    hu = jnp.einsum('td,edf->tef', h, p['w_up'])
    act = jax.nn.silu(hg) * hu * gates[:, :, None].astype(h.dtype)
    routed = jnp.einsum('tef,efd->td', act, p['w_down'])
    shared = (jax.nn.silu(h @ p['ws_gate']) * (h @ p['ws_up'])) @ p['ws_down']
    return routed + shared


def _trunk_layer(x, mod, layer_idx, p, ctx=None):
    b, l, _ = x.shape
    sh1, sc1, g1, sh2, sc2, g2 = jnp.split(mod[:, None, :], 6, axis=-1)
    h = _rmsnorm(x, p['norm_g'][0]) * (1 + sc1) + sh1
    proj = h @ p['w_in']
    widths = [FN_WIDTH, DA_WIDTH, DA_WIDTH, DA_WIDTH, HG_WIDTH, HG_WIDTH, HG_WIDTH, HG_WIDTH]
    cuts = [int(v) for v in np.cumsum(widths)]
    u_fn, u_q, u_k, u_v, hq, hi, hf_fwd, hf_bwd, hg = jnp.split(proj, cuts, axis=-1)

    fn_out = _fourier_mixer(u_fn, p['w_fourier'])

    q = u_q.reshape(b, l, DA_HEADS, 2, DA_HALF)
    k = u_k.reshape(b, l, DA_HEADS, 2, DA_HALF)
    v = u_v.reshape(b, l, DA_HEADS, DA_VDIM)
    lam_init = 0.8 - 0.6 * math.exp(-0.3 * layer_idx)
    lmb = p['lambdas'].astype(jnp.float32)
    lam = jnp.exp(jnp.sum(lmb[0] * lmb[1])) - jnp.exp(jnp.sum(lmb[2] * lmb[3])) + lam_init
    if ctx is None:
        a = _diff_attention(q, k, v, lam)
        s0_fwd = jnp.zeros((b, HG_HEADS, HG_DK, HG_DV), jnp.float32)
        s0_bwd = s0_fwd
    else:
        cos, sin, k_ctx, v_ctx, s_ctx = ctx
        q = _apply_rope(q, cos, sin)
        k_lat = _apply_rope(k, cos, sin)
        k_all = jnp.concatenate([k_lat, k_ctx.reshape(b, -1, DA_HEADS, 2, DA_HALF).astype(k.dtype)], axis=1)
        v_all = jnp.concatenate([v, v_ctx.astype(v.dtype)], axis=1)
        a = _diff_attention(q, k_all, v_all, lam)
        s0_fwd, s0_bwd = s_ctx[:, 0], s_ctx[:, 1]
    a = (_rmsnorm(a, p['attn_norm_g']) * (1.0 - lam_init)).reshape(b, l, DA_WIDTH).astype(x.dtype)

    hg_out, s_fwd, s_bwd = _hgrn2_mixer(hq, hi, hf_fwd, hf_bwd, hg, p['lb'], p['hg_norm_g'], s0_fwd, s0_bwd)

    mix = jnp.concatenate([fn_out, a, hg_out], axis=-1) @ p['w_out']
    x = x + g1 * mix
    h2 = _rmsnorm(x, p['norm_g'][1]) * (1 + sc2) + sh2
    x = x + g2 * _moe(h2.reshape(b * l, D_MODEL), p).reshape(b, l, D_MODEL)
    if ctx is None:
        new_ctx = (k.reshape(b, l, DA_HEADS, 2 * DA_HALF), v, jnp.stack([s_fwd, s_bwd], axis=1))
        return x, new_ctx
    return x, None


def setup_inputs(seed: int = 0) -> dict:
    key = jax.random.key(seed)
    ks = jax.random.split(key, 26)

    def nrm(k, shape, s):
        return jax.random.normal(k, shape, jnp.float32) * s

    return {
        'x_prompt': nrm(ks[0], (BATCH, SEQ, D_MODEL), 1.0),
        'x_sample': nrm(ks[1], (DEC_BATCH, DEC_SEQ, D_MODEL), 1.0),
        'cache_k': nrm(ks[2], (DEC_BATCH, DEPTH, PAST_LEN, DA_HEADS, 2 * DA_HALF), 1.0),
        'cache_v': nrm(ks[3], (DEC_BATCH, DEPTH, PAST_LEN, DA_HEADS, DA_VDIM), 1.0),
        'state_hgrn': nrm(ks[4], (DEC_BATCH, DEPTH, 2, HG_HEADS, HG_DK, HG_DV), 0.5),
        'c': nrm(ks[5], (DEC_BATCH, D_MODEL), 1.0),
        'c_ctx': nrm(ks[6], (D_MODEL,), 1.0),
        'w_ada': nrm(ks[7], (DEPTH, D_MODEL, 6 * D_MODEL), 0.5 * D_MODEL ** -0.5),
        'b_ada': nrm(ks[8], (DEPTH, 6 * D_MODEL), 0.02),
        'norm_g': 1.0 + nrm(ks[9], (DEPTH, 2, D_MODEL), 0.02),
        'w_in': nrm(ks[10], (DEPTH, D_MODEL, PROJ_WIDTH), D_MODEL ** -0.5),
        'w_fourier': nrm(ks[11], (DEPTH, FN_WIDTH, FN_WIDTH), FN_WIDTH ** -0.5),
        'lambdas': nrm(ks[12], (DEPTH, 4, DA_HALF), 0.1),
        'attn_norm_g': 1.0 + nrm(ks[13], (DEPTH, DA_VDIM), 0.02),
        'lower_bounds': nrm(ks[14], (DEPTH, 2, HG_WIDTH), 1.0),
        'hg_norm_g': 1.0 + nrm(ks[15], (DEPTH, HG_DV), 0.02),
        'w_out': nrm(ks[16], (DEPTH, MIX_WIDTH, D_MODEL), MIX_WIDTH ** -0.5),
        'w_router': nrm(ks[17], (DEPTH, D_MODEL, N_EXPERTS), D_MODEL ** -0.5),
        'router_bias': nrm(ks[18], (DEPTH, N_EXPERTS), 0.01),
        'w_gate': nrm(ks[19], (DEPTH, N_EXPERTS, D_MODEL, D_EXPERT), D_MODEL ** -0.5),
        'w_up': nrm(ks[20], (DEPTH, N_EXPERTS, D_MODEL, D_EXPERT), D_MODEL ** -0.5),
        'w_down': nrm(ks[21], (DEPTH, N_EXPERTS, D_EXPERT, D_MODEL), D_EXPERT ** -0.5),
        'ws_gate': nrm(ks[22], (DEPTH, D_MODEL, D_SHARED), D_MODEL ** -0.5),
        'ws_up': nrm(ks[23], (DEPTH, D_MODEL, D_SHARED), D_MODEL ** -0.5),
        'ws_down': nrm(ks[24], (DEPTH, D_SHARED, D_MODEL), D_SHARED ** -0.5),
        'final_g': 1.0 + nrm(ks[25], (D_MODEL,), 0.02),
    }


def reference(x_prompt, x_sample, cache_k, cache_v, state_hgrn, c, c_ctx, w_ada, b_ada, norm_g,
              w_in, w_fourier, lambdas, attn_norm_g, lower_bounds, hg_norm_g, w_out, w_router,
              router_bias, w_gate, w_up, w_down, ws_gate, ws_up, ws_down, final_g):
    cs = jnp.cumsum(jax.nn.softmax(lower_bounds.astype(jnp.float32), axis=0), axis=0)
    lbs = cs - cs[0:1]
    cos, sin = _axial_rope(x_sample.shape[1])
    xp, xs = x_prompt, x_sample
    new_k, new_v, new_s = [], [], []
    for l in range(DEPTH):
        p = {
            'norm_g': norm_g[l], 'w_in': w_in[l], 'w_fourier': w_fourier[l], 'lambdas': lambdas[l],
            'attn_norm_g': attn_norm_g[l], 'lb': lbs[l], 'hg_norm_g': hg_norm_g[l], 'w_out': w_out[l],
            'w_router': w_router[l], 'router_bias': router_bias[l], 'w_gate': w_gate[l], 'w_up': w_up[l],
            'w_down': w_down[l], 'ws_gate': ws_gate[l], 'ws_up': ws_up[l], 'ws_down': ws_down[l],
        }
        mod_ctx = (jax.nn.silu(c_ctx) @ w_ada[l] + b_ada[l])[None, :]
        xp, (kc, vc, sc) = _trunk_layer(xp, mod_ctx, l, p)
        new_k.append(kc)
        new_v.append(vc)
        new_s.append(sc)
        mod_lat = jax.nn.silu(c) @ w_ada[l] + b_ada[l]
        xs, _ = _trunk_layer(xs, mod_lat, l, p, (cos, sin, cache_k[:, l], cache_v[:, l], state_hgrn[:, l]))
    y_prompt = _rmsnorm(xp, final_g)
    y_sample = _rmsnorm(xs, final_g)
    new_cache_k = jnp.stack(new_k, axis=1)
    new_cache_v = jnp.stack(new_v, axis=1)
    new_state_hgrn = jnp.stack(new_s, axis=1)
    return (y_prompt, y_sample, new_cache_k, new_cache_v, new_state_hgrn)
```

```python
import functools
import math

import numpy as np
import jax
import jax.numpy as jnp
from jax import lax
from jax.experimental import pallas as pl
from jax.experimental.pallas import tpu as pltpu

F32 = jnp.float32
BF16 = jnp.bfloat16

D_MODEL = 1024
DEPTH = 2
GRID_W = 64
FN_WIDTH = 256
FN_GROUPS = 4
FN_GROUP_DIM = 64
DA_WIDTH = 512
DA_HEADS = 4
DA_VDIM = 128
DA_HALF = 64
HG_WIDTH = 256
HG_HEADS = 4
HG_DK = 64
PROJ_WIDTH = 3072
CHUNK = 64
ROPE_THETA = 10000.0
N_EXPERTS = 64
TOP_K = 8
N_GROUPS = 8
TOPK_GROUPS = 4
D_EXPERT = 128
ROUTED_SCALE = 2.5
EPS = 1e-6

COL_Q, COL_K, COL_V = 0, 512, 1024
COL_HQ, COL_HI, COL_HF, COL_HB, COL_HG = 1536, 1792, 2048, 2304, 2560
COL_FN = 2816

TOKEN_TILE = 256
MOE_TILE = 1024
EXPERTS_PER_STEP = 8
ROUTE_TILE = 512
VMEM_LIMIT = 56 * 1024 * 1024


def _dot(a, b):
    return jnp.dot(a, b, preferred_element_type=F32)


def _dot_nt(a, b):
    return lax.dot_general(a, b, (((1,), (1,)), ((), ())), preferred_element_type=F32)


def _dot_tn(a, b):
    return lax.dot_general(a, b, (((0,), (0,)), ((), ())), preferred_element_type=F32)


def _split(x, n):
    parts = []
    r = x
    for i in range(n):
        p = r.astype(BF16)
        parts.append(p)
        if i + 1 < n:
            r = r - p.astype(F32)
    return parts


def _sigmoid(x):
    return 1.0 / (1.0 + jnp.exp(-x))


def _silu(x):
    return x * _sigmoid(x)


def _params(n_axes):
    return pltpu.CompilerParams(dimension_semantics=("arbitrary",) * n_axes,
                                vmem_limit_bytes=VMEM_LIMIT)


def _ada_body(c_ref, w_ref, b_ref, o_ref):
    a = _silu(c_ref[...])
    a_hi, a_lo = _split(a, 2)
    w_hi, w_lo = _split(w_ref[...], 2)
    o_ref[...] = _dot(a_hi, w_hi) + _dot(a_lo, w_hi) + _dot(a_hi, w_lo) + b_ref[...]


def _ada_mods(c8, w_ada, b_ada):
    tn = 1536
    return pl.pallas_call(
        _ada_body,
        grid=(DEPTH, 6 * D_MODEL // tn),
        in_specs=[
            pl.BlockSpec((8, D_MODEL), lambda l, j: (0, 0)),
            pl.BlockSpec((None, D_MODEL, tn), lambda l, j: (l, 0, j)),
            pl.BlockSpec((None, 1, tn), lambda l, j: (l, 0, j)),
        ],
        out_specs=pl.BlockSpec((None, 8, tn), lambda l, j: (l, 0, j)),
        out_shape=jax.ShapeDtypeStruct((DEPTH, 8, 6 * D_MODEL), F32),
        compiler_params=_params(2),
        name="ada_mods",
    )(c8, w_ada, b_ada.reshape(DEPTH, 1, 6 * D_MODEL))


def _modnorm(x, g, shift, scale):
    ms = jnp.mean(x * x, axis=-1, keepdims=True)
    return (x * lax.rsqrt(ms + EPS) * g) * (1.0 + scale) + shift


def _inproj_body(x_ref, mod_ref, g_ref, w_ref, o_ref):
    h = _modnorm(x_ref[...], g_ref[...], mod_ref[0:1, :], mod_ref[1:2, :])
    o_ref[...] = _dot(h.astype(BF16), w_ref[...])


def _mod_row(n_ctx_tiles, tiles_per_latent):
    def f(i):
        return jnp.where(i < n_ctx_tiles, 0, 1 + (i - n_ctx_tiles) // tiles_per_latent)
    return f


def _inproj(x, mod, g, w, n_ctx_tok, lat_len):
    t = x.shape[0]
    tm = TOKEN_TILE
    row = _mod_row(n_ctx_tok // tm, lat_len // tm)
    return pl.pallas_call(
        _inproj_body,
        grid=(t // tm,),
        in_specs=[
            pl.BlockSpec((tm, D_MODEL), lambda i: (i, 0)),
            pl.BlockSpec((None, 6, D_MODEL), lambda i: (row(i), 0, 0)),
            pl.BlockSpec((1, D_MODEL), lambda i: (0, 0)),
            pl.BlockSpec((D_MODEL, PROJ_WIDTH), lambda i: (0, 0)),
        ],
        out_specs=pl.BlockSpec((tm, PROJ_WIDTH), lambda i: (i, 0)),
        out_shape=jax.ShapeDtypeStruct((t, PROJ_WIDTH), F32),
        compiler_params=_params(1),
        name="inproj",
    )(x, mod, g, w)


def _fourier_body(u_ref, cl_ref, sl_ref, cc_ref, sc_ref, w_ref, o_ref):
    z = u_ref[...].astype(BF16)
    a = _dot(z, cc_ref[...]).astype(BF16)
    b = _dot(z, sc_ref[...]).astype(BF16)
    y = _dot(cl_ref[...], a) - _dot(sl_ref[...], b)
    o_ref[...] = _dot(y.astype(BF16), w_ref[...]).astype(o_ref.dtype)


def _dft_tables(n, block):
    i = jnp.arange(n, dtype=jnp.int32)
    prod = (i[:, None] % block) * (i[None, :] % block) % block
    ang = prod.astype(F32) * (2.0 * math.pi / block)
    same = (i[:, None] // block) == (i[None, :] // block)
    scale = 1.0 / math.sqrt(block)
    c = jnp.where(same, jnp.cos(ang) * scale, 0.0)
    s = jnp.where(same, jnp.sin(ang) * scale, 0.0)
    return c.astype(BF16), s.astype(BF16)


def _fourier(proj, row0, n_seq, seq_len, w_f):
    cl, sl = _dft_tables(seq_len, seq_len)
    cc, sc = _dft_tables(FN_WIDTH, FN_GROUP_DIM)
    blk0 = row0 // seq_len
    full = lambda shape: pl.BlockSpec(shape, lambda i: (0, 0))
    return pl.pallas_call(
        _fourier_body,
        grid=(n_seq,),
        in_specs=[
            pl.BlockSpec((seq_len, FN_WIDTH), lambda i: (blk0 + i, COL_FN // FN_WIDTH)),
            full((seq_len, seq_len)), full((seq_len, seq_len)),
            full((FN_WIDTH, FN_WIDTH)), full((FN_WIDTH, FN_WIDTH)), full((FN_WIDTH, FN_WIDTH)),
        ],
        out_specs=pl.BlockSpec((seq_len, FN_WIDTH), lambda i: (i, 0)),
        out_shape=jax.ShapeDtypeStruct((n_seq * seq_len, FN_WIDTH), BF16),
        compiler_params=_params(1),
        name=f"fourier_{seq_len}",
    )(proj, cl, sl, cc, sc, w_f)


def _lambda_full(lmb, lam_init):
    a = jnp.sum(lmb[0:1, :] * lmb[1:2, :], axis=-1, keepdims=True)
    b = jnp.sum(lmb[2:3, :] * lmb[3:4, :], axis=-1, keepdims=True)
    return jnp.exp(a) - jnp.exp(b) + lam_init


def _softmax_parts(parts):
    m = parts[0].max(axis=-1, keepdims=True)
    for p in parts[1:]:
        m = jnp.maximum(m, p.max(axis=-1, keepdims=True))
    es = [jnp.exp(p - m) for p in parts]
    tot = es[0].sum(axis=-1, keepdims=True)
    for e in es[1:]:
        tot = tot + e.sum(axis=-1, keepdims=True)
    return es, 1.0 / tot


def _diff_head(q, ks, vs, lam, g, lam_init):
    outs = []
    for m in range(2):
        qm = q[:, m * DA_HALF:(m + 1) * DA_HALF].astype(BF16)
        parts = [_dot_nt(qm, k[:, m * DA_HALF:(m + 1) * DA_HALF].astype(BF16)) for k in ks]
        es, inv = _softmax_parts(parts)
        o = _dot(es[0].astype(BF16), vs[0].astype(BF16))
        for e, v in zip(es[1:], vs[1:]):
            o = o + _dot(e.astype(BF16), v.astype(BF16))
        outs.append(o * inv)
    a = outs[0] - lam * outs[1]
    ms = jnp.mean(a * a, axis=-1, keepdims=True)
    return a * lax.rsqrt(ms + EPS) * g * (1.0 - lam_init)


def _attn_ctx_body(lam_init, q_ref, k_ref, v_ref, lmb_ref, g_ref, o_ref):
    lam = _lambda_full(lmb_ref[...], lam_init)
    scale = DA_HALF ** -0.5
    for h in range(DA_HEADS):
        sl = slice(h * DA_VDIM, (h + 1) * DA_VDIM)
        o = _diff_head(q_ref[:, sl] * scale, [k_ref[:, sl]], [v_ref[:, sl]], lam, g_ref[...], lam_init)
        o_ref[:, sl] = o.astype(o_ref.dtype)


def _attn_ctx(proj, n_seq, seq_len, lmb, g, lam_init):
    blk = lambda c: pl.BlockSpec((seq_len, DA_WIDTH), lambda i: (i, c))
    return pl.pallas_call(
        functools.partial(_attn_ctx_body, lam_init),
        grid=(n_seq,),
        in_specs=[
            blk(COL_Q // DA_WIDTH), blk(COL_K // DA_WIDTH), blk(COL_V // DA_WIDTH),
            pl.BlockSpec((4, DA_HALF), lambda i: (0, 0)),
            pl.BlockSpec((1, DA_VDIM), lambda i: (0, 0)),
        ],
        out_specs=pl.BlockSpec((seq_len, DA_WIDTH), lambda i: (i, 0)),
        out_shape=jax.ShapeDtypeStruct((n_seq * seq_len, DA_WIDTH), BF16),
        compiler_params=_params(1),
        name="attn_ctx",
    )(proj, proj, proj, lmb, g)


def _rope(x, cos, sin):
    lane = lax.broadcasted_iota(jnp.int32, x.shape, 1)
    first = ((lane >> 4) & 1) == 0
    rot = jnp.where(first, -pltpu.roll(x, 128 - DA_HALF // 4, 1), pltpu.roll(x, DA_HALF // 4, 1))
    return x * cos + rot * sin


def _attn_lat_body(lam_init, q_ref, k_ref, v_ref, kc_ref, vc_ref, cq_ref, sq_ref, ck_ref, sk_ref,
                   lmb_ref, g_ref, o_ref):
    lam = _lambda_full(lmb_ref[...], lam_init)
    scale = DA_HALF ** -0.5
    for h in range(DA_HEADS):
        sl = slice(h * DA_VDIM, (h + 1) * DA_VDIM)
        q = _rope(q_ref[:, sl], cq_ref[...], sq_ref[...]) * scale
        k = _rope(k_ref[:, sl], ck_ref[...], sk_ref[...])
        o = _diff_head(q, [k, kc_ref[:, sl]], [v_ref[:, sl], vc_ref[:, sl]], lam, g_ref[...], lam_init)
        o_ref[:, sl] = o.astype(o_ref.dtype)


def _attn_lat(proj, row0, n_seq, seq_len, cache_k, cache_v, layer, cos, sin, lmb, g, lam_init):
    tq = 512
    nq = seq_len // tq
    past = cache_k.shape[2]
    qb0 = row0 // tq
    kb0 = row0 // seq_len
    cache_spec = pl.BlockSpec((None, None, past, DA_WIDTH), lambda b, j: (b, layer, 0, 0))
    kv_spec = lambda c: pl.BlockSpec((seq_len, DA_WIDTH), lambda b, j: (kb0 + b, c))
    return pl.pallas_call(
        functools.partial(_attn_lat_body, lam_init),
        grid=(n_seq, nq),
        in_specs=[
            pl.BlockSpec((tq, DA_WIDTH), lambda b, j: (qb0 + b * nq + j, COL_Q // DA_WIDTH)),
            kv_spec(COL_K // DA_WIDTH), kv_spec(COL_V // DA_WIDTH),
            cache_spec, cache_spec,
            pl.BlockSpec((tq, DA_VDIM), lambda b, j: (j, 0)),
            pl.BlockSpec((tq, DA_VDIM), lambda b, j: (j, 0)),
            pl.BlockSpec((seq_len, DA_VDIM), lambda b, j: (0, 0)),
            pl.BlockSpec((seq_len, DA_VDIM), lambda b, j: (0, 0)),
            pl.BlockSpec((4, DA_HALF), lambda b, j: (0, 0)),
            pl.BlockSpec((1, DA_VDIM), lambda b, j: (0, 0)),
        ],
        out_specs=pl.BlockSpec((tq, DA_WIDTH), lambda b, j: (b * nq + j, 0)),
        out_shape=jax.ShapeDtypeStruct((n_seq * seq_len, DA_WIDTH), BF16),
        compiler_params=_params(2),
        name="attn_lat",
    )(proj, proj, proj, cache_k, cache_v, cos, sin, cos, sin, lmb, g)


def _rope_tables(n_tokens):
    rows = n_tokens // GRID_W
    row = jnp.repeat(jnp.arange(rows, dtype=F32), GRID_W)
    col = jnp.tile(jnp.arange(GRID_W, dtype=F32), rows)
    axis_dim = DA_HALF // 2
    inv_freq = ROPE_THETA ** (-jnp.arange(0, axis_dim, 2, dtype=F32) / axis_dim)
    ang_r = row[:, None] * inv_freq[None, :]
    ang_c = col[:, None] * inv_freq[None, :]
    ang = jnp.concatenate([ang_r, ang_r, ang_c, ang_c] * 2, axis=-1)
    return jnp.cos(ang), jnp.sin(ang)


def _hgrn_body(n_chunks, has_s0, *refs):
    if has_s0:
        (hq_ref, hi_ref, hf_ref, hb_ref, hg_ref, lb_ref, ng_ref, s0_ref,
         o_ref, st_ref, of_scr, ob_scr) = refs
    else:
        (hq_ref, hi_ref, hf_ref, hb_ref, hg_ref, lb_ref, ng_ref,
         o_ref, st_ref, of_scr, ob_scr) = refs
    c = CHUNK
    w = HG_WIDTH
    r_io = lax.broadcasted_iota(jnp.int32, (c, c), 0)
    c_io = lax.broadcasted_iota(jnp.int32, (c, c), 1)
    tri_f = (c_io <= r_io).astype(BF16)
    tri_b = (c_io >= r_io).astype(BF16)
    lane = lax.broadcasted_iota(jnp.int32, (1, w), 1)
    dk_bits = HG_DK.bit_length() - 1
    head_masks = [((lane >> dk_bits) == h).astype(F32) for h in range(HG_HEADS)]
    bd = ((lax.broadcasted_iota(jnp.int32, (w, w), 0) >> dk_bits)
          == (lax.broadcasted_iota(jnp.int32, (w, w), 1) >> dk_bits))
    t_of_row = lax.broadcasted_iota(jnp.int32, (HG_HEADS * c, c), 0) & (c - 1)
    s_of_col = lax.broadcasted_iota(jnp.int32, (HG_HEADS * c, c), 1)
    causal_f = s_of_col <= t_of_row
    causal_b = s_of_col >= t_of_row

    if has_s0:
        st_ref[0] = s0_ref[0]
        st_ref[1] = s0_ref[1]
    else:
        st_ref[...] = jnp.zeros_like(st_ref)

    def one_dir(d, start, fp_ref, tri, causal, edge_row, out_scr):
        rows = pl.ds(start, c)
        lb = lb_ref[d:d + 1, :]
        fp = fp_ref[rows, :]
        lsig = jnp.minimum(fp, 0.0) - jnp.log1p(jnp.exp(-jnp.abs(fp)))
        la = jnp.log(lb)
        lbb = jnp.log1p(-lb) + lsig
        logf = jnp.maximum(la, lbb) + jnp.log1p(jnp.exp(-jnp.abs(la - lbb)))
        kk = (1.0 - lb) * (1.0 / (1.0 + jnp.exp(fp)))
        q = _silu(hq_ref[rows, :]) * (HG_DK ** -0.5)
        v = hi_ref[rows, :].astype(BF16)
        g3 = _split(logf, 3)
        cum = _dot(tri, g3[0]) + _dot(tri, g3[1]) + _dot(tri, g3[2])
        total = cum[edge_row:edge_row + 1, :]
        ref = cum[c // 2:c // 2 + 1, :]
        qc = q * jnp.exp(cum - ref)
        kc = (kk * jnp.exp(ref - cum)).astype(BF16)
        qe = (q * jnp.exp(cum)).astype(BF16)
        ke = (kk * jnp.exp(total - cum)).astype(BF16)
        dec = jnp.exp(total)
        st = st_ref[d]
        o_inter = _dot_nt(qe, st.astype(BF16))
        lhs = jnp.concatenate([qc * hm for hm in head_masks], axis=0).astype(BF16)
        a = jnp.where(causal, _dot_nt(lhs, kc), 0.0).astype(BF16)
        o_stack = _dot(a, v)
        o = o_inter
        for h in range(HG_HEADS):
            o = o + o_stack[h * c:(h + 1) * c, :] * head_masks[h]
        out_scr[rows, :] = o
        st_ref[d] = st * dec + jnp.where(bd, _dot_tn(v, ke), 0.0)

    def chunk_step(i, carry):
        sf = pl.multiple_of(i * c, c)
        sb = pl.multiple_of((n_chunks - 1 - i) * c, c)
        one_dir(0, sf, hf_ref, tri_f, causal_f, c - 1, of_scr)
        one_dir(1, sb, hb_ref, tri_b, causal_b, 0, ob_scr)
        return carry

    lax.fori_loop(0, n_chunks, chunk_step, 0)

    ones_bd = bd.astype(BF16)

    def finish(i, carry):
        rows = pl.ds(pl.multiple_of(i * c, c), c)
        o = of_scr[rows, :] + ob_scr[rows, :]
        sq = _split(o * o, 2)
        ms = (_dot(sq[0], ones_bd) + _dot(sq[1], ones_bd)) * (1.0 / HG_DK)
        y = o * lax.rsqrt(ms + EPS) * ng_ref[...]
        o_ref[rows, :] = (y * _silu(hg_ref[rows, :])).astype(o_ref.dtype)
        return carry

    lax.fori_loop(0, n_chunks, finish, 0)


def _hgrn(proj, row0, n_seq, seq_len, lb, ng, s0t):
    n_chunks = seq_len // CHUNK
    blk0 = row0 // seq_len
    col = lambda c0: pl.BlockSpec((seq_len, HG_WIDTH), lambda i: (blk0 + i, c0 // HG_WIDTH))
    in_specs = [col(COL_HQ), col(COL_HI), col(COL_HF), col(COL_HB), col(COL_HG),
                pl.BlockSpec((2, HG_WIDTH), lambda i: (0, 0)),
                pl.BlockSpec((1, HG_WIDTH), lambda i: (0, 0))]
    args = [proj, proj, proj, proj, proj, lb, ng]
    if s0t is not None:
        in_specs.append(pl.BlockSpec((None, 2, HG_WIDTH, HG_WIDTH), lambda i: (i, 0, 0, 0)))
        args.append(s0t)
    return pl.pallas_call(
        functools.partial(_hgrn_body, n_chunks, s0t is not None),
        grid=(n_seq,),
        in_specs=in_specs,
        out_specs=[pl.BlockSpec((seq_len, HG_WIDTH), lambda i: (i, 0)),
                   pl.BlockSpec((None, 2, HG_WIDTH, HG_WIDTH), lambda i: (i, 0, 0, 0))],
        out_shape=[jax.ShapeDtypeStruct((n_seq * seq_len, HG_WIDTH), BF16),
                   jax.ShapeDtypeStruct((n_seq, 2, HG_WIDTH, HG_WIDTH), F32)],
        scratch_shapes=[pltpu.VMEM((seq_len, HG_WIDTH), F32), pltpu.VMEM((seq_len, HG_WIDTH), F32)],
        compiler_params=_params(1),
        name=f"hgrn_{seq_len}",
    )(*args)


def _outproj_body(fn_ref, a_ref, hg_ref, x_ref, mod_ref, g_ref, w_ref, wr_ref, xo_ref, h_ref, lt_ref):
    mix = (_dot(fn_ref[...], w_ref[0:FN_WIDTH, :])
           + _dot(a_ref[...], w_ref[FN_WIDTH:FN_WIDTH + DA_WIDTH, :])
           + _dot(hg_ref[...], w_ref[FN_WIDTH + DA_WIDTH:, :]))
    x = x_ref[...] + mod_ref[2:3, :] * mix
    xo_ref[...] = x
    h = _modnorm(x, g_ref[...], mod_ref[3:4, :], mod_ref[4:5, :])
    h_ref[...] = h.astype(BF16)
    h_hi, h_lo = _split(h, 2)
    w_hi, w_lo = _split(wr_ref[...], 2)
    lt_ref[...] = _dot_nt(w_hi, h_hi) + _dot_nt(w_lo, h_hi) + _dot_nt(w_hi, h_lo)


def _outproj(fn, a, hg, x, mod, g, w_out, w_router_t, n_ctx_tok, lat_len):
    t = x.shape[0]
    tm = TOKEN_TILE
    row = _mod_row(n_ctx_tok // tm, lat_len // tm)
    rows = lambda width: pl.BlockSpec((tm, width), lambda i: (i, 0))
    return pl.pallas_call(
        _outproj_body,
        grid=(t // tm,),
        in_specs=[
            rows(FN_WIDTH), rows(DA_WIDTH), rows(HG_WIDTH), rows(D_MODEL),
            pl.BlockSpec((None, 6, D_MODEL), lambda i: (row(i), 0, 0)),
            pl.BlockSpec((1, D_MODEL), lambda i: (0, 0)),
            pl.BlockSpec((D_MODEL, D_MODEL), lambda i: (0, 0)),
            pl.BlockSpec((N_EXPERTS, D_MODEL), lambda i: (0, 0)),
        ],
        out_specs=[rows(D_MODEL), rows(D_MODEL), pl.BlockSpec((N_EXPERTS, tm), lambda i: (0, i))],
        out_shape=[jax.ShapeDtypeStruct((t, D_MODEL), F32),
                   jax.ShapeDtypeStruct((t, D_MODEL), BF16),
                   jax.ShapeDtypeStruct((N_EXPERTS, t), F32)],
        compiler_params=_params(1),
        name="outproj",
    )(fn, a, hg, x, mod, g, w_out, w_router_t)


def _route_body(lt_ref, bias_ref, o_ref):
    per = N_EXPERTS // N_GROUPS
    tt = lt_ref.shape[1]
    neg = -jnp.inf
    io = lax.broadcasted_iota(jnp.int32, (per, tt), 0).astype(F32)
    s_g, b_g = [], []
    for g in range(N_GROUPS):
        s = _sigmoid(lt_ref[g * per:(g + 1) * per, :])
        s_g.append(s)
        b_g.append(s + bias_ref[g * per:(g + 1) * per, :])
    gs = []
    for g in range(N_GROUPS):
        x = b_g[g]
        m1 = x.max(axis=0, keepdims=True)
        i1 = jnp.where(x == m1, io, float(per)).min(axis=0, keepdims=True)
        m2 = jnp.where(io == i1, neg, x).max(axis=0, keepdims=True)
        gs.append(m1 + m2)
    gsel = [jnp.zeros((1, tt), jnp.bool_) for _ in range(N_GROUPS)]
    for _ in range(TOPK_GROUPS):
        m = functools.reduce(jnp.maximum, gs)
        found = jnp.zeros((1, tt), jnp.bool_)
        for g in range(N_GROUPS):
            hit = (gs[g] == m) & jnp.logical_not(found)
            gsel[g] = gsel[g] | hit
            found = found | hit
            gs[g] = jnp.where(hit, neg, gs[g])
    x_g = [jnp.where(gsel[g], b_g[g], neg) for g in range(N_GROUPS)]
    e_g = [io + g * per for g in range(N_GROUPS)]
    sel_g = [jnp.zeros((per, tt), jnp.bool_) for _ in range(N_GROUPS)]
    for _ in range(TOP_K):
        m = functools.reduce(jnp.maximum, [x.max(axis=0, keepdims=True) for x in x_g])
        idx = functools.reduce(jnp.minimum, [jnp.where(x_g[g] == m, e_g[g], float(N_EXPERTS)).min(axis=0, keepdims=True)
                                             for g in range(N_GROUPS)])
        for g in range(N_GROUPS):
            hit = e_g[g] == idx
            sel_g[g] = sel_g[g] | hit
            x_g[g] = jnp.where(hit, neg, x_g[g])
    w_g = [jnp.where(sel_g[g], s_g[g], 0.0) for g in range(N_GROUPS)]
    denom = functools.reduce(lambda a, b: a + b, [w.sum(axis=0, keepdims=True) for w in w_g])
    gates_t = jnp.concatenate([w / denom * ROUTED_SCALE for w in w_g], axis=0)
    eye = (lax.broadcasted_iota(jnp.int32, (N_EXPERTS, N_EXPERTS), 0)
           == lax.broadcasted_iota(jnp.int32, (N_EXPERTS, N_EXPERTS), 1)).astype(BF16)
    p = _split(gates_t, 3)
    gates = _dot_tn(p[0], eye) + _dot_tn(p[1], eye) + _dot_tn(p[2], eye)
    for c in range(N_EXPERTS // EXPERTS_PER_STEP):
        o_ref[c] = gates[:, c * EXPERTS_PER_STEP:(c + 1) * EXPERTS_PER_STEP]


def _route(logits_t, bias):
    t = logits_t.shape[1]
    tt = ROUTE_TILE
    nch = N_EXPERTS // EXPERTS_PER_STEP
    return pl.pallas_call(
        _route_body,
        grid=(t // tt,),
        in_specs=[pl.BlockSpec((N_EXPERTS, tt), lambda i: (0, i)),
                  pl.BlockSpec((N_EXPERTS, 1), lambda i: (0, 0))],
        out_specs=pl.BlockSpec((nch, tt, EXPERTS_PER_STEP), lambda i: (0, i, 0)),
        out_shape=jax.ShapeDtypeStruct((nch, t, EXPERTS_PER_STEP), F32),
        compiler_params=_params(1),
        name="route",
    )(logits_t, bias)


def _moe_body(final_norm, h_ref, gate_ref, wgu_ref, wd_ref, sgu_ref, sd_ref, x_ref, mod_ref, fg_ref,
              o_ref, acc_ref):
    j = pl.program_id(1)
    h = h_ref[...]

    def act_of(w, gate):
        gu = _dot(h, w)
        a = _silu(gu[:, :D_EXPERT]) * gu[:, D_EXPERT:]
        return a if gate is None else a * gate

    @pl.when(j == 0)
    def _():
        acc_ref[...] = _dot(act_of(sgu_ref[...], None).astype(BF16), sd_ref[...])

    for p in range(EXPERTS_PER_STEP // 2):
        a0 = act_of(wgu_ref[2 * p], gate_ref[:, 2 * p:2 * p + 1])
        a1 = act_of(wgu_ref[2 * p + 1], gate_ref[:, 2 * p + 1:2 * p + 2])
        pair = jnp.concatenate([a0, a1], axis=1).astype(BF16)
        acc_ref[...] += _dot(pair, wd_ref[p])

    @pl.when(j == pl.num_programs(1) - 1)
    def _():
        x = x_ref[...] + mod_ref[5:6, :] * acc_ref[...]
        if final_norm:
            ms = jnp.mean(x * x, axis=-1, keepdims=True)
            x = x * lax.rsqrt(ms + EPS) * fg_ref[...]
        o_ref[...] = x


def _moe(h, gates, wgu, wd, sgu, sd, x, mod, final_g, final_norm, n_ctx_tok, lat_len):
    t = x.shape[0]
    tm = MOE_TILE
    eps_ = EXPERTS_PER_STEP
    row = _mod_row(n_ctx_tok // tm, lat_len // tm)
    return pl.pallas_call(
        functools.partial(_moe_body, final_norm),
        grid=(t // tm, N_EXPERTS // eps_),
        in_specs=[
            pl.BlockSpec((tm, D_MODEL), lambda i, j: (i, 0)),
            pl.BlockSpec((None, tm, eps_), lambda i, j: (j, i, 0)),
            pl.BlockSpec((eps_, D_MODEL, 2 * D_EXPERT), lambda i, j: (j, 0, 0)),
            pl.BlockSpec((eps_ // 2, 2 * D_EXPERT, D_MODEL), lambda i, j: (j, 0, 0)),
            pl.BlockSpec((D_MODEL, 2 * D_EXPERT), lambda i, j: (0, 0)),
            pl.BlockSpec((D_EXPERT, D_MODEL), lambda i, j: (0, 0)),
            pl.BlockSpec((tm, D_MODEL), lambda i, j: (i, 0)),
            pl.BlockSpec((None, 6, D_MODEL), lambda i, j: (row(i), 0, 0)),
            pl.BlockSpec((1, D_MODEL), lambda i, j: (0, 0)),
        ],
        out_specs=pl.BlockSpec((tm, D_MODEL), lambda i, j: (i, 0)),
        out_shape=jax.ShapeDtypeStruct((t, D_MODEL), F32),
        scratch_shapes=[pltpu.VMEM((tm, D_MODEL), F32)],
        compiler_params=_params(2),
        name="moe",
    )(h, gates, wgu, wd, sgu, sd, x, mod, final_g)


def _block_diag_t(s):
    eye = jnp.eye(HG_HEADS, dtype=s.dtype)
    out = jnp.einsum('...hkv,hg->...hvgk', s, eye)
    return out.reshape(s.shape[:-3] + (HG_WIDTH, HG_WIDTH))


def _unblock_diag_t(st):
    lead = st.shape[:-2]
    s5 = st.reshape(lead + (HG_HEADS, HG_DK, HG_HEADS, HG_DK))
    diag = jnp.stack([s5[..., h, :, h, :] for h in range(HG_HEADS)], axis=-3)
    return jnp.swapaxes(diag, -1, -2)


def kernel(x_prompt, x_sample, cache_k, cache_v, state_hgrn, c, c_ctx, w_ada, b_ada, norm_g, w_in,
           w_fourier, lambdas, attn_norm_g, lower_bounds, hg_norm_g, w_out, w_router, router_bias,
           w_gate, w_up, w_down, ws_gate, ws_up, ws_down, final_g):
    n_ctx, ctx_len, _ = x_prompt.shape
    n_lat, lat_len, _ = x_sample.shape
    n_ctx_tok = n_ctx * ctx_len
    past = cache_k.shape[2]

    x = jnp.concatenate([x_prompt.reshape(n_ctx_tok, D_MODEL), x_sample.reshape(n_lat * lat_len, D_MODEL)], axis=0)

    c8 = jnp.zeros((8, D_MODEL), F32).at[0].set(c_ctx).at[1:1 + n_lat].set(c)
    mods = _ada_mods(c8, w_ada, b_ada).reshape(DEPTH, 8, 6, D_MODEL)

    cs = jnp.cumsum(jax.nn.softmax(lower_bounds.astype(F32), axis=0), axis=0)
    lbs = cs - cs[0:1]

    cos, sin = _rope_tables(lat_len)
    cache_k4 = cache_k.reshape(n_lat, DEPTH, past, DA_WIDTH)
    cache_v4 = cache_v.reshape(n_lat, DEPTH, past, DA_WIDTH)
    s0t = _block_diag_t(state_hgrn.astype(F32))

    perm = np.concatenate([np.arange(256, 1792), np.arange(1792, 3072), np.arange(0, 256)])
    w_in_b = w_in[:, :, perm].astype(BF16)
    w_f_b = w_fourier.astype(BF16)
    w_out_b = w_out.astype(BF16)
    w_router_t = jnp.swapaxes(w_router, 1, 2)
    wgu = jnp.concatenate([w_gate, w_up], axis=-1).astype(BF16)
    wd = w_down.astype(BF16).reshape(DEPTH, N_EXPERTS // 2, 2 * D_EXPERT, D_MODEL)
    sgu = jnp.concatenate([ws_gate, ws_up], axis=-1).astype(BF16)
    sd = ws_down.astype(BF16)
    ng = jnp.tile(hg_norm_g, (1, HG_HEADS))

    new_k, new_v, new_s = [], [], []
    for l in range(DEPTH):
        lam_init = 0.8 - 0.6 * math.exp(-0.3 * l)
        mod = mods[l]
        proj = _inproj(x, mod, norm_g[l, 0:1], w_in_b[l], n_ctx_tok, lat_len)
        new_k.append(proj[:n_ctx_tok, COL_K:COL_K + DA_WIDTH].reshape(n_ctx, ctx_len, DA_HEADS, DA_VDIM))
        new_v.append(proj[:n_ctx_tok, COL_V:COL_V + DA_WIDTH].reshape(n_ctx, ctx_len, DA_HEADS, DA_VDIM))

        fn = jnp.concatenate([_fourier(proj, 0, n_ctx, ctx_len, w_f_b[l]),
                              _fourier(proj, n_ctx_tok, n_lat, lat_len, w_f_b[l])], axis=0)
        ag = attn_norm_g[l].reshape(1, DA_VDIM)
        a = jnp.concatenate([
            _attn_ctx(proj, n_ctx, ctx_len, lambdas[l], ag, lam_init),
            _attn_lat(proj, n_ctx_tok, n_lat, lat_len, cache_k4, cache_v4, l, cos, sin, lambdas[l], ag, lam_init),
        ], axis=0)
        hg_ctx, st_ctx = _hgrn(proj, 0, n_ctx, ctx_len, lbs[l], ng[l:l + 1], None)
        hg_lat, _ = _hgrn(proj, n_ctx_tok, n_lat, lat_len, lbs[l], ng[l:l + 1], s0t[:, l])
        new_s.append(_unblock_diag_t(st_ctx))
        hg = jnp.concatenate([hg_ctx, hg_lat], axis=0)

        x, h2, logits_t = _outproj(fn, a, hg, x, mod, norm_g[l, 1:2], w_out_b[l], w_router_t[l],
                                   n_ctx_tok, lat_len)
        gates = _route(logits_t, router_bias[l].reshape(N_EXPERTS, 1))
        x = _moe(h2, gates, wgu[l], wd[l], sgu[l], sd[l], x, mod, final_g.reshape(1, D_MODEL),
                 l == DEPTH - 1, n_ctx_tok, lat_len)

    y_prompt = x[:n_ctx_tok].reshape(n_ctx, ctx_len, D_MODEL)
    y_sample = x[n_ctx_tok:].reshape(n_lat, lat_len, D_MODEL)
    return (y_prompt, y_sample, jnp.stack(new_k, axis=1), jnp.stack(new_v, axis=1), jnp.stack(new_s, axis=1))
```

```python
import functools
import math

import numpy as np
import jax
import jax.numpy as jnp
from jax import lax
from jax.experimental import pallas as pl
from jax.experimental.pallas import tpu as pltpu

F32 = jnp.float32
BF16 = jnp.bfloat16

D_MODEL = 1024
DEPTH = 2
GRID_W = 64
FN_WIDTH = 256
FN_GROUPS = 4
FN_GROUP_DIM = 64
DA_WIDTH = 512
DA_HEADS = 4
DA_VDIM = 128
DA_HALF = 64
HG_WIDTH = 256
HG_HEADS = 4
HG_DK = 64
PROJ_WIDTH = 3072
CHUNK = 64
ROPE_THETA = 10000.0
N_EXPERTS = 64
TOP_K = 8
N_GROUPS = 8
TOPK_GROUPS = 4
D_EXPERT = 128
ROUTED_SCALE = 2.5
EPS = 1e-6

COL_Q, COL_K, COL_V = 0, 512, 1024
COL_HQ, COL_HI, COL_HF, COL_HB, COL_HG = 1536, 1792, 2048, 2304, 2560
COL_FN = 2816

TOKEN_TILE = 256
MOE_TILE = 1024
EXPERTS_PER_STEP = 8
ROUTE_TILE = 512
VMEM_LIMIT = 56 * 1024 * 1024


def _dot(a, b):
    return jnp.dot(a, b, preferred_element_type=F32)


def _dot_nt(a, b):
    return lax.dot_general(a, b, (((1,), (1,)), ((), ())), preferred_element_type=F32)


def _dot_tn(a, b):
    return lax.dot_general(a, b, (((0,), (0,)), ((), ())), preferred_element_type=F32)


def _split(x, n):
    parts = []
    r = x
    for i in range(n):
        p = r.astype(BF16)
        parts.append(p)
        if i + 1 < n:
            r = r - p.astype(F32)
    return parts


def _sigmoid(x):
    return 1.0 / (1.0 + jnp.exp(-x))


def _silu(x):
    return x * _sigmoid(x)


def _params(n_axes):
    return pltpu.CompilerParams(dimension_semantics=("arbitrary",) * n_axes,
                                vmem_limit_bytes=VMEM_LIMIT)


def _ada_body(c_ref, w_ref, b_ref, o_ref):
    a = _silu(c_ref[...])
    a_hi, a_lo = _split(a, 2)
    w_hi, w_lo = _split(w_ref[...], 2)
    o_ref[...] = _dot(a_hi, w_hi) + _dot(a_lo, w_hi) + _dot(a_hi, w_lo) + b_ref[...]


def _ada_mods(c8, w_ada, b_ada):
    tn = 1536
    return pl.pallas_call(
        _ada_body,
        grid=(DEPTH, 6 * D_MODEL // tn),
        in_specs=[
            pl.BlockSpec((8, D_MODEL), lambda l, j: (0, 0)),
            pl.BlockSpec((None, D_MODEL, tn), lambda l, j: (l, 0, j)),
            pl.BlockSpec((None, 1, tn), lambda l, j: (l, 0, j)),
        ],
        out_specs=pl.BlockSpec((None, 8, tn), lambda l, j: (l, 0, j)),
        out_shape=jax.ShapeDtypeStruct((DEPTH, 8, 6 * D_MODEL), F32),
        compiler_params=_params(2),
        name="ada_mods",
    )(c8, w_ada, b_ada.reshape(DEPTH, 1, 6 * D_MODEL))


def _modnorm(x, g, shift, scale):
    ms = jnp.mean(x * x, axis=-1, keepdims=True)
    return (x * lax.rsqrt(ms + EPS) * g) * (1.0 + scale) + shift


def _mod_row(n_ctx_tiles, tiles_per_latent):
    def f(i):
        return jnp.where(i < n_ctx_tiles, 0, 1 + (i - n_ctx_tiles) // tiles_per_latent)
    return f


def _row_specs(n_parts, tm, width, n_ctx_tiles):
    if n_parts == 1:
        return [pl.BlockSpec((tm, width), lambda i, *_: (i, 0))]
    return [pl.BlockSpec((tm, width), lambda i, *_: (jnp.minimum(i, n_ctx_tiles - 1), 0)),
            pl.BlockSpec((tm, width), lambda i, *_: (jnp.maximum(i - n_ctx_tiles, 0), 0))]


def _pick(is_ctx, refs):
    if len(refs) == 1:
        return refs[0][...]
    return jnp.where(is_ctx, refs[0][...], refs[1][...])


def _inproj_body(n_x, n_ctx_tiles, *refs):
    x_refs = refs[:n_x]
    mod_ref, g_ref, w_ref = refs[n_x:n_x + 3]
    o_ref, k_ref, v_ref = refs[-3:]
    is_ctx = pl.program_id(0) < n_ctx_tiles
    x = _pick(is_ctx, x_refs)
    h = _modnorm(x, g_ref[...], mod_ref[0:1, :], mod_ref[1:2, :])
    proj = _dot(h.astype(BF16), w_ref[...])
    o_ref[...] = proj

    @pl.when(is_ctx)
    def _():
        k_ref[...] = proj[:, COL_K:COL_K + DA_WIDTH]
        v_ref[...] = proj[:, COL_V:COL_V + DA_WIDTH]


def _inproj(xs, mod, g, w, layer, caches, n_ctx, ctx_len, lat_len):
    t = sum(x.shape[0] for x in xs)
    tm = TOKEN_TILE
    assert ctx_len == tm
    n_ctx_tiles = n_ctx
    row = _mod_row(n_ctx_tiles, lat_len // tm)
    cache_spec = pl.BlockSpec((None, ctx_len, DA_WIDTH), lambda i: (jnp.minimum(i, n_ctx - 1), layer, 0))
    cache_shape = jax.ShapeDtypeStruct((n_ctx, DEPTH * ctx_len, DA_WIDTH), F32)
    in_specs = _row_specs(len(xs), tm, D_MODEL, n_ctx_tiles) + [
        pl.BlockSpec((None, 6, D_MODEL), lambda i: (row(i), 0, 0)),
        pl.BlockSpec((1, D_MODEL), lambda i: (0, 0)),
        pl.BlockSpec((D_MODEL, PROJ_WIDTH), lambda i: (0, 0)),
    ]
    args = list(xs) + [mod, g, w]
    aliases = {}
    if caches is not None:
        aliases = {len(args): 1, len(args) + 1: 2}
        in_specs += [pl.BlockSpec(memory_space=pl.ANY), pl.BlockSpec(memory_space=pl.ANY)]
        args += list(caches)
    body = functools.partial(_inproj_body, len(xs), n_ctx_tiles)
    if caches is not None:
        body = functools.partial(_drop_refs, body, len(args) - 2, 2)
    return pl.pallas_call(
        body,
        grid=(t // tm,),
        in_specs=in_specs,
        out_specs=[pl.BlockSpec((tm, PROJ_WIDTH), lambda i: (i, 0)), cache_spec, cache_spec],
        out_shape=[jax.ShapeDtypeStruct((t, PROJ_WIDTH), F32), cache_shape, cache_shape],
        input_output_aliases=aliases,
        compiler_params=_params(1),
        name="inproj",
    )(*args)


def _drop_refs(body, start, count, *refs):
    return body(*refs[:start], *refs[start + count:])


def _fourier_body(u_ref, cl_ref, sl_ref, cc_ref, sc_ref, w_ref, o_ref):
    z = u_ref[...].astype(BF16)
    a = _dot(z, cc_ref[...]).astype(BF16)
    b = _dot(z, sc_ref[...]).astype(BF16)
    y = _dot(cl_ref[...], a) - _dot(sl_ref[...], b)
    o_ref[...] = _dot(y.astype(BF16), w_ref[...]).astype(o_ref.dtype)


def _dft_tables(n, block):
    i = np.arange(n)
    prod = (i[:, None] % block) * (i[None, :] % block) % block
    ang = prod.astype(np.float64) * (2.0 * math.pi / block)
    same = (i[:, None] // block) == (i[None, :] // block)
    scale = 1.0 / math.sqrt(block)
    c = np.where(same, np.cos(ang) * scale, 0.0).astype(np.float32)
    s = np.where(same, np.sin(ang) * scale, 0.0).astype(np.float32)
    return jnp.asarray(c).astype(BF16), jnp.asarray(s).astype(BF16)


def _fourier(proj, row0, n_seq, seq_len, w_f):
    cl, sl = _dft_tables(seq_len, seq_len)
    cc, sc = _dft_tables(FN_WIDTH, FN_GROUP_DIM)
    blk0 = row0 // seq_len
    full = lambda shape: pl.BlockSpec(shape, lambda i: (0, 0))
    return pl.pallas_call(
        _fourier_body,
        grid=(n_seq,),
        in_specs=[
            pl.BlockSpec((seq_len, FN_WIDTH), lambda i: (blk0 + i, COL_FN // FN_WIDTH)),
            full((seq_len, seq_len)), full((seq_len, seq_len)),
            full((FN_WIDTH, FN_WIDTH)), full((FN_WIDTH, FN_WIDTH)), full((FN_WIDTH, FN_WIDTH)),
        ],
        out_specs=pl.BlockSpec((seq_len, FN_WIDTH), lambda i: (i, 0)),
        out_shape=jax.ShapeDtypeStruct((n_seq * seq_len, FN_WIDTH), BF16),
        compiler_params=_params(1),
        name=f"fourier_{seq_len}",
    )(proj, cl, sl, cc, sc, w_f)


def _lambda_full(lmb, lam_init):
    a = jnp.sum(lmb[0:1, :] * lmb[1:2, :], axis=-1, keepdims=True)
    b = jnp.sum(lmb[2:3, :] * lmb[3:4, :], axis=-1, keepdims=True)
    return jnp.exp(a) - jnp.exp(b) + lam_init


def _softmax_parts(parts):
    m = parts[0].max(axis=-1, keepdims=True)
    for p in parts[1:]:
        m = jnp.maximum(m, p.max(axis=-1, keepdims=True))
    es = [jnp.exp(p - m) for p in parts]
    tot = es[0].sum(axis=-1, keepdims=True)
    for e in es[1:]:
        tot = tot + e.sum(axis=-1, keepdims=True)
    return es, 1.0 / tot


def _diff_head(q, ks, vs, lam, g, lam_init):
    outs = []
    for m in range(2):
        qm = q[:, m * DA_HALF:(m + 1) * DA_HALF].astype(BF16)
        parts = [_dot_nt(qm, k[:, m * DA_HALF:(m + 1) * DA_HALF].astype(BF16)) for k in ks]
        es, inv = _softmax_parts(parts)
        o = _dot(es[0].astype(BF16), vs[0].astype(BF16))
        for e, v in zip(es[1:], vs[1:]):
            o = o + _dot(e.astype(BF16), v.astype(BF16))
        outs.append(o * inv)
    a = outs[0] - lam * outs[1]
    ms = jnp.mean(a * a, axis=-1, keepdims=True)
    return a * lax.rsqrt(ms + EPS) * g * (1.0 - lam_init)


def _attn_ctx_body(lam_init, q_ref, k_ref, v_ref, lmb_ref, g_ref, o_ref):
    lam = _lambda_full(lmb_ref[...], lam_init)
    scale = DA_HALF ** -0.5
    for h in range(DA_HEADS):
        sl = slice(h * DA_VDIM, (h + 1) * DA_VDIM)
        o = _diff_head(q_ref[:, sl] * scale, [k_ref[:, sl]], [v_ref[:, sl]], lam, g_ref[...], lam_init)
        o_ref[:, sl] = o.astype(o_ref.dtype)


def _attn_ctx(proj, n_seq, seq_len, lmb, g, lam_init):
    blk = lambda c: pl.BlockSpec((seq_len, DA_WIDTH), lambda i: (i, c))
    return pl.pallas_call(
        functools.partial(_attn_ctx_body, lam_init),
        grid=(n_seq,),
        in_specs=[
            blk(COL_Q // DA_WIDTH), blk(COL_K // DA_WIDTH), blk(COL_V // DA_WIDTH),
            pl.BlockSpec((4, DA_HALF), lambda i: (0, 0)),
            pl.BlockSpec((1, DA_VDIM), lambda i: (0, 0)),
        ],
        out_specs=pl.BlockSpec((seq_len, DA_WIDTH), lambda i: (i, 0)),
        out_shape=jax.ShapeDtypeStruct((n_seq * seq_len, DA_WIDTH), BF16),
        compiler_params=_params(1),
        name="attn_ctx",
    )(proj, proj, proj, lmb, g)


def _rope(x, cos, sin):
    lane = lax.broadcasted_iota(jnp.int32, x.shape, 1)
    first = ((lane >> 4) & 1) == 0
    rot = jnp.where(first, -pltpu.roll(x, 128 - DA_HALF // 4, 1), pltpu.roll(x, DA_HALF // 4, 1))
    return x * cos + rot * sin


def _attn_lat_body(lam_init, q_ref, k_ref, v_ref, kc_ref, vc_ref, cq_ref, sq_ref, ck_ref, sk_ref,
                   lmb_ref, g_ref, o_ref):
    lam = _lambda_full(lmb_ref[...], lam_init)
    scale = DA_HALF ** -0.5
    for h in range(DA_HEADS):
        sl = slice(h * DA_VDIM, (h + 1) * DA_VDIM)
        q = _rope(q_ref[:, sl], cq_ref[...], sq_ref[...]) * scale
        k = _rope(k_ref[:, sl], ck_ref[...], sk_ref[...])
        o = _diff_head(q, [k, kc_ref[:, sl]], [v_ref[:, sl], vc_ref[:, sl]], lam, g_ref[...], lam_init)
        o_ref[:, sl] = o.astype(o_ref.dtype)


def _attn_lat(proj, row0, n_seq, seq_len, cache_k, cache_v, layer, cos, sin, lmb, g, lam_init):
    tq = 512
    nq = seq_len // tq
    past = cache_k.shape[2]
    qb0 = row0 // tq
    kb0 = row0 // seq_len
    cache_spec = pl.BlockSpec((None, None, past, DA_WIDTH), lambda b, j: (b, layer, 0, 0))
    kv_spec = lambda c: pl.BlockSpec((seq_len, DA_WIDTH), lambda b, j: (kb0 + b, c))
    return pl.pallas_call(
        functools.partial(_attn_lat_body, lam_init),
        grid=(n_seq, nq),
        in_specs=[
            pl.BlockSpec((tq, DA_WIDTH), lambda b, j: (qb0 + b * nq + j, COL_Q // DA_WIDTH)),
            kv_spec(COL_K // DA_WIDTH), kv_spec(COL_V // DA_WIDTH),
            cache_spec, cache_spec,
            pl.BlockSpec((tq, DA_VDIM), lambda b, j: (j, 0)),
            pl.BlockSpec((tq, DA_VDIM), lambda b, j: (j, 0)),
            pl.BlockSpec((seq_len, DA_VDIM), lambda b, j: (0, 0)),
            pl.BlockSpec((seq_len, DA_VDIM), lambda b, j: (0, 0)),
            pl.BlockSpec((4, DA_HALF), lambda b, j: (0, 0)),
            pl.BlockSpec((1, DA_VDIM), lambda b, j: (0, 0)),
        ],
        out_specs=pl.BlockSpec((tq, DA_WIDTH), lambda b, j: (b * nq + j, 0)),
        out_shape=jax.ShapeDtypeStruct((n_seq * seq_len, DA_WIDTH), BF16),
        compiler_params=_params(2),
        name="attn_lat",
    )(proj, proj, proj, cache_k, cache_v, cos, sin, cos, sin, lmb, g)


def _rope_tables(n_tokens):
    rows = n_tokens // GRID_W
    row = np.repeat(np.arange(rows, dtype=np.float64), GRID_W)
    col = np.tile(np.arange(GRID_W, dtype=np.float64), rows)
    axis_dim = DA_HALF // 2
    inv_freq = ROPE_THETA ** (-np.arange(0, axis_dim, 2, dtype=np.float64) / axis_dim)
    ang_r = row[:, None] * inv_freq[None, :]
    ang_c = col[:, None] * inv_freq[None, :]
    ang = np.concatenate([ang_r, ang_r, ang_c, ang_c] * 2, axis=-1)
    return jnp.asarray(np.cos(ang).astype(np.float32)), jnp.asarray(np.sin(ang).astype(np.float32))


def _hgrn_body(n_chunks, has_s0, *refs):
    if has_s0:
        (hq_ref, hi_ref, hf_ref, hb_ref, hg_ref, lb_ref, ng_ref, s0_ref,
         o_ref, st_ref, of_scr, ob_scr) = refs
    else:
        (hq_ref, hi_ref, hf_ref, hb_ref, hg_ref, lb_ref, ng_ref,
         o_ref, st_ref, of_scr, ob_scr) = refs
    c = CHUNK
    w = HG_WIDTH
    r_io = lax.broadcasted_iota(jnp.int32, (c, c), 0)
    c_io = lax.broadcasted_iota(jnp.int32, (c, c), 1)
    tri_f = (c_io <= r_io).astype(BF16)
    tri_b = (c_io >= r_io).astype(BF16)
    lane = lax.broadcasted_iota(jnp.int32, (1, w), 1)
    dk_bits = HG_DK.bit_length() - 1
    head_masks = [((lane >> dk_bits) == h).astype(F32) for h in range(HG_HEADS)]
    bd = ((lax.broadcasted_iota(jnp.int32, (w, w), 0) >> dk_bits)
          == (lax.broadcasted_iota(jnp.int32, (w, w), 1) >> dk_bits))
    t_of_row = lax.broadcasted_iota(jnp.int32, (HG_HEADS * c, c), 0) & (c - 1)
    s_of_col = lax.broadcasted_iota(jnp.int32, (HG_HEADS * c, c), 1)
    causal_f = s_of_col <= t_of_row
    causal_b = s_of_col >= t_of_row

    if has_s0:
        st_ref[0] = s0_ref[0]
        st_ref[1] = s0_ref[1]
    else:
        st_ref[...] = jnp.zeros_like(st_ref)

    def one_dir(d, start, fp_ref, tri, causal, edge_row, out_scr):
        rows = pl.ds(start, c)
        lb = lb_ref[d:d + 1, :]
        fp = fp_ref[rows, :]
        lsig = jnp.minimum(fp, 0.0) - jnp.log1p(jnp.exp(-jnp.abs(fp)))
        la = jnp.log(lb)
        lbb = jnp.log1p(-lb) + lsig
        logf = jnp.maximum(la, lbb) + jnp.log1p(jnp.exp(-jnp.abs(la - lbb)))
        kk = (1.0 - lb) * (1.0 / (1.0 + jnp.exp(fp)))
        q = _silu(hq_ref[rows, :]) * (HG_DK ** -0.5)
        v = hi_ref[rows, :].astype(BF16)
        g3 = _split(logf, 3)
        cum = _dot(tri, g3[0]) + _dot(tri, g3[1]) + _dot(tri, g3[2])
        total = cum[edge_row:edge_row + 1, :]
        ref = cum[c // 2:c // 2 + 1, :]
        qc = q * jnp.exp(cum - ref)
        kc = (kk * jnp.exp(ref - cum)).astype(BF16)
        qe = (q * jnp.exp(cum)).astype(BF16)
        ke = (kk * jnp.exp(total - cum)).astype(BF16)
        dec = jnp.exp(total)
        st = st_ref[d]
        o_inter = _dot_nt(qe, st.astype(BF16))
        lhs = jnp.concatenate([qc * hm for hm in head_masks], axis=0).astype(BF16)
        a = jnp.where(causal, _dot_nt(lhs, kc), 0.0).astype(BF16)
        o_stack = _dot(a, v)
        o = o_inter
        for h in range(HG_HEADS):
            o = o + o_stack[h * c:(h + 1) * c, :] * head_masks[h]
        out_scr[rows, :] = o
        st_ref[d] = st * dec + jnp.where(bd, _dot_tn(v, ke), 0.0)

    def chunk_step(i, carry):
        sf = pl.multiple_of(i * c, c)
        sb = pl.multiple_of((n_chunks - 1 - i) * c, c)
        one_dir(0, sf, hf_ref, tri_f, causal_f, c - 1, of_scr)
        one_dir(1, sb, hb_ref, tri_b, causal_b, 0, ob_scr)
        return carry

    lax.fori_loop(0, n_chunks, chunk_step, 0)

    ones_bd = bd.astype(BF16)

    def finish(i, carry):
        rows = pl.ds(pl.multiple_of(i * c, c), c)
        o = of_scr[rows, :] + ob_scr[rows, :]
        sq = _split(o * o, 2)
        ms = (_dot(sq[0], ones_bd) + _dot(sq[1], ones_bd)) * (1.0 / HG_DK)
        y = o * lax.rsqrt(ms + EPS) * ng_ref[...]
        o_ref[rows, :] = (y * _silu(hg_ref[rows, :])).astype(o_ref.dtype)
        return carry

    lax.fori_loop(0, n_chunks, finish, 0)


def _hgrn(proj, row0, n_seq, seq_len, lb, ng, s0t):
    n_chunks = seq_len // CHUNK
    blk0 = row0 // seq_len
    col = lambda c0: pl.BlockSpec((seq_len, HG_WIDTH), lambda i: (blk0 + i, c0 // HG_WIDTH))
    in_specs = [col(COL_HQ), col(COL_HI), col(COL_HF), col(COL_HB), col(COL_HG),
                pl.BlockSpec((2, HG_WIDTH), lambda i: (0, 0)),
                pl.BlockSpec((1, HG_WIDTH), lambda i: (0, 0))]
    args = [proj, proj, proj, proj, proj, lb, ng]
    if s0t is not None:
        in_specs.append(pl.BlockSpec((None, 2, HG_WIDTH, HG_WIDTH), lambda i: (i, 0, 0, 0)))
        args.append(s0t)
    return pl.pallas_call(
        functools.partial(_hgrn_body, n_chunks, s0t is not None),
        grid=(n_seq,),
        in_specs=in_specs,
        out_specs=[pl.BlockSpec((seq_len, HG_WIDTH), lambda i: (i, 0)),
                   pl.BlockSpec((None, 2, HG_WIDTH, HG_WIDTH), lambda i: (i, 0, 0, 0))],
        out_shape=[jax.ShapeDtypeStruct((n_seq * seq_len, HG_WIDTH), BF16),
                   jax.ShapeDtypeStruct((n_seq, 2, HG_WIDTH, HG_WIDTH), F32)],
        scratch_shapes=[pltpu.VMEM((seq_len, HG_WIDTH), F32), pltpu.VMEM((seq_len, HG_WIDTH), F32)],
        compiler_params=_params(1),
        name=f"hgrn_{seq_len}",
    )(*args)


def _outproj_body(n_x, n_ctx_tiles, *refs):
    fn_refs, a_refs, hg_refs = refs[0:2], refs[2:4], refs[4:6]
    x_refs = refs[6:6 + n_x]
    mod_ref, g_ref, w_ref, wr_ref, xo_ref, h_ref, lt_ref = refs[6 + n_x:]
    is_ctx = pl.program_id(0) < n_ctx_tiles
    mix = (_dot(_pick(is_ctx, fn_refs), w_ref[0:FN_WIDTH, :])
           + _dot(_pick(is_ctx, a_refs), w_ref[FN_WIDTH:FN_WIDTH + DA_WIDTH, :])
           + _dot(_pick(is_ctx, hg_refs), w_ref[FN_WIDTH + DA_WIDTH:, :]))
    x = _pick(is_ctx, x_refs) + mod_ref[2:3, :] * mix
    xo_ref[...] = x
    h = _modnorm(x, g_ref[...], mod_ref[3:4, :], mod_ref[4:5, :])
    h_ref[...] = h.astype(BF16)
    h_hi, h_lo = _split(h, 2)
    w_hi, w_lo = _split(wr_ref[...], 2)
    lt_ref[...] = _dot_nt(w_hi, h_hi) + _dot_nt(w_lo, h_hi) + _dot_nt(w_hi, h_lo)


def _outproj(fn, a, hg, xs, mod, g, w_out, w_router_t, n_ctx_tok, lat_len):
    t = sum(x.shape[0] for x in xs)
    tm = TOKEN_TILE
    n_ctx_tiles = n_ctx_tok // tm
    row = _mod_row(n_ctx_tiles, lat_len // tm)
    rows = lambda width: pl.BlockSpec((tm, width), lambda i: (i, 0))
    parts = lambda n, width: _row_specs(n, tm, width, n_ctx_tiles)
    return pl.pallas_call(
        functools.partial(_outproj_body, len(xs), n_ctx_tiles),
        grid=(t // tm,),
        in_specs=parts(2, FN_WIDTH) + parts(2, DA_WIDTH) + parts(2, HG_WIDTH) + parts(len(xs), D_MODEL) + [
            pl.BlockSpec((None, 6, D_MODEL), lambda i: (row(i), 0, 0)),
            pl.BlockSpec((1, D_MODEL), lambda i: (0, 0)),
            pl.BlockSpec((D_MODEL, D_MODEL), lambda i: (0, 0)),
            pl.BlockSpec((N_EXPERTS, D_MODEL), lambda i: (0, 0)),
        ],
        out_specs=[rows(D_MODEL), rows(D_MODEL), pl.BlockSpec((N_EXPERTS, tm), lambda i: (0, i))],
        out_shape=[jax.ShapeDtypeStruct((t, D_MODEL), F32),
                   jax.ShapeDtypeStruct((t, D_MODEL), BF16),
                   jax.ShapeDtypeStruct((N_EXPERTS, t), F32)],
        compiler_params=_params(1),
        name="outproj",
    )(*fn, *a, *hg, *xs, mod, g, w_out, w_router_t)


def _route_body(lt_ref, bias_ref, o_ref):
    per = N_EXPERTS // N_GROUPS
    tt = lt_ref.shape[1]
    neg = -jnp.inf
    io = lax.broadcasted_iota(jnp.int32, (per, tt), 0).astype(F32)
    s_g, b_g = [], []
    for g in range(N_GROUPS):
        s = _sigmoid(lt_ref[g * per:(g + 1) * per, :])
        s_g.append(s)
        b_g.append(s + bias_ref[g * per:(g + 1) * per, :])
    gs = []
    for g in range(N_GROUPS):
        x = b_g[g]
        m1 = x.max(axis=0, keepdims=True)
        i1 = jnp.where(x == m1, io, float(per)).min(axis=0, keepdims=True)
        m2 = jnp.where(io == i1, neg, x).max(axis=0, keepdims=True)
        gs.append(m1 + m2)
    gsel = [jnp.zeros((1, tt), jnp.bool_) for _ in range(N_GROUPS)]
    for _ in range(TOPK_GROUPS):
        m = functools.reduce(jnp.maximum, gs)
        found = jnp.zeros((1, tt), jnp.bool_)
        for g in range(N_GROUPS):
            hit = (gs[g] == m) & jnp.logical_not(found)
            gsel[g] = gsel[g] | hit
            found = found | hit
            gs[g] = jnp.where(hit, neg, gs[g])
    x_g = [jnp.where(gsel[g], b_g[g], neg) for g in range(N_GROUPS)]
    e_g = [io + g * per for g in range(N_GROUPS)]
    sel_g = [jnp.zeros((per, tt), jnp.bool_) for _ in range(N_GROUPS)]
    for _ in range(TOP_K):
        m = functools.reduce(jnp.maximum, [x.max(axis=0, keepdims=True) for x in x_g])
        idx = functools.reduce(jnp.minimum, [jnp.where(x_g[g] == m, e_g[g], float(N_EXPERTS)).min(axis=0, keepdims=True)
                                             for g in range(N_GROUPS)])
        for g in range(N_GROUPS):
            hit = e_g[g] == idx
            sel_g[g] = sel_g[g] | hit
            x_g[g] = jnp.where(hit, neg, x_g[g])
    w_g = [jnp.where(sel_g[g], s_g[g], 0.0) for g in range(N_GROUPS)]
    denom = functools.reduce(lambda a, b: a + b, [w.sum(axis=0, keepdims=True) for w in w_g])
    gates_t = jnp.concatenate([w / denom * ROUTED_SCALE for w in w_g], axis=0)
    eye = (lax.broadcasted_iota(jnp.int32, (N_EXPERTS, N_EXPERTS), 0)
           == lax.broadcasted_iota(jnp.int32, (N_EXPERTS, N_EXPERTS), 1)).astype(BF16)
    p = _split(gates_t, 3)
    gates = _dot_tn(p[0], eye) + _dot_tn(p[1], eye) + _dot_tn(p[2], eye)
    for c in range(N_EXPERTS // EXPERTS_PER_STEP):
        o_ref[c] = gates[:, c * EXPERTS_PER_STEP:(c + 1) * EXPERTS_PER_STEP]


def _route(logits_t, bias):
    t = logits_t.shape[1]
    tt = ROUTE_TILE
    nch = N_EXPERTS // EXPERTS_PER_STEP
    return pl.pallas_call(
        _route_body,
        grid=(t // tt,),
        in_specs=[pl.BlockSpec((N_EXPERTS, tt), lambda i: (0, i)),
                  pl.BlockSpec((N_EXPERTS, 1), lambda i: (0, 0))],
        out_specs=pl.BlockSpec((nch, tt, EXPERTS_PER_STEP), lambda i: (0, i, 0)),
        out_shape=jax.ShapeDtypeStruct((nch, t, EXPERTS_PER_STEP), F32),
        compiler_params=_params(1),
        name="route",
    )(logits_t, bias)


def _moe_body(final_norm, n_ctx_tiles, h_ref, gate_ref, wgu_ref, wd_ref, sgu_ref, sd_ref, x_ref, mod_ref,
              fg_ref, *out_and_scratch):
    acc_ref = out_and_scratch[-1]
    o_refs = out_and_scratch[:-1]
    is_ctx = pl.program_id(0) < n_ctx_tiles
    j = pl.program_id(1)
    h = h_ref[...]

    def act_of(w, gate):
        gu = _dot(h, w)
        a = _silu(gu[:, :D_EXPERT]) * gu[:, D_EXPERT:]
        return a if gate is None else a * gate

    @pl.when(j == 0)
    def _():
        acc_ref[...] = _dot(act_of(sgu_ref[...], None).astype(BF16), sd_ref[...])

    for p in range(EXPERTS_PER_STEP // 2):
        a0 = act_of(wgu_ref[2 * p], gate_ref[:, 2 * p:2 * p + 1])
        a1 = act_of(wgu_ref[2 * p + 1], gate_ref[:, 2 * p + 1:2 * p + 2])
        pair = jnp.concatenate([a0, a1], axis=1).astype(BF16)
        acc_ref[...] += _dot(pair, wd_ref[p])

    @pl.when(j == pl.num_programs(1) - 1)
    def _():
        x = x_ref[...] + mod_ref[5:6, :] * acc_ref[...]
        if not final_norm:
            o_refs[0][...] = x
        else:
            ms = jnp.mean(x * x, axis=-1, keepdims=True)
            y = x * lax.rsqrt(ms + EPS) * fg_ref[...]

            @pl.when(is_ctx)
            def _():
                o_refs[0][...] = y

            @pl.when(jnp.logical_not(is_ctx))
            def _():
                o_refs[1][...] = y


def _moe(h, gates, wgu, wd, sgu, sd, x, mod, final_g, final_norm, n_ctx_tok, lat_len):
    t = x.shape[0]
    tm = MOE_TILE
    eps_ = EXPERTS_PER_STEP
    n_ctx_tiles = n_ctx_tok // tm
    row = _mod_row(n_ctx_tiles, lat_len // tm)
    if final_norm:
        out_specs = _row_specs(2, tm, D_MODEL, n_ctx_tiles)
        out_shape = [jax.ShapeDtypeStruct((n_ctx_tok, D_MODEL), F32),
                     jax.ShapeDtypeStruct((t - n_ctx_tok, D_MODEL), F32)]
    else:
        out_specs = [pl.BlockSpec((tm, D_MODEL), lambda i, j: (i, 0))]
        out_shape = [jax.ShapeDtypeStruct((t, D_MODEL), F32)]
    return pl.pallas_call(
        functools.partial(_moe_body, final_norm, n_ctx_tiles),
        grid=(t // tm, N_EXPERTS // eps_),
        in_specs=[
            pl.BlockSpec((tm, D_MODEL), lambda i, j: (i, 0)),
            pl.BlockSpec((None, tm, eps_), lambda i, j: (j, i, 0)),
            pl.BlockSpec((eps_, D_MODEL, 2 * D_EXPERT), lambda i, j: (j, 0, 0)),
            pl.BlockSpec((eps_ // 2, 2 * D_EXPERT, D_MODEL), lambda i, j: (j, 0, 0)),
            pl.BlockSpec((D_MODEL, 2 * D_EXPERT), lambda i, j: (0, 0)),
            pl.BlockSpec((D_EXPERT, D_MODEL), lambda i, j: (0, 0)),
            pl.BlockSpec((tm, D_MODEL), lambda i, j: (i, 0)),
            pl.BlockSpec((None, 6, D_MODEL), lambda i, j: (row(i), 0, 0)),
            pl.BlockSpec((1, D_MODEL), lambda i, j: (0, 0)),
        ],
        out_specs=out_specs,
        out_shape=out_shape,
        scratch_shapes=[pltpu.VMEM((tm, D_MODEL), F32)],
        compiler_params=_params(2),
        name="moe",
    )(h, gates, wgu, wd, sgu, sd, x, mod, final_g)


def _block_diag_t(s):
    eye = jnp.eye(HG_HEADS, dtype=s.dtype)
    out = jnp.einsum('...hkv,hg->...hvgk', s, eye)
    return out.reshape(s.shape[:-3] + (HG_WIDTH, HG_WIDTH))


def _unblock_diag_t(st):
    lead = st.shape[:-2]
    s5 = st.reshape(lead + (HG_HEADS, HG_DK, HG_HEADS, HG_DK))
    diag = jnp.stack([s5[..., h, :, h, :] for h in range(HG_HEADS)], axis=-3)
    return jnp.swapaxes(diag, -1, -2)


def kernel(x_prompt, x_sample, cache_k, cache_v, state_hgrn, c, c_ctx, w_ada, b_ada, norm_g, w_in,
           w_fourier, lambdas, attn_norm_g, lower_bounds, hg_norm_g, w_out, w_router, router_bias,
           w_gate, w_up, w_down, ws_gate, ws_up, ws_down, final_g):
    n_ctx, ctx_len, _ = x_prompt.shape
    n_lat, lat_len, _ = x_sample.shape
    n_ctx_tok = n_ctx * ctx_len
    past = cache_k.shape[2]

    xs = (x_prompt.reshape(n_ctx_tok, D_MODEL), x_sample.reshape(n_lat * lat_len, D_MODEL))

    c8 = jnp.zeros((8, D_MODEL), F32).at[0].set(c_ctx).at[1:1 + n_lat].set(c)
    mods = _ada_mods(c8, w_ada, b_ada).reshape(DEPTH, 8, 6, D_MODEL)

    cs = jnp.cumsum(jax.nn.softmax(lower_bounds.astype(F32), axis=0), axis=0)
    lbs = cs - cs[0:1]

    cos, sin = _rope_tables(lat_len)
    cache_k4 = cache_k.reshape(n_lat, DEPTH, past, DA_WIDTH)
    cache_v4 = cache_v.reshape(n_lat, DEPTH, past, DA_WIDTH)
    s0t = _block_diag_t(state_hgrn.astype(F32))

    w_in_b = jnp.concatenate([w_in[:, :, FN_WIDTH:], w_in[:, :, :FN_WIDTH]], axis=-1).astype(BF16)
    w_f_b = w_fourier.astype(BF16)
    w_out_b = w_out.astype(BF16)
    w_router_t = jnp.swapaxes(w_router, 1, 2)
    wgu = jnp.concatenate([w_gate, w_up], axis=-1).astype(BF16)
    wd = w_down.astype(BF16).reshape(DEPTH, N_EXPERTS // 2, 2 * D_EXPERT, D_MODEL)
    sgu = jnp.concatenate([ws_gate, ws_up], axis=-1).astype(BF16)
    sd = ws_down.astype(BF16)
    ng = jnp.tile(hg_norm_g, (1, HG_HEADS))

    caches = None
    new_s = []
    for l in range(DEPTH):
        lam_init = 0.8 - 0.6 * math.exp(-0.3 * l)
        mod = mods[l]
        proj, *caches = _inproj(xs, mod, norm_g[l, 0:1], w_in_b[l], l, caches, n_ctx, ctx_len, lat_len)

        fn = (_fourier(proj, 0, n_ctx, ctx_len, w_f_b[l]),
              _fourier(proj, n_ctx_tok, n_lat, lat_len, w_f_b[l]))
        ag = attn_norm_g[l].reshape(1, DA_VDIM)
        a = (_attn_ctx(proj, n_ctx, ctx_len, lambdas[l], ag, lam_init),
             _attn_lat(proj, n_ctx_tok, n_lat, lat_len, cache_k4, cache_v4, l, cos, sin, lambdas[l], ag, lam_init))
        hg_ctx, st_ctx = _hgrn(proj, 0, n_ctx, ctx_len, lbs[l], ng[l:l + 1], None)
        hg_lat, _ = _hgrn(proj, n_ctx_tok, n_lat, lat_len, lbs[l], ng[l:l + 1], s0t[:, l])
        new_s.append(_unblock_diag_t(st_ctx))

        x, h2, logits_t = _outproj(fn, a, (hg_ctx, hg_lat), xs, mod, norm_g[l, 1:2], w_out_b[l],
                                   w_router_t[l], n_ctx_tok, lat_len)
        gates = _route(logits_t, router_bias[l].reshape(N_EXPERTS, 1))
        xs = _moe(h2, gates, wgu[l], wd[l], sgu[l], sd[l], x, mod, final_g.reshape(1, D_MODEL),
                  l == DEPTH - 1, n_ctx_tok, lat_len)

    y_prompt = xs[0].reshape(n_ctx, ctx_len, D_MODEL)
    y_sample = xs[1].reshape(n_lat, lat_len, D_MODEL)
    new_k = caches[0].reshape(n_ctx, DEPTH, ctx_len, DA_HEADS, DA_VDIM)
    new_v = caches[1].reshape(n_ctx, DEPTH, ctx_len, DA_HEADS, DA_VDIM)
    return (y_prompt, y_sample, new_k, new_v, jnp.stack(new_s, axis=1))
```

```python
import functools
import math

import numpy as np
import jax
import jax.numpy as jnp
from jax import lax
from jax.experimental import pallas as pl
from jax.experimental.pallas import tpu as pltpu

F32 = jnp.float32
BF16 = jnp.bfloat16

D_MODEL = 1024
DEPTH = 2
GRID_W = 64
FN_WIDTH = 256
FN_GROUPS = 4
FN_GROUP_DIM = 64
DA_WIDTH = 512
DA_HEADS = 4
DA_VDIM = 128
DA_HALF = 64
HG_WIDTH = 256
HG_HEADS = 4
HG_DK = 64
PROJ_WIDTH = 3072
CHUNK = 64
ROPE_THETA = 10000.0
N_EXPERTS = 64
TOP_K = 8
N_GROUPS = 8
TOPK_GROUPS = 4
D_EXPERT = 128
ROUTED_SCALE = 2.5
EPS = 1e-6

COL_Q, COL_K, COL_V = 0, 512, 1024
COL_HQ, COL_HI, COL_HF, COL_HB, COL_HG = 1536, 1792, 2048, 2304, 2560
COL_FN = 2816

TOKEN_TILE = 256
MOE_TILE = 1024
EXPERTS_PER_STEP = 4
ROUTE_TILE = 512
VMEM_LIMIT = 56 * 1024 * 1024


def _dot(a, b):
    return jnp.dot(a, b, preferred_element_type=F32)


def _dot_nt(a, b):
    return lax.dot_general(a, b, (((1,), (1,)), ((), ())), preferred_element_type=F32)


def _dot_tn(a, b):
    return lax.dot_general(a, b, (((0,), (0,)), ((), ())), preferred_element_type=F32)


def _split(x, n):
    parts = []
    r = x
    for i in range(n):
        p = r.astype(BF16)
        parts.append(p)
        if i + 1 < n:
            r = r - p.astype(F32)
    return parts


def _sigmoid(x):
    return 1.0 / (1.0 + jnp.exp(-x))


def _silu(x):
    return x * _sigmoid(x)


def _params(n_axes):
    return pltpu.CompilerParams(dimension_semantics=("arbitrary",) * n_axes,
                                vmem_limit_bytes=VMEM_LIMIT)


def _ada_body(c_ref, w_ref, b_ref, o_ref):
    a = _silu(c_ref[...])
    a_hi, a_lo = _split(a, 2)
    w_hi, w_lo = _split(w_ref[...], 2)
    o_ref[...] = _dot(a_hi, w_hi) + _dot(a_lo, w_hi) + _dot(a_hi, w_lo) + b_ref[...]


def _ada_mods(c8, w_ada, b_ada):
    tn = 1536
    return pl.pallas_call(
        _ada_body,
        grid=(DEPTH, 6 * D_MODEL // tn),
        in_specs=[
            pl.BlockSpec((8, D_MODEL), lambda l, j: (0, 0)),
            pl.BlockSpec((None, D_MODEL, tn), lambda l, j: (l, 0, j)),
            pl.BlockSpec((None, 1, tn), lambda l, j: (l, 0, j)),
        ],
        out_specs=pl.BlockSpec((None, 8, tn), lambda l, j: (l, 0, j)),
        out_shape=jax.ShapeDtypeStruct((DEPTH, 8, 6 * D_MODEL), F32),
        compiler_params=_params(2),
        name="ada_mods",
    )(c8, w_ada, b_ada.reshape(DEPTH, 1, 6 * D_MODEL))


def _modnorm(x, g, shift, scale):
    ms = jnp.mean(x * x, axis=-1, keepdims=True)
    return (x * lax.rsqrt(ms + EPS) * g) * (1.0 + scale) + shift


def _mod_row(n_ctx_tiles, tiles_per_latent):
    def f(i):
        return jnp.where(i < n_ctx_tiles, 0, 1 + (i - n_ctx_tiles) // tiles_per_latent)
    return f


def _row_specs(n_parts, tm, width, n_ctx_tiles):
    if n_parts == 1:
        return [pl.BlockSpec((tm, width), lambda i, *_: (i, 0))]
    return [pl.BlockSpec((tm, width), lambda i, *_: (jnp.minimum(i, n_ctx_tiles - 1), 0)),
            pl.BlockSpec((tm, width), lambda i, *_: (jnp.maximum(i - n_ctx_tiles, 0), 0))]


def _pick(is_ctx, refs):
    if len(refs) == 1:
        return refs[0][...]
    return jnp.where(is_ctx, refs[0][...], refs[1][...])


def _inproj_body(n_x, n_ctx_tiles, *refs):
    x_refs = refs[:n_x]
    mod_ref, g_ref, w_ref = refs[n_x:n_x + 3]
    o_ref, k_ref, v_ref = refs[-3:]
    is_ctx = pl.program_id(0) < n_ctx_tiles
    x = _pick(is_ctx, x_refs)
    h = _modnorm(x, g_ref[...], mod_ref[0:1, :], mod_ref[1:2, :])
    proj = _dot(h.astype(BF16), w_ref[...])
    o_ref[...] = proj

    @pl.when(is_ctx)
    def _():
        n_tok = proj.shape[0]
        for h in range(DA_HEADS):
            rows = pl.ds(h, n_tok, stride=DA_HEADS)
            k_ref[rows, :] = proj[:, COL_K + h * DA_VDIM:COL_K + (h + 1) * DA_VDIM]
            v_ref[rows, :] = proj[:, COL_V + h * DA_VDIM:COL_V + (h + 1) * DA_VDIM]


def _inproj(xs, mod, g, w, layer, caches, n_ctx, ctx_len, lat_len):
    t = sum(x.shape[0] for x in xs)
    tm = TOKEN_TILE
    assert ctx_len == tm
    n_ctx_tiles = n_ctx
    row = _mod_row(n_ctx_tiles, lat_len // tm)
    cache_spec = pl.BlockSpec((ctx_len * DA_HEADS, DA_VDIM),
                              lambda i: (jnp.minimum(i, n_ctx - 1) * DEPTH + layer, 0))
    cache_shape = jax.ShapeDtypeStruct((n_ctx * DEPTH * ctx_len * DA_HEADS, DA_VDIM), F32)
    in_specs = _row_specs(len(xs), tm, D_MODEL, n_ctx_tiles) + [
        pl.BlockSpec((None, 6, D_MODEL), lambda i: (row(i), 0, 0)),
        pl.BlockSpec((1, D_MODEL), lambda i: (0, 0)),
        pl.BlockSpec((D_MODEL, PROJ_WIDTH), lambda i: (0, 0)),
    ]
    args = list(xs) + [mod, g, w]
    aliases = {}
    if caches is not None:
        aliases = {len(args): 1, len(args) + 1: 2}
        in_specs += [pl.BlockSpec(memory_space=pl.ANY), pl.BlockSpec(memory_space=pl.ANY)]
        args += list(caches)
    body = functools.partial(_inproj_body, len(xs), n_ctx_tiles)
    if caches is not None:
        body = functools.partial(_drop_refs, body, len(args) - 2, 2)
    return pl.pallas_call(
        body,
        grid=(t // tm,),
        in_specs=in_specs,
        out_specs=[pl.BlockSpec((tm, PROJ_WIDTH), lambda i: (i, 0)), cache_spec, cache_spec],
        out_shape=[jax.ShapeDtypeStruct((t, PROJ_WIDTH), F32), cache_shape, cache_shape],
        input_output_aliases=aliases,
        compiler_params=_params(1),
        name="inproj",
    )(*args)


def _drop_refs(body, start, count, *refs):
    return body(*refs[:start], *refs[start + count:])


def _fourier_body(u_ref, cl_ref, sl_ref, cc_ref, sc_ref, w_ref, o_ref):
    z = u_ref[...].astype(BF16)
    a = _dot(z, cc_ref[...]).astype(BF16)
    b = _dot(z, sc_ref[...]).astype(BF16)
    y = _dot(cl_ref[...], a) - _dot(sl_ref[...], b)
    o_ref[...] = _dot(y.astype(BF16), w_ref[...]).astype(o_ref.dtype)


def _dft_tables(n, block):
    i = np.arange(n)
    prod = (i[:, None] % block) * (i[None, :] % block) % block
    ang = prod.astype(np.float64) * (2.0 * math.pi / block)
    same = (i[:, None] // block) == (i[None, :] // block)
    scale = 1.0 / math.sqrt(block)
    c = np.where(same, np.cos(ang) * scale, 0.0).astype(np.float32)
    s = np.where(same, np.sin(ang) * scale, 0.0).astype(np.float32)
    return jnp.asarray(c).astype(BF16), jnp.asarray(s).astype(BF16)


def _fourier(proj, row0, n_seq, seq_len, w_f):
    cl, sl = _dft_tables(seq_len, seq_len)
    cc, sc = _dft_tables(FN_WIDTH, FN_GROUP_DIM)
    blk0 = row0 // seq_len
    full = lambda shape: pl.BlockSpec(shape, lambda i: (0, 0))
    return pl.pallas_call(
        _fourier_body,
        grid=(n_seq,),
        in_specs=[
            pl.BlockSpec((seq_len, FN_WIDTH), lambda i: (blk0 + i, COL_FN // FN_WIDTH)),
            full((seq_len, seq_len)), full((seq_len, seq_len)),
            full((FN_WIDTH, FN_WIDTH)), full((FN_WIDTH, FN_WIDTH)), full((FN_WIDTH, FN_WIDTH)),
        ],
        out_specs=pl.BlockSpec((seq_len, FN_WIDTH), lambda i: (i, 0)),
        out_shape=jax.ShapeDtypeStruct((n_seq * seq_len, FN_WIDTH), BF16),
        compiler_params=_params(1),
        name=f"fourier_{seq_len}",
    )(proj, cl, sl, cc, sc, w_f)


def _lambda_full(lmb, lam_init):
    a = jnp.sum(lmb[0:1, :] * lmb[1:2, :], axis=-1, keepdims=True)
    b = jnp.sum(lmb[2:3, :] * lmb[3:4, :], axis=-1, keepdims=True)
    return jnp.exp(a) - jnp.exp(b) + lam_init


def _softmax_parts(parts):
    m = parts[0].max(axis=-1, keepdims=True)
    for p in parts[1:]:
        m = jnp.maximum(m, p.max(axis=-1, keepdims=True))
    es = [jnp.exp(p - m) for p in parts]
    tot = es[0].sum(axis=-1, keepdims=True)
    for e in es[1:]:
        tot = tot + e.sum(axis=-1, keepdims=True)
    return es, 1.0 / tot


def _diff_head(q, ks, vs, lam, g, lam_init):
    outs = []
    for m in range(2):
        qm = q[:, m * DA_HALF:(m + 1) * DA_HALF].astype(BF16)
        parts = [_dot_nt(qm, k[:, m * DA_HALF:(m + 1) * DA_HALF].astype(BF16)) for k in ks]
        es, inv = _softmax_parts(parts)
        o = _dot(es[0].astype(BF16), vs[0].astype(BF16))
        for e, v in zip(es[1:], vs[1:]):
            o = o + _dot(e.astype(BF16), v.astype(BF16))
        outs.append(o * inv)
    a = outs[0] - lam * outs[1]
    ms = jnp.mean(a * a, axis=-1, keepdims=True)
    return a * lax.rsqrt(ms + EPS) * g * (1.0 - lam_init)


def _attn_ctx_body(lam_init, q_ref, k_ref, v_ref, lmb_ref, g_ref, o_ref):
    lam = _lambda_full(lmb_ref[...], lam_init)
    scale = DA_HALF ** -0.5
    for h in range(DA_HEADS):
        sl = slice(h * DA_VDIM, (h + 1) * DA_VDIM)
        o = _diff_head(q_ref[:, sl] * scale, [k_ref[:, sl]], [v_ref[:, sl]], lam, g_ref[...], lam_init)
        o_ref[:, sl] = o.astype(o_ref.dtype)


def _attn_ctx(proj, n_seq, seq_len, lmb, g, lam_init):
    blk = lambda c: pl.BlockSpec((seq_len, DA_WIDTH), lambda i: (i, c))
    return pl.pallas_call(
        functools.partial(_attn_ctx_body, lam_init),
        grid=(n_seq,),
        in_specs=[
            blk(COL_Q // DA_WIDTH), blk(COL_K // DA_WIDTH), blk(COL_V // DA_WIDTH),
            pl.BlockSpec((4, DA_HALF), lambda i: (0, 0)),
            pl.BlockSpec((1, DA_VDIM), lambda i: (0, 0)),
        ],
        out_specs=pl.BlockSpec((seq_len, DA_WIDTH), lambda i: (i, 0)),
        out_shape=jax.ShapeDtypeStruct((n_seq * seq_len, DA_WIDTH), BF16),
        compiler_params=_params(1),
        name="attn_ctx",
    )(proj, proj, proj, lmb, g)


def _rope(x, cos, sin):
    lane = lax.broadcasted_iota(jnp.int32, x.shape, 1)
    first = ((lane >> 4) & 1) == 0
    rot = jnp.where(first, -pltpu.roll(x, 128 - DA_HALF // 4, 1), pltpu.roll(x, DA_HALF // 4, 1))
    return x * cos + rot * sin


def _attn_lat_body(lam_init, q_ref, k_ref, v_ref, kc_ref, vc_ref, cq_ref, sq_ref, ck_ref, sk_ref,
                   lmb_ref, g_ref, o_ref):
    lam = _lambda_full(lmb_ref[...], lam_init)
    scale = DA_HALF ** -0.5
    for h in range(DA_HEADS):
        sl = slice(h * DA_VDIM, (h + 1) * DA_VDIM)
        q = _rope(q_ref[:, sl], cq_ref[...], sq_ref[...]) * scale
        k = _rope(k_ref[:, sl], ck_ref[...], sk_ref[...])
        o = _diff_head(q, [k, kc_ref[:, sl]], [v_ref[:, sl], vc_ref[:, sl]], lam, g_ref[...], lam_init)
        o_ref[:, sl] = o.astype(o_ref.dtype)


def _attn_lat(proj, row0, n_seq, seq_len, cache_k, cache_v, layer, cos, sin, lmb, g, lam_init):
    tq = 512
    nq = seq_len // tq
    past = cache_k.shape[2]
    qb0 = row0 // tq
    kb0 = row0 // seq_len
    cache_spec = pl.BlockSpec((None, None, past, DA_WIDTH), lambda b, j: (b, layer, 0, 0))
    kv_spec = lambda c: pl.BlockSpec((seq_len, DA_WIDTH), lambda b, j: (kb0 + b, c))
    return pl.pallas_call(
        functools.partial(_attn_lat_body, lam_init),
        grid=(n_seq, nq),
        in_specs=[
            pl.BlockSpec((tq, DA_WIDTH), lambda b, j: (qb0 + b * nq + j, COL_Q // DA_WIDTH)),
            kv_spec(COL_K // DA_WIDTH), kv_spec(COL_V // DA_WIDTH),
            cache_spec, cache_spec,
            pl.BlockSpec((tq, DA_VDIM), lambda b, j: (j, 0)),
            pl.BlockSpec((tq, DA_VDIM), lambda b, j: (j, 0)),
            pl.BlockSpec((seq_len, DA_VDIM), lambda b, j: (0, 0)),
            pl.BlockSpec((seq_len, DA_VDIM), lambda b, j: (0, 0)),
            pl.BlockSpec((4, DA_HALF), lambda b, j: (0, 0)),
            pl.BlockSpec((1, DA_VDIM), lambda b, j: (0, 0)),
        ],
        out_specs=pl.BlockSpec((tq, DA_WIDTH), lambda b, j: (b * nq + j, 0)),
        out_shape=jax.ShapeDtypeStruct((n_seq * seq_len, DA_WIDTH), BF16),
        compiler_params=_params(2),
        name="attn_lat",
    )(proj, proj, proj, cache_k, cache_v, cos, sin, cos, sin, lmb, g)


def _rope_tables(n_tokens):
    rows = n_tokens // GRID_W
    row = np.repeat(np.arange(rows, dtype=np.float64), GRID_W)
    col = np.tile(np.arange(GRID_W, dtype=np.float64), rows)
    axis_dim = DA_HALF // 2
    inv_freq = ROPE_THETA ** (-np.arange(0, axis_dim, 2, dtype=np.float64) / axis_dim)
    ang_r = row[:, None] * inv_freq[None, :]
    ang_c = col[:, None] * inv_freq[None, :]
    ang = np.concatenate([ang_r, ang_r, ang_c, ang_c] * 2, axis=-1)
    return jnp.asarray(np.cos(ang).astype(np.float32)), jnp.asarray(np.sin(ang).astype(np.float32))


def _hgrn_body(n_chunks, has_s0, *refs):
    if has_s0:
        (hq_ref, hi_ref, hf_ref, hb_ref, hg_ref, lb_ref, ng_ref, s0_ref,
         o_ref, of_scr, ob_scr, st_ref) = refs
        so_ref = None
    else:
        (hq_ref, hi_ref, hf_ref, hb_ref, hg_ref, lb_ref, ng_ref,
         o_ref, so_ref, of_scr, ob_scr, st_ref) = refs
    c = CHUNK
    w = HG_WIDTH
    r_io = lax.broadcasted_iota(jnp.int32, (c, c), 0)
    c_io = lax.broadcasted_iota(jnp.int32, (c, c), 1)
    tri_f = (c_io <= r_io).astype(BF16)
    tri_b = (c_io >= r_io).astype(BF16)
    lane = lax.broadcasted_iota(jnp.int32, (1, w), 1)
    dk_bits = HG_DK.bit_length() - 1
    head_masks = [((lane >> dk_bits) == h).astype(F32) for h in range(HG_HEADS)]
    bd = ((lax.broadcasted_iota(jnp.int32, (w, w), 0) >> dk_bits)
          == (lax.broadcasted_iota(jnp.int32, (w, w), 1) >> dk_bits))
    t_of_row = lax.broadcasted_iota(jnp.int32, (HG_HEADS * c, c), 0) & (c - 1)
    s_of_col = lax.broadcasted_iota(jnp.int32, (HG_HEADS * c, c), 1)
    causal_f = s_of_col <= t_of_row
    causal_b = s_of_col >= t_of_row

    if has_s0:
        st_ref[0] = s0_ref[0]
        st_ref[1] = s0_ref[1]
    else:
        st_ref[...] = jnp.zeros_like(st_ref)

    def one_dir(d, start, fp_ref, tri, causal, edge_row, out_scr):
        rows = pl.ds(start, c)
        lb = lb_ref[d:d + 1, :]
        fp = fp_ref[rows, :]
        lsig = jnp.minimum(fp, 0.0) - jnp.log1p(jnp.exp(-jnp.abs(fp)))
        la = jnp.log(lb)
        lbb = jnp.log1p(-lb) + lsig
        logf = jnp.maximum(la, lbb) + jnp.log1p(jnp.exp(-jnp.abs(la - lbb)))
        kk = (1.0 - lb) * (1.0 / (1.0 + jnp.exp(fp)))
        q = _silu(hq_ref[rows, :]) * (HG_DK ** -0.5)
        v = hi_ref[rows, :].astype(BF16)
        g3 = _split(logf, 3)
        cum = _dot(tri, g3[0]) + _dot(tri, g3[1]) + _dot(tri, g3[2])
        total = cum[edge_row:edge_row + 1, :]
        ref = cum[c // 2:c // 2 + 1, :]
        qc = q * jnp.exp(cum - ref)
        kc = (kk * jnp.exp(ref - cum)).astype(BF16)
        qe = (q * jnp.exp(cum)).astype(BF16)
        ke = (kk * jnp.exp(total - cum)).astype(BF16)
        dec = jnp.exp(total)
        st = st_ref[d]
        o_inter = _dot_nt(qe, st.astype(BF16))
        lhs = jnp.concatenate([qc * hm for hm in head_masks], axis=0).astype(BF16)
        a = jnp.where(causal, _dot_nt(lhs, kc), 0.0).astype(BF16)
        o_stack = _dot(a, v)
        o = o_inter
        for h in range(HG_HEADS):
            o = o + o_stack[h * c:(h + 1) * c, :] * head_masks[h]
        out_scr[rows, :] = o
        st_ref[d] = st * dec + jnp.where(bd, _dot_tn(v, ke), 0.0)

    def chunk_step(i, carry):
        sf = pl.multiple_of(i * c, c)
        sb = pl.multiple_of((n_chunks - 1 - i) * c, c)
        one_dir(0, sf, hf_ref, tri_f, causal_f, c - 1, of_scr)
        one_dir(1, sb, hb_ref, tri_b, causal_b, 0, ob_scr)
        return carry

    lax.fori_loop(0, n_chunks, chunk_step, 0)

    if so_ref is not None:
        for d in range(2):
            for h in range(HG_HEADS):
                hs = slice(h * HG_DK, (h + 1) * HG_DK)
                so_ref[d, h] = st_ref[d, hs, hs]

    ones_bd = bd.astype(BF16)

    def finish(i, carry):
        rows = pl.ds(pl.multiple_of(i * c, c), c)
        o = of_scr[rows, :] + ob_scr[rows, :]
        sq = _split(o * o, 2)
        ms = (_dot(sq[0], ones_bd) + _dot(sq[1], ones_bd)) * (1.0 / HG_DK)
        y = o * lax.rsqrt(ms + EPS) * ng_ref[...]
        o_ref[rows, :] = (y * _silu(hg_ref[rows, :])).astype(o_ref.dtype)
        return carry

    lax.fori_loop(0, n_chunks, finish, 0)


def _hgrn(proj, row0, n_seq, seq_len, lb, ng, s0t):
    n_chunks = seq_len // CHUNK
    blk0 = row0 // seq_len
    col = lambda c0: pl.BlockSpec((seq_len, HG_WIDTH), lambda i: (blk0 + i, c0 // HG_WIDTH))
    in_specs = [col(COL_HQ), col(COL_HI), col(COL_HF), col(COL_HB), col(COL_HG),
                pl.BlockSpec((2, HG_WIDTH), lambda i: (0, 0)),
                pl.BlockSpec((1, HG_WIDTH), lambda i: (0, 0))]
    args = [proj, proj, proj, proj, proj, lb, ng]
    out_specs = [pl.BlockSpec((seq_len, HG_WIDTH), lambda i: (i, 0))]
    out_shape = [jax.ShapeDtypeStruct((n_seq * seq_len, HG_WIDTH), BF16)]
    if s0t is not None:
        in_specs.append(pl.BlockSpec((None, 2, HG_WIDTH, HG_WIDTH), lambda i: (i, 0, 0, 0)))
        args.append(s0t)
    else:
        out_specs.append(pl.BlockSpec((None, 2, HG_HEADS, HG_DK, HG_DK), lambda i: (i, 0, 0, 0, 0)))
        out_shape.append(jax.ShapeDtypeStruct((n_seq, 2, HG_HEADS, HG_DK, HG_DK), F32))
    return pl.pallas_call(
        functools.partial(_hgrn_body, n_chunks, s0t is not None),
        grid=(n_seq,),
        in_specs=in_specs,
        out_specs=out_specs,
        out_shape=out_shape,
        scratch_shapes=[pltpu.VMEM((seq_len, HG_WIDTH), F32), pltpu.VMEM((seq_len, HG_WIDTH), F32),
                        pltpu.VMEM((2, HG_WIDTH, HG_WIDTH), F32)],
        compiler_params=_params(1),
        name=f"hgrn_{seq_len}",
    )(*args)


def _outproj_body(n_x, n_ctx_tiles, *refs):
    fn_refs, a_refs, hg_refs = refs[0:2], refs[2:4], refs[4:6]
    x_refs = refs[6:6 + n_x]
    mod_ref, g_ref, w_ref, wr_ref, xo_ref, h_ref, lt_ref = refs[6 + n_x:]
    is_ctx = pl.program_id(0) < n_ctx_tiles
    mix = (_dot(_pick(is_ctx, fn_refs), w_ref[0:FN_WIDTH, :])
           + _dot(_pick(is_ctx, a_refs), w_ref[FN_WIDTH:FN_WIDTH + DA_WIDTH, :])
           + _dot(_pick(is_ctx, hg_refs), w_ref[FN_WIDTH + DA_WIDTH:, :]))
    x = _pick(is_ctx, x_refs) + mod_ref[2:3, :] * mix
    xo_ref[...] = x
    h = _modnorm(x, g_ref[...], mod_ref[3:4, :], mod_ref[4:5, :])
    h_ref[...] = h.astype(BF16)
    h_hi, h_lo = _split(h, 2)
    w_hi, w_lo = _split(wr_ref[...], 2)
    lt_ref[...] = _dot_nt(w_hi, h_hi) + _dot_nt(w_lo, h_hi) + _dot_nt(w_hi, h_lo)


def _outproj(fn, a, hg, xs, mod, g, w_out, w_router_t, n_ctx_tok, lat_len):
    t = sum(x.shape[0] for x in xs)
    tm = TOKEN_TILE
    n_ctx_tiles = n_ctx_tok // tm
    row = _mod_row(n_ctx_tiles, lat_len // tm)
    rows = lambda width: pl.BlockSpec((tm, width), lambda i: (i, 0))
    parts = lambda n, width: _row_specs(n, tm, width, n_ctx_tiles)
    return pl.pallas_call(
        functools.partial(_outproj_body, len(xs), n_ctx_tiles),
        grid=(t // tm,),
        in_specs=parts(2, FN_WIDTH) + parts(2, DA_WIDTH) + parts(2, HG_WIDTH) + parts(len(xs), D_MODEL) + [
            pl.BlockSpec((None, 6, D_MODEL), lambda i: (row(i), 0, 0)),
            pl.BlockSpec((1, D_MODEL), lambda i: (0, 0)),
            pl.BlockSpec((D_MODEL, D_MODEL), lambda i: (0, 0)),
            pl.BlockSpec((N_EXPERTS, D_MODEL), lambda i: (0, 0)),
        ],
        out_specs=[rows(D_MODEL), rows(D_MODEL), pl.BlockSpec((N_EXPERTS, tm), lambda i: (0, i))],
        out_shape=[jax.ShapeDtypeStruct((t, D_MODEL), F32),
                   jax.ShapeDtypeStruct((t, D_MODEL), BF16),
                   jax.ShapeDtypeStruct((N_EXPERTS, t), F32)],
        compiler_params=_params(1),
        name="outproj",
    )(*fn, *a, *hg, *xs, mod, g, w_out, w_router_t)


def _route_body(lt_ref, bias_ref, o_ref):
    per = N_EXPERTS // N_GROUPS
    tt = lt_ref.shape[1]
    neg = -jnp.inf
    assert per == N_GROUPS == 8
    gi = lax.broadcasted_iota(jnp.int32, (N_GROUPS, tt), 0).astype(F32)
    s_j, b_j = [], []
    for j in range(per):
        s = _sigmoid(lt_ref[j * N_GROUPS:(j + 1) * N_GROUPS, :])
        s_j.append(s)
        b_j.append(s + bias_ref[j * N_GROUPS:(j + 1) * N_GROUPS, :])
    m1 = functools.reduce(jnp.maximum, b_j)
    i1 = functools.reduce(jnp.minimum, [jnp.where(b_j[j] == m1, float(j), float(per)) for j in range(per)])
    m2 = functools.reduce(jnp.maximum, [jnp.where(i1 == float(j), neg, b_j[j]) for j in range(per)])
    gs = m1 + m2
    gsel = jnp.zeros((N_GROUPS, tt), jnp.bool_)
    for _ in range(TOPK_GROUPS):
        m = gs.max(axis=0, keepdims=True)
        idx = jnp.where(gs == m, gi, float(N_GROUPS)).min(axis=0, keepdims=True)
        hit = gi == idx
        gsel = gsel | hit
        gs = jnp.where(hit, neg, gs)
    x_j = [jnp.where(gsel, b_j[j], neg) for j in range(per)]
    e_j = [gi * per + j for j in range(per)]
    sel_j = [jnp.zeros((N_GROUPS, tt), jnp.bool_) for _ in range(per)]
    for _ in range(TOP_K):
        m = functools.reduce(jnp.maximum, x_j).max(axis=0, keepdims=True)
        idx = functools.reduce(jnp.minimum, [jnp.where(x_j[j] == m, e_j[j], float(N_EXPERTS))
                                             for j in range(per)]).min(axis=0, keepdims=True)
        for j in range(per):
            hit = e_j[j] == idx
            sel_j[j] = sel_j[j] | hit
            x_j[j] = jnp.where(hit, neg, x_j[j])
    w_j = [jnp.where(sel_j[j], s_j[j], 0.0) for j in range(per)]
    denom = functools.reduce(lambda a, b: a + b, w_j).sum(axis=0, keepdims=True)
    gates_t = jnp.concatenate([w / denom * ROUTED_SCALE for w in w_j], axis=0)
    r_io = lax.broadcasted_iota(jnp.int32, (N_EXPERTS, N_EXPERTS), 0)
    e_io = lax.broadcasted_iota(jnp.int32, (N_EXPERTS, N_EXPERTS), 1)
    eye = (e_io == (r_io & (N_GROUPS - 1)) * per + (r_io >> 3)).astype(BF16)
    p = _split(gates_t, 3)
    gates = _dot_tn(p[0], eye) + _dot_tn(p[1], eye) + _dot_tn(p[2], eye)
    for c in range(N_EXPERTS // EXPERTS_PER_STEP):
        o_ref[c] = gates[:, c * EXPERTS_PER_STEP:(c + 1) * EXPERTS_PER_STEP]


def _route(logits_t, bias):
    t = logits_t.shape[1]
    tt = ROUTE_TILE
    nch = N_EXPERTS // EXPERTS_PER_STEP
    return pl.pallas_call(
        _route_body,
        grid=(t // tt,),
        in_specs=[pl.BlockSpec((N_EXPERTS, tt), lambda i: (0, i)),
                  pl.BlockSpec((N_EXPERTS, 1), lambda i: (0, 0))],
        out_specs=pl.BlockSpec((nch, tt, EXPERTS_PER_STEP), lambda i: (0, i, 0)),
        out_shape=jax.ShapeDtypeStruct((nch, t, EXPERTS_PER_STEP), F32),
        compiler_params=_params(1),
        name="route",
    )(logits_t, bias)


def _moe_body(final_norm, n_ctx_tiles, h_ref, gate_ref, wg_ref, wu_ref, wd_ref, sg_ref, su_ref, sd_ref,
              x_ref, mod_ref, fg_ref, *out_and_scratch):
    acc_ref = out_and_scratch[-1]
    o_refs = out_and_scratch[:-1]
    is_ctx = pl.program_id(0) < n_ctx_tiles
    j = pl.program_id(1)
    h = h_ref[...]

    def act_of(wg, wu, gate):
        gu = _dot(h, jnp.concatenate([wg.astype(BF16), wu.astype(BF16)], axis=1))
        a = _silu(gu[:, :D_EXPERT]) * gu[:, D_EXPERT:]
        return a if gate is None else a * gate

    @pl.when(j == 0)
    def _():
        acc_ref[...] = _dot(act_of(sg_ref[...], su_ref[...], None).astype(BF16), sd_ref[...].astype(BF16))

    for p in range(EXPERTS_PER_STEP // 2):
        a0 = act_of(wg_ref[2 * p], wu_ref[2 * p], gate_ref[:, 2 * p:2 * p + 1])
        a1 = act_of(wg_ref[2 * p + 1], wu_ref[2 * p + 1], gate_ref[:, 2 * p + 1:2 * p + 2])
        pair = jnp.concatenate([a0, a1], axis=1).astype(BF16)
        acc_ref[...] += _dot(pair, wd_ref[p].astype(BF16))

    @pl.when(j == pl.num_programs(1) - 1)
    def _():
        x = x_ref[...] + mod_ref[5:6, :] * acc_ref[...]
        if not final_norm:
            o_refs[0][...] = x
        else:
            ms = jnp.mean(x * x, axis=-1, keepdims=True)
            y = x * lax.rsqrt(ms + EPS) * fg_ref[...]

            @pl.when(is_ctx)
            def _():
                o_refs[0][...] = y

            @pl.when(jnp.logical_not(is_ctx))
            def _():
                o_refs[1][...] = y


def _moe(h, gates, w_gate, w_up, w_down2, ws_gate, ws_up, ws_down, layer, x, mod, final_g, final_norm,
         n_ctx_tok, lat_len):
    t = x.shape[0]
    tm = MOE_TILE
    eps_ = EXPERTS_PER_STEP
    n_ctx_tiles = n_ctx_tok // tm
    row = _mod_row(n_ctx_tiles, lat_len // tm)
    if final_norm:
        out_specs = _row_specs(2, tm, D_MODEL, n_ctx_tiles)
        out_shape = [jax.ShapeDtypeStruct((n_ctx_tok, D_MODEL), F32),
                     jax.ShapeDtypeStruct((t - n_ctx_tok, D_MODEL), F32)]
    else:
        out_specs = [pl.BlockSpec((tm, D_MODEL), lambda i, j: (i, 0))]
        out_shape = [jax.ShapeDtypeStruct((t, D_MODEL), F32)]
    return pl.pallas_call(
        functools.partial(_moe_body, final_norm, n_ctx_tiles),
        grid=(t // tm, N_EXPERTS // eps_),
        in_specs=[
            pl.BlockSpec((tm, D_MODEL), lambda i, j: (i, 0)),
            pl.BlockSpec((None, tm, eps_), lambda i, j: (j, i, 0)),
            pl.BlockSpec((None, eps_, D_MODEL, D_EXPERT), lambda i, j: (layer, j, 0, 0)),
            pl.BlockSpec((None, eps_, D_MODEL, D_EXPERT), lambda i, j: (layer, j, 0, 0)),
            pl.BlockSpec((None, eps_ // 2, 2 * D_EXPERT, D_MODEL), lambda i, j: (layer, j, 0, 0)),
            pl.BlockSpec((None, D_MODEL, D_EXPERT), lambda i, j: (layer, 0, 0)),
            pl.BlockSpec((None, D_MODEL, D_EXPERT), lambda i, j: (layer, 0, 0)),
            pl.BlockSpec((None, D_EXPERT, D_MODEL), lambda i, j: (layer, 0, 0)),
            pl.BlockSpec((tm, D_MODEL), lambda i, j: (i, 0)),
            pl.BlockSpec((None, 6, D_MODEL), lambda i, j: (row(i), 0, 0)),
            pl.BlockSpec((1, D_MODEL), lambda i, j: (0, 0)),
        ],
        out_specs=out_specs,
        out_shape=out_shape,
        scratch_shapes=[pltpu.VMEM((tm, D_MODEL), F32)],
        compiler_params=_params(2),
        name="moe",
    )(h, gates, w_gate, w_up, w_down2, ws_gate, ws_up, ws_down, x, mod, final_g)


def _block_diag_t(s):
    eye = jnp.eye(HG_HEADS, dtype=s.dtype)
    out = jnp.einsum('...hkv,hg->...hvgk', s, eye)
    return out.reshape(s.shape[:-3] + (HG_WIDTH, HG_WIDTH))


def kernel(x_prompt, x_sample, cache_k, cache_v, state_hgrn, c, c_ctx, w_ada, b_ada, norm_g, w_in,
           w_fourier, lambdas, attn_norm_g, lower_bounds, hg_norm_g, w_out, w_router, router_bias,
           w_gate, w_up, w_down, ws_gate, ws_up, ws_down, final_g):
    n_ctx, ctx_len, _ = x_prompt.shape
    n_lat, lat_len, _ = x_sample.shape
    n_ctx_tok = n_ctx * ctx_len
    past = cache_k.shape[2]

    xs = (x_prompt.reshape(n_ctx_tok, D_MODEL), x_sample.reshape(n_lat * lat_len, D_MODEL))

    c8 = jnp.zeros((8, D_MODEL), F32).at[0].set(c_ctx).at[1:1 + n_lat].set(c)
    mods = _ada_mods(c8, w_ada, b_ada).reshape(DEPTH, 8, 6, D_MODEL)

    cs = jnp.cumsum(jax.nn.softmax(lower_bounds.astype(F32), axis=0), axis=0)
    lbs = cs - cs[0:1]

    cos, sin = _rope_tables(lat_len)
    cache_k4 = cache_k.reshape(n_lat, DEPTH, past, DA_WIDTH)
    cache_v4 = cache_v.reshape(n_lat, DEPTH, past, DA_WIDTH)
    s0t = _block_diag_t(state_hgrn.astype(F32))

    w_in_b = jnp.concatenate([w_in[:, :, FN_WIDTH:], w_in[:, :, :FN_WIDTH]], axis=-1).astype(BF16)
    w_f_b = w_fourier.astype(BF16)
    w_out_b = w_out.astype(BF16)
    per = N_EXPERTS // N_GROUPS
    w_router_t = (jnp.swapaxes(w_router, 1, 2).reshape(DEPTH, N_GROUPS, per, D_MODEL)
                  .swapaxes(1, 2).reshape(DEPTH, N_EXPERTS, D_MODEL))
    bias_mm = router_bias.reshape(DEPTH, N_GROUPS, per).swapaxes(1, 2).reshape(DEPTH, N_EXPERTS, 1)
    w_down2 = w_down.reshape(DEPTH, N_EXPERTS // 2, 2 * D_EXPERT, D_MODEL)
    ng = jnp.tile(hg_norm_g, (1, HG_HEADS))

    caches = None
    new_s = []
    for l in range(DEPTH):
        lam_init = 0.8 - 0.6 * math.exp(-0.3 * l)
        mod = mods[l]
        proj, *caches = _inproj(xs, mod, norm_g[l, 0:1], w_in_b[l], l, caches, n_ctx, ctx_len, lat_len)

        fn = (_fourier(proj, 0, n_ctx, ctx_len, w_f_b[l]),
              _fourier(proj, n_ctx_tok, n_lat, lat_len, w_f_b[l]))
        ag = attn_norm_g[l].reshape(1, DA_VDIM)
        a = (_attn_ctx(proj, n_ctx, ctx_len, lambdas[l], ag, lam_init),
             _attn_lat(proj, n_ctx_tok, n_lat, lat_len, cache_k4, cache_v4, l, cos, sin, lambdas[l], ag, lam_init))
        hg_ctx, st_ctx = _hgrn(proj, 0, n_ctx, ctx_len, lbs[l], ng[l:l + 1], None)
        (hg_lat,) = _hgrn(proj, n_ctx_tok, n_lat, lat_len, lbs[l], ng[l:l + 1], s0t[:, l])
        new_s.append(jnp.swapaxes(st_ctx, -1, -2))

        x, h2, logits_t = _outproj(fn, a, (hg_ctx, hg_lat), xs, mod, norm_g[l, 1:2], w_out_b[l],
                                   w_router_t[l], n_ctx_tok, lat_len)
        gates = _route(logits_t, bias_mm[l])
        xs = _moe(h2, gates, w_gate, w_up, w_down2, ws_gate, ws_up, ws_down, l, x, mod,
                  final_g.reshape(1, D_MODEL), l == DEPTH - 1, n_ctx_tok, lat_len)

    y_prompt = xs[0].reshape(n_ctx, ctx_len, D_MODEL)
    y_sample = xs[1].reshape(n_lat, lat_len, D_MODEL)
    new_k = caches[0].reshape(n_ctx, DEPTH, ctx_len, DA_HEADS, DA_VDIM)
    new_v = caches[1].reshape(n_ctx, DEPTH, ctx_len, DA_HEADS, DA_VDIM)
    return (y_prompt, y_sample, new_k, new_v, jnp.stack(new_s, axis=1))
```

```python
import functools
import math

import numpy as np
import jax
import jax.numpy as jnp
from jax import lax
from jax.experimental import pallas as pl
from jax.experimental.pallas import tpu as pltpu

F32 = jnp.float32
BF16 = jnp.bfloat16

D_MODEL = 1024
DEPTH = 2
GRID_W = 64
FN_WIDTH = 256
FN_GROUPS = 4
FN_GROUP_DIM = 64
DA_WIDTH = 512
DA_HEADS = 4
DA_VDIM = 128
DA_HALF = 64
HG_WIDTH = 256
HG_HEADS = 4
HG_DK = 64
PROJ_WIDTH = 3072
CHUNK = 64
ROPE_THETA = 10000.0
N_EXPERTS = 64
TOP_K = 8
N_GROUPS = 8
TOPK_GROUPS = 4
D_EXPERT = 128
ROUTED_SCALE = 2.5
EPS = 1e-6

COL_Q, COL_K, COL_V = 0, 512, 1024
COL_HQ, COL_HI, COL_HF, COL_HB, COL_HG = 1536, 1792, 2048, 2304, 2560
COL_FN = 2816

TOKEN_TILE = 256
MOE_TILE = 1024
EXPERTS_PER_STEP = 4
ROUTE_TILE = 512
VMEM_LIMIT = 56 * 1024 * 1024


def _dot(a, b):
    return jnp.dot(a, b, preferred_element_type=F32)


def _dot_nt(a, b):
    return lax.dot_general(a, b, (((1,), (1,)), ((), ())), preferred_element_type=F32)


def _dot_tn(a, b):
    return lax.dot_general(a, b, (((0,), (0,)), ((), ())), preferred_element_type=F32)


def _split(x, n):
    parts = []
    r = x
    for i in range(n):
        p = r.astype(BF16)
        parts.append(p)
        if i + 1 < n:
            r = r - p.astype(F32)
    return parts


def _sigmoid(x):
    return 1.0 / (1.0 + jnp.exp(-x))


def _silu(x):
    return x * _sigmoid(x)


def _params(n_axes):
    return pltpu.CompilerParams(dimension_semantics=("arbitrary",) * n_axes,
                                vmem_limit_bytes=VMEM_LIMIT)


def _ada_body(c_ref, w_ref, b_ref, o_ref):
    a = _silu(c_ref[...])
    a_hi, a_lo = _split(a, 2)
    w_hi, w_lo = _split(w_ref[...], 2)
    o_ref[...] = _dot(a_hi, w_hi) + _dot(a_lo, w_hi) + _dot(a_hi, w_lo) + b_ref[...]


def _ada_mods(c8, w_ada, b_ada):
    tn = 1536
    return pl.pallas_call(
        _ada_body,
        grid=(DEPTH, 6 * D_MODEL // tn),
        in_specs=[
            pl.BlockSpec((8, D_MODEL), lambda l, j: (0, 0)),
            pl.BlockSpec((None, D_MODEL, tn), lambda l, j: (l, 0, j)),
            pl.BlockSpec((None, 1, tn), lambda l, j: (l, 0, j)),
        ],
        out_specs=pl.BlockSpec((None, 8, tn), lambda l, j: (l, 0, j)),
        out_shape=jax.ShapeDtypeStruct((DEPTH, 8, 6 * D_MODEL), F32),
        compiler_params=_params(2),
        name="ada_mods",
    )(c8, w_ada, b_ada.reshape(DEPTH, 1, 6 * D_MODEL))


def _modnorm(x, g, shift, scale):
    ms = jnp.mean(x * x, axis=-1, keepdims=True)
    return (x * lax.rsqrt(ms + EPS) * g) * (1.0 + scale) + shift


def _mod_row(n_ctx_tiles, tiles_per_latent):
    def f(i):
        return jnp.where(i < n_ctx_tiles, 0, 1 + (i - n_ctx_tiles) // tiles_per_latent)
    return f


def _row_specs(n_parts, tm, width, n_ctx_tiles):
    if n_parts == 1:
        return [pl.BlockSpec((tm, width), lambda i, *_: (i, 0))]
    return [pl.BlockSpec((tm, width), lambda i, *_: (jnp.minimum(i, n_ctx_tiles - 1), 0)),
            pl.BlockSpec((tm, width), lambda i, *_: (jnp.maximum(i - n_ctx_tiles, 0), 0))]


def _pick(is_ctx, refs):
    if len(refs) == 1:
        return refs[0][...]
    return jnp.where(is_ctx, refs[0][...], refs[1][...])


def _inproj_body(n_x, n_ctx_tiles, *refs):
    x_refs = refs[:n_x]
    mod_ref, g_ref, w_ref = refs[n_x:n_x + 3]
    o_ref, k_ref, v_ref = refs[-3:]
    is_ctx = pl.program_id(0) < n_ctx_tiles
    x = _pick(is_ctx, x_refs)
    h = _modnorm(x, g_ref[...], mod_ref[0:1, :], mod_ref[1:2, :])
    proj = _dot(h.astype(BF16), w_ref[...])
    o_ref[...] = proj

    @pl.when(is_ctx)
    def _():
        n_tok = proj.shape[0]
        for h in range(DA_HEADS):
            rows = pl.ds(h, n_tok, stride=DA_HEADS)
            k_ref[rows, :] = proj[:, COL_K + h * DA_VDIM:COL_K + (h + 1) * DA_VDIM]
            v_ref[rows, :] = proj[:, COL_V + h * DA_VDIM:COL_V + (h + 1) * DA_VDIM]


def _inproj(xs, mod, g, w, layer, caches, n_ctx, ctx_len, lat_len):
    t = sum(x.shape[0] for x in xs)
    tm = TOKEN_TILE
    assert ctx_len == tm
    n_ctx_tiles = n_ctx
    row = _mod_row(n_ctx_tiles, lat_len // tm)
    cache_spec = pl.BlockSpec((ctx_len * DA_HEADS, DA_VDIM),
                              lambda i: (jnp.minimum(i, n_ctx - 1) * DEPTH + layer, 0))
    cache_shape = jax.ShapeDtypeStruct((n_ctx * DEPTH * ctx_len * DA_HEADS, DA_VDIM), F32)
    in_specs = _row_specs(len(xs), tm, D_MODEL, n_ctx_tiles) + [
        pl.BlockSpec((None, 6, D_MODEL), lambda i: (row(i), 0, 0)),
        pl.BlockSpec((1, D_MODEL), lambda i: (0, 0)),
        pl.BlockSpec((D_MODEL, PROJ_WIDTH), lambda i: (0, 0)),
    ]
    args = list(xs) + [mod, g, w]
    aliases = {}
    if caches is not None:
        aliases = {len(args): 1, len(args) + 1: 2}
        in_specs += [pl.BlockSpec(memory_space=pl.ANY), pl.BlockSpec(memory_space=pl.ANY)]
        args += list(caches)
    body = functools.partial(_inproj_body, len(xs), n_ctx_tiles)
    if caches is not None:
        body = functools.partial(_drop_refs, body, len(args) - 2, 2)
    return pl.pallas_call(
        body,
        grid=(t // tm,),
        in_specs=in_specs,
        out_specs=[pl.BlockSpec((tm, PROJ_WIDTH), lambda i: (i, 0)), cache_spec, cache_spec],
        out_shape=[jax.ShapeDtypeStruct((t, PROJ_WIDTH), F32), cache_shape, cache_shape],
        input_output_aliases=aliases,
        compiler_params=_params(1),
        name="inproj",
    )(*args)


def _drop_refs(body, start, count, *refs):
    return body(*refs[:start], *refs[start + count:])


def _fourier_body(u_ref, cl_ref, sl_ref, cc_ref, sc_ref, w_ref, o_ref):
    z = u_ref[...].astype(BF16)
    a = _dot(z, cc_ref[...]).astype(BF16)
    b = _dot(z, sc_ref[...]).astype(BF16)
    y = _dot(cl_ref[...], a) - _dot(sl_ref[...], b)
    o_ref[...] = _dot(y.astype(BF16), w_ref[...]).astype(o_ref.dtype)


def _dft_tables(n, block):
    i = np.arange(n)
    prod = (i[:, None] % block) * (i[None, :] % block) % block
    ang = prod.astype(np.float64) * (2.0 * math.pi / block)
    same = (i[:, None] // block) == (i[None, :] // block)
    scale = 1.0 / math.sqrt(block)
    c = np.where(same, np.cos(ang) * scale, 0.0).astype(np.float32)
    s = np.where(same, np.sin(ang) * scale, 0.0).astype(np.float32)
    return jnp.asarray(c).astype(BF16), jnp.asarray(s).astype(BF16)


def _fourier(proj, row0, n_seq, seq_len, w_f):
    cl, sl = _dft_tables(seq_len, seq_len)
    cc, sc = _dft_tables(FN_WIDTH, FN_GROUP_DIM)
    blk0 = row0 // seq_len
    full = lambda shape: pl.BlockSpec(shape, lambda i: (0, 0))
    return pl.pallas_call(
        _fourier_body,
        grid=(n_seq,),
        in_specs=[
            pl.BlockSpec((seq_len, FN_WIDTH), lambda i: (blk0 + i, COL_FN // FN_WIDTH)),
            full((seq_len, seq_len)), full((seq_len, seq_len)),
            full((FN_WIDTH, FN_WIDTH)), full((FN_WIDTH, FN_WIDTH)), full((FN_WIDTH, FN_WIDTH)),
        ],
        out_specs=pl.BlockSpec((seq_len, FN_WIDTH), lambda i: (i, 0)),
        out_shape=jax.ShapeDtypeStruct((n_seq * seq_len, FN_WIDTH), BF16),
        compiler_params=_params(1),
        name=f"fourier_{seq_len}",
    )(proj, cl, sl, cc, sc, w_f)


def _lambda_full(lmb, lam_init):
    a = jnp.sum(lmb[0:1, :] * lmb[1:2, :], axis=-1, keepdims=True)
    b = jnp.sum(lmb[2:3, :] * lmb[3:4, :], axis=-1, keepdims=True)
    return jnp.exp(a) - jnp.exp(b) + lam_init


def _softmax_parts(parts):
    m = parts[0].max(axis=-1, keepdims=True)
    for p in parts[1:]:
        m = jnp.maximum(m, p.max(axis=-1, keepdims=True))
    es = [jnp.exp(p - m) for p in parts]
    tot = es[0].sum(axis=-1, keepdims=True)
    for e in es[1:]:
        tot = tot + e.sum(axis=-1, keepdims=True)
    return es, 1.0 / tot


def _diff_head(q, ks, vs, lam, g, lam_init):
    outs = []
    for m in range(2):
        qm = q[:, m * DA_HALF:(m + 1) * DA_HALF].astype(BF16)
        parts = [_dot_nt(qm, k[:, m * DA_HALF:(m + 1) * DA_HALF].astype(BF16)) for k in ks]
        es, inv = _softmax_parts(parts)
        o = _dot(es[0].astype(BF16), vs[0].astype(BF16))
        for e, v in zip(es[1:], vs[1:]):
            o = o + _dot(e.astype(BF16), v.astype(BF16))
        outs.append(o * inv)
    a = outs[0] - lam * outs[1]
    ms = jnp.mean(a * a, axis=-1, keepdims=True)
    return a * lax.rsqrt(ms + EPS) * g * (1.0 - lam_init)


def _attn_ctx_body(lam_init, q_ref, k_ref, v_ref, lmb_ref, g_ref, o_ref):
    lam = _lambda_full(lmb_ref[...], lam_init)
    scale = DA_HALF ** -0.5
    for h in range(DA_HEADS):
        sl = slice(h * DA_VDIM, (h + 1) * DA_VDIM)
        o = _diff_head(q_ref[:, sl] * scale, [k_ref[:, sl]], [v_ref[:, sl]], lam, g_ref[...], lam_init)
        o_ref[:, sl] = o.astype(o_ref.dtype)


def _attn_ctx(proj, n_seq, seq_len, lmb, g, lam_init):
    blk = lambda c: pl.BlockSpec((seq_len, DA_WIDTH), lambda i: (i, c))
    return pl.pallas_call(
        functools.partial(_attn_ctx_body, lam_init),
        grid=(n_seq,),
        in_specs=[
            blk(COL_Q // DA_WIDTH), blk(COL_K // DA_WIDTH), blk(COL_V // DA_WIDTH),
            pl.BlockSpec((4, DA_HALF), lambda i: (0, 0)),
            pl.BlockSpec((1, DA_VDIM), lambda i: (0, 0)),
        ],
        out_specs=pl.BlockSpec((seq_len, DA_WIDTH), lambda i: (i, 0)),
        out_shape=jax.ShapeDtypeStruct((n_seq * seq_len, DA_WIDTH), BF16),
        compiler_params=_params(1),
        name="attn_ctx",
    )(proj, proj, proj, lmb, g)


def _rope(x, cos, sin):
    lane = lax.broadcasted_iota(jnp.int32, x.shape, 1)
    first = ((lane >> 4) & 1) == 0
    rot = jnp.where(first, -pltpu.roll(x, 128 - DA_HALF // 4, 1), pltpu.roll(x, DA_HALF // 4, 1))
    return x * cos + rot * sin


def _attn_lat_body(lam_init, q_ref, k_ref, v_ref, kc_ref, vc_ref, cq_ref, sq_ref, ck_ref, sk_ref,
                   lmb_ref, g_ref, o_ref):
    lam = _lambda_full(lmb_ref[...], lam_init)
    scale = DA_HALF ** -0.5
    for h in range(DA_HEADS):
        sl = slice(h * DA_VDIM, (h + 1) * DA_VDIM)
        q = _rope(q_ref[:, sl], cq_ref[...], sq_ref[...]) * scale
        k = _rope(k_ref[:, sl], ck_ref[...], sk_ref[...])
        o = _diff_head(q, [k, kc_ref[:, sl]], [v_ref[:, sl], vc_ref[:, sl]], lam, g_ref[...], lam_init)
        o_ref[:, sl] = o.astype(o_ref.dtype)


def _attn_lat(proj, row0, n_seq, seq_len, cache_k, cache_v, layer, cos, sin, lmb, g, lam_init):
    tq = 512
    nq = seq_len // tq
    past = cache_k.shape[2]
    qb0 = row0 // tq
    kb0 = row0 // seq_len
    cache_spec = pl.BlockSpec((None, None, past, DA_WIDTH), lambda b, j: (b, layer, 0, 0))
    kv_spec = lambda c: pl.BlockSpec((seq_len, DA_WIDTH), lambda b, j: (kb0 + b, c))
    return pl.pallas_call(
        functools.partial(_attn_lat_body, lam_init),
        grid=(n_seq, nq),
        in_specs=[
            pl.BlockSpec((tq, DA_WIDTH), lambda b, j: (qb0 + b * nq + j, COL_Q // DA_WIDTH)),
            kv_spec(COL_K // DA_WIDTH), kv_spec(COL_V // DA_WIDTH),
            cache_spec, cache_spec,
            pl.BlockSpec((tq, DA_VDIM), lambda b, j: (j, 0)),
            pl.BlockSpec((tq, DA_VDIM), lambda b, j: (j, 0)),
            pl.BlockSpec((seq_len, DA_VDIM), lambda b, j: (0, 0)),
            pl.BlockSpec((seq_len, DA_VDIM), lambda b, j: (0, 0)),
            pl.BlockSpec((4, DA_HALF), lambda b, j: (0, 0)),
            pl.BlockSpec((1, DA_VDIM), lambda b, j: (0, 0)),
        ],
        out_specs=pl.BlockSpec((tq, DA_WIDTH), lambda b, j: (b * nq + j, 0)),
        out_shape=jax.ShapeDtypeStruct((n_seq * seq_len, DA_WIDTH), BF16),
        compiler_params=_params(2),
        name="attn_lat",
    )(proj, proj, proj, cache_k, cache_v, cos, sin, cos, sin, lmb, g)


def _rope_tables(n_tokens):
    rows = n_tokens // GRID_W
    row = np.repeat(np.arange(rows, dtype=np.float64), GRID_W)
    col = np.tile(np.arange(GRID_W, dtype=np.float64), rows)
    axis_dim = DA_HALF // 2
    inv_freq = ROPE_THETA ** (-np.arange(0, axis_dim, 2, dtype=np.float64) / axis_dim)
    ang_r = row[:, None] * inv_freq[None, :]
    ang_c = col[:, None] * inv_freq[None, :]
    ang = np.concatenate([ang_r, ang_r, ang_c, ang_c] * 2, axis=-1)
    return jnp.asarray(np.cos(ang).astype(np.float32)), jnp.asarray(np.sin(ang).astype(np.float32))


def _hgrn_body(n_chunks, has_s0, *refs):
    if has_s0:
        (hq_ref, hi_ref, hf_ref, hb_ref, hg_ref, lb_ref, ng_ref, s0_ref,
         o_ref, of_scr, ob_scr, st_ref) = refs
        so_ref = None
    else:
        (hq_ref, hi_ref, hf_ref, hb_ref, hg_ref, lb_ref, ng_ref,
         o_ref, so_ref, of_scr, ob_scr, st_ref) = refs
    c = CHUNK
    w = HG_WIDTH
    r_io = lax.broadcasted_iota(jnp.int32, (c, c), 0)
    c_io = lax.broadcasted_iota(jnp.int32, (c, c), 1)
    tri_f = (c_io <= r_io).astype(BF16)
    tri_b = (c_io >= r_io).astype(BF16)
    lane = lax.broadcasted_iota(jnp.int32, (1, w), 1)
    dk_bits = HG_DK.bit_length() - 1
    head_masks = [((lane >> dk_bits) == h).astype(F32) for h in range(HG_HEADS)]
    bd = ((lax.broadcasted_iota(jnp.int32, (w, w), 0) >> dk_bits)
          == (lax.broadcasted_iota(jnp.int32, (w, w), 1) >> dk_bits))
    t_of_row = lax.broadcasted_iota(jnp.int32, (HG_HEADS * c, c), 0) & (c - 1)
    s_of_col = lax.broadcasted_iota(jnp.int32, (HG_HEADS * c, c), 1)
    causal_f = s_of_col <= t_of_row
    causal_b = s_of_col >= t_of_row

    if has_s0:
        st_ref[0] = s0_ref[0]
        st_ref[1] = s0_ref[1]
    else:
        st_ref[...] = jnp.zeros_like(st_ref)

    def one_dir(d, start, fp_ref, tri, causal, edge_row, out_scr):
        rows = pl.ds(start, c)
        lb = lb_ref[d:d + 1, :]
        fp = fp_ref[rows, :]
        lsig = jnp.minimum(fp, 0.0) - jnp.log1p(jnp.exp(-jnp.abs(fp)))
        la = jnp.log(lb)
        lbb = jnp.log1p(-lb) + lsig
        logf = jnp.maximum(la, lbb) + jnp.log1p(jnp.exp(-jnp.abs(la - lbb)))
        kk = (1.0 - lb) * (1.0 / (1.0 + jnp.exp(fp)))
        q = _silu(hq_ref[rows, :]) * (HG_DK ** -0.5)
        v = hi_ref[rows, :].astype(BF16)
        g3 = _split(logf, 3)
        cum = _dot(tri, g3[0]) + _dot(tri, g3[1]) + _dot(tri, g3[2])
        total = cum[edge_row:edge_row + 1, :]
        ref = cum[c // 2:c // 2 + 1, :]
        qc = q * jnp.exp(cum - ref)
        kc = (kk * jnp.exp(ref - cum)).astype(BF16)
        qe = (q * jnp.exp(cum)).astype(BF16)
        ke = (kk * jnp.exp(total - cum)).astype(BF16)
        dec = jnp.exp(total)
        st = st_ref[d]
        o_inter = _dot_nt(qe, st.astype(BF16))
        lhs = jnp.concatenate([qc * hm for hm in head_masks], axis=0).astype(BF16)
        a = jnp.where(causal, _dot_nt(lhs, kc), 0.0).astype(BF16)
        o_stack = _dot(a, v)
        o = o_inter
        for h in range(HG_HEADS):
            o = o + o_stack[h * c:(h + 1) * c, :] * head_masks[h]
        out_scr[rows, :] = o
        st_ref[d] = st * dec + jnp.where(bd, _dot_tn(v, ke), 0.0)

    def chunk_step(i, carry):
        sf = pl.multiple_of(i * c, c)
        sb = pl.multiple_of((n_chunks - 1 - i) * c, c)
        one_dir(0, sf, hf_ref, tri_f, causal_f, c - 1, of_scr)
        one_dir(1, sb, hb_ref, tri_b, causal_b, 0, ob_scr)
        return carry

    lax.fori_loop(0, n_chunks, chunk_step, 0)

    if so_ref is not None:
        for d in range(2):
            for h in range(HG_HEADS):
                hs = slice(h * HG_DK, (h + 1) * HG_DK)
                so_ref[d, h] = st_ref[d, hs, hs]

    ones_bd = bd.astype(BF16)

    def finish(i, carry):
        rows = pl.ds(pl.multiple_of(i * c, c), c)
        o = of_scr[rows, :] + ob_scr[rows, :]
        sq = _split(o * o, 2)
        ms = (_dot(sq[0], ones_bd) + _dot(sq[1], ones_bd)) * (1.0 / HG_DK)
        y = o * lax.rsqrt(ms + EPS) * ng_ref[...]
        o_ref[rows, :] = (y * _silu(hg_ref[rows, :])).astype(o_ref.dtype)
        return carry

    lax.fori_loop(0, n_chunks, finish, 0)


def _hgrn(proj, row0, n_seq, seq_len, lb, ng, s0t):
    n_chunks = seq_len // CHUNK
    blk0 = row0 // seq_len
    col = lambda c0: pl.BlockSpec((seq_len, HG_WIDTH), lambda i: (blk0 + i, c0 // HG_WIDTH))
    in_specs = [col(COL_HQ), col(COL_HI), col(COL_HF), col(COL_HB), col(COL_HG),
                pl.BlockSpec((2, HG_WIDTH), lambda i: (0, 0)),
                pl.BlockSpec((1, HG_WIDTH), lambda i: (0, 0))]
    args = [proj, proj, proj, proj, proj, lb, ng]
    out_specs = [pl.BlockSpec((seq_len, HG_WIDTH), lambda i: (i, 0))]
    out_shape = [jax.ShapeDtypeStruct((n_seq * seq_len, HG_WIDTH), BF16)]
    if s0t is not None:
        in_specs.append(pl.BlockSpec((None, 2, HG_WIDTH, HG_WIDTH), lambda i: (i, 0, 0, 0)))
        args.append(s0t)
    else:
        out_specs.append(pl.BlockSpec((None, 2, HG_HEADS, HG_DK, HG_DK), lambda i: (i, 0, 0, 0, 0)))
        out_shape.append(jax.ShapeDtypeStruct((n_seq, 2, HG_HEADS, HG_DK, HG_DK), F32))
    return pl.pallas_call(
        functools.partial(_hgrn_body, n_chunks, s0t is not None),
        grid=(n_seq,),
        in_specs=in_specs,
        out_specs=out_specs,
        out_shape=out_shape,
        scratch_shapes=[pltpu.VMEM((seq_len, HG_WIDTH), F32), pltpu.VMEM((seq_len, HG_WIDTH), F32),
                        pltpu.VMEM((2, HG_WIDTH, HG_WIDTH), F32)],
        compiler_params=_params(1),
        name=f"hgrn_{seq_len}",
    )(*args)


def _outproj_body(n_x, n_ctx_tiles, *refs):
    fn_refs, a_refs, hg_refs = refs[0:2], refs[2:4], refs[4:6]
    x_refs = refs[6:6 + n_x]
    mod_ref, g_ref, w_ref, wr_ref, xo_ref, h_ref, lt_ref = refs[6 + n_x:]
    is_ctx = pl.program_id(0) < n_ctx_tiles
    mix = (_dot(_pick(is_ctx, fn_refs), w_ref[0:FN_WIDTH, :])
           + _dot(_pick(is_ctx, a_refs), w_ref[FN_WIDTH:FN_WIDTH + DA_WIDTH, :])
           + _dot(_pick(is_ctx, hg_refs), w_ref[FN_WIDTH + DA_WIDTH:, :]))
    x = _pick(is_ctx, x_refs) + mod_ref[2:3, :] * mix
    xo_ref[...] = x
    h = _modnorm(x, g_ref[...], mod_ref[3:4, :], mod_ref[4:5, :])
    h_ref[...] = h.astype(BF16)
    h_hi, h_lo = _split(h, 2)
    w_hi, w_lo = _split(wr_ref[...], 2)
    lt_ref[...] = _dot_nt(w_hi, h_hi) + _dot_nt(w_lo, h_hi) + _dot_nt(w_hi, h_lo)


def _outproj(fn, a, hg, xs, mod, g, w_out, w_router_t, n_ctx_tok, lat_len):
    t = sum(x.shape[0] for x in xs)
    tm = TOKEN_TILE
    n_ctx_tiles = n_ctx_tok // tm
    row = _mod_row(n_ctx_tiles, lat_len // tm)
    rows = lambda width: pl.BlockSpec((tm, width), lambda i: (i, 0))
    parts = lambda n, width: _row_specs(n, tm, width, n_ctx_tiles)
    return pl.pallas_call(
        functools.partial(_outproj_body, len(xs), n_ctx_tiles),
        grid=(t // tm,),
        in_specs=parts(2, FN_WIDTH) + parts(2, DA_WIDTH) + parts(2, HG_WIDTH) + parts(len(xs), D_MODEL) + [
            pl.BlockSpec((None, 6, D_MODEL), lambda i: (row(i), 0, 0)),
            pl.BlockSpec((1, D_MODEL), lambda i: (0, 0)),
            pl.BlockSpec((D_MODEL, D_MODEL), lambda i: (0, 0)),
            pl.BlockSpec((N_EXPERTS, D_MODEL), lambda i: (0, 0)),
        ],
        out_specs=[rows(D_MODEL), rows(D_MODEL), pl.BlockSpec((N_EXPERTS, tm), lambda i: (0, i))],
        out_shape=[jax.ShapeDtypeStruct((t, D_MODEL), F32),
                   jax.ShapeDtypeStruct((t, D_MODEL), BF16),
                   jax.ShapeDtypeStruct((N_EXPERTS, t), F32)],
        compiler_params=_params(1),
        name="outproj",
    )(*fn, *a, *hg, *xs, mod, g, w_out, w_router_t)


def _route_body(lt_ref, bias_ref, o_ref):
    per = N_EXPERTS // N_GROUPS
    tt = lt_ref.shape[1]
    neg = -jnp.inf
    assert per == N_GROUPS == 8
    gi = lax.broadcasted_iota(jnp.int32, (N_GROUPS, tt), 0).astype(F32)
    s_j, b_j = [], []
    for j in range(per):
        s = _sigmoid(lt_ref[j * N_GROUPS:(j + 1) * N_GROUPS, :])
        s_j.append(s)
        b_j.append(s + bias_ref[j * N_GROUPS:(j + 1) * N_GROUPS, :])
    m1 = functools.reduce(jnp.maximum, b_j)
    i1 = functools.reduce(jnp.minimum, [jnp.where(b_j[j] == m1, float(j), float(per)) for j in range(per)])
    m2 = functools.reduce(jnp.maximum, [jnp.where(i1 == float(j), neg, b_j[j]) for j in range(per)])
    gs = m1 + m2
    gsel = jnp.zeros((N_GROUPS, tt), jnp.bool_)
    for _ in range(TOPK_GROUPS):
        m = gs.max(axis=0, keepdims=True)
        idx = jnp.where(gs == m, gi, float(N_GROUPS)).min(axis=0, keepdims=True)
        hit = gi == idx
        gsel = gsel | hit
        gs = jnp.where(hit, neg, gs)
    x_j = [jnp.where(gsel, b_j[j], neg) for j in range(per)]
    e_j = [gi * per + j for j in range(per)]
    sel_j = [jnp.zeros((N_GROUPS, tt), jnp.bool_) for _ in range(per)]
    for _ in range(TOP_K):
        m = functools.reduce(jnp.maximum, x_j).max(axis=0, keepdims=True)
        idx = functools.reduce(jnp.minimum, [jnp.where(x_j[j] == m, e_j[j], float(N_EXPERTS))
                                             for j in range(per)]).min(axis=0, keepdims=True)
        for j in range(per):
            hit = e_j[j] == idx
            sel_j[j] = sel_j[j] | hit
            x_j[j] = jnp.where(hit, neg, x_j[j])
    w_j = [jnp.where(sel_j[j], s_j[j], 0.0) for j in range(per)]
    denom = functools.reduce(lambda a, b: a + b, w_j).sum(axis=0, keepdims=True)
    gates_t = jnp.concatenate([w / denom * ROUTED_SCALE for w in w_j], axis=0)
    r_io = lax.broadcasted_iota(jnp.int32, (N_EXPERTS, N_EXPERTS), 0)
    e_io = lax.broadcasted_iota(jnp.int32, (N_EXPERTS, N_EXPERTS), 1)
    eye = (e_io == (r_io & (N_GROUPS - 1)) * per + (r_io >> 3)).astype(BF16)
    p = _split(gates_t, 3)
    o_ref[...] = _dot_tn(p[0], eye) + _dot_tn(p[1], eye) + _dot_tn(p[2], eye)


def _route(logits_t, bias):
    t = logits_t.shape[1]
    tt = ROUTE_TILE
    return pl.pallas_call(
        _route_body,
        grid=(t // tt,),
        in_specs=[pl.BlockSpec((N_EXPERTS, tt), lambda i: (0, i)),
                  pl.BlockSpec((N_EXPERTS, 1), lambda i: (0, 0))],
        out_specs=pl.BlockSpec((tt, N_EXPERTS), lambda i: (i, 0)),
        out_shape=jax.ShapeDtypeStruct((t, N_EXPERTS), F32),
        compiler_params=_params(1),
        name="route",
    )(logits_t, bias)


def _moe_body(final_norm, n_ctx_tiles, h_ref, gate_ref, wg_ref, wu_ref, wd_ref, sg_ref, su_ref, sd_ref,
              x_ref, mod_ref, fg_ref, *out_and_scratch):
    acc_ref = out_and_scratch[-1]
    o_refs = out_and_scratch[:-1]
    is_ctx = pl.program_id(0) < n_ctx_tiles
    j = pl.program_id(1)
    h = h_ref[...]

    def act_of(wg, wu, gate):
        gu = _dot(h, jnp.concatenate([wg.astype(BF16), wu.astype(BF16)], axis=1))
        a = _silu(gu[:, :D_EXPERT]) * gu[:, D_EXPERT:]
        return a if gate is None else a * gate

    @pl.when(j == 0)
    def _():
        acc_ref[...] = _dot(act_of(sg_ref[...], su_ref[...], None).astype(BF16), sd_ref[...].astype(BF16))

    gates = gate_ref[...]
    expert_of_lane = lax.broadcasted_iota(jnp.int32, gates.shape, 1)

    def gate_col(p):
        e = j * EXPERTS_PER_STEP + p
        return jnp.sum(jnp.where(expert_of_lane == e, gates, 0.0), axis=1, keepdims=True)

    for p in range(EXPERTS_PER_STEP // 2):
        a0 = act_of(wg_ref[2 * p], wu_ref[2 * p], gate_col(2 * p))
        a1 = act_of(wg_ref[2 * p + 1], wu_ref[2 * p + 1], gate_col(2 * p + 1))
        pair = jnp.concatenate([a0, a1], axis=1).astype(BF16)
        acc_ref[...] += _dot(pair, wd_ref[p].astype(BF16))

    @pl.when(j == pl.num_programs(1) - 1)
    def _():
        x = x_ref[...] + mod_ref[5:6, :] * acc_ref[...]
        if not final_norm:
            o_refs[0][...] = x
        else:
            ms = jnp.mean(x * x, axis=-1, keepdims=True)
            y = x * lax.rsqrt(ms + EPS) * fg_ref[...]

            @pl.when(is_ctx)
            def _():
                o_refs[0][...] = y

            @pl.when(jnp.logical_not(is_ctx))
            def _():
                o_refs[1][...] = y


def _moe(h, gates, w_gate, w_up, w_down2, ws_gate, ws_up, ws_down, layer, x, mod, final_g, final_norm,
         n_ctx_tok, lat_len):
    t = x.shape[0]
    tm = MOE_TILE
    eps_ = EXPERTS_PER_STEP
    n_ctx_tiles = n_ctx_tok // tm
    row = _mod_row(n_ctx_tiles, lat_len // tm)
    if final_norm:
        out_specs = _row_specs(2, tm, D_MODEL, n_ctx_tiles)
        out_shape = [jax.ShapeDtypeStruct((n_ctx_tok, D_MODEL), F32),
                     jax.ShapeDtypeStruct((t - n_ctx_tok, D_MODEL), F32)]
    else:
        out_specs = [pl.BlockSpec((tm, D_MODEL), lambda i, j: (i, 0))]
        out_shape = [jax.ShapeDtypeStruct((t, D_MODEL), F32)]
    return pl.pallas_call(
        functools.partial(_moe_body, final_norm, n_ctx_tiles),
        grid=(t // tm, N_EXPERTS // eps_),
        in_specs=[
            pl.BlockSpec((tm, D_MODEL), lambda i, j: (i, 0)),
            pl.BlockSpec((tm, N_EXPERTS), lambda i, j: (i, 0)),
            pl.BlockSpec((None, eps_, D_MODEL, D_EXPERT), lambda i, j: (layer, j, 0, 0)),
            pl.BlockSpec((None, eps_, D_MODEL, D_EXPERT), lambda i, j: (layer, j, 0, 0)),
            pl.BlockSpec((None, eps_ // 2, 2 * D_EXPERT, D_MODEL), lambda i, j: (layer, j, 0, 0)),
            pl.BlockSpec((None, D_MODEL, D_EXPERT), lambda i, j: (layer, 0, 0)),
            pl.BlockSpec((None, D_MODEL, D_EXPERT), lambda i, j: (layer, 0, 0)),
            pl.BlockSpec((None, D_EXPERT, D_MODEL), lambda i, j: (layer, 0, 0)),
            pl.BlockSpec((tm, D_MODEL), lambda i, j: (i, 0)),
            pl.BlockSpec((None, 6, D_MODEL), lambda i, j: (row(i), 0, 0)),
            pl.BlockSpec((1, D_MODEL), lambda i, j: (0, 0)),
        ],
        out_specs=out_specs,
        out_shape=out_shape,
        scratch_shapes=[pltpu.VMEM((tm, D_MODEL), F32)],
        compiler_params=_params(2),
        name="moe",
    )(h, gates, w_gate, w_up, w_down2, ws_gate, ws_up, ws_down, x, mod, final_g)


def _block_diag_t(s):
    eye = jnp.eye(HG_HEADS, dtype=s.dtype)
    out = jnp.einsum('...hkv,hg->...hvgk', s, eye)
    return out.reshape(s.shape[:-3] + (HG_WIDTH, HG_WIDTH))


def kernel(x_prompt, x_sample, cache_k, cache_v, state_hgrn, c, c_ctx, w_ada, b_ada, norm_g, w_in,
           w_fourier, lambdas, attn_norm_g, lower_bounds, hg_norm_g, w_out, w_router, router_bias,
           w_gate, w_up, w_down, ws_gate, ws_up, ws_down, final_g):
    n_ctx, ctx_len, _ = x_prompt.shape
    n_lat, lat_len, _ = x_sample.shape
    n_ctx_tok = n_ctx * ctx_len
    past = cache_k.shape[2]

    xs = (x_prompt.reshape(n_ctx_tok, D_MODEL), x_sample.reshape(n_lat * lat_len, D_MODEL))

    c8 = jnp.zeros((8, D_MODEL), F32).at[0].set(c_ctx).at[1:1 + n_lat].set(c)
    mods = _ada_mods(c8, w_ada, b_ada).reshape(DEPTH, 8, 6, D_MODEL)

    cs = jnp.cumsum(jax.nn.softmax(lower_bounds.astype(F32), axis=0), axis=0)
    lbs = cs - cs[0:1]

    cos, sin = _rope_tables(lat_len)
    cache_k4 = cache_k.reshape(n_lat, DEPTH, past, DA_WIDTH)
    cache_v4 = cache_v.reshape(n_lat, DEPTH, past, DA_WIDTH)
    s0t = _block_diag_t(state_hgrn.astype(F32))

    w_in_b = jnp.concatenate([w_in[:, :, FN_WIDTH:], w_in[:, :, :FN_WIDTH]], axis=-1).astype(BF16)
    w_f_b = w_fourier.astype(BF16)
    w_out_b = w_out.astype(BF16)
    per = N_EXPERTS // N_GROUPS
    w_router_t = (jnp.swapaxes(w_router, 1, 2).reshape(DEPTH, N_GROUPS, per, D_MODEL)
                  .swapaxes(1, 2).reshape(DEPTH, N_EXPERTS, D_MODEL))
    bias_mm = router_bias.reshape(DEPTH, N_GROUPS, per).swapaxes(1, 2).reshape(DEPTH, N_EXPERTS, 1)
    w_down2 = w_down.reshape(DEPTH, N_EXPERTS // 2, 2 * D_EXPERT, D_MODEL)
    ng = jnp.tile(hg_norm_g, (1, HG_HEADS))

    caches = None
    new_s = []
    for l in range(DEPTH):
        lam_init = 0.8 - 0.6 * math.exp(-0.3 * l)
        mod = mods[l]
        proj, *caches = _inproj(xs, mod, norm_g[l, 0:1], w_in_b[l], l, caches, n_ctx, ctx_len, lat_len)

        fn = (_fourier(proj, 0, n_ctx, ctx_len, w_f_b[l]),
              _fourier(proj, n_ctx_tok, n_lat, lat_len, w_f_b[l]))
        ag = attn_norm_g[l].reshape(1, DA_VDIM)
        a = (_attn_ctx(proj, n_ctx, ctx_len, lambdas[l], ag, lam_init),
             _attn_lat(proj, n_ctx_tok, n_lat, lat_len, cache_k4, cache_v4, l, cos, sin, lambdas[l], ag, lam_init))
        hg_ctx, st_ctx = _hgrn(proj, 0, n_ctx, ctx_len, lbs[l], ng[l:l + 1], None)
        (hg_lat,) = _hgrn(proj, n_ctx_tok, n_lat, lat_len, lbs[l], ng[l:l + 1], s0t[:, l])
        new_s.append(jnp.swapaxes(st_ctx, -1, -2))

        x, h2, logits_t = _outproj(fn, a, (hg_ctx, hg_lat), xs, mod, norm_g[l, 1:2], w_out_b[l],
                                   w_router_t[l], n_ctx_tok, lat_len)
        gates = _route(logits_t, bias_mm[l])
        xs = _moe(h2, gates, w_gate, w_up, w_down2, ws_gate, ws_up, ws_down, l, x, mod,
                  final_g.reshape(1, D_MODEL), l == DEPTH - 1, n_ctx_tok, lat_len)

    y_prompt = xs[0].reshape(n_ctx, ctx_len, D_MODEL)
    y_sample = xs[1].reshape(n_lat, lat_len, D_MODEL)
    new_k = caches[0].reshape(n_ctx, DEPTH, ctx_len, DA_HEADS, DA_VDIM)
    new_v = caches[1].reshape(n_ctx, DEPTH, ctx_len, DA_HEADS, DA_VDIM)
    return (y_prompt, y_sample, new_k, new_v, jnp.stack(new_s, axis=1))
```

```python
import functools
import math

import numpy as np
import jax
import jax.numpy as jnp
from jax import lax
from jax.experimental import pallas as pl
from jax.experimental.pallas import tpu as pltpu

F32 = jnp.float32
BF16 = jnp.bfloat16

D_MODEL = 1024
DEPTH = 2
GRID_W = 64
FN_WIDTH = 256
FN_GROUPS = 4
FN_GROUP_DIM = 64
DA_WIDTH = 512
DA_HEADS = 4
DA_VDIM = 128
DA_HALF = 64
HG_WIDTH = 256
HG_HEADS = 4
HG_DK = 64
PROJ_WIDTH = 3072
CHUNK = 64
ROPE_THETA = 10000.0
N_EXPERTS = 64
TOP_K = 8
N_GROUPS = 8
TOPK_GROUPS = 4
D_EXPERT = 128
ROUTED_SCALE = 2.5
EPS = 1e-6

COL_Q, COL_K, COL_V = 0, 512, 1024
COL_HQ, COL_HI, COL_HF, COL_HB, COL_HG = 1536, 1792, 2048, 2304, 2560
COL_FN = 2816

TOKEN_TILE = 256
MOE_TILE = 1024
EXPERTS_PER_STEP = 4
ROUTE_TILE = 512
CHUNKS_PER_GROUP = 4
VMEM_LIMIT = 56 * 1024 * 1024


def _dot(a, b):
    return jnp.dot(a, b, preferred_element_type=F32)


def _dot_nt(a, b):
    return lax.dot_general(a, b, (((1,), (1,)), ((), ())), preferred_element_type=F32)


def _dot_tn(a, b):
    return lax.dot_general(a, b, (((0,), (0,)), ((), ())), preferred_element_type=F32)


def _split(x, n):
    parts = []
    r = x
    for i in range(n):
        p = r.astype(BF16)
        parts.append(p)
        if i + 1 < n:
            r = r - p.astype(F32)
    return parts


def _sigmoid(x):
    return 1.0 / (1.0 + jnp.exp(-x))


def _silu(x):
    return x * _sigmoid(x)


def _params(n_axes):
    return pltpu.CompilerParams(dimension_semantics=("arbitrary",) * n_axes,
                                vmem_limit_bytes=VMEM_LIMIT)


def _ada_body(c_ref, w_ref, b_ref, o_ref):
    a = _silu(c_ref[...])
    a_hi, a_lo = _split(a, 2)
    w_hi, w_lo = _split(w_ref[...], 2)
    o_ref[...] = _dot(a_hi, w_hi) + _dot(a_lo, w_hi) + _dot(a_hi, w_lo) + b_ref[...]


def _ada_mods(c8, w_ada, b_ada):
    tn = 1536
    return pl.pallas_call(
        _ada_body,
        grid=(DEPTH, 6 * D_MODEL // tn),
        in_specs=[
            pl.BlockSpec((8, D_MODEL), lambda l, j: (0, 0)),
            pl.BlockSpec((None, D_MODEL, tn), lambda l, j: (l, 0, j)),
            pl.BlockSpec((None, 1, tn), lambda l, j: (l, 0, j)),
        ],
        out_specs=pl.BlockSpec((None, 8, tn), lambda l, j: (l, 0, j)),
        out_shape=jax.ShapeDtypeStruct((DEPTH, 8, 6 * D_MODEL), F32),
        compiler_params=_params(2),
        name="ada_mods",
    )(c8, w_ada, b_ada.reshape(DEPTH, 1, 6 * D_MODEL))


def _modnorm(x, g, shift, scale):
    ms = jnp.mean(x * x, axis=-1, keepdims=True)
    return (x * lax.rsqrt(ms + EPS) * g) * (1.0 + scale) + shift


def _mod_row(n_ctx_tiles, tiles_per_latent):
    def f(i):
        return jnp.where(i < n_ctx_tiles, 0, 1 + (i - n_ctx_tiles) // tiles_per_latent)
    return f


def _row_specs(n_parts, tm, width, n_ctx_tiles):
    if n_parts == 1:
        return [pl.BlockSpec((tm, width), lambda i, *_: (i, 0))]
    return [pl.BlockSpec((tm, width), lambda i, *_: (jnp.minimum(i, n_ctx_tiles - 1), 0)),
            pl.BlockSpec((tm, width), lambda i, *_: (jnp.maximum(i - n_ctx_tiles, 0), 0))]


def _pick(is_ctx, refs):
    if len(refs) == 1:
        return refs[0][...]
    return jnp.where(is_ctx, refs[0][...], refs[1][...])


def _inproj_body(n_x, n_prev, n_ctx_tiles, *refs):
    x_refs = refs[:n_x]
    mod_ref, g_ref, w_ref = refs[n_x:n_x + 3]
    prev_refs = refs[n_x + 3:n_x + 3 + 2 * n_prev]
    o_ref, k_ref, v_ref = refs[-3:]
    is_ctx = pl.program_id(0) < n_ctx_tiles
    x = _pick(is_ctx, x_refs)
    h = _modnorm(x, g_ref[...], mod_ref[0:1, :], mod_ref[1:2, :])
    proj = _dot(h.astype(BF16), w_ref[...])
    o_ref[...] = proj

    @pl.when(is_ctx)
    def _():
        n_tok = proj.shape[0]
        per_layer = n_tok * DA_HEADS
        for l in range(n_prev):
            k_ref[l * per_layer:(l + 1) * per_layer, :] = prev_refs[2 * l][...]
            v_ref[l * per_layer:(l + 1) * per_layer, :] = prev_refs[2 * l + 1][...]
        for h in range(DA_HEADS):
            rows = pl.ds(n_prev * per_layer + h, n_tok, stride=DA_HEADS)
            k_ref[rows, :] = proj[:, COL_K + h * DA_VDIM:COL_K + (h + 1) * DA_VDIM]
            v_ref[rows, :] = proj[:, COL_V + h * DA_VDIM:COL_V + (h + 1) * DA_VDIM]


def _inproj(xs, mod, g, w, prev_caches, n_ctx, ctx_len, lat_len):
    t = sum(x.shape[0] for x in xs)
    tm = TOKEN_TILE
    assert ctx_len == tm
    n_ctx_tiles = n_ctx
    n_prev = len(prev_caches) // 2
    row = _mod_row(n_ctx_tiles, lat_len // tm)
    seq_rows = ctx_len * DA_HEADS
    seq_block = lambda i: (jnp.minimum(i, n_ctx - 1), 0)
    cache_spec = pl.BlockSpec(((n_prev + 1) * seq_rows, DA_VDIM), seq_block)
    cache_shape = jax.ShapeDtypeStruct((n_ctx * (n_prev + 1) * seq_rows, DA_VDIM), F32)
    in_specs = _row_specs(len(xs), tm, D_MODEL, n_ctx_tiles) + [
        pl.BlockSpec((None, 6, D_MODEL), lambda i: (row(i), 0, 0)),
        pl.BlockSpec((1, D_MODEL), lambda i: (0, 0)),
        pl.BlockSpec((D_MODEL, PROJ_WIDTH), lambda i: (0, 0)),
    ] + [pl.BlockSpec((seq_rows, DA_VDIM), seq_block)] * (2 * n_prev)
    return pl.pallas_call(
        functools.partial(_inproj_body, len(xs), n_prev, n_ctx_tiles),
        grid=(t // tm,),
        in_specs=in_specs,
        out_specs=[pl.BlockSpec((tm, PROJ_WIDTH), lambda i: (i, 0)), cache_spec, cache_spec],
        out_shape=[jax.ShapeDtypeStruct((t, PROJ_WIDTH), F32), cache_shape, cache_shape],
        compiler_params=_params(1),
        name="inproj",
    )(*xs, mod, g, w, *prev_caches)


def _fourier_body(u_ref, cl_ref, sl_ref, cc_ref, sc_ref, w_ref, o_ref):
    z = u_ref[...].astype(BF16)
    a = _dot(z, cc_ref[...]).astype(BF16)
    b = _dot(z, sc_ref[...]).astype(BF16)
    y = _dot(cl_ref[...], a) - _dot(sl_ref[...], b)
    o_ref[...] = _dot(y.astype(BF16), w_ref[...]).astype(o_ref.dtype)


def _dft_tables(n, block):
    i = np.arange(n)
    prod = (i[:, None] % block) * (i[None, :] % block) % block
    ang = prod.astype(np.float64) * (2.0 * math.pi / block)
    same = (i[:, None] // block) == (i[None, :] // block)
    scale = 1.0 / math.sqrt(block)
    c = np.where(same, np.cos(ang) * scale, 0.0).astype(np.float32)
    s = np.where(same, np.sin(ang) * scale, 0.0).astype(np.float32)
    return jnp.asarray(c).astype(BF16), jnp.asarray(s).astype(BF16)


def _fourier(proj, row0, n_seq, seq_len, w_f):
    cl, sl = _dft_tables(seq_len, seq_len)
    cc, sc = _dft_tables(FN_WIDTH, FN_GROUP_DIM)
    blk0 = row0 // seq_len
    full = lambda shape: pl.BlockSpec(shape, lambda i: (0, 0))
    return pl.pallas_call(
        _fourier_body,
        grid=(n_seq,),
        in_specs=[
            pl.BlockSpec((seq_len, FN_WIDTH), lambda i: (blk0 + i, COL_FN // FN_WIDTH)),
            full((seq_len, seq_len)), full((seq_len, seq_len)),
            full((FN_WIDTH, FN_WIDTH)), full((FN_WIDTH, FN_WIDTH)), full((FN_WIDTH, FN_WIDTH)),
        ],
        out_specs=pl.BlockSpec((seq_len, FN_WIDTH), lambda i: (i, 0)),
        out_shape=jax.ShapeDtypeStruct((n_seq * seq_len, FN_WIDTH), BF16),
        compiler_params=_params(1),
        name=f"fourier_{seq_len}",
    )(proj, cl, sl, cc, sc, w_f)


def _lambda_full(lmb, lam_init):
    a = jnp.sum(lmb[0:1, :] * lmb[1:2, :], axis=-1, keepdims=True)
    b = jnp.sum(lmb[2:3, :] * lmb[3:4, :], axis=-1, keepdims=True)
    return jnp.exp(a) - jnp.exp(b) + lam_init


def _softmax_parts(parts):
    m = parts[0].max(axis=-1, keepdims=True)
    for p in parts[1:]:
        m = jnp.maximum(m, p.max(axis=-1, keepdims=True))
    es = [jnp.exp(p - m) for p in parts]
    tot = es[0].sum(axis=-1, keepdims=True)
    for e in es[1:]:
        tot = tot + e.sum(axis=-1, keepdims=True)
    return es, 1.0 / tot


def _diff_head(q, ks, vs, lam, g, lam_init):
    outs = []
    for m in range(2):
        qm = q[:, m * DA_HALF:(m + 1) * DA_HALF].astype(BF16)
        parts = [_dot_nt(qm, k[:, m * DA_HALF:(m + 1) * DA_HALF].astype(BF16)) for k in ks]
        es, inv = _softmax_parts(parts)
        o = _dot(es[0].astype(BF16), vs[0].astype(BF16))
        for e, v in zip(es[1:], vs[1:]):
            o = o + _dot(e.astype(BF16), v.astype(BF16))
        outs.append(o * inv)
    a = outs[0] - lam * outs[1]
    ms = jnp.mean(a * a, axis=-1, keepdims=True)
    return a * lax.rsqrt(ms + EPS) * g * (1.0 - lam_init)


def _attn_ctx_body(lam_init, q_ref, k_ref, v_ref, lmb_ref, g_ref, o_ref):
    lam = _lambda_full(lmb_ref[...], lam_init)
    scale = DA_HALF ** -0.5
    for h in range(DA_HEADS):
        sl = slice(h * DA_VDIM, (h + 1) * DA_VDIM)
        o = _diff_head(q_ref[:, sl] * scale, [k_ref[:, sl]], [v_ref[:, sl]], lam, g_ref[...], lam_init)
        o_ref[:, sl] = o.astype(o_ref.dtype)


def _attn_ctx(proj, n_seq, seq_len, lmb, g, lam_init):
    blk = lambda c: pl.BlockSpec((seq_len, DA_WIDTH), lambda i: (i, c))
    return pl.pallas_call(
        functools.partial(_attn_ctx_body, lam_init),
        grid=(n_seq,),
        in_specs=[
            blk(COL_Q // DA_WIDTH), blk(COL_K // DA_WIDTH), blk(COL_V // DA_WIDTH),
            pl.BlockSpec((4, DA_HALF), lambda i: (0, 0)),
            pl.BlockSpec((1, DA_VDIM), lambda i: (0, 0)),
        ],
        out_specs=pl.BlockSpec((seq_len, DA_WIDTH), lambda i: (i, 0)),
        out_shape=jax.ShapeDtypeStruct((n_seq * seq_len, DA_WIDTH), BF16),
        compiler_params=_params(1),
        name="attn_ctx",
    )(proj, proj, proj, lmb, g)


def _rope(x, cos, sin):
    lane = lax.broadcasted_iota(jnp.int32, x.shape, 1)
    first = ((lane >> 4) & 1) == 0
    rot = jnp.where(first, -pltpu.roll(x, 128 - DA_HALF // 4, 1), pltpu.roll(x, DA_HALF // 4, 1))
    return x * cos + rot * sin


def _attn_lat_body(lam_init, q_ref, k_ref, v_ref, kc_ref, vc_ref, cq_ref, sq_ref, ck_ref, sk_ref,
                   lmb_ref, g_ref, o_ref):
    lam = _lambda_full(lmb_ref[...], lam_init)
    scale = DA_HALF ** -0.5
    for h in range(DA_HEADS):
        sl = slice(h * DA_VDIM, (h + 1) * DA_VDIM)
        q = _rope(q_ref[:, sl], cq_ref[...], sq_ref[...]) * scale
        k = _rope(k_ref[:, sl], ck_ref[...], sk_ref[...])
        o = _diff_head(q, [k, kc_ref[:, sl]], [v_ref[:, sl], vc_ref[:, sl]], lam, g_ref[...], lam_init)
        o_ref[:, sl] = o.astype(o_ref.dtype)


def _attn_lat(proj, row0, n_seq, seq_len, cache_k, cache_v, layer, cos, sin, lmb, g, lam_init):
    tq = 512
    nq = seq_len // tq
    past = cache_k.shape[2]
    qb0 = row0 // tq
    kb0 = row0 // seq_len
    cache_spec = pl.BlockSpec((None, None, past, DA_WIDTH), lambda b, j: (b, layer, 0, 0))
    kv_spec = lambda c: pl.BlockSpec((seq_len, DA_WIDTH), lambda b, j: (kb0 + b, c))
    return pl.pallas_call(
        functools.partial(_attn_lat_body, lam_init),
        grid=(n_seq, nq),
        in_specs=[
            pl.BlockSpec((tq, DA_WIDTH), lambda b, j: (qb0 + b * nq + j, COL_Q // DA_WIDTH)),
            kv_spec(COL_K // DA_WIDTH), kv_spec(COL_V // DA_WIDTH),
            cache_spec, cache_spec,
            pl.BlockSpec((tq, DA_VDIM), lambda b, j: (j, 0)),
            pl.BlockSpec((tq, DA_VDIM), lambda b, j: (j, 0)),
            pl.BlockSpec((seq_len, DA_VDIM), lambda b, j: (0, 0)),
            pl.BlockSpec((seq_len, DA_VDIM), lambda b, j: (0, 0)),
            pl.BlockSpec((4, DA_HALF), lambda b, j: (0, 0)),
            pl.BlockSpec((1, DA_VDIM), lambda b, j: (0, 0)),
        ],
        out_specs=pl.BlockSpec((tq, DA_WIDTH), lambda b, j: (b * nq + j, 0)),
        out_shape=jax.ShapeDtypeStruct((n_seq * seq_len, DA_WIDTH), BF16),
        compiler_params=_params(2),
        name="attn_lat",
    )(proj, proj, proj, cache_k, cache_v, cos, sin, cos, sin, lmb, g)


def _rope_tables(n_tokens):
    rows = n_tokens // GRID_W
    row = np.repeat(np.arange(rows, dtype=np.float64), GRID_W)
    col = np.tile(np.arange(GRID_W, dtype=np.float64), rows)
    axis_dim = DA_HALF // 2
    inv_freq = ROPE_THETA ** (-np.arange(0, axis_dim, 2, dtype=np.float64) / axis_dim)
    ang_r = row[:, None] * inv_freq[None, :]
    ang_c = col[:, None] * inv_freq[None, :]
    ang = np.concatenate([ang_r, ang_r, ang_c, ang_c] * 2, axis=-1)
    return jnp.asarray(np.cos(ang).astype(np.float32)), jnp.asarray(np.sin(ang).astype(np.float32))


def _hgrn_body(n_chunks, has_s0, *refs):
    if has_s0:
        (hq_ref, hi_ref, hf_ref, hb_ref, hg_ref, lb_ref, ng_ref, s0_ref,
         o_ref, oi_scr, qe_scr, u_scr, st_scr, dec_scr) = refs
        so_ref = None
    else:
        (hq_ref, hi_ref, hf_ref, hb_ref, hg_ref, lb_ref, ng_ref,
         o_ref, so_ref, oi_scr, qe_scr, u_scr, st_scr, dec_scr) = refs
    c = CHUNK
    w = HG_WIDTH
    gc = CHUNKS_PER_GROUP
    gl = gc * c
    c_bits = c.bit_length() - 1
    r_io = lax.broadcasted_iota(jnp.int32, (gl, gl), 0)
    c_io = lax.broadcasted_iota(jnp.int32, (gl, gl), 1)
    same_chunk = (r_io >> c_bits) == (c_io >> c_bits)
    tri_f = (same_chunk & (c_io <= r_io)).astype(BF16)
    tri_b = (same_chunk & (c_io >= r_io)).astype(BF16)
    lane = lax.broadcasted_iota(jnp.int32, (1, w), 1)
    dk_bits = HG_DK.bit_length() - 1
    head_masks = [((lane >> dk_bits) == h).astype(F32) for h in range(HG_HEADS)]
    head_masks_b = [hm.astype(BF16) for hm in head_masks]
    bd = ((lax.broadcasted_iota(jnp.int32, (w, w), 0) >> dk_bits)
          == (lax.broadcasted_iota(jnp.int32, (w, w), 1) >> dk_bits))
    t_of_row = lax.broadcasted_iota(jnp.int32, (HG_HEADS * gl, gl), 0) & (gl - 1)
    s_of_col = lax.broadcasted_iota(jnp.int32, (HG_HEADS * gl, gl), 1)
    same = (t_of_row >> c_bits) == (s_of_col >> c_bits)
    causal_f = same & (s_of_col <= t_of_row)
    causal_b = same & (s_of_col >= t_of_row)

    dirs = ((hf_ref, tri_f, causal_f, c - 1), (hb_ref, tri_b, causal_b, 0))

    def per_chunk_row(x, row):
        return jnp.concatenate([jnp.broadcast_to(x[k * c + row:k * c + row + 1, :], (c, w)) for k in range(gc)],
                               axis=0)

    def group_terms(gi):
        rows = pl.ds(gi * gl if isinstance(gi, int) else pl.multiple_of(gi * gl, gl), gl)
        q = _silu(hq_ref[rows, :]) * (HG_DK ** -0.5)
        v = hi_ref[rows, :].astype(BF16)
        for d, (fp_ref, tri, causal, edge_row) in enumerate(dirs):
            lb = lb_ref[d:d + 1, :]
            fp = fp_ref[rows, :]
            lsig = jnp.minimum(fp, 0.0) - jnp.log1p(jnp.exp(-jnp.abs(fp)))
            la = jnp.log(lb)
            lbb = jnp.log1p(-lb) + lsig
            logf = jnp.maximum(la, lbb) + jnp.log1p(jnp.exp(-jnp.abs(la - lbb)))
            kk = (1.0 - lb) * (1.0 / (1.0 + jnp.exp(fp)))
            g3 = _split(logf, 3)
            cum = _dot(tri, g3[0]) + _dot(tri, g3[1]) + _dot(tri, g3[2])
            total = per_chunk_row(cum, edge_row)
            ref = per_chunk_row(cum, c // 2)
            qc = (q * jnp.exp(cum - ref)).astype(BF16)
            kc = (kk * jnp.exp(ref - cum)).astype(BF16)
            ke = (kk * jnp.exp(total - cum)).astype(BF16)
            qe_scr[d, rows, :] = (q * jnp.exp(cum)).astype(BF16)
            lhs = jnp.concatenate([qc * hm for hm in head_masks_b], axis=0)
            a = jnp.where(causal, _dot_nt(lhs, kc), 0.0).astype(BF16)
            o_stack = _dot(a, v)
            o = o_stack[0:gl, :] * head_masks[0]
            for h in range(1, HG_HEADS):
                o = o + o_stack[h * gl:(h + 1) * gl, :] * head_masks[h]
            oi_scr[d, rows, :] = o
            for k in range(gc):
                ck = slice(k * c, (k + 1) * c)
                i = gi * gc + k
                dec_scr[d, pl.ds(i, 1), :] = jnp.exp(cum[k * c + edge_row:k * c + edge_row + 1, :])
                u_scr[d, i] = jnp.where(bd, _dot_tn(v[ck, :], ke[ck, :]), 0.0)

    if n_chunks == gc:
        group_terms(0)
    else:
        def terms_step(gi, carry):
            group_terms(gi)
            return carry

        lax.fori_loop(0, n_chunks // gc, terms_step, 0)

    slab = 32
    for d in range(2):
        order = range(n_chunks) if d == 0 else range(n_chunks - 1, -1, -1)
        for r in range(w // slab):
            rs = slice(r * slab, (r + 1) * slab)
            st = s0_ref[d, rs, :] if has_s0 else jnp.zeros((slab, w), F32)
            for i in order:
                st_scr[d, i, rs, :] = st.astype(BF16)
                st = st * dec_scr[d, i:i + 1, :] + u_scr[d, i, rs, :]
            if so_ref is not None:
                h = (r * slab) // HG_DK
                off = (r * slab) % HG_DK
                so_ref[d, h, off:off + slab, :] = st[:, h * HG_DK:(h + 1) * HG_DK]

    ones_bd = bd.astype(BF16)

    def finish(i, carry):
        rows = pl.ds(pl.multiple_of(i * c, c), c)
        o = (oi_scr[0, rows, :] + oi_scr[1, rows, :]
             + _dot_nt(qe_scr[0, rows, :], st_scr[0, i]) + _dot_nt(qe_scr[1, rows, :], st_scr[1, i]))
        sq = _split(o * o, 2)
        ms = (_dot(sq[0], ones_bd) + _dot(sq[1], ones_bd)) * (1.0 / HG_DK)
        y = o * lax.rsqrt(ms + EPS) * ng_ref[...]
        o_ref[rows, :] = (y * _silu(hg_ref[rows, :])).astype(o_ref.dtype)
        return carry

    lax.fori_loop(0, n_chunks, finish, 0, unroll=2)


def _hgrn(proj, row0, n_seq, seq_len, lb, ng, s0t):
    n_chunks = seq_len // CHUNK
    blk0 = row0 // seq_len
    col = lambda c0: pl.BlockSpec((seq_len, HG_WIDTH), lambda i: (blk0 + i, c0 // HG_WIDTH))
    in_specs = [col(COL_HQ), col(COL_HI), col(COL_HF), col(COL_HB), col(COL_HG),
                pl.BlockSpec((2, HG_WIDTH), lambda i: (0, 0)),
                pl.BlockSpec((1, HG_WIDTH), lambda i: (0, 0))]
    args = [proj, proj, proj, proj, proj, lb, ng]
    out_specs = [pl.BlockSpec((seq_len, HG_WIDTH), lambda i: (i, 0))]
    out_shape = [jax.ShapeDtypeStruct((n_seq * seq_len, HG_WIDTH), BF16)]
    if s0t is not None:
        in_specs.append(pl.BlockSpec((None, 2, HG_WIDTH, HG_WIDTH), lambda i: (i, 0, 0, 0)))
        args.append(s0t)
    else:
        out_specs.append(pl.BlockSpec((None, 2, HG_HEADS, HG_DK, HG_DK), lambda i: (i, 0, 0, 0, 0)))
        out_shape.append(jax.ShapeDtypeStruct((n_seq, 2, HG_HEADS, HG_DK, HG_DK), F32))
    return pl.pallas_call(
        functools.partial(_hgrn_body, n_chunks, s0t is not None),
        grid=(n_seq,),
        in_specs=in_specs,
        out_specs=out_specs,
        out_shape=out_shape,
        scratch_shapes=[pltpu.VMEM((2, seq_len, HG_WIDTH), F32),
                        pltpu.VMEM((2, seq_len, HG_WIDTH), BF16),
                        pltpu.VMEM((2, n_chunks, HG_WIDTH, HG_WIDTH), F32),
                        pltpu.VMEM((2, n_chunks, HG_WIDTH, HG_WIDTH), BF16),
                        pltpu.VMEM((2, max(n_chunks, 8), HG_WIDTH), F32)],
        compiler_params=_params(1),
        name=f"hgrn_{seq_len}",
    )(*args)


def _sigmoid_t(x):
    return 0.5 * jnp.tanh(0.5 * x) + 0.5


def _hgrn_tables():
    gl = CHUNKS_PER_GROUP * CHUNK
    t = np.arange(gl)
    same = (t[:, None] // CHUNK) == (t[None, :] // CHUNK)
    fwd = same & (t[None, :] <= t[:, None])
    bwd = same & (t[None, :] >= t[:, None])
    f = np.arange(HG_WIDTH)
    bd = (f[:, None] // HG_DK) == (f[None, :] // HG_DK)
    tri = jnp.asarray(np.stack([fwd, bwd]).astype(np.float32)).astype(BF16)
    causal = jnp.asarray(np.stack([fwd, bwd]).astype(np.float32))
    return tri, causal, jnp.asarray(bd.astype(np.float32))


def _hgrn2_body(n_seq, groups_per_seq, has_s0, *refs):
    (hq_ref, hi_ref, hf_ref, hb_ref, hg_ref, lb_ref, ng_ref, tri_ref, causal_ref, bd_ref) = refs[:10]
    if has_s0:
        s0_ref, o_ref, oi_scr, qe_scr, u_scr, st_scr, dec_scr = refs[10:]
        so_ref = None
    else:
        o_ref, so_ref, oi_scr, qe_scr, u_scr, st_scr, dec_scr = refs[10:]
        s0_ref = None
    c = CHUNK
    w = HG_WIDTH
    gc = CHUNKS_PER_GROUP
    gl = gc * c
    n_groups = n_seq * groups_per_seq
    chunks_per_seq = groups_per_seq * gc
    lane = lax.broadcasted_iota(jnp.int32, (1, w), 1)
    dk_bits = HG_DK.bit_length() - 1
    head_masks_b = [((lane >> dk_bits) == h).astype(BF16) for h in range(HG_HEADS)]
    bd = bd_ref[...] > 0.5
    edge_rows = (c - 1, 0)
    fp_refs = (hf_ref, hb_ref)

    def per_chunk_row(x, row):
        return jnp.concatenate([jnp.broadcast_to(x[k * c + row:k * c + row + 1, :], (c, w)) for k in range(gc)],
                               axis=0)

    def group_terms(gi):
        rows = pl.ds(gi * gl if isinstance(gi, int) else pl.multiple_of(gi * gl, gl), gl)
        hq = hq_ref[rows, :]
        q = hq * _sigmoid_t(hq) * (HG_DK ** -0.5)
        v = hi_ref[rows, :].astype(BF16)
        v_stack = jnp.concatenate([v * hm for hm in head_masks_b], axis=0)
        for d in range(2):
            lb = lb_ref[d:d + 1, :]
            fp = fp_refs[d][rows, :]
            lsig = jnp.minimum(fp, 0.0) - jnp.log(1.0 + jnp.exp(-jnp.abs(fp)))
            la = jnp.log(lb)
            lbb = jnp.log(1.0 - lb) + lsig
            logf = jnp.maximum(la, lbb) + jnp.log(1.0 + jnp.exp(-jnp.abs(la - lbb)))
            kk = (1.0 - lb) * _sigmoid_t(-fp)
            g_hi, g_lo = _split(logf, 2)
            cum = _dot(tri_ref[d], g_hi) + _dot(tri_ref[d], g_lo)
            total = per_chunk_row(cum, edge_rows[d])
            ref = per_chunk_row(cum, c // 2)
            qc = (q * jnp.exp(cum - ref)).astype(BF16)
            kc = (kk * jnp.exp(ref - cum)).astype(BF16)
            ke = (kk * jnp.exp(total - cum)).astype(BF16)
            qe_scr[d, rows, :] = (q * jnp.exp(cum)).astype(BF16)
            kc_stack = jnp.concatenate([kc * hm for hm in head_masks_b], axis=0)
            a = _dot_nt(qc, kc_stack)
            keep = causal_ref[d] > 0.5
            a = jnp.concatenate([jnp.where(keep, a[:, h * gl:(h + 1) * gl], 0.0) for h in range(HG_HEADS)],
                                axis=1).astype(BF16)
            oi_scr[d, rows, :] = _dot(a, v_stack)
            for k in range(gc):
                ck = slice(k * c, (k + 1) * c)
                i = gi * gc + k
                dec_scr[d, pl.ds(i, 1), :] = jnp.exp(cum[k * c + edge_rows[d]:k * c + edge_rows[d] + 1, :])
                u_scr[d, i] = jnp.where(bd, _dot_tn(v[ck, :], ke[ck, :]), 0.0)

    if n_groups <= 2:
        for gi in range(n_groups):
            group_terms(gi)
    else:
        def terms_step(gi, carry):
            group_terms(gi)
            return carry

        lax.fori_loop(0, n_groups, terms_step, 0)

    slab = 32
    for s in range(n_seq):
        first = s * chunks_per_seq
        for d in range(2):
            order = range(chunks_per_seq) if d == 0 else range(chunks_per_seq - 1, -1, -1)
            for r in range(w // slab):
                rs = slice(r * slab, (r + 1) * slab)
                st = s0_ref[s, d, rs, :] if has_s0 else jnp.zeros((slab, w), F32)
                for j in order:
                    i = first + j
                    st_scr[d, i, rs, :] = st.astype(BF16)
                    st = st * dec_scr[d, i:i + 1, :] + u_scr[d, i, rs, :]
                if so_ref is not None:
                    h = (r * slab) // HG_DK
                    off = (r * slab) % HG_DK
                    so_ref[s, d, h, off:off + slab, :] = st[:, h * HG_DK:(h + 1) * HG_DK]

    ones_bd = bd.astype(BF16)

    def finish(gi):
        rows = pl.ds(gi * gl if isinstance(gi, int) else pl.multiple_of(gi * gl, gl), gl)
        inter = []
        for k in range(gc):
            i = gi * gc + k
            ck = pl.ds(gi * gl + k * c if isinstance(gi, int) else pl.multiple_of(gi * gl + k * c, c), c)
            inter.append(_dot_nt(qe_scr[0, ck, :], st_scr[0, i]) + _dot_nt(qe_scr[1, ck, :], st_scr[1, i]))
        o = oi_scr[0, rows, :] + oi_scr[1, rows, :] + jnp.concatenate(inter, axis=0)
        sq = _split(o * o, 2)
        ms = (_dot(sq[0], ones_bd) + _dot(sq[1], ones_bd)) * (1.0 / HG_DK)
        y = o * lax.rsqrt(ms + EPS) * ng_ref[...]
        hg = hg_ref[rows, :]
        o_ref[rows, :] = (y * hg * _sigmoid_t(hg)).astype(o_ref.dtype)

    if n_groups <= 2:
        for gi in range(n_groups):
            finish(gi)
    else:
        def finish_step(gi, carry):
            finish(gi)
            return carry

        lax.fori_loop(0, n_groups, finish_step, 0)


def _hgrn2(proj, row0, n_seq, seq_len, seqs_per_step, lb, ng, s0t):
    gl = CHUNKS_PER_GROUP * CHUNK
    groups_per_seq = seq_len // gl
    n_chunks = seqs_per_step * seq_len // CHUNK
    rows = seqs_per_step * seq_len
    blk0 = row0 // rows
    tri, causal, bd = _hgrn_tables()
    col = lambda c0: pl.BlockSpec((rows, HG_WIDTH), lambda i: (blk0 + i, c0 // HG_WIDTH))
    const = lambda shape: pl.BlockSpec(shape, lambda i: (0,) * len(shape))
    in_specs = [col(COL_HQ), col(COL_HI), col(COL_HF), col(COL_HB), col(COL_HG),
                const((2, HG_WIDTH)), const((1, HG_WIDTH)),
                const((2, gl, gl)), const((2, gl, gl)), const((HG_WIDTH, HG_WIDTH))]
    args = [proj, proj, proj, proj, proj, lb, ng, tri, causal, bd]
    out_specs = [pl.BlockSpec((rows, HG_WIDTH), lambda i: (i, 0))]
    out_shape = [jax.ShapeDtypeStruct((n_seq * seq_len, HG_WIDTH), BF16)]
    if s0t is not None:
        in_specs.append(pl.BlockSpec((seqs_per_step, 2, HG_WIDTH, HG_WIDTH), lambda i: (i, 0, 0, 0)))
        args.append(s0t)
    else:
        out_specs.append(pl.BlockSpec((seqs_per_step, 2, HG_HEADS, HG_DK, HG_DK), lambda i: (i, 0, 0, 0, 0)))
        out_shape.append(jax.ShapeDtypeStruct((n_seq, 2, HG_HEADS, HG_DK, HG_DK), F32))
    return pl.pallas_call(
        functools.partial(_hgrn2_body, seqs_per_step, groups_per_seq, s0t is not None),
        grid=(n_seq // seqs_per_step,),
        in_specs=in_specs,
        out_specs=out_specs,
        out_shape=out_shape,
        scratch_shapes=[pltpu.VMEM((2, rows, HG_WIDTH), F32),
                        pltpu.VMEM((2, rows, HG_WIDTH), BF16),
                        pltpu.VMEM((2, n_chunks, HG_WIDTH, HG_WIDTH), F32),
                        pltpu.VMEM((2, n_chunks, HG_WIDTH, HG_WIDTH), BF16),
                        pltpu.VMEM((2, max(n_chunks, 8), HG_WIDTH), F32)],
        compiler_params=_params(1),
        name=f"hgrn_{seq_len}",
    )(*args)


def _outproj_body(n_x, n_ctx_tiles, *refs):
    fn_refs, a_refs, hg_refs = refs[0:2], refs[2:4], refs[4:6]
    x_refs = refs[6:6 + n_x]
    mod_ref, g_ref, w_ref, wr_ref, xo_ref, h_ref, lt_ref = refs[6 + n_x:]
    is_ctx = pl.program_id(0) < n_ctx_tiles
    mix = (_dot(_pick(is_ctx, fn_refs), w_ref[0:FN_WIDTH, :])
           + _dot(_pick(is_ctx, a_refs), w_ref[FN_WIDTH:FN_WIDTH + DA_WIDTH, :])
           + _dot(_pick(is_ctx, hg_refs), w_ref[FN_WIDTH + DA_WIDTH:, :]))
    x = _pick(is_ctx, x_refs) + mod_ref[2:3, :] * mix
    xo_ref[...] = x
    h = _modnorm(x, g_ref[...], mod_ref[3:4, :], mod_ref[4:5, :])
    h_ref[...] = h.astype(BF16)
    h_hi, h_lo = _split(h, 2)
    w_hi, w_lo = _split(wr_ref[...], 2)
    lt_ref[...] = _dot_nt(w_hi, h_hi) + _dot_nt(w_lo, h_hi) + _dot_nt(w_hi, h_lo)


def _outproj(fn, a, hg, xs, mod, g, w_out, w_router_t, n_ctx_tok, lat_len):
    t = sum(x.shape[0] for x in xs)
    tm = TOKEN_TILE
    n_ctx_tiles = n_ctx_tok // tm
    row = _mod_row(n_ctx_tiles, lat_len // tm)
    rows = lambda width: pl.BlockSpec((tm, width), lambda i: (i, 0))
    parts = lambda n, width: _row_specs(n, tm, width, n_ctx_tiles)
    return pl.pallas_call(
        functools.partial(_outproj_body, len(xs), n_ctx_tiles),
        grid=(t // tm,),
        in_specs=parts(2, FN_WIDTH) + parts(2, DA_WIDTH) + parts(2, HG_WIDTH) + parts(len(xs), D_MODEL) + [
            pl.BlockSpec((None, 6, D_MODEL), lambda i: (row(i), 0, 0)),
            pl.BlockSpec((1, D_MODEL), lambda i: (0, 0)),
            pl.BlockSpec((D_MODEL, D_MODEL), lambda i: (0, 0)),
            pl.BlockSpec((N_EXPERTS, D_MODEL), lambda i: (0, 0)),
        ],
        out_specs=[rows(D_MODEL), rows(D_MODEL), pl.BlockSpec((N_EXPERTS, tm), lambda i: (0, i))],
        out_shape=[jax.ShapeDtypeStruct((t, D_MODEL), F32),
                   jax.ShapeDtypeStruct((t, D_MODEL), BF16),
                   jax.ShapeDtypeStruct((N_EXPERTS, t), F32)],
        compiler_params=_params(1),
        name="outproj",
    )(*fn, *a, *hg, *xs, mod, g, w_out, w_router_t)


def _route_body(lt_ref, bias_ref, o_ref):
    per = N_EXPERTS // N_GROUPS
    tt = lt_ref.shape[1]
    neg = -jnp.inf
    assert per == N_GROUPS == 8
    gi = lax.broadcasted_iota(jnp.int32, (N_GROUPS, tt), 0).astype(F32)
    s_j, b_j = [], []
    for j in range(per):
        s = _sigmoid(lt_ref[j * N_GROUPS:(j + 1) * N_GROUPS, :])
        s_j.append(s)
        b_j.append(s + bias_ref[j * N_GROUPS:(j + 1) * N_GROUPS, :])
    m1 = functools.reduce(jnp.maximum, b_j)
    i1 = functools.reduce(jnp.minimum, [jnp.where(b_j[j] == m1, float(j), float(per)) for j in range(per)])
    m2 = functools.reduce(jnp.maximum, [jnp.where(i1 == float(j), neg, b_j[j]) for j in range(per)])
    gs = m1 + m2
    gsel = jnp.zeros((N_GROUPS, tt), jnp.bool_)
    for _ in range(TOPK_GROUPS):
        m = gs.max(axis=0, keepdims=True)
        idx = jnp.where(gs == m, gi, float(N_GROUPS)).min(axis=0, keepdims=True)
        hit = gi == idx
        gsel = gsel | hit
        gs = jnp.where(hit, neg, gs)
    x_j = [jnp.where(gsel, b_j[j], neg) for j in range(per)]
    e_j = [gi * per + j for j in range(per)]
    sel_j = [jnp.zeros((N_GROUPS, tt), jnp.bool_) for _ in range(per)]
    for _ in range(TOP_K):
        m = functools.reduce(jnp.maximum, x_j).max(axis=0, keepdims=True)
        idx = functools.reduce(jnp.minimum, [jnp.where(x_j[j] == m, e_j[j], float(N_EXPERTS))
                                             for j in range(per)]).min(axis=0, keepdims=True)
        for j in range(per):
            hit = e_j[j] == idx
            sel_j[j] = sel_j[j] | hit
            x_j[j] = jnp.where(hit, neg, x_j[j])
    w_j = [jnp.where(sel_j[j], s_j[j], 0.0) for j in range(per)]
    denom = functools.reduce(lambda a, b: a + b, w_j).sum(axis=0, keepdims=True)
    gates_t = jnp.concatenate([w / denom * ROUTED_SCALE for w in w_j], axis=0)
    r_io = lax.broadcasted_iota(jnp.int32, (N_EXPERTS, N_EXPERTS), 0)
    e_io = lax.broadcasted_iota(jnp.int32, (N_EXPERTS, N_EXPERTS), 1)
    eye = (e_io == (r_io & (N_GROUPS - 1)) * per + (r_io >> 3)).astype(BF16)
    p = _split(gates_t, 3)
    o_ref[...] = _dot_tn(p[0], eye) + _dot_tn(p[1], eye) + _dot_tn(p[2], eye)


def _route(logits_t, bias):
    t = logits_t.shape[1]
    tt = ROUTE_TILE
    return pl.pallas_call(
        _route_body,
        grid=(t // tt,),
        in_specs=[pl.BlockSpec((N_EXPERTS, tt), lambda i: (0, i)),
                  pl.BlockSpec((N_EXPERTS, 1), lambda i: (0, 0))],
        out_specs=pl.BlockSpec((tt, N_EXPERTS), lambda i: (i, 0)),
        out_shape=jax.ShapeDtypeStruct((t, N_EXPERTS), F32),
        compiler_params=_params(1),
        name="route",
    )(logits_t, bias)


def _moe_body(final_norm, n_ctx_tiles, h_ref, gate_ref, wg_ref, wu_ref, wd_ref, sg_ref, su_ref, sd_ref,
              x_ref, mod_ref, fg_ref, *out_and_scratch):
    acc_ref = out_and_scratch[-1]
    o_refs = out_and_scratch[:-1]
    is_ctx = pl.program_id(0) < n_ctx_tiles
    j = pl.program_id(1)
    h = h_ref[...]

    def act_of(wg, wu, gate):
        gu = _dot(h, jnp.concatenate([wg.astype(BF16), wu.astype(BF16)], axis=1))
        a = _silu(gu[:, :D_EXPERT]) * gu[:, D_EXPERT:]
        return a if gate is None else a * gate

    @pl.when(j == 0)
    def _():
        acc_ref[...] = _dot(act_of(sg_ref[...], su_ref[...], None).astype(BF16), sd_ref[...].astype(BF16))

    gates = gate_ref[...]
    expert_of_lane = lax.broadcasted_iota(jnp.int32, gates.shape, 1)

    def gate_col(p):
        e = j * EXPERTS_PER_STEP + p
        return jnp.sum(jnp.where(expert_of_lane == e, gates, 0.0), axis=1, keepdims=True)

    for p in range(EXPERTS_PER_STEP // 2):
        a0 = act_of(wg_ref[2 * p], wu_ref[2 * p], gate_col(2 * p))
        a1 = act_of(wg_ref[2 * p + 1], wu_ref[2 * p + 1], gate_col(2 * p + 1))
        pair = jnp.concatenate([a0, a1], axis=1).astype(BF16)
        acc_ref[...] += _dot(pair, wd_ref[p].astype(BF16))

    @pl.when(j == pl.num_programs(1) - 1)
    def _():
        x = x_ref[...] + mod_ref[5:6, :] * acc_ref[...]
        if not final_norm:
            o_refs[0][...] = x
        else:
            ms = jnp.mean(x * x, axis=-1, keepdims=True)
            y = x * lax.rsqrt(ms + EPS) * fg_ref[...]

            @pl.when(is_ctx)
            def _():
                o_refs[0][...] = y

            @pl.when(jnp.logical_not(is_ctx))
            def _():
                o_refs[1][...] = y


def _moe(h, gates, w_gate, w_up, w_down2, ws_gate, ws_up, ws_down, layer, x, mod, final_g, final_norm,
         n_ctx_tok, lat_len):
    t = x.shape[0]
    tm = MOE_TILE
    eps_ = EXPERTS_PER_STEP
    n_ctx_tiles = n_ctx_tok // tm
    row = _mod_row(n_ctx_tiles, lat_len // tm)
    if final_norm:
        out_specs = _row_specs(2, tm, D_MODEL, n_ctx_tiles)
        out_shape = [jax.ShapeDtypeStruct((n_ctx_tok, D_MODEL), F32),
                     jax.ShapeDtypeStruct((t - n_ctx_tok, D_MODEL), F32)]
    else:
        out_specs = [pl.BlockSpec((tm, D_MODEL), lambda i, j: (i, 0))]
        out_shape = [jax.ShapeDtypeStruct((t, D_MODEL), F32)]
    return pl.pallas_call(
        functools.partial(_moe_body, final_norm, n_ctx_tiles),
        grid=(t // tm, N_EXPERTS // eps_),
        in_specs=[
            pl.BlockSpec((tm, D_MODEL), lambda i, j: (i, 0)),
            pl.BlockSpec((tm, N_EXPERTS), lambda i, j: (i, 0)),
            pl.BlockSpec((None, eps_, D_MODEL, D_EXPERT), lambda i, j: (layer, j, 0, 0)),
            pl.BlockSpec((None, eps_, D_MODEL, D_EXPERT), lambda i, j: (layer, j, 0, 0)),
            pl.BlockSpec((None, eps_ // 2, 2 * D_EXPERT, D_MODEL), lambda i, j: (layer, j, 0, 0)),
            pl.BlockSpec((None, D_MODEL, D_EXPERT), lambda i, j: (layer, 0, 0)),
            pl.BlockSpec((None, D_MODEL, D_EXPERT), lambda i, j: (layer, 0, 0)),
            pl.BlockSpec((None, D_EXPERT, D_MODEL), lambda i, j: (layer, 0, 0)),
            pl.BlockSpec((tm, D_MODEL), lambda i, j: (i, 0)),
            pl.BlockSpec((None, 6, D_MODEL), lambda i, j: (row(i), 0, 0)),
            pl.BlockSpec((1, D_MODEL), lambda i, j: (0, 0)),
        ],
        out_specs=out_specs,
        out_shape=out_shape,
        scratch_shapes=[pltpu.VMEM((tm, D_MODEL), F32)],
        compiler_params=_params(2),
        name="moe",
    )(h, gates, w_gate, w_up, w_down2, ws_gate, ws_up, ws_down, x, mod, final_g)


def _block_diag_t(s):
    eye = jnp.eye(HG_HEADS, dtype=s.dtype)
    out = jnp.einsum('...hkv,hg->...hvgk', s, eye)
    return out.reshape(s.shape[:-3] + (HG_WIDTH, HG_WIDTH))


def kernel(x_prompt, x_sample, cache_k, cache_v, state_hgrn, c, c_ctx, w_ada, b_ada, norm_g, w_in,
           w_fourier, lambdas, attn_norm_g, lower_bounds, hg_norm_g, w_out, w_router, router_bias,
           w_gate, w_up, w_down, ws_gate, ws_up, ws_down, final_g):
    n_ctx, ctx_len, _ = x_prompt.shape
    n_lat, lat_len, _ = x_sample.shape
    n_ctx_tok = n_ctx * ctx_len
    past = cache_k.shape[2]

    xs = (x_prompt.reshape(n_ctx_tok, D_MODEL), x_sample.reshape(n_lat * lat_len, D_MODEL))

    c8 = jnp.zeros((8, D_MODEL), F32).at[0].set(c_ctx).at[1:1 + n_lat].set(c)
    mods = _ada_mods(c8, w_ada, b_ada).reshape(DEPTH, 8, 6, D_MODEL)

    cs = jnp.cumsum(jax.nn.softmax(lower_bounds.astype(F32), axis=0), axis=0)
    lbs = cs - cs[0:1]

    cos, sin = _rope_tables(lat_len)
    cache_k4 = cache_k.reshape(n_lat, DEPTH, past, DA_WIDTH)
    cache_v4 = cache_v.reshape(n_lat, DEPTH, past, DA_WIDTH)
    s0t = _block_diag_t(state_hgrn.astype(F32))

    w_in_b = jnp.concatenate([w_in[:, :, FN_WIDTH:], w_in[:, :, :FN_WIDTH]], axis=-1).astype(BF16)
    w_f_b = w_fourier.astype(BF16)
    w_out_b = w_out.astype(BF16)
    per = N_EXPERTS // N_GROUPS
    w_router_t = (jnp.swapaxes(w_router, 1, 2).reshape(DEPTH, N_GROUPS, per, D_MODEL)
                  .swapaxes(1, 2).reshape(DEPTH, N_EXPERTS, D_MODEL))
    bias_mm = router_bias.reshape(DEPTH, N_GROUPS, per).swapaxes(1, 2).reshape(DEPTH, N_EXPERTS, 1)
    w_down2 = w_down.reshape(DEPTH, N_EXPERTS // 2, 2 * D_EXPERT, D_MODEL)
    ng = jnp.tile(hg_norm_g, (1, HG_HEADS))

    caches = []
    new_s = []
    for l in range(DEPTH):
        lam_init = 0.8 - 0.6 * math.exp(-0.3 * l)
        mod = mods[l]
        last = l == DEPTH - 1
        proj, k_l, v_l = _inproj(xs, mod, norm_g[l, 0:1], w_in_b[l], tuple(caches) if last else (),
                                 n_ctx, ctx_len, lat_len)
        caches = [k_l, v_l] if last else caches + [k_l, v_l]

        fn = (_fourier(proj, 0, n_ctx, ctx_len, w_f_b[l]),
              _fourier(proj, n_ctx_tok, n_lat, lat_len, w_f_b[l]))
        ag = attn_norm_g[l].reshape(1, DA_VDIM)
        a = (_attn_ctx(proj, n_ctx, ctx_len, lambdas[l], ag, lam_init),
             _attn_lat(proj, n_ctx_tok, n_lat, lat_len, cache_k4, cache_v4, l, cos, sin, lambdas[l], ag, lam_init))
        hg_ctx, st_ctx = _hgrn2(proj, 0, n_ctx, ctx_len, 2, lbs[l], ng[l:l + 1], None)
        (hg_lat,) = _hgrn2(proj, n_ctx_tok, n_lat, lat_len, 1, lbs[l], ng[l:l + 1], s0t[:, l])
        new_s.append(jnp.swapaxes(st_ctx, -1, -2))

        x, h2, logits_t = _outproj(fn, a, (hg_ctx, hg_lat), xs, mod, norm_g[l, 1:2], w_out_b[l],
                                   w_router_t[l], n_ctx_tok, lat_len)
        gates = _route(logits_t, bias_mm[l])
        xs = _moe(h2, gates, w_gate, w_up, w_down2, ws_gate, ws_up, ws_down, l, x, mod,
                  final_g.reshape(1, D_MODEL), l == DEPTH - 1, n_ctx_tok, lat_len)

    y_prompt = xs[0].reshape(n_ctx, ctx_len, D_MODEL)
    y_sample = xs[1].reshape(n_lat, lat_len, D_MODEL)
    new_k = caches[0].reshape(n_ctx, DEPTH, ctx_len, DA_HEADS, DA_VDIM)
    new_v = caches[1].reshape(n_ctx, DEPTH, ctx_len, DA_HEADS, DA_VDIM)
    return (y_prompt, y_sample, new_k, new_v, jnp.stack(new_s, axis=1))
```

```python
import functools
import math

import numpy as np
import jax
import jax.numpy as jnp
from jax import lax
from jax.experimental import pallas as pl
from jax.experimental.pallas import tpu as pltpu

F32 = jnp.float32
BF16 = jnp.bfloat16

D_MODEL = 1024
DEPTH = 2
GRID_W = 64
FN_WIDTH = 256
FN_GROUPS = 4
FN_GROUP_DIM = 64
DA_WIDTH = 512
DA_HEADS = 4
DA_VDIM = 128
DA_HALF = 64
HG_WIDTH = 256
HG_HEADS = 4
HG_DK = 64
PROJ_WIDTH = 3072
CHUNK = 64
ROPE_THETA = 10000.0
N_EXPERTS = 64
TOP_K = 8
N_GROUPS = 8
TOPK_GROUPS = 4
D_EXPERT = 128
ROUTED_SCALE = 2.5
EPS = 1e-6

COL_Q, COL_K, COL_V = 0, 512, 1024
COL_HQ, COL_HI, COL_HF, COL_HB, COL_HG = 1536, 1792, 2048, 2304, 2560
COL_FN = 2816

TOKEN_TILE = 256
MOE_TILE = 1024
EXPERTS_PER_STEP = 4
ROUTE_TILE = 512
CHUNKS_PER_GROUP = 4
VMEM_LIMIT = 56 * 1024 * 1024


def _dot(a, b):
    return jnp.dot(a, b, preferred_element_type=F32)


def _dot_nt(a, b):
    return lax.dot_general(a, b, (((1,), (1,)), ((), ())), preferred_element_type=F32)


def _dot_tn(a, b):
    return lax.dot_general(a, b, (((0,), (0,)), ((), ())), preferred_element_type=F32)


def _split(x, n):
    parts = []
    r = x
    for i in range(n):
        p = r.astype(BF16)
        parts.append(p)
        if i + 1 < n:
            r = r - p.astype(F32)
    return parts


def _sigmoid(x):
    return 1.0 / (1.0 + jnp.exp(-x))


def _silu(x):
    return x * _sigmoid(x)


def _params(n_axes):
    return pltpu.CompilerParams(dimension_semantics=("arbitrary",) * n_axes,
                                vmem_limit_bytes=VMEM_LIMIT)


def _ada_body(c_ref, w_ref, b_ref, o_ref):
    a = _silu(c_ref[...])
    a_hi, a_lo = _split(a, 2)
    w_hi, w_lo = _split(w_ref[...], 2)
    o_ref[...] = _dot(a_hi, w_hi) + _dot(a_lo, w_hi) + _dot(a_hi, w_lo) + b_ref[...]


def _ada_mods(c8, w_ada, b_ada):
    tn = 1536
    return pl.pallas_call(
        _ada_body,
        grid=(DEPTH, 6 * D_MODEL // tn),
        in_specs=[
            pl.BlockSpec((8, D_MODEL), lambda l, j: (0, 0)),
            pl.BlockSpec((None, D_MODEL, tn), lambda l, j: (l, 0, j)),
            pl.BlockSpec((None, 1, tn), lambda l, j: (l, 0, j)),
        ],
        out_specs=pl.BlockSpec((None, 8, tn), lambda l, j: (l, 0, j)),
        out_shape=jax.ShapeDtypeStruct((DEPTH, 8, 6 * D_MODEL), F32),
        compiler_params=_params(2),
        name="ada_mods",
    )(c8, w_ada, b_ada.reshape(DEPTH, 1, 6 * D_MODEL))


def _modnorm(x, g, shift, scale):
    ms = jnp.mean(x * x, axis=-1, keepdims=True)
    return (x * lax.rsqrt(ms + EPS) * g) * (1.0 + scale) + shift


def _mod_row(n_ctx_tiles, tiles_per_latent):
    def f(i):
        return jnp.where(i < n_ctx_tiles, 0, 1 + (i - n_ctx_tiles) // tiles_per_latent)
    return f


def _row_specs(n_parts, tm, width, n_ctx_tiles):
    if n_parts == 1:
        return [pl.BlockSpec((tm, width), lambda i, *_: (i, 0))]
    return [pl.BlockSpec((tm, width), lambda i, *_: (jnp.minimum(i, n_ctx_tiles - 1), 0)),
            pl.BlockSpec((tm, width), lambda i, *_: (jnp.maximum(i - n_ctx_tiles, 0), 0))]


def _pick(is_ctx, refs):
    if len(refs) == 1:
        return refs[0][...]
    return jnp.where(is_ctx, refs[0][...], refs[1][...])


def _inproj_body(n_x, n_prev, n_ctx_tiles, *refs):
    x_refs = refs[:n_x]
    mod_ref, g_ref, w_ref = refs[n_x:n_x + 3]
    prev_refs = refs[n_x + 3:n_x + 3 + 2 * n_prev]
    o_ref, k_ref, v_ref = refs[-3:]
    is_ctx = pl.program_id(0) < n_ctx_tiles
    x = _pick(is_ctx, x_refs)
    h = _modnorm(x, g_ref[...], mod_ref[0:1, :], mod_ref[1:2, :])
    proj = _dot(h.astype(BF16), w_ref[...])
    o_ref[...] = proj

    @pl.when(is_ctx)
    def _():
        n_tok = proj.shape[0]
        per_layer = n_tok * DA_HEADS
        for l in range(n_prev):
            k_ref[l * per_layer:(l + 1) * per_layer, :] = prev_refs[2 * l][...]
            v_ref[l * per_layer:(l + 1) * per_layer, :] = prev_refs[2 * l + 1][...]
        for h in range(DA_HEADS):
            rows = pl.ds(n_prev * per_layer + h, n_tok, stride=DA_HEADS)
            k_ref[rows, :] = proj[:, COL_K + h * DA_VDIM:COL_K + (h + 1) * DA_VDIM]
            v_ref[rows, :] = proj[:, COL_V + h * DA_VDIM:COL_V + (h + 1) * DA_VDIM]


def _inproj(xs, mod, g, w, prev_caches, n_ctx, ctx_len, lat_len):
    t = sum(x.shape[0] for x in xs)
    tm = TOKEN_TILE
    assert ctx_len == tm
    n_ctx_tiles = n_ctx
    n_prev = len(prev_caches) // 2
    row = _mod_row(n_ctx_tiles, lat_len // tm)
    seq_rows = ctx_len * DA_HEADS
    seq_block = lambda i: (jnp.minimum(i, n_ctx - 1), 0)
    cache_spec = pl.BlockSpec(((n_prev + 1) * seq_rows, DA_VDIM), seq_block)
    cache_shape = jax.ShapeDtypeStruct((n_ctx * (n_prev + 1) * seq_rows, DA_VDIM), F32)
    in_specs = _row_specs(len(xs), tm, D_MODEL, n_ctx_tiles) + [
        pl.BlockSpec((None, 6, D_MODEL), lambda i: (row(i), 0, 0)),
        pl.BlockSpec((1, D_MODEL), lambda i: (0, 0)),
        pl.BlockSpec((D_MODEL, PROJ_WIDTH), lambda i: (0, 0)),
    ] + [pl.BlockSpec((seq_rows, DA_VDIM), seq_block)] * (2 * n_prev)
    return pl.pallas_call(
        functools.partial(_inproj_body, len(xs), n_prev, n_ctx_tiles),
        grid=(t // tm,),
        in_specs=in_specs,
        out_specs=[pl.BlockSpec((tm, PROJ_WIDTH), lambda i: (i, 0)), cache_spec, cache_spec],
        out_shape=[jax.ShapeDtypeStruct((t, PROJ_WIDTH), F32), cache_shape, cache_shape],
        compiler_params=_params(1),
        name="inproj",
    )(*xs, mod, g, w, *prev_caches)


def _fourier_body(u_ref, cl_ref, sl_ref, cc_ref, sc_ref, w_ref, o_ref):
    z = u_ref[...].astype(BF16)
    a = _dot(z, cc_ref[...]).astype(BF16)
    b = _dot(z, sc_ref[...]).astype(BF16)
    y = _dot(cl_ref[...], a) - _dot(sl_ref[...], b)
    o_ref[...] = _dot(y.astype(BF16), w_ref[...]).astype(o_ref.dtype)


def _dft_tables(n, block):
    i = np.arange(n)
    prod = (i[:, None] % block) * (i[None, :] % block) % block
    ang = prod.astype(np.float64) * (2.0 * math.pi / block)
    same = (i[:, None] // block) == (i[None, :] // block)
    scale = 1.0 / math.sqrt(block)
    c = np.where(same, np.cos(ang) * scale, 0.0).astype(np.float32)
    s = np.where(same, np.sin(ang) * scale, 0.0).astype(np.float32)
    return jnp.asarray(c).astype(BF16), jnp.asarray(s).astype(BF16)


def _fourier(proj, row0, n_seq, seq_len, w_f):
    cl, sl = _dft_tables(seq_len, seq_len)
    cc, sc = _dft_tables(FN_WIDTH, FN_GROUP_DIM)
    blk0 = row0 // seq_len
    full = lambda shape: pl.BlockSpec(shape, lambda i: (0, 0))
    return pl.pallas_call(
        _fourier_body,
        grid=(n_seq,),
        in_specs=[
            pl.BlockSpec((seq_len, FN_WIDTH), lambda i: (blk0 + i, COL_FN // FN_WIDTH)),
            full((seq_len, seq_len)), full((seq_len, seq_len)),
            full((FN_WIDTH, FN_WIDTH)), full((FN_WIDTH, FN_WIDTH)), full((FN_WIDTH, FN_WIDTH)),
        ],
        out_specs=pl.BlockSpec((seq_len, FN_WIDTH), lambda i: (i, 0)),
        out_shape=jax.ShapeDtypeStruct((n_seq * seq_len, FN_WIDTH), BF16),
        compiler_params=_params(1),
        name=f"fourier_{seq_len}",
    )(proj, cl, sl, cc, sc, w_f)


def _lambda_full(lmb, lam_init):
    a = jnp.sum(lmb[0:1, :] * lmb[1:2, :], axis=-1, keepdims=True)
    b = jnp.sum(lmb[2:3, :] * lmb[3:4, :], axis=-1, keepdims=True)
    return jnp.exp(a) - jnp.exp(b) + lam_init


def _softmax_parts(parts):
    m = parts[0].max(axis=-1, keepdims=True)
    for p in parts[1:]:
        m = jnp.maximum(m, p.max(axis=-1, keepdims=True))
    es = [jnp.exp(p - m) for p in parts]
    tot = es[0].sum(axis=-1, keepdims=True)
    for e in es[1:]:
        tot = tot + e.sum(axis=-1, keepdims=True)
    return es, 1.0 / tot


def _diff_head(q, ks, vs, lam, g, lam_init):
    vas = [jnp.concatenate([v.astype(BF16), jnp.ones(v.shape, BF16)], axis=1) for v in vs]
    outs = []
    for m in range(2):
        qm = q[:, m * DA_HALF:(m + 1) * DA_HALF].astype(BF16)
        parts = [_dot_nt(qm, k[:, m * DA_HALF:(m + 1) * DA_HALF].astype(BF16)) for k in ks]
        mx = parts[0].max(axis=-1, keepdims=True)
        for p in parts[1:]:
            mx = jnp.maximum(mx, p.max(axis=-1, keepdims=True))
        oa = _dot(jnp.exp(parts[0] - mx).astype(BF16), vas[0])
        for p, va in zip(parts[1:], vas[1:]):
            oa = oa + _dot(jnp.exp(p - mx).astype(BF16), va)
        outs.append(oa[:, :DA_VDIM] * (1.0 / oa[:, DA_VDIM:DA_VDIM + 1]))
    a = outs[0] - lam * outs[1]
    ms = jnp.mean(a * a, axis=-1, keepdims=True)
    return a * lax.rsqrt(ms + EPS) * g * (1.0 - lam_init)


def _attn_ctx_body(lam_init, q_ref, k_ref, v_ref, lmb_ref, g_ref, o_ref):
    lam = _lambda_full(lmb_ref[...], lam_init)
    scale = DA_HALF ** -0.5
    for h in range(DA_HEADS):
        sl = slice(h * DA_VDIM, (h + 1) * DA_VDIM)
        o = _diff_head(q_ref[:, sl] * scale, [k_ref[:, sl]], [v_ref[:, sl]], lam, g_ref[...], lam_init)
        o_ref[:, sl] = o.astype(o_ref.dtype)


def _attn_ctx(proj, n_seq, seq_len, lmb, g, lam_init):
    blk = lambda c: pl.BlockSpec((seq_len, DA_WIDTH), lambda i: (i, c))
    return pl.pallas_call(
        functools.partial(_attn_ctx_body, lam_init),
        grid=(n_seq,),
        in_specs=[
            blk(COL_Q // DA_WIDTH), blk(COL_K // DA_WIDTH), blk(COL_V // DA_WIDTH),
            pl.BlockSpec((4, DA_HALF), lambda i: (0, 0)),
            pl.BlockSpec((1, DA_VDIM), lambda i: (0, 0)),
        ],
        out_specs=pl.BlockSpec((seq_len, DA_WIDTH), lambda i: (i, 0)),
        out_shape=jax.ShapeDtypeStruct((n_seq * seq_len, DA_WIDTH), BF16),
        compiler_params=_params(1),
        name="attn_ctx",
    )(proj, proj, proj, lmb, g)


def _rope(x, cos, sin):
    lane = lax.broadcasted_iota(jnp.int32, x.shape, 1)
    first = ((lane >> 4) & 1) == 0
    rot = jnp.where(first, -pltpu.roll(x, 128 - DA_HALF // 4, 1), pltpu.roll(x, DA_HALF // 4, 1))
    return x * cos + rot * sin


def _attn_lat_body(lam_init, q_ref, k_ref, v_ref, kc_ref, vc_ref, cq_ref, sq_ref, ck_ref, sk_ref,
                   lmb_ref, g_ref, o_ref):
    lam = _lambda_full(lmb_ref[...], lam_init)
    scale = DA_HALF ** -0.5
    for h in range(DA_HEADS):
        sl = slice(h * DA_VDIM, (h + 1) * DA_VDIM)
        q = _rope(q_ref[:, sl], cq_ref[...], sq_ref[...]) * scale
        k = _rope(k_ref[:, sl], ck_ref[...], sk_ref[...])
        o = _diff_head(q, [k, kc_ref[:, sl]], [v_ref[:, sl], vc_ref[:, sl]], lam, g_ref[...], lam_init)
        o_ref[:, sl] = o.astype(o_ref.dtype)


def _attn_lat(proj, row0, n_seq, seq_len, cache_k, cache_v, layer, cos, sin, lmb, g, lam_init):
    tq = 512
    nq = seq_len // tq
    past = cache_k.shape[2]
    qb0 = row0 // tq
    kb0 = row0 // seq_len
    cache_spec = pl.BlockSpec((None, None, past, DA_WIDTH), lambda b, j: (b, layer, 0, 0))
    kv_spec = lambda c: pl.BlockSpec((seq_len, DA_WIDTH), lambda b, j: (kb0 + b, c))
    return pl.pallas_call(
        functools.partial(_attn_lat_body, lam_init),
        grid=(n_seq, nq),
        in_specs=[
            pl.BlockSpec((tq, DA_WIDTH), lambda b, j: (qb0 + b * nq + j, COL_Q // DA_WIDTH)),
            kv_spec(COL_K // DA_WIDTH), kv_spec(COL_V // DA_WIDTH),
            cache_spec, cache_spec,
            pl.BlockSpec((tq, DA_VDIM), lambda b, j: (j, 0)),
            pl.BlockSpec((tq, DA_VDIM), lambda b, j: (j, 0)),
            pl.BlockSpec((seq_len, DA_VDIM), lambda b, j: (0, 0)),
            pl.BlockSpec((seq_len, DA_VDIM), lambda b, j: (0, 0)),
            pl.BlockSpec((4, DA_HALF), lambda b, j: (0, 0)),
            pl.BlockSpec((1, DA_VDIM), lambda b, j: (0, 0)),
        ],
        out_specs=pl.BlockSpec((tq, DA_WIDTH), lambda b, j: (b * nq + j, 0)),
        out_shape=jax.ShapeDtypeStruct((n_seq * seq_len, DA_WIDTH), BF16),
        compiler_params=_params(2),
        name="attn_lat",
    )(proj, proj, proj, cache_k, cache_v, cos, sin, cos, sin, lmb, g)


def _rope_tables(n_tokens):
    rows = n_tokens // GRID_W
    row = np.repeat(np.arange(rows, dtype=np.float64), GRID_W)
    col = np.tile(np.arange(GRID_W, dtype=np.float64), rows)
    axis_dim = DA_HALF // 2
    inv_freq = ROPE_THETA ** (-np.arange(0, axis_dim, 2, dtype=np.float64) / axis_dim)
    ang_r = row[:, None] * inv_freq[None, :]
    ang_c = col[:, None] * inv_freq[None, :]
    ang = np.concatenate([ang_r, ang_r, ang_c, ang_c] * 2, axis=-1)
    return jnp.asarray(np.cos(ang).astype(np.float32)), jnp.asarray(np.sin(ang).astype(np.float32))


def _hgrn_body(n_chunks, has_s0, *refs):
    if has_s0:
        (hq_ref, hi_ref, hf_ref, hb_ref, hg_ref, lb_ref, ng_ref, s0_ref,
         o_ref, oi_scr, qe_scr, u_scr, st_scr, dec_scr) = refs
        so_ref = None
    else:
        (hq_ref, hi_ref, hf_ref, hb_ref, hg_ref, lb_ref, ng_ref,
         o_ref, so_ref, oi_scr, qe_scr, u_scr, st_scr, dec_scr) = refs
    c = CHUNK
    w = HG_WIDTH
    gc = CHUNKS_PER_GROUP
    gl = gc * c
    c_bits = c.bit_length() - 1
    r_io = lax.broadcasted_iota(jnp.int32, (gl, gl), 0)
    c_io = lax.broadcasted_iota(jnp.int32, (gl, gl), 1)
    same_chunk = (r_io >> c_bits) == (c_io >> c_bits)
    tri_f = (same_chunk & (c_io <= r_io)).astype(BF16)
    tri_b = (same_chunk & (c_io >= r_io)).astype(BF16)
    lane = lax.broadcasted_iota(jnp.int32, (1, w), 1)
    dk_bits = HG_DK.bit_length() - 1
    head_masks = [((lane >> dk_bits) == h).astype(F32) for h in range(HG_HEADS)]
    head_masks_b = [hm.astype(BF16) for hm in head_masks]
    bd = ((lax.broadcasted_iota(jnp.int32, (w, w), 0) >> dk_bits)
          == (lax.broadcasted_iota(jnp.int32, (w, w), 1) >> dk_bits))
    t_of_row = lax.broadcasted_iota(jnp.int32, (HG_HEADS * gl, gl), 0) & (gl - 1)
    s_of_col = lax.broadcasted_iota(jnp.int32, (HG_HEADS * gl, gl), 1)
    same = (t_of_row >> c_bits) == (s_of_col >> c_bits)
    causal_f = same & (s_of_col <= t_of_row)
    causal_b = same & (s_of_col >= t_of_row)

    dirs = ((hf_ref, tri_f, causal_f, c - 1), (hb_ref, tri_b, causal_b, 0))

    def per_chunk_row(x, row):
        return jnp.concatenate([jnp.broadcast_to(x[k * c + row:k * c + row + 1, :], (c, w)) for k in range(gc)],
                               axis=0)

    def group_terms(gi):
        rows = pl.ds(gi * gl if isinstance(gi, int) else pl.multiple_of(gi * gl, gl), gl)
        q = _silu(hq_ref[rows, :]) * (HG_DK ** -0.5)
        v = hi_ref[rows, :].astype(BF16)
        for d, (fp_ref, tri, causal, edge_row) in enumerate(dirs):
            lb = lb_ref[d:d + 1, :]
            fp = fp_ref[rows, :]
            lsig = jnp.minimum(fp, 0.0) - jnp.log1p(jnp.exp(-jnp.abs(fp)))
            la = jnp.log(lb)
            lbb = jnp.log1p(-lb) + lsig
            logf = jnp.maximum(la, lbb) + jnp.log1p(jnp.exp(-jnp.abs(la - lbb)))
            kk = (1.0 - lb) * (1.0 / (1.0 + jnp.exp(fp)))
            g3 = _split(logf, 3)
            cum = _dot(tri, g3[0]) + _dot(tri, g3[1]) + _dot(tri, g3[2])
            total = per_chunk_row(cum, edge_row)
            ref = per_chunk_row(cum, c // 2)
            qc = (q * jnp.exp(cum - ref)).astype(BF16)
            kc = (kk * jnp.exp(ref - cum)).astype(BF16)
            ke = (kk * jnp.exp(total - cum)).astype(BF16)
            qe_scr[d, rows, :] = (q * jnp.exp(cum)).astype(BF16)
            lhs = jnp.concatenate([qc * hm for hm in head_masks_b], axis=0)
            a = jnp.where(causal, _dot_nt(lhs, kc), 0.0).astype(BF16)
            o_stack = _dot(a, v)
            o = o_stack[0:gl, :] * head_masks[0]
            for h in range(1, HG_HEADS):
                o = o + o_stack[h * gl:(h + 1) * gl, :] * head_masks[h]
            oi_scr[d, rows, :] = o
            for k in range(gc):
                ck = slice(k * c, (k + 1) * c)
                i = gi * gc + k
                dec_scr[d, pl.ds(i, 1), :] = jnp.exp(cum[k * c + edge_row:k * c + edge_row + 1, :])
                u_scr[d, i] = jnp.where(bd, _dot_tn(v[ck, :], ke[ck, :]), 0.0)

    if n_chunks == gc:
        group_terms(0)
    else:
        def terms_step(gi, carry):
            group_terms(gi)
            return carry

        lax.fori_loop(0, n_chunks // gc, terms_step, 0)

    slab = 32
    for d in range(2):
        order = range(n_chunks) if d == 0 else range(n_chunks - 1, -1, -1)
        for r in range(w // slab):
            rs = slice(r * slab, (r + 1) * slab)
            st = s0_ref[d, rs, :] if has_s0 else jnp.zeros((slab, w), F32)
            for i in order:
                st_scr[d, i, rs, :] = st.astype(BF16)
                st = st * dec_scr[d, i:i + 1, :] + u_scr[d, i, rs, :]
            if so_ref is not None:
                h = (r * slab) // HG_DK
                off = (r * slab) % HG_DK
                so_ref[d, h, off:off + slab, :] = st[:, h * HG_DK:(h + 1) * HG_DK]

    ones_bd = bd.astype(BF16)

    def finish(i, carry):
        rows = pl.ds(pl.multiple_of(i * c, c), c)
        o = (oi_scr[0, rows, :] + oi_scr[1, rows, :]
             + _dot_nt(qe_scr[0, rows, :], st_scr[0, i]) + _dot_nt(qe_scr[1, rows, :], st_scr[1, i]))
        sq = _split(o * o, 2)
        ms = (_dot(sq[0], ones_bd) + _dot(sq[1], ones_bd)) * (1.0 / HG_DK)
        y = o * lax.rsqrt(ms + EPS) * ng_ref[...]
        o_ref[rows, :] = (y * _silu(hg_ref[rows, :])).astype(o_ref.dtype)
        return carry

    lax.fori_loop(0, n_chunks, finish, 0, unroll=2)


def _hgrn(proj, row0, n_seq, seq_len, lb, ng, s0t):
    n_chunks = seq_len // CHUNK
    blk0 = row0 // seq_len
    col = lambda c0: pl.BlockSpec((seq_len, HG_WIDTH), lambda i: (blk0 + i, c0 // HG_WIDTH))
    in_specs = [col(COL_HQ), col(COL_HI), col(COL_HF), col(COL_HB), col(COL_HG),
                pl.BlockSpec((2, HG_WIDTH), lambda i: (0, 0)),
                pl.BlockSpec((1, HG_WIDTH), lambda i: (0, 0))]
    args = [proj, proj, proj, proj, proj, lb, ng]
    out_specs = [pl.BlockSpec((seq_len, HG_WIDTH), lambda i: (i, 0))]
    out_shape = [jax.ShapeDtypeStruct((n_seq * seq_len, HG_WIDTH), BF16)]
    if s0t is not None:
        in_specs.append(pl.BlockSpec((None, 2, HG_WIDTH, HG_WIDTH), lambda i: (i, 0, 0, 0)))
        args.append(s0t)
    else:
        out_specs.append(pl.BlockSpec((None, 2, HG_HEADS, HG_DK, HG_DK), lambda i: (i, 0, 0, 0, 0)))
        out_shape.append(jax.ShapeDtypeStruct((n_seq, 2, HG_HEADS, HG_DK, HG_DK), F32))
    return pl.pallas_call(
        functools.partial(_hgrn_body, n_chunks, s0t is not None),
        grid=(n_seq,),
        in_specs=in_specs,
        out_specs=out_specs,
        out_shape=out_shape,
        scratch_shapes=[pltpu.VMEM((2, seq_len, HG_WIDTH), F32),
                        pltpu.VMEM((2, seq_len, HG_WIDTH), BF16),
                        pltpu.VMEM((2, n_chunks, HG_WIDTH, HG_WIDTH), F32),
                        pltpu.VMEM((2, n_chunks, HG_WIDTH, HG_WIDTH), BF16),
                        pltpu.VMEM((2, max(n_chunks, 8), HG_WIDTH), F32)],
        compiler_params=_params(1),
        name=f"hgrn_{seq_len}",
    )(*args)


def _sigmoid_t(x):
    return 0.5 * jnp.tanh(0.5 * x) + 0.5


def _hgrn_tables():
    gl = CHUNKS_PER_GROUP * CHUNK
    t = np.arange(gl)
    same = (t[:, None] // CHUNK) == (t[None, :] // CHUNK)
    fwd = same & (t[None, :] <= t[:, None])
    bwd = same & (t[None, :] >= t[:, None])
    f = np.arange(HG_WIDTH)
    bd = (f[:, None] // HG_DK) == (f[None, :] // HG_DK)
    tri = jnp.asarray(np.stack([fwd, bwd]).astype(np.float32)).astype(BF16)
    causal = jnp.asarray(np.stack([fwd, bwd]).astype(np.float32))
    return tri, causal, jnp.asarray(bd.astype(np.float32))


def _hgrn2_body(n_seq, groups_per_seq, has_s0, *refs):
    (hq_ref, hi_ref, hf_ref, hb_ref, hg_ref, lb_ref, ng_ref, tri_ref, causal_ref, bd_ref) = refs[:10]
    if has_s0:
        s0_ref, o_ref, oi_scr, qe_scr, u_scr, st_scr, dec_scr = refs[10:]
        so_ref = None
    else:
        o_ref, so_ref, oi_scr, qe_scr, u_scr, st_scr, dec_scr = refs[10:]
        s0_ref = None
    c = CHUNK
    w = HG_WIDTH
    gc = CHUNKS_PER_GROUP
    gl = gc * c
    n_groups = n_seq * groups_per_seq
    chunks_per_seq = groups_per_seq * gc
    lane = lax.broadcasted_iota(jnp.int32, (1, w), 1)
    dk_bits = HG_DK.bit_length() - 1
    head_masks_b = [((lane >> dk_bits) == h).astype(BF16) for h in range(HG_HEADS)]
    bd = bd_ref[...] > 0.5
    edge_rows = (c - 1, 0)
    fp_refs = (hf_ref, hb_ref)

    def per_chunk_row(x, row):
        return jnp.concatenate([jnp.broadcast_to(x[k * c + row:k * c + row + 1, :], (c, w)) for k in range(gc)],
                               axis=0)

    def group_terms(gi):
        rows = pl.ds(gi * gl if isinstance(gi, int) else pl.multiple_of(gi * gl, gl), gl)
        hq = hq_ref[rows, :]
        q = hq * _sigmoid_t(hq) * (HG_DK ** -0.5)
        v = hi_ref[rows, :].astype(BF16)
        v_stack = jnp.concatenate([v * hm for hm in head_masks_b], axis=0)
        for d in range(2):
            lb = lb_ref[d:d + 1, :]
            fp = fp_refs[d][rows, :]
            lsig = jnp.minimum(fp, 0.0) - jnp.log(1.0 + jnp.exp(-jnp.abs(fp)))
            la = jnp.log(lb)
            lbb = jnp.log(1.0 - lb) + lsig
            logf = jnp.maximum(la, lbb) + jnp.log(1.0 + jnp.exp(-jnp.abs(la - lbb)))
            kk = (1.0 - lb) * _sigmoid_t(-fp)
            g_hi, g_lo = _split(logf, 2)
            cum = _dot(tri_ref[d], g_hi) + _dot(tri_ref[d], g_lo)
            total = per_chunk_row(cum, edge_rows[d])
            ref = per_chunk_row(cum, c // 2)
            qc = (q * jnp.exp(cum - ref)).astype(BF16)
            kc = (kk * jnp.exp(ref - cum)).astype(BF16)
            ke = (kk * jnp.exp(total - cum)).astype(BF16)
            qe_scr[d, rows, :] = (q * jnp.exp(cum)).astype(BF16)
            kc_stack = jnp.concatenate([kc * hm for hm in head_masks_b], axis=0)
            a = _dot_nt(qc, kc_stack)
            keep = causal_ref[d] > 0.5
            a = jnp.concatenate([jnp.where(keep, a[:, h * gl:(h + 1) * gl], 0.0) for h in range(HG_HEADS)],
                                axis=1).astype(BF16)
            oi_scr[d, rows, :] = _dot(a, v_stack)
            for k in range(gc):
                ck = slice(k * c, (k + 1) * c)
                i = gi * gc + k
                dec_scr[d, pl.ds(i, 1), :] = jnp.exp(cum[k * c + edge_rows[d]:k * c + edge_rows[d] + 1, :])
                u_scr[d, i] = jnp.where(bd, _dot_tn(v[ck, :], ke[ck, :]), 0.0)

    if n_groups <= 2:
        for gi in range(n_groups):
            group_terms(gi)
    else:
        def terms_step(gi, carry):
            group_terms(gi)
            return carry

        lax.fori_loop(0, n_groups, terms_step, 0)

    slab = 32
    for s in range(n_seq):
        first = s * chunks_per_seq
        for d in range(2):
            order = range(chunks_per_seq) if d == 0 else range(chunks_per_seq - 1, -1, -1)
            for r in range(w // slab):
                rs = slice(r * slab, (r + 1) * slab)
                st = s0_ref[s, d, rs, :] if has_s0 else jnp.zeros((slab, w), F32)
                for j in order:
                    i = first + j
                    st_scr[d, i, rs, :] = st.astype(BF16)
                    st = st * dec_scr[d, i:i + 1, :] + u_scr[d, i, rs, :]
                if so_ref is not None:
                    h = (r * slab) // HG_DK
                    off = (r * slab) % HG_DK
                    so_ref[s, d, h, off:off + slab, :] = st[:, h * HG_DK:(h + 1) * HG_DK]

    ones_bd = bd.astype(BF16)

    def finish(gi):
        rows = pl.ds(gi * gl if isinstance(gi, int) else pl.multiple_of(gi * gl, gl), gl)
        inter = []
        for k in range(gc):
            i = gi * gc + k
            ck = pl.ds(gi * gl + k * c if isinstance(gi, int) else pl.multiple_of(gi * gl + k * c, c), c)
            inter.append(_dot_nt(qe_scr[0, ck, :], st_scr[0, i]) + _dot_nt(qe_scr[1, ck, :], st_scr[1, i]))
        o = oi_scr[0, rows, :] + oi_scr[1, rows, :] + jnp.concatenate(inter, axis=0)
        sq = _split(o * o, 2)
        ms = (_dot(sq[0], ones_bd) + _dot(sq[1], ones_bd)) * (1.0 / HG_DK)
        y = o * lax.rsqrt(ms + EPS) * ng_ref[...]
        hg = hg_ref[rows, :]
        o_ref[rows, :] = (y * hg * _sigmoid_t(hg)).astype(o_ref.dtype)

    if n_groups <= 2:
        for gi in range(n_groups):
            finish(gi)
    else:
        def finish_step(gi, carry):
            finish(gi)
            return carry

        lax.fori_loop(0, n_groups, finish_step, 0)


def _hgrn2(proj, row0, n_seq, seq_len, seqs_per_step, lb, ng, s0t):
    gl = CHUNKS_PER_GROUP * CHUNK
    groups_per_seq = seq_len // gl
    n_chunks = seqs_per_step * seq_len // CHUNK
    rows = seqs_per_step * seq_len
    blk0 = row0 // rows
    tri, causal, bd = _hgrn_tables()
    col = lambda c0: pl.BlockSpec((rows, HG_WIDTH), lambda i: (blk0 + i, c0 // HG_WIDTH))
    const = lambda shape: pl.BlockSpec(shape, lambda i: (0,) * len(shape))
    in_specs = [col(COL_HQ), col(COL_HI), col(COL_HF), col(COL_HB), col(COL_HG),
                const((2, HG_WIDTH)), const((1, HG_WIDTH)),
                const((2, gl, gl)), const((2, gl, gl)), const((HG_WIDTH, HG_WIDTH))]
    args = [proj, proj, proj, proj, proj, lb, ng, tri, causal, bd]
    out_specs = [pl.BlockSpec((rows, HG_WIDTH), lambda i: (i, 0))]
    out_shape = [jax.ShapeDtypeStruct((n_seq * seq_len, HG_WIDTH), BF16)]
    if s0t is not None:
        in_specs.append(pl.BlockSpec((seqs_per_step, 2, HG_WIDTH, HG_WIDTH), lambda i: (i, 0, 0, 0)))
        args.append(s0t)
    else:
        out_specs.append(pl.BlockSpec((seqs_per_step, 2, HG_HEADS, HG_DK, HG_DK), lambda i: (i, 0, 0, 0, 0)))
        out_shape.append(jax.ShapeDtypeStruct((n_seq, 2, HG_HEADS, HG_DK, HG_DK), F32))
    return pl.pallas_call(
        functools.partial(_hgrn2_body, seqs_per_step, groups_per_seq, s0t is not None),
        grid=(n_seq // seqs_per_step,),
        in_specs=in_specs,
        out_specs=out_specs,
        out_shape=out_shape,
        scratch_shapes=[pltpu.VMEM((2, rows, HG_WIDTH), F32),
                        pltpu.VMEM((2, rows, HG_WIDTH), BF16),
                        pltpu.VMEM((2, n_chunks, HG_WIDTH, HG_WIDTH), F32),
                        pltpu.VMEM((2, n_chunks, HG_WIDTH, HG_WIDTH), BF16),
                        pltpu.VMEM((2, max(n_chunks, 8), HG_WIDTH), F32)],
        compiler_params=_params(1),
        name=f"hgrn_{seq_len}",
    )(*args)


def _ctx_mixers_body(n_seq, seq_len, lam_init, *refs):
    (q_ref, k_ref, v_ref, u_ref, hq_ref, hi_ref, hf_ref, hb_ref, hg_ref,
     cl_ref, sl_ref, cc_ref, sc_ref, wf_ref, lmb_ref, ag_ref,
     lb_ref, ng_ref, tri_ref, causal_ref, bd_ref,
     fn_ref, a_ref, ho_ref, so_ref) = refs[:25]
    scratch = refs[25:]

    z = u_ref[...].astype(BF16)
    za = _dot(z, cc_ref[...]).astype(BF16)
    zb = _dot(z, sc_ref[...]).astype(BF16)
    for s in range(n_seq):
        rs = slice(s * seq_len, (s + 1) * seq_len)
        y = _dot(cl_ref[...], za[rs, :]) - _dot(sl_ref[...], zb[rs, :])
        fn_ref[rs, :] = _dot(y.astype(BF16), wf_ref[...]).astype(fn_ref.dtype)

    lam = _lambda_full(lmb_ref[...], lam_init)
    scale = DA_HALF ** -0.5
    for s in range(n_seq):
        rs = slice(s * seq_len, (s + 1) * seq_len)
        for h in range(DA_HEADS):
            sl = slice(h * DA_VDIM, (h + 1) * DA_VDIM)
            o = _diff_head(q_ref[rs, sl] * scale, [k_ref[rs, sl]], [v_ref[rs, sl]], lam, ag_ref[...], lam_init)
            a_ref[rs, sl] = o.astype(a_ref.dtype)

    _hgrn2_body(n_seq, seq_len // (CHUNKS_PER_GROUP * CHUNK), False,
                hq_ref, hi_ref, hf_ref, hb_ref, hg_ref, lb_ref, ng_ref, tri_ref, causal_ref, bd_ref,
                ho_ref, so_ref, *scratch)


def _ctx_mixers(proj, n_seq, seq_len, seqs_per_step, w_f, lmb, ag, lam_init, lb, ng):
    rows = seqs_per_step * seq_len
    n_chunks = rows // CHUNK
    gl = CHUNKS_PER_GROUP * CHUNK
    cl, sl = _dft_tables(seq_len, seq_len)
    cc, sc = _dft_tables(FN_WIDTH, FN_GROUP_DIM)
    tri, causal, bd = _hgrn_tables()
    col = lambda c0, width: pl.BlockSpec((rows, width), lambda i: (i, c0 // width))
    const = lambda shape: pl.BlockSpec(shape, lambda i: (0,) * len(shape))
    out_rows = lambda width: pl.BlockSpec((rows, width), lambda i: (i, 0))
    n_tok = n_seq * seq_len
    return pl.pallas_call(
        functools.partial(_ctx_mixers_body, seqs_per_step, seq_len, lam_init),
        grid=(n_seq // seqs_per_step,),
        in_specs=[col(COL_Q, DA_WIDTH), col(COL_K, DA_WIDTH), col(COL_V, DA_WIDTH), col(COL_FN, FN_WIDTH),
                  col(COL_HQ, HG_WIDTH), col(COL_HI, HG_WIDTH), col(COL_HF, HG_WIDTH), col(COL_HB, HG_WIDTH),
                  col(COL_HG, HG_WIDTH),
                  const((seq_len, seq_len)), const((seq_len, seq_len)),
                  const((FN_WIDTH, FN_WIDTH)), const((FN_WIDTH, FN_WIDTH)), const((FN_WIDTH, FN_WIDTH)),
                  const((4, DA_HALF)), const((1, DA_VDIM)),
                  const((2, HG_WIDTH)), const((1, HG_WIDTH)),
                  const((2, gl, gl)), const((2, gl, gl)), const((HG_WIDTH, HG_WIDTH))],
        out_specs=[out_rows(FN_WIDTH), out_rows(DA_WIDTH), out_rows(HG_WIDTH),
                   pl.BlockSpec((seqs_per_step, 2, HG_HEADS, HG_DK, HG_DK), lambda i: (i, 0, 0, 0, 0))],
        out_shape=[jax.ShapeDtypeStruct((n_tok, FN_WIDTH), BF16),
                   jax.ShapeDtypeStruct((n_tok, DA_WIDTH), BF16),
                   jax.ShapeDtypeStruct((n_tok, HG_WIDTH), BF16),
                   jax.ShapeDtypeStruct((n_seq, 2, HG_HEADS, HG_DK, HG_DK), F32)],
        scratch_shapes=[pltpu.VMEM((2, rows, HG_WIDTH), F32),
                        pltpu.VMEM((2, rows, HG_WIDTH), BF16),
                        pltpu.VMEM((2, n_chunks, HG_WIDTH, HG_WIDTH), F32),
                        pltpu.VMEM((2, n_chunks, HG_WIDTH, HG_WIDTH), BF16),
                        pltpu.VMEM((2, max(n_chunks, 8), HG_WIDTH), F32)],
        compiler_params=_params(1),
        name="ctx_mixers",
    )(proj, proj, proj, proj, proj, proj, proj, proj, proj, cl, sl, cc, sc, w_f, lmb, ag, lb, ng, tri, causal, bd)


def _outproj_body(n_x, n_ctx_tiles, *refs):
    fn_refs, a_refs, hg_refs = refs[0:2], refs[2:4], refs[4:6]
    x_refs = refs[6:6 + n_x]
    mod_ref, g_ref, w_ref, wr_ref, xo_ref, h_ref, lt_ref = refs[6 + n_x:]
    is_ctx = pl.program_id(0) < n_ctx_tiles
    mix = (_dot(_pick(is_ctx, fn_refs), w_ref[0:FN_WIDTH, :])
           + _dot(_pick(is_ctx, a_refs), w_ref[FN_WIDTH:FN_WIDTH + DA_WIDTH, :])
           + _dot(_pick(is_ctx, hg_refs), w_ref[FN_WIDTH + DA_WIDTH:, :]))
    x = _pick(is_ctx, x_refs) + mod_ref[2:3, :] * mix
    xo_ref[...] = x
    h = _modnorm(x, g_ref[...], mod_ref[3:4, :], mod_ref[4:5, :])
    h_ref[...] = h.astype(BF16)
    h_hi, h_lo = _split(h, 2)
    w_hi, w_lo = _split(wr_ref[...], 2)
    lt_ref[...] = _dot_nt(w_hi, h_hi) + _dot_nt(w_lo, h_hi) + _dot_nt(w_hi, h_lo)


def _outproj(fn, a, hg, xs, mod, g, w_out, w_router_t, n_ctx_tok, lat_len):
    t = sum(x.shape[0] for x in xs)
    tm = TOKEN_TILE
    n_ctx_tiles = n_ctx_tok // tm
    row = _mod_row(n_ctx_tiles, lat_len // tm)
    rows = lambda width: pl.BlockSpec((tm, width), lambda i: (i, 0))
    parts = lambda n, width: _row_specs(n, tm, width, n_ctx_tiles)
    return pl.pallas_call(
        functools.partial(_outproj_body, len(xs), n_ctx_tiles),
        grid=(t // tm,),
        in_specs=parts(2, FN_WIDTH) + parts(2, DA_WIDTH) + parts(2, HG_WIDTH) + parts(len(xs), D_MODEL) + [
            pl.BlockSpec((None, 6, D_MODEL), lambda i: (row(i), 0, 0)),
            pl.BlockSpec((1, D_MODEL), lambda i: (0, 0)),
            pl.BlockSpec((D_MODEL, D_MODEL), lambda i: (0, 0)),
            pl.BlockSpec((N_EXPERTS, D_MODEL), lambda i: (0, 0)),
        ],
        out_specs=[rows(D_MODEL), rows(D_MODEL), pl.BlockSpec((N_EXPERTS, tm), lambda i: (0, i))],
        out_shape=[jax.ShapeDtypeStruct((t, D_MODEL), F32),
                   jax.ShapeDtypeStruct((t, D_MODEL), BF16),
                   jax.ShapeDtypeStruct((N_EXPERTS, t), F32)],
        compiler_params=_params(1),
        name="outproj",
    )(*fn, *a, *hg, *xs, mod, g, w_out, w_router_t)


def _route_body(lt_ref, bias_ref, o_ref):
    per = N_EXPERTS // N_GROUPS
    tt = lt_ref.shape[1]
    neg = -jnp.inf
    assert per == N_GROUPS == 8
    gi = lax.broadcasted_iota(jnp.int32, (N_GROUPS, tt), 0).astype(F32)
    s_j, b_j = [], []
    for j in range(per):
        s = _sigmoid(lt_ref[j * N_GROUPS:(j + 1) * N_GROUPS, :])
        s_j.append(s)
        b_j.append(s + bias_ref[j * N_GROUPS:(j + 1) * N_GROUPS, :])
    m1 = functools.reduce(jnp.maximum, b_j)
    i1 = functools.reduce(jnp.minimum, [jnp.where(b_j[j] == m1, float(j), float(per)) for j in range(per)])
    m2 = functools.reduce(jnp.maximum, [jnp.where(i1 == float(j), neg, b_j[j]) for j in range(per)])
    gs = m1 + m2
    gsel = jnp.zeros((N_GROUPS, tt), jnp.bool_)
    for _ in range(TOPK_GROUPS):
        m = gs.max(axis=0, keepdims=True)
        idx = jnp.where(gs == m, gi, float(N_GROUPS)).min(axis=0, keepdims=True)
        hit = gi == idx
        gsel = gsel | hit
        gs = jnp.where(hit, neg, gs)
    x_j = [jnp.where(gsel, b_j[j], neg) for j in range(per)]
    e_j = [gi * per + j for j in range(per)]
    sel_j = [jnp.zeros((N_GROUPS, tt), jnp.bool_) for _ in range(per)]
    for _ in range(TOP_K):
        m = functools.reduce(jnp.maximum, x_j).max(axis=0, keepdims=True)
        idx = functools.reduce(jnp.minimum, [jnp.where(x_j[j] == m, e_j[j], float(N_EXPERTS))
                                             for j in range(per)]).min(axis=0, keepdims=True)
        for j in range(per):
            hit = e_j[j] == idx
            sel_j[j] = sel_j[j] | hit
            x_j[j] = jnp.where(hit, neg, x_j[j])
    w_j = [jnp.where(sel_j[j], s_j[j], 0.0) for j in range(per)]
    denom = functools.reduce(lambda a, b: a + b, w_j).sum(axis=0, keepdims=True)
    gates_t = jnp.concatenate([w / denom * ROUTED_SCALE for w in w_j], axis=0)
    r_io = lax.broadcasted_iota(jnp.int32, (N_EXPERTS, N_EXPERTS), 0)
    e_io = lax.broadcasted_iota(jnp.int32, (N_EXPERTS, N_EXPERTS), 1)
    eye = (e_io == (r_io & (N_GROUPS - 1)) * per + (r_io >> 3)).astype(BF16)
    p = _split(gates_t, 3)
    o_ref[...] = _dot_tn(p[0], eye) + _dot_tn(p[1], eye) + _dot_tn(p[2], eye)


def _route(logits_t, bias):
    t = logits_t.shape[1]
    tt = ROUTE_TILE
    return pl.pallas_call(
        _route_body,
        grid=(t // tt,),
        in_specs=[pl.BlockSpec((N_EXPERTS, tt), lambda i: (0, i)),
                  pl.BlockSpec((N_EXPERTS, 1), lambda i: (0, 0))],
        out_specs=pl.BlockSpec((tt, N_EXPERTS), lambda i: (i, 0)),
        out_shape=jax.ShapeDtypeStruct((t, N_EXPERTS), F32),
        compiler_params=_params(1),
        name="route",
    )(logits_t, bias)


def _moe_body(final_norm, n_ctx_tiles, h_ref, gate_ref, wg_ref, wu_ref, wd_ref, sg_ref, su_ref, sd_ref,
              x_ref, mod_ref, fg_ref, *out_and_scratch):
    acc_ref = out_and_scratch[-1]
    o_refs = out_and_scratch[:-1]
    is_ctx = pl.program_id(0) < n_ctx_tiles
    j = pl.program_id(1)
    h = h_ref[...]

    def act_of(wg, wu, gate):
        gu = _dot(h, jnp.concatenate([wg.astype(BF16), wu.astype(BF16)], axis=1))
        a = _silu(gu[:, :D_EXPERT]) * gu[:, D_EXPERT:]
        return a if gate is None else a * gate

    @pl.when(j == 0)
    def _():
        acc_ref[...] = _dot(act_of(sg_ref[...], su_ref[...], None).astype(BF16), sd_ref[...].astype(BF16))

    gates = gate_ref[...]
    expert_of_lane = lax.broadcasted_iota(jnp.int32, gates.shape, 1)

    def gate_col(p):
        e = j * EXPERTS_PER_STEP + p
        return jnp.sum(jnp.where(expert_of_lane == e, gates, 0.0), axis=1, keepdims=True)

    for p in range(EXPERTS_PER_STEP // 2):
        a0 = act_of(wg_ref[2 * p], wu_ref[2 * p], gate_col(2 * p))
        a1 = act_of(wg_ref[2 * p + 1], wu_ref[2 * p + 1], gate_col(2 * p + 1))
        pair = jnp.concatenate([a0, a1], axis=1).astype(BF16)
        acc_ref[...] += _dot(pair, wd_ref[p].astype(BF16))

    @pl.when(j == pl.num_programs(1) - 1)
    def _():
        x = x_ref[...] + mod_ref[5:6, :] * acc_ref[...]
        if not final_norm:
            o_refs[0][...] = x
        else:
            ms = jnp.mean(x * x, axis=-1, keepdims=True)
            y = x * lax.rsqrt(ms + EPS) * fg_ref[...]

            @pl.when(is_ctx)
            def _():
                o_refs[0][...] = y

            @pl.when(jnp.logical_not(is_ctx))
            def _():
                o_refs[1][...] = y


def _moe(h, gates, w_gate, w_up, w_down2, ws_gate, ws_up, ws_down, layer, x, mod, final_g, final_norm,
         n_ctx_tok, lat_len):
    t = x.shape[0]
    tm = MOE_TILE
    eps_ = EXPERTS_PER_STEP
    n_ctx_tiles = n_ctx_tok // tm
    row = _mod_row(n_ctx_tiles, lat_len // tm)
    if final_norm:
        out_specs = _row_specs(2, tm, D_MODEL, n_ctx_tiles)
        out_shape = [jax.ShapeDtypeStruct((n_ctx_tok, D_MODEL), F32),
                     jax.ShapeDtypeStruct((t - n_ctx_tok, D_MODEL), F32)]
    else:
        out_specs = [pl.BlockSpec((tm, D_MODEL), lambda i, j: (i, 0))]
        out_shape = [jax.ShapeDtypeStruct((t, D_MODEL), F32)]
    return pl.pallas_call(
        functools.partial(_moe_body, final_norm, n_ctx_tiles),
        grid=(t // tm, N_EXPERTS // eps_),
        in_specs=[
            pl.BlockSpec((tm, D_MODEL), lambda i, j: (i, 0)),
            pl.BlockSpec((tm, N_EXPERTS), lambda i, j: (i, 0)),
            pl.BlockSpec((None, eps_, D_MODEL, D_EXPERT), lambda i, j: (layer, j, 0, 0)),
            pl.BlockSpec((None, eps_, D_MODEL, D_EXPERT), lambda i, j: (layer, j, 0, 0)),
            pl.BlockSpec((None, eps_ // 2, 2 * D_EXPERT, D_MODEL), lambda i, j: (layer, j, 0, 0)),
            pl.BlockSpec((None, D_MODEL, D_EXPERT), lambda i, j: (layer, 0, 0)),
            pl.BlockSpec((None, D_MODEL, D_EXPERT), lambda i, j: (layer, 0, 0)),
            pl.BlockSpec((None, D_EXPERT, D_MODEL), lambda i, j: (layer, 0, 0)),
            pl.BlockSpec((tm, D_MODEL), lambda i, j: (i, 0)),
            pl.BlockSpec((None, 6, D_MODEL), lambda i, j: (row(i), 0, 0)),
            pl.BlockSpec((1, D_MODEL), lambda i, j: (0, 0)),
        ],
        out_specs=out_specs,
        out_shape=out_shape,
        scratch_shapes=[pltpu.VMEM((tm, D_MODEL), F32)],
        compiler_params=_params(2),
        name="moe",
    )(h, gates, w_gate, w_up, w_down2, ws_gate, ws_up, ws_down, x, mod, final_g)


def _block_diag_t(s):
    eye = jnp.eye(HG_HEADS, dtype=s.dtype)
    out = jnp.einsum('...hkv,hg->...hvgk', s, eye)
    return out.reshape(s.shape[:-3] + (HG_WIDTH, HG_WIDTH))


def kernel(x_prompt, x_sample, cache_k, cache_v, state_hgrn, c, c_ctx, w_ada, b_ada, norm_g, w_in,
           w_fourier, lambdas, attn_norm_g, lower_bounds, hg_norm_g, w_out, w_router, router_bias,
           w_gate, w_up, w_down, ws_gate, ws_up, ws_down, final_g):
    n_ctx, ctx_len, _ = x_prompt.shape
    n_lat, lat_len, _ = x_sample.shape
    n_ctx_tok = n_ctx * ctx_len
    past = cache_k.shape[2]

    xs = (x_prompt.reshape(n_ctx_tok, D_MODEL), x_sample.reshape(n_lat * lat_len, D_MODEL))

    c8 = jnp.zeros((8, D_MODEL), F32).at[0].set(c_ctx).at[1:1 + n_lat].set(c)
    mods = _ada_mods(c8, w_ada, b_ada).reshape(DEPTH, 8, 6, D_MODEL)

    cs = jnp.cumsum(jax.nn.softmax(lower_bounds.astype(F32), axis=0), axis=0)
    lbs = cs - cs[0:1]

    cos, sin = _rope_tables(lat_len)
    cache_k4 = cache_k.reshape(n_lat, DEPTH, past, DA_WIDTH)
    cache_v4 = cache_v.reshape(n_lat, DEPTH, past, DA_WIDTH)
    s0t = _block_diag_t(state_hgrn.astype(F32))

    w_in_b = jnp.concatenate([w_in[:, :, FN_WIDTH:], w_in[:, :, :FN_WIDTH]], axis=-1).astype(BF16)
    w_f_b = w_fourier.astype(BF16)
    w_out_b = w_out.astype(BF16)
    per = N_EXPERTS // N_GROUPS
    w_router_t = (jnp.swapaxes(w_router, 1, 2).reshape(DEPTH, N_GROUPS, per, D_MODEL)
                  .swapaxes(1, 2).reshape(DEPTH, N_EXPERTS, D_MODEL))
    bias_mm = router_bias.reshape(DEPTH, N_GROUPS, per).swapaxes(1, 2).reshape(DEPTH, N_EXPERTS, 1)
    w_down2 = w_down.reshape(DEPTH, N_EXPERTS // 2, 2 * D_EXPERT, D_MODEL)
    ng = jnp.tile(hg_norm_g, (1, HG_HEADS))

    caches = []
    new_s = []
    for l in range(DEPTH):
        lam_init = 0.8 - 0.6 * math.exp(-0.3 * l)
        mod = mods[l]
        last = l == DEPTH - 1
        proj, k_l, v_l = _inproj(xs, mod, norm_g[l, 0:1], w_in_b[l], tuple(caches) if last else (),
                                 n_ctx, ctx_len, lat_len)
        caches = [k_l, v_l] if last else caches + [k_l, v_l]

        ag = attn_norm_g[l].reshape(1, DA_VDIM)
        fn_ctx, a_ctx, hg_ctx, st_ctx = _ctx_mixers(proj, n_ctx, ctx_len, 2, w_f_b[l], lambdas[l], ag,
                                                    lam_init, lbs[l], ng[l:l + 1])
        fn_lat = _fourier(proj, n_ctx_tok, n_lat, lat_len, w_f_b[l])
        a_lat = _attn_lat(proj, n_ctx_tok, n_lat, lat_len, cache_k4, cache_v4, l, cos, sin, lambdas[l], ag,
                          lam_init)
        (hg_lat,) = _hgrn2(proj, n_ctx_tok, n_lat, lat_len, 1, lbs[l], ng[l:l + 1], s0t[:, l])
        new_s.append(jnp.swapaxes(st_ctx, -1, -2))

        x, h2, logits_t = _outproj((fn_ctx, fn_lat), (a_ctx, a_lat), (hg_ctx, hg_lat), xs, mod,
                                   norm_g[l, 1:2], w_out_b[l], w_router_t[l], n_ctx_tok, lat_len)
        gates = _route(logits_t, bias_mm[l])
        xs = _moe(h2, gates, w_gate, w_up, w_down2, ws_gate, ws_up, ws_down, l, x, mod,
                  final_g.reshape(1, D_MODEL), l == DEPTH - 1, n_ctx_tok, lat_len)

    y_prompt = xs[0].reshape(n_ctx, ctx_len, D_MODEL)
    y_sample = xs[1].reshape(n_lat, lat_len, D_MODEL)
    new_k = caches[0].reshape(n_ctx, DEPTH, ctx_len, DA_HEADS, DA_VDIM)
    new_v = caches[1].reshape(n_ctx, DEPTH, ctx_len, DA_HEADS, DA_VDIM)
    return (y_prompt, y_sample, new_k, new_v, jnp.stack(new_s, axis=1))
```

```python
import functools
import math

import numpy as np
import jax
import jax.numpy as jnp
from jax import lax
from jax.experimental import pallas as pl
from jax.experimental.pallas import tpu as pltpu

F32 = jnp.float32
BF16 = jnp.bfloat16

D_MODEL = 1024
DEPTH = 2
GRID_W = 64
FN_WIDTH = 256
FN_GROUPS = 4
FN_GROUP_DIM = 64
DA_WIDTH = 512
DA_HEADS = 4
DA_VDIM = 128
DA_HALF = 64
HG_WIDTH = 256
HG_HEADS = 4
HG_DK = 64
PROJ_WIDTH = 3072
CHUNK = 64
ROPE_THETA = 10000.0
N_EXPERTS = 64
TOP_K = 8
N_GROUPS = 8
TOPK_GROUPS = 4
D_EXPERT = 128
ROUTED_SCALE = 2.5
EPS = 1e-6

COL_Q, COL_K, COL_V = 0, 512, 1024
COL_HQ, COL_HI, COL_HF, COL_HB, COL_HG = 1536, 1792, 2048, 2304, 2560
COL_FN = 2816

TOKEN_TILE = 256
MOE_TILE = 1024
EXPERTS_PER_STEP = 8
ROUTE_TILE = 512
CHUNKS_PER_GROUP = 4
VMEM_LIMIT = 56 * 1024 * 1024


def _dot(a, b):
    return jnp.dot(a, b, preferred_element_type=F32)


def _dot_nt(a, b):
    return lax.dot_general(a, b, (((1,), (1,)), ((), ())), preferred_element_type=F32)


def _dot_tn(a, b):
    return lax.dot_general(a, b, (((0,), (0,)), ((), ())), preferred_element_type=F32)


def _split(x, n):
    parts = []
    r = x
    for i in range(n):
        p = r.astype(BF16)
        parts.append(p)
        if i + 1 < n:
            r = r - p.astype(F32)
    return parts


def _sigmoid(x):
    return 1.0 / (1.0 + jnp.exp(-x))


def _silu(x):
    return x * _sigmoid(x)


def _params(n_axes):
    return pltpu.CompilerParams(dimension_semantics=("arbitrary",) * n_axes,
                                vmem_limit_bytes=VMEM_LIMIT)


def _ada_body(c_ref, w_ref, b_ref, o_ref):
    a = _silu(c_ref[...])
    a_hi, a_lo = _split(a, 2)
    w_hi, w_lo = _split(w_ref[...], 2)
    o_ref[...] = _dot(a_hi, w_hi) + _dot(a_lo, w_hi) + _dot(a_hi, w_lo) + b_ref[...]


def _ada_mods(c8, w_ada, b_ada):
    tn = 1536
    return pl.pallas_call(
        _ada_body,
        grid=(DEPTH, 6 * D_MODEL // tn),
        in_specs=[
            pl.BlockSpec((8, D_MODEL), lambda l, j: (0, 0)),
            pl.BlockSpec((None, D_MODEL, tn), lambda l, j: (l, 0, j)),
            pl.BlockSpec((None, 1, tn), lambda l, j: (l, 0, j)),
        ],
        out_specs=pl.BlockSpec((None, 8, tn), lambda l, j: (l, 0, j)),
        out_shape=jax.ShapeDtypeStruct((DEPTH, 8, 6 * D_MODEL), F32),
        compiler_params=_params(2),
        name="ada_mods",
    )(c8, w_ada, b_ada.reshape(DEPTH, 1, 6 * D_MODEL))


def _modnorm(x, g, shift, scale):
    ms = jnp.mean(x * x, axis=-1, keepdims=True)
    return (x * lax.rsqrt(ms + EPS) * g) * (1.0 + scale) + shift


def _mod_row(n_ctx_tiles, tiles_per_latent):
    def f(i):
        return jnp.where(i < n_ctx_tiles, 0, 1 + (i - n_ctx_tiles) // tiles_per_latent)
    return f


def _row_specs(n_parts, tm, width, n_ctx_tiles):
    if n_parts == 1:
        return [pl.BlockSpec((tm, width), lambda i, *_: (i, 0))]
    return [pl.BlockSpec((tm, width), lambda i, *_: (jnp.minimum(i, n_ctx_tiles - 1), 0)),
            pl.BlockSpec((tm, width), lambda i, *_: (jnp.maximum(i - n_ctx_tiles, 0), 0))]


def _pick(is_ctx, refs):
    if len(refs) == 1:
        return refs[0][...]
    return jnp.where(is_ctx, refs[0][...], refs[1][...])


def _inproj_body(n_x, n_prev, n_ctx_tiles, *refs):
    x_refs = refs[:n_x]
    mod_ref, g_ref, w_ref = refs[n_x:n_x + 3]
    prev_refs = refs[n_x + 3:n_x + 3 + 2 * n_prev]
    o_ref, k_ref, v_ref = refs[-3:]
    is_ctx = pl.program_id(0) < n_ctx_tiles
    x = _pick(is_ctx, x_refs)
    h = _modnorm(x, g_ref[...], mod_ref[0:1, :], mod_ref[1:2, :])
    proj = _dot(h.astype(BF16), w_ref[...])
    o_ref[...] = proj

    @pl.when(is_ctx)
    def _():
        n_tok = proj.shape[0]
        per_layer = n_tok * DA_HEADS
        for l in range(n_prev):
            k_ref[l * per_layer:(l + 1) * per_layer, :] = prev_refs[2 * l][...]
            v_ref[l * per_layer:(l + 1) * per_layer, :] = prev_refs[2 * l + 1][...]
        for h in range(DA_HEADS):
            rows = pl.ds(n_prev * per_layer + h, n_tok, stride=DA_HEADS)
            k_ref[rows, :] = proj[:, COL_K + h * DA_VDIM:COL_K + (h + 1) * DA_VDIM]
            v_ref[rows, :] = proj[:, COL_V + h * DA_VDIM:COL_V + (h + 1) * DA_VDIM]


def _inproj(xs, mod, g, w, prev_caches, n_ctx, ctx_len, lat_len):
    t = sum(x.shape[0] for x in xs)
    tm = TOKEN_TILE
    assert ctx_len == tm
    n_ctx_tiles = n_ctx
    n_prev = len(prev_caches) // 2
    row = _mod_row(n_ctx_tiles, lat_len // tm)
    seq_rows = ctx_len * DA_HEADS
    seq_block = lambda i: (jnp.minimum(i, n_ctx - 1), 0)
    cache_spec = pl.BlockSpec(((n_prev + 1) * seq_rows, DA_VDIM), seq_block)
    cache_shape = jax.ShapeDtypeStruct((n_ctx * (n_prev + 1) * seq_rows, DA_VDIM), F32)
    in_specs = _row_specs(len(xs), tm, D_MODEL, n_ctx_tiles) + [
        pl.BlockSpec((None, 6, D_MODEL), lambda i: (row(i), 0, 0)),
        pl.BlockSpec((1, D_MODEL), lambda i: (0, 0)),
        pl.BlockSpec((D_MODEL, PROJ_WIDTH), lambda i: (0, 0)),
    ] + [pl.BlockSpec((seq_rows, DA_VDIM), seq_block)] * (2 * n_prev)
    return pl.pallas_call(
        functools.partial(_inproj_body, len(xs), n_prev, n_ctx_tiles),
        grid=(t // tm,),
        in_specs=in_specs,
        out_specs=[pl.BlockSpec((tm, PROJ_WIDTH), lambda i: (i, 0)), cache_spec, cache_spec],
        out_shape=[jax.ShapeDtypeStruct((t, PROJ_WIDTH), F32), cache_shape, cache_shape],
        compiler_params=_params(1),
        name="inproj",
    )(*xs, mod, g, w, *prev_caches)


def _fourier_body(u_ref, cl_ref, sl_ref, cc_ref, sc_ref, w_ref, o_ref):
    z = u_ref[...].astype(BF16)
    a = _dot(z, cc_ref[...]).astype(BF16)
    b = _dot(z, sc_ref[...]).astype(BF16)
    y = _dot(cl_ref[...], a) - _dot(sl_ref[...], b)
    o_ref[...] = _dot(y.astype(BF16), w_ref[...]).astype(o_ref.dtype)


def _dft_tables(n, block):
    i = np.arange(n)
    prod = (i[:, None] % block) * (i[None, :] % block) % block
    ang = prod.astype(np.float64) * (2.0 * math.pi / block)
    same = (i[:, None] // block) == (i[None, :] // block)
    scale = 1.0 / math.sqrt(block)
    c = np.where(same, np.cos(ang) * scale, 0.0).astype(np.float32)
    s = np.where(same, np.sin(ang) * scale, 0.0).astype(np.float32)
    return jnp.asarray(c).astype(BF16), jnp.asarray(s).astype(BF16)


def _fourier(proj, row0, n_seq, seq_len, w_f):
    cl, sl = _dft_tables(seq_len, seq_len)
    cc, sc = _dft_tables(FN_WIDTH, FN_GROUP_DIM)
    blk0 = row0 // seq_len
    full = lambda shape: pl.BlockSpec(shape, lambda i: (0, 0))
    return pl.pallas_call(
        _fourier_body,
        grid=(n_seq,),
        in_specs=[
            pl.BlockSpec((seq_len, FN_WIDTH), lambda i: (blk0 + i, COL_FN // FN_WIDTH)),
            full((seq_len, seq_len)), full((seq_len, seq_len)),
            full((FN_WIDTH, FN_WIDTH)), full((FN_WIDTH, FN_WIDTH)), full((FN_WIDTH, FN_WIDTH)),
        ],
        out_specs=pl.BlockSpec((seq_len, FN_WIDTH), lambda i: (i, 0)),
        out_shape=jax.ShapeDtypeStruct((n_seq * seq_len, FN_WIDTH), BF16),
        compiler_params=_params(1),
        name=f"fourier_{seq_len}",
    )(proj, cl, sl, cc, sc, w_f)


def _lambda_full(lmb, lam_init):
    a = jnp.sum(lmb[0:1, :] * lmb[1:2, :], axis=-1, keepdims=True)
    b = jnp.sum(lmb[2:3, :] * lmb[3:4, :], axis=-1, keepdims=True)
    return jnp.exp(a) - jnp.exp(b) + lam_init


def _softmax_parts(parts):
    m = parts[0].max(axis=-1, keepdims=True)
    for p in parts[1:]:
        m = jnp.maximum(m, p.max(axis=-1, keepdims=True))
    es = [jnp.exp(p - m) for p in parts]
    tot = es[0].sum(axis=-1, keepdims=True)
    for e in es[1:]:
        tot = tot + e.sum(axis=-1, keepdims=True)
    return es, 1.0 / tot


def _diff_head(q, ks, vs, lam, g, lam_init):
    vas = [jnp.concatenate([v.astype(BF16), jnp.ones(v.shape, BF16)], axis=1) for v in vs]
    outs = []
    for m in range(2):
        qm = q[:, m * DA_HALF:(m + 1) * DA_HALF].astype(BF16)
        parts = [_dot_nt(qm, k[:, m * DA_HALF:(m + 1) * DA_HALF].astype(BF16)) for k in ks]
        mx = parts[0].max(axis=-1, keepdims=True)
        for p in parts[1:]:
            mx = jnp.maximum(mx, p.max(axis=-1, keepdims=True))
        oa = _dot(jnp.exp(parts[0] - mx).astype(BF16), vas[0])
        for p, va in zip(parts[1:], vas[1:]):
            oa = oa + _dot(jnp.exp(p - mx).astype(BF16), va)
        outs.append(oa[:, :DA_VDIM] * (1.0 / oa[:, DA_VDIM:DA_VDIM + 1]))
    a = outs[0] - lam * outs[1]
    ms = jnp.mean(a * a, axis=-1, keepdims=True)
    return a * lax.rsqrt(ms + EPS) * g * (1.0 - lam_init)


def _attn_ctx_body(lam_init, q_ref, k_ref, v_ref, lmb_ref, g_ref, o_ref):
    lam = _lambda_full(lmb_ref[...], lam_init)
    scale = DA_HALF ** -0.5
    for h in range(DA_HEADS):
        sl = slice(h * DA_VDIM, (h + 1) * DA_VDIM)
        o = _diff_head(q_ref[:, sl] * scale, [k_ref[:, sl]], [v_ref[:, sl]], lam, g_ref[...], lam_init)
        o_ref[:, sl] = o.astype(o_ref.dtype)


def _attn_ctx(proj, n_seq, seq_len, lmb, g, lam_init):
    blk = lambda c: pl.BlockSpec((seq_len, DA_WIDTH), lambda i: (i, c))
    return pl.pallas_call(
        functools.partial(_attn_ctx_body, lam_init),
        grid=(n_seq,),
        in_specs=[
            blk(COL_Q // DA_WIDTH), blk(COL_K // DA_WIDTH), blk(COL_V // DA_WIDTH),
            pl.BlockSpec((4, DA_HALF), lambda i: (0, 0)),
            pl.BlockSpec((1, DA_VDIM), lambda i: (0, 0)),
        ],
        out_specs=pl.BlockSpec((seq_len, DA_WIDTH), lambda i: (i, 0)),
        out_shape=jax.ShapeDtypeStruct((n_seq * seq_len, DA_WIDTH), BF16),
        compiler_params=_params(1),
        name="attn_ctx",
    )(proj, proj, proj, lmb, g)


def _rope(x, cos, sin):
    lane = lax.broadcasted_iota(jnp.int32, x.shape, 1)
    first = ((lane >> 4) & 1) == 0
    rot = jnp.where(first, -pltpu.roll(x, 128 - DA_HALF // 4, 1), pltpu.roll(x, DA_HALF // 4, 1))
    return x * cos + rot * sin


def _attn_lat_body(lam_init, q_ref, k_ref, v_ref, kc_ref, vc_ref, cq_ref, sq_ref, ck_ref, sk_ref,
                   lmb_ref, g_ref, o_ref):
    lam = _lambda_full(lmb_ref[...], lam_init)
    scale = DA_HALF ** -0.5
    for h in range(DA_HEADS):
        sl = slice(h * DA_VDIM, (h + 1) * DA_VDIM)
        q = _rope(q_ref[:, sl], cq_ref[...], sq_ref[...]) * scale
        k = _rope(k_ref[:, sl], ck_ref[...], sk_ref[...])
        o = _diff_head(q, [k, kc_ref[:, sl]], [v_ref[:, sl], vc_ref[:, sl]], lam, g_ref[...], lam_init)
        o_ref[:, sl] = o.astype(o_ref.dtype)


def _attn_lat(proj, row0, n_seq, seq_len, cache_k, cache_v, layer, cos, sin, lmb, g, lam_init):
    tq = 512
    nq = seq_len // tq
    past = cache_k.shape[2]
    qb0 = row0 // tq
    kb0 = row0 // seq_len
    cache_spec = pl.BlockSpec((None, None, past, DA_WIDTH), lambda b, j: (b, layer, 0, 0))
    kv_spec = lambda c: pl.BlockSpec((seq_len, DA_WIDTH), lambda b, j: (kb0 + b, c))
    return pl.pallas_call(
        functools.partial(_attn_lat_body, lam_init),
        grid=(n_seq, nq),
        in_specs=[
            pl.BlockSpec((tq, DA_WIDTH), lambda b, j: (qb0 + b * nq + j, COL_Q // DA_WIDTH)),
            kv_spec(COL_K // DA_WIDTH), kv_spec(COL_V // DA_WIDTH),
            cache_spec, cache_spec,
            pl.BlockSpec((tq, DA_VDIM), lambda b, j: (j, 0)),
            pl.BlockSpec((tq, DA_VDIM), lambda b, j: (j, 0)),
            pl.BlockSpec((seq_len, DA_VDIM), lambda b, j: (0, 0)),
            pl.BlockSpec((seq_len, DA_VDIM), lambda b, j: (0, 0)),
            pl.BlockSpec((4, DA_HALF), lambda b, j: (0, 0)),
            pl.BlockSpec((1, DA_VDIM), lambda b, j: (0, 0)),
        ],
        out_specs=pl.BlockSpec((tq, DA_WIDTH), lambda b, j: (b * nq + j, 0)),
        out_shape=jax.ShapeDtypeStruct((n_seq * seq_len, DA_WIDTH), BF16),
        compiler_params=_params(2),
        name="attn_lat",
    )(proj, proj, proj, cache_k, cache_v, cos, sin, cos, sin, lmb, g)


def _rope_tables(n_tokens):
    rows = n_tokens // GRID_W
    row = np.repeat(np.arange(rows, dtype=np.float64), GRID_W)
    col = np.tile(np.arange(GRID_W, dtype=np.float64), rows)
    axis_dim = DA_HALF // 2
    inv_freq = ROPE_THETA ** (-np.arange(0, axis_dim, 2, dtype=np.float64) / axis_dim)
    ang_r = row[:, None] * inv_freq[None, :]
    ang_c = col[:, None] * inv_freq[None, :]
    ang = np.concatenate([ang_r, ang_r, ang_c, ang_c] * 2, axis=-1)
    return jnp.asarray(np.cos(ang).astype(np.float32)), jnp.asarray(np.sin(ang).astype(np.float32))


def _hgrn_body(n_chunks, has_s0, *refs):
    if has_s0:
        (hq_ref, hi_ref, hf_ref, hb_ref, hg_ref, lb_ref, ng_ref, s0_ref,
         o_ref, oi_scr, qe_scr, u_scr, st_scr, dec_scr) = refs
        so_ref = None
    else:
        (hq_ref, hi_ref, hf_ref, hb_ref, hg_ref, lb_ref, ng_ref,
         o_ref, so_ref, oi_scr, qe_scr, u_scr, st_scr, dec_scr) = refs
    c = CHUNK
    w = HG_WIDTH
    gc = CHUNKS_PER_GROUP
    gl = gc * c
    c_bits = c.bit_length() - 1
    r_io = lax.broadcasted_iota(jnp.int32, (gl, gl), 0)
    c_io = lax.broadcasted_iota(jnp.int32, (gl, gl), 1)
    same_chunk = (r_io >> c_bits) == (c_io >> c_bits)
    tri_f = (same_chunk & (c_io <= r_io)).astype(BF16)
    tri_b = (same_chunk & (c_io >= r_io)).astype(BF16)
    lane = lax.broadcasted_iota(jnp.int32, (1, w), 1)
    dk_bits = HG_DK.bit_length() - 1
    head_masks = [((lane >> dk_bits) == h).astype(F32) for h in range(HG_HEADS)]
    head_masks_b = [hm.astype(BF16) for hm in head_masks]
    bd = ((lax.broadcasted_iota(jnp.int32, (w, w), 0) >> dk_bits)
          == (lax.broadcasted_iota(jnp.int32, (w, w), 1) >> dk_bits))
    t_of_row = lax.broadcasted_iota(jnp.int32, (HG_HEADS * gl, gl), 0) & (gl - 1)
    s_of_col = lax.broadcasted_iota(jnp.int32, (HG_HEADS * gl, gl), 1)
    same = (t_of_row >> c_bits) == (s_of_col >> c_bits)
    causal_f = same & (s_of_col <= t_of_row)
    causal_b = same & (s_of_col >= t_of_row)

    dirs = ((hf_ref, tri_f, causal_f, c - 1), (hb_ref, tri_b, causal_b, 0))

    def per_chunk_row(x, row):
        return jnp.concatenate([jnp.broadcast_to(x[k * c + row:k * c + row + 1, :], (c, w)) for k in range(gc)],
                               axis=0)

    def group_terms(gi):
        rows = pl.ds(gi * gl if isinstance(gi, int) else pl.multiple_of(gi * gl, gl), gl)
        q = _silu(hq_ref[rows, :]) * (HG_DK ** -0.5)
        v = hi_ref[rows, :].astype(BF16)
        for d, (fp_ref, tri, causal, edge_row) in enumerate(dirs):
            lb = lb_ref[d:d + 1, :]
            fp = fp_ref[rows, :]
            lsig = jnp.minimum(fp, 0.0) - jnp.log1p(jnp.exp(-jnp.abs(fp)))
            la = jnp.log(lb)
            lbb = jnp.log1p(-lb) + lsig
            logf = jnp.maximum(la, lbb) + jnp.log1p(jnp.exp(-jnp.abs(la - lbb)))
            kk = (1.0 - lb) * (1.0 / (1.0 + jnp.exp(fp)))
            g3 = _split(logf, 3)
            cum = _dot(tri, g3[0]) + _dot(tri, g3[1]) + _dot(tri, g3[2])
            total = per_chunk_row(cum, edge_row)
            ref = per_chunk_row(cum, c // 2)
            qc = (q * jnp.exp(cum - ref)).astype(BF16)
            kc = (kk * jnp.exp(ref - cum)).astype(BF16)
            ke = (kk * jnp.exp(total - cum)).astype(BF16)
            qe_scr[d, rows, :] = (q * jnp.exp(cum)).astype(BF16)
            lhs = jnp.concatenate([qc * hm for hm in head_masks_b], axis=0)
            a = jnp.where(causal, _dot_nt(lhs, kc), 0.0).astype(BF16)
            o_stack = _dot(a, v)
            o = o_stack[0:gl, :] * head_masks[0]
            for h in range(1, HG_HEADS):
                o = o + o_stack[h * gl:(h + 1) * gl, :] * head_masks[h]
            oi_scr[d, rows, :] = o
            for k in range(gc):
                ck = slice(k * c, (k + 1) * c)
                i = gi * gc + k
                dec_scr[d, pl.ds(i, 1), :] = jnp.exp(cum[k * c + edge_row:k * c + edge_row + 1, :])
                u_scr[d, i] = jnp.where(bd, _dot_tn(v[ck, :], ke[ck, :]), 0.0)

    if n_chunks == gc:
        group_terms(0)
    else:
        def terms_step(gi, carry):
            group_terms(gi)
            return carry

        lax.fori_loop(0, n_chunks // gc, terms_step, 0)

    slab = 32
    for d in range(2):
        order = range(n_chunks) if d == 0 else range(n_chunks - 1, -1, -1)
        for r in range(w // slab):
            rs = slice(r * slab, (r + 1) * slab)
            st = s0_ref[d, rs, :] if has_s0 else jnp.zeros((slab, w), F32)
            for i in order:
                st_scr[d, i, rs, :] = st.astype(BF16)
                st = st * dec_scr[d, i:i + 1, :] + u_scr[d, i, rs, :]
            if so_ref is not None:
                h = (r * slab) // HG_DK
                off = (r * slab) % HG_DK
                so_ref[d, h, off:off + slab, :] = st[:, h * HG_DK:(h + 1) * HG_DK]

    ones_bd = bd.astype(BF16)

    def finish(i, carry):
        rows = pl.ds(pl.multiple_of(i * c, c), c)
        o = (oi_scr[0, rows, :] + oi_scr[1, rows, :]
             + _dot_nt(qe_scr[0, rows, :], st_scr[0, i]) + _dot_nt(qe_scr[1, rows, :], st_scr[1, i]))
        sq = _split(o * o, 2)
        ms = (_dot(sq[0], ones_bd) + _dot(sq[1], ones_bd)) * (1.0 / HG_DK)
        y = o * lax.rsqrt(ms + EPS) * ng_ref[...]
        o_ref[rows, :] = (y * _silu(hg_ref[rows, :])).astype(o_ref.dtype)
        return carry

    lax.fori_loop(0, n_chunks, finish, 0, unroll=2)


def _hgrn(proj, row0, n_seq, seq_len, lb, ng, s0t):
    n_chunks = seq_len // CHUNK
    blk0 = row0 // seq_len
    col = lambda c0: pl.BlockSpec((seq_len, HG_WIDTH), lambda i: (blk0 + i, c0 // HG_WIDTH))
    in_specs = [col(COL_HQ), col(COL_HI), col(COL_HF), col(COL_HB), col(COL_HG),
                pl.BlockSpec((2, HG_WIDTH), lambda i: (0, 0)),
                pl.BlockSpec((1, HG_WIDTH), lambda i: (0, 0))]
    args = [proj, proj, proj, proj, proj, lb, ng]
    out_specs = [pl.BlockSpec((seq_len, HG_WIDTH), lambda i: (i, 0))]
    out_shape = [jax.ShapeDtypeStruct((n_seq * seq_len, HG_WIDTH), BF16)]
    if s0t is not None:
        in_specs.append(pl.BlockSpec((None, 2, HG_WIDTH, HG_WIDTH), lambda i: (i, 0, 0, 0)))
        args.append(s0t)
    else:
        out_specs.append(pl.BlockSpec((None, 2, HG_HEADS, HG_DK, HG_DK), lambda i: (i, 0, 0, 0, 0)))
        out_shape.append(jax.ShapeDtypeStruct((n_seq, 2, HG_HEADS, HG_DK, HG_DK), F32))
    return pl.pallas_call(
        functools.partial(_hgrn_body, n_chunks, s0t is not None),
        grid=(n_seq,),
        in_specs=in_specs,
        out_specs=out_specs,
        out_shape=out_shape,
        scratch_shapes=[pltpu.VMEM((2, seq_len, HG_WIDTH), F32),
                        pltpu.VMEM((2, seq_len, HG_WIDTH), BF16),
                        pltpu.VMEM((2, n_chunks, HG_WIDTH, HG_WIDTH), F32),
                        pltpu.VMEM((2, n_chunks, HG_WIDTH, HG_WIDTH), BF16),
                        pltpu.VMEM((2, max(n_chunks, 8), HG_WIDTH), F32)],
        compiler_params=_params(1),
        name=f"hgrn_{seq_len}",
    )(*args)


def _sigmoid_t(x):
    return 0.5 * jnp.tanh(0.5 * x) + 0.5


def _hgrn_tables():
    gl = CHUNKS_PER_GROUP * CHUNK
    t = np.arange(gl)
    same = (t[:, None] // CHUNK) == (t[None, :] // CHUNK)
    fwd = same & (t[None, :] <= t[:, None])
    bwd = same & (t[None, :] >= t[:, None])
    f = np.arange(HG_WIDTH)
    bd = (f[:, None] // HG_DK) == (f[None, :] // HG_DK)
    tri = jnp.asarray(np.stack([fwd, bwd]).astype(np.float32)).astype(BF16)
    causal = jnp.asarray(np.stack([fwd, bwd]).astype(np.float32))
    return tri, causal, jnp.asarray(bd.astype(np.float32))


def _hgrn2_body(n_seq, groups_per_seq, has_s0, *refs):
    (hq_ref, hi_ref, hf_ref, hb_ref, hg_ref, lb_ref, ng_ref, tri_ref, causal_ref, bd_ref) = refs[:10]
    if has_s0:
        s0_ref, o_ref, oi_scr, qe_scr, u_scr, st_scr, dec_scr = refs[10:]
        so_ref = None
    else:
        o_ref, so_ref, oi_scr, qe_scr, u_scr, st_scr, dec_scr = refs[10:]
        s0_ref = None
    c = CHUNK
    w = HG_WIDTH
    gc = CHUNKS_PER_GROUP
    gl = gc * c
    n_groups = n_seq * groups_per_seq
    chunks_per_seq = groups_per_seq * gc
    lane = lax.broadcasted_iota(jnp.int32, (1, w), 1)
    dk_bits = HG_DK.bit_length() - 1
    head_masks_b = [((lane >> dk_bits) == h).astype(BF16) for h in range(HG_HEADS)]
    bd = bd_ref[...] > 0.5
    edge_rows = (c - 1, 0)
    fp_refs = (hf_ref, hb_ref)

    def per_chunk_row(x, row):
        return jnp.concatenate([jnp.broadcast_to(x[k * c + row:k * c + row + 1, :], (c, w)) for k in range(gc)],
                               axis=0)

    def group_terms(gi):
        rows = pl.ds(gi * gl if isinstance(gi, int) else pl.multiple_of(gi * gl, gl), gl)
        hq = hq_ref[rows, :]
        q = hq * _sigmoid_t(hq) * (HG_DK ** -0.5)
        v = hi_ref[rows, :].astype(BF16)
        v_stack = jnp.concatenate([v * hm for hm in head_masks_b], axis=0)
        for d in range(2):
            lb = lb_ref[d:d + 1, :]
            fp = fp_refs[d][rows, :]
            lsig = jnp.minimum(fp, 0.0) - jnp.log(1.0 + jnp.exp(-jnp.abs(fp)))
            la = jnp.log(lb)
            lbb = jnp.log(1.0 - lb) + lsig
            logf = jnp.maximum(la, lbb) + jnp.log(1.0 + jnp.exp(-jnp.abs(la - lbb)))
            kk = (1.0 - lb) * _sigmoid_t(-fp)
            g_hi, g_lo = _split(logf, 2)
            cum = _dot(tri_ref[d], g_hi) + _dot(tri_ref[d], g_lo)
            total = per_chunk_row(cum, edge_rows[d])
            ref = per_chunk_row(cum, c // 2)
            qc = (q * jnp.exp(cum - ref)).astype(BF16)
            kc = (kk * jnp.exp(ref - cum)).astype(BF16)
            ke = (kk * jnp.exp(total - cum)).astype(BF16)
            qe_scr[d, rows, :] = (q * jnp.exp(cum)).astype(BF16)
            kc_stack = jnp.concatenate([kc * hm for hm in head_masks_b], axis=0)
            a = _dot_nt(qc, kc_stack)
            keep = causal_ref[d] > 0.5
            a = jnp.concatenate([jnp.where(keep, a[:, h * gl:(h + 1) * gl], 0.0) for h in range(HG_HEADS)],
                                axis=1).astype(BF16)
            oi_scr[d, rows, :] = _dot(a, v_stack)
            for k in range(gc):
                ck = slice(k * c, (k + 1) * c)
                i = gi * gc + k
                dec_scr[d, pl.ds(i, 1), :] = jnp.exp(cum[k * c + edge_rows[d]:k * c + edge_rows[d] + 1, :])
                u_scr[d, i] = jnp.where(bd, _dot_tn(v[ck, :], ke[ck, :]), 0.0)

    if n_groups <= 2:
        for gi in range(n_groups):
            group_terms(gi)
    else:
        def terms_step(gi, carry):
            group_terms(gi)
            return carry

        lax.fori_loop(0, n_groups, terms_step, 0)

    slab = 32
    for s in range(n_seq):
        first = s * chunks_per_seq
        for d in range(2):
            order = range(chunks_per_seq) if d == 0 else range(chunks_per_seq - 1, -1, -1)
            for r in range(w // slab):
                rs = slice(r * slab, (r + 1) * slab)
                st = s0_ref[s, d, rs, :] if has_s0 else jnp.zeros((slab, w), F32)
                for j in order:
                    i = first + j
                    st_scr[d, i, rs, :] = st.astype(BF16)
                    st = st * dec_scr[d, i:i + 1, :] + u_scr[d, i, rs, :]
                if so_ref is not None:
                    h = (r * slab) // HG_DK
                    off = (r * slab) % HG_DK
                    so_ref[s, d, h, off:off + slab, :] = st[:, h * HG_DK:(h + 1) * HG_DK]

    ones_bd = bd.astype(BF16)

    def finish(gi):
        rows = pl.ds(gi * gl if isinstance(gi, int) else pl.multiple_of(gi * gl, gl), gl)
        inter = []
        for k in range(gc):
            i = gi * gc + k
            ck = pl.ds(gi * gl + k * c if isinstance(gi, int) else pl.multiple_of(gi * gl + k * c, c), c)
            inter.append(_dot_nt(qe_scr[0, ck, :], st_scr[0, i]) + _dot_nt(qe_scr[1, ck, :], st_scr[1, i]))
        o = oi_scr[0, rows, :] + oi_scr[1, rows, :] + jnp.concatenate(inter, axis=0)
        sq = _split(o * o, 2)
        ms = (_dot(sq[0], ones_bd) + _dot(sq[1], ones_bd)) * (1.0 / HG_DK)
        y = o * lax.rsqrt(ms + EPS) * ng_ref[...]
        hg = hg_ref[rows, :]
        o_ref[rows, :] = (y * hg * _sigmoid_t(hg)).astype(o_ref.dtype)

    if n_groups <= 2:
        for gi in range(n_groups):
            finish(gi)
    else:
        def finish_step(gi, carry):
            finish(gi)
            return carry

        lax.fori_loop(0, n_groups, finish_step, 0)


def _hgrn2(proj, row0, n_seq, seq_len, seqs_per_step, lb, ng, s0t):
    gl = CHUNKS_PER_GROUP * CHUNK
    groups_per_seq = seq_len // gl
    n_chunks = seqs_per_step * seq_len // CHUNK
    rows = seqs_per_step * seq_len
    blk0 = row0 // rows
    tri, causal, bd = _hgrn_tables()
    col = lambda c0: pl.BlockSpec((rows, HG_WIDTH), lambda i: (blk0 + i, c0 // HG_WIDTH))
    const = lambda shape: pl.BlockSpec(shape, lambda i: (0,) * len(shape))
    in_specs = [col(COL_HQ), col(COL_HI), col(COL_HF), col(COL_HB), col(COL_HG),
                const((2, HG_WIDTH)), const((1, HG_WIDTH)),
                const((2, gl, gl)), const((2, gl, gl)), const((HG_WIDTH, HG_WIDTH))]
    args = [proj, proj, proj, proj, proj, lb, ng, tri, causal, bd]
    out_specs = [pl.BlockSpec((rows, HG_WIDTH), lambda i: (i, 0))]
    out_shape = [jax.ShapeDtypeStruct((n_seq * seq_len, HG_WIDTH), BF16)]
    if s0t is not None:
        in_specs.append(pl.BlockSpec((seqs_per_step, 2, HG_WIDTH, HG_WIDTH), lambda i: (i, 0, 0, 0)))
        args.append(s0t)
    else:
        out_specs.append(pl.BlockSpec((seqs_per_step, 2, HG_HEADS, HG_DK, HG_DK), lambda i: (i, 0, 0, 0, 0)))
        out_shape.append(jax.ShapeDtypeStruct((n_seq, 2, HG_HEADS, HG_DK, HG_DK), F32))
    return pl.pallas_call(
        functools.partial(_hgrn2_body, seqs_per_step, groups_per_seq, s0t is not None),
        grid=(n_seq // seqs_per_step,),
        in_specs=in_specs,
        out_specs=out_specs,
        out_shape=out_shape,
        scratch_shapes=[pltpu.VMEM((2, rows, HG_WIDTH), F32),
                        pltpu.VMEM((2, rows, HG_WIDTH), BF16),
                        pltpu.VMEM((2, n_chunks, HG_WIDTH, HG_WIDTH), F32),
                        pltpu.VMEM((2, n_chunks, HG_WIDTH, HG_WIDTH), BF16),
                        pltpu.VMEM((2, max(n_chunks, 8), HG_WIDTH), F32)],
        compiler_params=_params(1),
        name=f"hgrn_{seq_len}",
    )(*args)


def _ctx_mixers_body(n_seq, seq_len, lam_init, *refs):
    (q_ref, k_ref, v_ref, u_ref, hq_ref, hi_ref, hf_ref, hb_ref, hg_ref,
     cl_ref, sl_ref, cc_ref, sc_ref, wf_ref, lmb_ref, ag_ref,
     lb_ref, ng_ref, tri_ref, causal_ref, bd_ref,
     fn_ref, a_ref, ho_ref, so_ref) = refs[:25]
    scratch = refs[25:]

    z = u_ref[...].astype(BF16)
    za = _dot(z, cc_ref[...]).astype(BF16)
    zb = _dot(z, sc_ref[...]).astype(BF16)
    for s in range(n_seq):
        rs = slice(s * seq_len, (s + 1) * seq_len)
        y = _dot(cl_ref[...], za[rs, :]) - _dot(sl_ref[...], zb[rs, :])
        fn_ref[rs, :] = _dot(y.astype(BF16), wf_ref[...]).astype(fn_ref.dtype)

    lam = _lambda_full(lmb_ref[...], lam_init)
    scale = DA_HALF ** -0.5
    for s in range(n_seq):
        rs = slice(s * seq_len, (s + 1) * seq_len)
        for h in range(DA_HEADS):
            sl = slice(h * DA_VDIM, (h + 1) * DA_VDIM)
            o = _diff_head(q_ref[rs, sl] * scale, [k_ref[rs, sl]], [v_ref[rs, sl]], lam, ag_ref[...], lam_init)
            a_ref[rs, sl] = o.astype(a_ref.dtype)

    _hgrn2_body(n_seq, seq_len // (CHUNKS_PER_GROUP * CHUNK), False,
                hq_ref, hi_ref, hf_ref, hb_ref, hg_ref, lb_ref, ng_ref, tri_ref, causal_ref, bd_ref,
                ho_ref, so_ref, *scratch)


def _ctx_mixers(proj, n_seq, seq_len, seqs_per_step, w_f, lmb, ag, lam_init, lb, ng):
    rows = seqs_per_step * seq_len
    n_chunks = rows // CHUNK
    gl = CHUNKS_PER_GROUP * CHUNK
    cl, sl = _dft_tables(seq_len, seq_len)
    cc, sc = _dft_tables(FN_WIDTH, FN_GROUP_DIM)
    tri, causal, bd = _hgrn_tables()
    col = lambda c0, width: pl.BlockSpec((rows, width), lambda i: (i, c0 // width))
    const = lambda shape: pl.BlockSpec(shape, lambda i: (0,) * len(shape))
    out_rows = lambda width: pl.BlockSpec((rows, width), lambda i: (i, 0))
    n_tok = n_seq * seq_len
    return pl.pallas_call(
        functools.partial(_ctx_mixers_body, seqs_per_step, seq_len, lam_init),
        grid=(n_seq // seqs_per_step,),
        in_specs=[col(COL_Q, DA_WIDTH), col(COL_K, DA_WIDTH), col(COL_V, DA_WIDTH), col(COL_FN, FN_WIDTH),
                  col(COL_HQ, HG_WIDTH), col(COL_HI, HG_WIDTH), col(COL_HF, HG_WIDTH), col(COL_HB, HG_WIDTH),
                  col(COL_HG, HG_WIDTH),
                  const((seq_len, seq_len)), const((seq_len, seq_len)),
                  const((FN_WIDTH, FN_WIDTH)), const((FN_WIDTH, FN_WIDTH)), const((FN_WIDTH, FN_WIDTH)),
                  const((4, DA_HALF)), const((1, DA_VDIM)),
                  const((2, HG_WIDTH)), const((1, HG_WIDTH)),
                  const((2, gl, gl)), const((2, gl, gl)), const((HG_WIDTH, HG_WIDTH))],
        out_specs=[out_rows(FN_WIDTH), out_rows(DA_WIDTH), out_rows(HG_WIDTH),
                   pl.BlockSpec((seqs_per_step, 2, HG_HEADS, HG_DK, HG_DK), lambda i: (i, 0, 0, 0, 0))],
        out_shape=[jax.ShapeDtypeStruct((n_tok, FN_WIDTH), BF16),
                   jax.ShapeDtypeStruct((n_tok, DA_WIDTH), BF16),
                   jax.ShapeDtypeStruct((n_tok, HG_WIDTH), BF16),
                   jax.ShapeDtypeStruct((n_seq, 2, HG_HEADS, HG_DK, HG_DK), F32)],
        scratch_shapes=[pltpu.VMEM((2, rows, HG_WIDTH), F32),
                        pltpu.VMEM((2, rows, HG_WIDTH), BF16),
                        pltpu.VMEM((2, n_chunks, HG_WIDTH, HG_WIDTH), F32),
                        pltpu.VMEM((2, n_chunks, HG_WIDTH, HG_WIDTH), BF16),
                        pltpu.VMEM((2, max(n_chunks, 8), HG_WIDTH), F32)],
        compiler_params=_params(1),
        name="ctx_mixers",
    )(proj, proj, proj, proj, proj, proj, proj, proj, proj, cl, sl, cc, sc, w_f, lmb, ag, lb, ng, tri, causal, bd)


def _outproj_body(n_x, n_ctx_tiles, *refs):
    fn_refs, a_refs, hg_refs = refs[0:2], refs[2:4], refs[4:6]
    x_refs = refs[6:6 + n_x]
    mod_ref, g_ref, w_ref, wr_ref, xo_ref, h_ref, lt_ref = refs[6 + n_x:]
    is_ctx = pl.program_id(0) < n_ctx_tiles
    mix = (_dot(_pick(is_ctx, fn_refs), w_ref[0:FN_WIDTH, :])
           + _dot(_pick(is_ctx, a_refs), w_ref[FN_WIDTH:FN_WIDTH + DA_WIDTH, :])
           + _dot(_pick(is_ctx, hg_refs), w_ref[FN_WIDTH + DA_WIDTH:, :]))
    x = _pick(is_ctx, x_refs) + mod_ref[2:3, :] * mix
    xo_ref[...] = x
    h = _modnorm(x, g_ref[...], mod_ref[3:4, :], mod_ref[4:5, :])
    h_ref[...] = h.astype(BF16)
    h_hi, h_lo = _split(h, 2)
    w_hi, w_lo = _split(wr_ref[...], 2)
    lt_ref[...] = _dot_nt(w_hi, h_hi) + _dot_nt(w_lo, h_hi) + _dot_nt(w_hi, h_lo)


def _outproj(fn, a, hg, xs, mod, g, w_out, w_router_t, n_ctx_tok, lat_len):
    t = sum(x.shape[0] for x in xs)
    tm = TOKEN_TILE
    n_ctx_tiles = n_ctx_tok // tm
    row = _mod_row(n_ctx_tiles, lat_len // tm)
    rows = lambda width: pl.BlockSpec((tm, width), lambda i: (i, 0))
    parts = lambda n, width: _row_specs(n, tm, width, n_ctx_tiles)
    return pl.pallas_call(
        functools.partial(_outproj_body, len(xs), n_ctx_tiles),
        grid=(t // tm,),
        in_specs=parts(2, FN_WIDTH) + parts(2, DA_WIDTH) + parts(2, HG_WIDTH) + parts(len(xs), D_MODEL) + [
            pl.BlockSpec((None, 6, D_MODEL), lambda i: (row(i), 0, 0)),
            pl.BlockSpec((1, D_MODEL), lambda i: (0, 0)),
            pl.BlockSpec((D_MODEL, D_MODEL), lambda i: (0, 0)),
            pl.BlockSpec((N_EXPERTS, D_MODEL), lambda i: (0, 0)),
        ],
        out_specs=[rows(D_MODEL), rows(D_MODEL), pl.BlockSpec((N_EXPERTS, tm), lambda i: (0, i))],
        out_shape=[jax.ShapeDtypeStruct((t, D_MODEL), F32),
                   jax.ShapeDtypeStruct((t, D_MODEL), BF16),
                   jax.ShapeDtypeStruct((N_EXPERTS, t), F32)],
        compiler_params=_params(1),
        name="outproj",
    )(*fn, *a, *hg, *xs, mod, g, w_out, w_router_t)


def _route_body(lt_ref, bias_ref, o_ref, rank_ref, cnt_ref):
    per = N_EXPERTS // N_GROUPS
    tt = lt_ref.shape[1]
    neg = -jnp.inf
    assert per == N_GROUPS == 8
    gi = lax.broadcasted_iota(jnp.int32, (N_GROUPS, tt), 0).astype(F32)
    s_j, b_j = [], []
    for j in range(per):
        s = _sigmoid(lt_ref[j * N_GROUPS:(j + 1) * N_GROUPS, :])
        s_j.append(s)
        b_j.append(s + bias_ref[j * N_GROUPS:(j + 1) * N_GROUPS, :])
    m1 = functools.reduce(jnp.maximum, b_j)
    i1 = functools.reduce(jnp.minimum, [jnp.where(b_j[j] == m1, float(j), float(per)) for j in range(per)])
    m2 = functools.reduce(jnp.maximum, [jnp.where(i1 == float(j), neg, b_j[j]) for j in range(per)])
    gs = m1 + m2
    gsel = jnp.zeros((N_GROUPS, tt), jnp.bool_)
    for _ in range(TOPK_GROUPS):
        m = gs.max(axis=0, keepdims=True)
        idx = jnp.where(gs == m, gi, float(N_GROUPS)).min(axis=0, keepdims=True)
        hit = gi == idx
        gsel = gsel | hit
        gs = jnp.where(hit, neg, gs)
    x_j = [jnp.where(gsel, b_j[j], neg) for j in range(per)]
    e_j = [gi * per + j for j in range(per)]
    sel_j = [jnp.zeros((N_GROUPS, tt), jnp.bool_) for _ in range(per)]
    for _ in range(TOP_K):
        m = functools.reduce(jnp.maximum, x_j).max(axis=0, keepdims=True)
        idx = functools.reduce(jnp.minimum, [jnp.where(x_j[j] == m, e_j[j], float(N_EXPERTS))
                                             for j in range(per)]).min(axis=0, keepdims=True)
        for j in range(per):
            hit = e_j[j] == idx
            sel_j[j] = sel_j[j] | hit
            x_j[j] = jnp.where(hit, neg, x_j[j])
    w_j = [jnp.where(sel_j[j], s_j[j], 0.0) for j in range(per)]
    denom = functools.reduce(lambda a, b: a + b, w_j).sum(axis=0, keepdims=True)
    gates_t = jnp.concatenate([w / denom * ROUTED_SCALE for w in w_j], axis=0)
    o_ref[...] = gates_t
    sel = (gates_t > 0.0).astype(BF16)
    before = (lax.broadcasted_iota(jnp.int32, (tt, tt), 0)
              < lax.broadcasted_iota(jnp.int32, (tt, tt), 1)).astype(BF16)
    rank = _dot(sel, before)
    rank_ref[...] = rank
    cnt_ref[...] = rank[:, tt - 1:tt] + sel[:, tt - 1:tt].astype(F32)


def _route(logits_t, bias):
    t = logits_t.shape[1]
    tt = ROUTE_TILE
    cols = pl.BlockSpec((N_EXPERTS, tt), lambda i: (0, i))
    return pl.pallas_call(
        _route_body,
        grid=(t // tt,),
        in_specs=[cols, pl.BlockSpec((N_EXPERTS, 1), lambda i: (0, 0))],
        out_specs=[cols, cols, pl.BlockSpec((N_EXPERTS, 1), lambda i: (i, 0))],
        out_shape=[jax.ShapeDtypeStruct((N_EXPERTS, t), F32), jax.ShapeDtypeStruct((N_EXPERTS, t), F32),
                   jax.ShapeDtypeStruct((t // tt * N_EXPERTS, 1), F32)],
        compiler_params=_params(1),
        name="route",
    )(logits_t, bias)


def _moe_body(final_norm, n_ctx_tiles, h_ref, gate_ref, wg_ref, wu_ref, wd_ref, sg_ref, su_ref, sd_ref,
              x_ref, mod_ref, fg_ref, *out_and_scratch):
    acc_ref = out_and_scratch[-1]
    o_refs = out_and_scratch[:-1]
    is_ctx = pl.program_id(0) < n_ctx_tiles
    j = pl.program_id(1)
    h = h_ref[...]

    def act_of(wg, wu, gate):
        gu = _dot(h, jnp.concatenate([wg.astype(BF16), wu.astype(BF16)], axis=1))
        a = _silu(gu[:, :D_EXPERT]) * gu[:, D_EXPERT:]
        return a if gate is None else a * gate

    @pl.when(j == 0)
    def _():
        acc_ref[...] = _dot(act_of(sg_ref[...], su_ref[...], None).astype(BF16), sd_ref[...].astype(BF16))

    gates = gate_ref[...]
    expert_of_lane = lax.broadcasted_iota(jnp.int32, gates.shape, 1)

    def gate_col(p):
        e = j * EXPERTS_PER_STEP + p
        return jnp.sum(jnp.where(expert_of_lane == e, gates, 0.0), axis=1, keepdims=True)

    for p in range(EXPERTS_PER_STEP // 2):
        a0 = act_of(wg_ref[2 * p], wu_ref[2 * p], gate_col(2 * p))
        a1 = act_of(wg_ref[2 * p + 1], wu_ref[2 * p + 1], gate_col(2 * p + 1))
        pair = jnp.concatenate([a0, a1], axis=1).astype(BF16)
        acc_ref[...] += _dot(pair, wd_ref[p].astype(BF16))

    @pl.when(j == pl.num_programs(1) - 1)
    def _():
        x = x_ref[...] + mod_ref[5:6, :] * acc_ref[...]
        if not final_norm:
            o_refs[0][...] = x
        else:
            ms = jnp.mean(x * x, axis=-1, keepdims=True)
            y = x * lax.rsqrt(ms + EPS) * fg_ref[...]

            @pl.when(is_ctx)
            def _():
                o_refs[0][...] = y

            @pl.when(jnp.logical_not(is_ctx))
            def _():
                o_refs[1][...] = y


def _moe(h, gates, w_gate, w_up, w_down2, ws_gate, ws_up, ws_down, layer, x, mod, final_g, final_norm,
         n_ctx_tok, lat_len):
    t = x.shape[0]
    tm = MOE_TILE
    eps_ = EXPERTS_PER_STEP
    n_ctx_tiles = n_ctx_tok // tm
    row = _mod_row(n_ctx_tiles, lat_len // tm)
    if final_norm:
        out_specs = _row_specs(2, tm, D_MODEL, n_ctx_tiles)
        out_shape = [jax.ShapeDtypeStruct((n_ctx_tok, D_MODEL), F32),
                     jax.ShapeDtypeStruct((t - n_ctx_tok, D_MODEL), F32)]
    else:
        out_specs = [pl.BlockSpec((tm, D_MODEL), lambda i, j: (i, 0))]
        out_shape = [jax.ShapeDtypeStruct((t, D_MODEL), F32)]
    return pl.pallas_call(
        functools.partial(_moe_body, final_norm, n_ctx_tiles),
        grid=(t // tm, N_EXPERTS // eps_),
        in_specs=[
            pl.BlockSpec((tm, D_MODEL), lambda i, j: (i, 0)),
            pl.BlockSpec((tm, N_EXPERTS), lambda i, j: (i, 0)),
            pl.BlockSpec((None, eps_, D_MODEL, D_EXPERT), lambda i, j: (layer, j, 0, 0)),
            pl.BlockSpec((None, eps_, D_MODEL, D_EXPERT), lambda i, j: (layer, j, 0, 0)),
            pl.BlockSpec((None, eps_ // 2, 2 * D_EXPERT, D_MODEL), lambda i, j: (layer, j, 0, 0)),
            pl.BlockSpec((None, D_MODEL, D_EXPERT), lambda i, j: (layer, 0, 0)),
            pl.BlockSpec((None, D_MODEL, D_EXPERT), lambda i, j: (layer, 0, 0)),
            pl.BlockSpec((None, D_EXPERT, D_MODEL), lambda i, j: (layer, 0, 0)),
            pl.BlockSpec((tm, D_MODEL), lambda i, j: (i, 0)),
            pl.BlockSpec((None, 6, D_MODEL), lambda i, j: (row(i), 0, 0)),
            pl.BlockSpec((1, D_MODEL), lambda i, j: (0, 0)),
        ],
        out_specs=out_specs,
        out_shape=out_shape,
        scratch_shapes=[pltpu.VMEM((tm, D_MODEL), F32)],
        compiler_params=_params(2),
        name="moe",
    )(h, gates, w_gate, w_up, w_down2, ws_gate, ws_up, ws_down, x, mod, final_g)


def _expert_cast_body(wg_ref, wu_ref, wd_ref, gu_ref, d_ref):
    gu_ref[:, :D_EXPERT] = wg_ref[...].astype(BF16)
    gu_ref[:, D_EXPERT:] = wu_ref[...].astype(BF16)
    d_ref[...] = wd_ref[...].astype(BF16)


def _expert_cast(w_gate, w_up, w_down):
    per = N_EXPERTS // N_GROUPS
    src = lambda l, r: (l, (r % N_GROUPS) * per + r // N_GROUPS, 0, 0)
    dst = lambda l, r: (l, r, 0, 0)
    return pl.pallas_call(
        _expert_cast_body,
        grid=(DEPTH, N_EXPERTS),
        in_specs=[pl.BlockSpec((None, None, D_MODEL, D_EXPERT), src),
                  pl.BlockSpec((None, None, D_MODEL, D_EXPERT), src),
                  pl.BlockSpec((None, None, D_EXPERT, D_MODEL), src)],
        out_specs=[pl.BlockSpec((None, None, D_MODEL, 2 * D_EXPERT), dst),
                   pl.BlockSpec((None, None, D_EXPERT, D_MODEL), dst)],
        out_shape=[jax.ShapeDtypeStruct((DEPTH, N_EXPERTS, D_MODEL, 2 * D_EXPERT), BF16),
                   jax.ShapeDtypeStruct((DEPTH, N_EXPERTS, D_EXPERT, D_MODEL), BF16)],
        compiler_params=_params(2),
        name="expert_cast",
    )(w_gate, w_up, w_down)


SLOT_ALIGN = 16
WIN = 128
ROW_BLOCK = 512
ZERO_TAIL = 640


def _slot_capacity(tt):
    most = TOP_K * tt + N_EXPERTS * (SLOT_ALIGN - 1)
    return -(-(most + ZERO_TAIL) // ROW_BLOCK) * ROW_BLOCK + WIN


def _moe_sparse_body(final_norm, n_ctx_tiles, cnt_ref, h_ref, gate_ref, rank_ref, wgu_ref, wd_ref,
                     sg_ref, su_ref, sd_ref, x_ref, mod_ref, fg_ref, *rest):
    o_refs = rest[:-4]
    pg_scr, xs_scr, acc_scr, offs = rest[-4:]
    i = pl.program_id(0)
    c = pl.program_id(1)
    is_ctx = i < n_ctx_tiles
    tt = h_ref.shape[0]
    eps_ = wgu_ref.shape[0]

    def count(r):
        return cnt_ref[i * N_EXPERTS + r]

    def n_windows(n):
        return (n + (WIN - 1)) // WIN

    @pl.when(c == 0)
    def _():
        def set_offset(r, run):
            offs[r] = run
            return run + ((count(r) + (SLOT_ALIGN - 1)) // SLOT_ALIGN) * SLOT_ALIGN

        total = lax.fori_loop(0, N_EXPERTS, set_offset, 0)
        offs[N_EXPERTS] = total

        slot_io = lax.broadcasted_iota(jnp.int32, (WIN, tt), 0)

        def build(r, carry):
            off = offs[r]
            rank_r = rank_ref[pl.ds(r, 1), :]
            gate_r = gate_ref[pl.ds(r, 1), :]
            live = gate_r > 0.0

            def window(b, carry2):
                mine = ((slot_io + b * WIN).astype(F32) == rank_r) & live
                rows = pl.ds(pl.multiple_of(off + b * WIN, SLOT_ALIGN), WIN)
                pg_scr[rows, :] = jnp.where(mine, gate_r, 0.0).astype(BF16)
                return carry2

            lax.fori_loop(0, n_windows(count(r)), window, 0)
            return carry

        lax.fori_loop(0, N_EXPERTS, build, 0)
        tail = pl.ds(pl.multiple_of(total, SLOT_ALIGN), ZERO_TAIL)
        pg_scr[tail, :] = jnp.zeros((ZERO_TAIL, tt), BF16)

        def gather(kb, carry):
            rows = pl.ds(pl.multiple_of(kb * ROW_BLOCK, ROW_BLOCK), ROW_BLOCK)
            one_hot = jnp.where(pg_scr[rows, :].astype(F32) > 0.0, 1.0, 0.0).astype(BF16)
            xs_scr[rows, :] = _dot(one_hot, h_ref[...]).astype(BF16)
            return carry

        lax.fori_loop(0, (total + (ROW_BLOCK - 1)) // ROW_BLOCK, gather, 0)

        h = h_ref[...]
        gu = _dot(h, jnp.concatenate([sg_ref[...].astype(BF16), su_ref[...].astype(BF16)], axis=1))
        act = gu[:, :D_EXPERT] * _sigmoid_t(gu[:, :D_EXPERT]) * gu[:, D_EXPERT:]
        acc_scr[...] = _dot(act.astype(BF16), sd_ref[...].astype(BF16))

    row_io = lax.broadcasted_iota(jnp.int32, (WIN, D_MODEL), 0)
    def expert_rows(p, x):
        gu = _dot(x, wgu_ref[p])
        act = gu[:, :D_EXPERT] * _sigmoid_t(gu[:, :D_EXPERT]) * gu[:, D_EXPERT:]
        return _dot(act.astype(BF16), wd_ref[p]).astype(BF16)

    offsets = [offs[c * eps_ + p] for p in range(eps_)]
    counts = [count(c * eps_ + p) for p in range(eps_)]
    wins = [pl.ds(pl.multiple_of(offsets[p], SLOT_ALIGN), WIN) for p in range(eps_)]
    xs = [xs_scr[wins[p], :] for p in range(eps_)]
    ys = [expert_rows(p, xs[p]) for p in range(eps_)]
    for p in range(eps_):
        xs_scr[wins[p], :] = jnp.where(row_io < counts[p], ys[p], xs[p])

    for p in range(eps_):
        def window(b, carry, p=p):
            rows = pl.ds(pl.multiple_of(offsets[p] + b * WIN, SLOT_ALIGN), WIN)
            x = xs_scr[rows, :]
            xs_scr[rows, :] = jnp.where(row_io + b * WIN < counts[p], expert_rows(p, x), x)
            return carry

        lax.fori_loop(1, n_windows(counts[p]), window, 0)

    @pl.when(c == pl.num_programs(1) - 1)
    def _():
        def combine(kb, carry):
            rows = pl.ds(pl.multiple_of(kb * ROW_BLOCK, ROW_BLOCK), ROW_BLOCK)
            acc_scr[...] += _dot_tn(pg_scr[rows, :], xs_scr[rows, :])
            return carry

        lax.fori_loop(0, (offs[N_EXPERTS] + (ROW_BLOCK - 1)) // ROW_BLOCK, combine, 0)
        x = x_ref[...] + mod_ref[5:6, :] * acc_scr[...]
        if not final_norm:
            o_refs[0][...] = x
        else:
            ms = jnp.mean(x * x, axis=-1, keepdims=True)
            y = x * lax.rsqrt(ms + EPS) * fg_ref[...]

            @pl.when(is_ctx)
            def _():
                o_refs[0][...] = y

            @pl.when(jnp.logical_not(is_ctx))
            def _():
                o_refs[1][...] = y


def _moe_sparse(h, gates_t, rank_t, counts, wgu, wd, ws_gate, ws_up, ws_down, layer, x, mod, final_g,
                final_norm, n_ctx_tok, lat_len):
    t = x.shape[0]
    tt = ROUTE_TILE
    eps_ = EXPERTS_PER_STEP
    n_ctx_tiles = n_ctx_tok // tt
    row = _mod_row(n_ctx_tiles, lat_len // tt)
    cap = _slot_capacity(tt)
    if final_norm:
        out_specs = [pl.BlockSpec((tt, D_MODEL), lambda i, c, cnt: (jnp.minimum(i, n_ctx_tiles - 1), 0)),
                     pl.BlockSpec((tt, D_MODEL), lambda i, c, cnt: (jnp.maximum(i - n_ctx_tiles, 0), 0))]
        out_shape = [jax.ShapeDtypeStruct((n_ctx_tok, D_MODEL), F32),
                     jax.ShapeDtypeStruct((t - n_ctx_tok, D_MODEL), F32)]
    else:
        out_specs = [pl.BlockSpec((tt, D_MODEL), lambda i, c, cnt: (i, 0))]
        out_shape = [jax.ShapeDtypeStruct((t, D_MODEL), F32)]
    grid_spec = pltpu.PrefetchScalarGridSpec(
        num_scalar_prefetch=1,
        grid=(t // tt, N_EXPERTS // eps_),
        in_specs=[
            pl.BlockSpec((tt, D_MODEL), lambda i, c, cnt: (i, 0)),
            pl.BlockSpec((N_EXPERTS, tt), lambda i, c, cnt: (0, i)),
            pl.BlockSpec((N_EXPERTS, tt), lambda i, c, cnt: (0, i)),
            pl.BlockSpec((None, eps_, D_MODEL, 2 * D_EXPERT), lambda i, c, cnt: (layer, c, 0, 0)),
            pl.BlockSpec((None, eps_, D_EXPERT, D_MODEL), lambda i, c, cnt: (layer, c, 0, 0)),
            pl.BlockSpec((None, D_MODEL, D_EXPERT), lambda i, c, cnt: (layer, 0, 0)),
            pl.BlockSpec((None, D_MODEL, D_EXPERT), lambda i, c, cnt: (layer, 0, 0)),
            pl.BlockSpec((None, D_EXPERT, D_MODEL), lambda i, c, cnt: (layer, 0, 0)),
            pl.BlockSpec((tt, D_MODEL), lambda i, c, cnt: (i, 0)),
            pl.BlockSpec((None, 6, D_MODEL), lambda i, c, cnt: (row(i), 0, 0)),
            pl.BlockSpec((1, D_MODEL), lambda i, c, cnt: (0, 0)),
        ],
        out_specs=out_specs,
        scratch_shapes=[pltpu.VMEM((cap, tt), BF16),
                        pltpu.VMEM((cap, D_MODEL), BF16), pltpu.VMEM((tt, D_MODEL), F32),
                        pltpu.SMEM((N_EXPERTS + 1,), jnp.int32)],
    )
    return pl.pallas_call(
        functools.partial(_moe_sparse_body, final_norm, n_ctx_tiles),
        grid_spec=grid_spec,
        out_shape=out_shape,
        compiler_params=_params(2),
        name="moe_sparse",
    )(counts, h, gates_t, rank_t, wgu, wd, ws_gate, ws_up, ws_down, x, mod, final_g)


def _block_diag_t(s):
    eye = jnp.eye(HG_HEADS, dtype=s.dtype)
    out = jnp.einsum('...hkv,hg->...hvgk', s, eye)
    return out.reshape(s.shape[:-3] + (HG_WIDTH, HG_WIDTH))


def kernel(x_prompt, x_sample, cache_k, cache_v, state_hgrn, c, c_ctx, w_ada, b_ada, norm_g, w_in,
           w_fourier, lambdas, attn_norm_g, lower_bounds, hg_norm_g, w_out, w_router, router_bias,
           w_gate, w_up, w_down, ws_gate, ws_up, ws_down, final_g):
    n_ctx, ctx_len, _ = x_prompt.shape
    n_lat, lat_len, _ = x_sample.shape
    n_ctx_tok = n_ctx * ctx_len
    past = cache_k.shape[2]

    xs = (x_prompt.reshape(n_ctx_tok, D_MODEL), x_sample.reshape(n_lat * lat_len, D_MODEL))

    c8 = jnp.zeros((8, D_MODEL), F32).at[0].set(c_ctx).at[1:1 + n_lat].set(c)
    mods = _ada_mods(c8, w_ada, b_ada).reshape(DEPTH, 8, 6, D_MODEL)

    cs = jnp.cumsum(jax.nn.softmax(lower_bounds.astype(F32), axis=0), axis=0)
    lbs = cs - cs[0:1]

    cos, sin = _rope_tables(lat_len)
    cache_k4 = cache_k.reshape(n_lat, DEPTH, past, DA_WIDTH)
    cache_v4 = cache_v.reshape(n_lat, DEPTH, past, DA_WIDTH)
    s0t = _block_diag_t(state_hgrn.astype(F32))

    w_in_b = jnp.concatenate([w_in[:, :, FN_WIDTH:], w_in[:, :, :FN_WIDTH]], axis=-1).astype(BF16)
    w_f_b = w_fourier.astype(BF16)
    w_out_b = w_out.astype(BF16)
    per = N_EXPERTS // N_GROUPS
    w_router_t = (jnp.swapaxes(w_router, 1, 2).reshape(DEPTH, N_GROUPS, per, D_MODEL)
                  .swapaxes(1, 2).reshape(DEPTH, N_EXPERTS, D_MODEL))
    bias_mm = router_bias.reshape(DEPTH, N_GROUPS, per).swapaxes(1, 2).reshape(DEPTH, N_EXPERTS, 1)
    wgu_b, wd_b = _expert_cast(w_gate, w_up, w_down)
    ng = jnp.tile(hg_norm_g, (1, HG_HEADS))

    caches = []
    new_s = []
    for l in range(DEPTH):
        lam_init = 0.8 - 0.6 * math.exp(-0.3 * l)
        mod = mods[l]
        last = l == DEPTH - 1
        proj, k_l, v_l = _inproj(xs, mod, norm_g[l, 0:1], w_in_b[l], tuple(caches) if last else (),
                                 n_ctx, ctx_len, lat_len)
        caches = [k_l, v_l] if last else caches + [k_l, v_l]

        ag = attn_norm_g[l].reshape(1, DA_VDIM)
        fn_ctx, a_ctx, hg_ctx, st_ctx = _ctx_mixers(proj, n_ctx, ctx_len, 2, w_f_b[l], lambdas[l], ag,
                                                    lam_init, lbs[l], ng[l:l + 1])
        fn_lat = _fourier(proj, n_ctx_tok, n_lat, lat_len, w_f_b[l])
        a_lat = _attn_lat(proj, n_ctx_tok, n_lat, lat_len, cache_k4, cache_v4, l, cos, sin, lambdas[l], ag,
                          lam_init)
        (hg_lat,) = _hgrn2(proj, n_ctx_tok, n_lat, lat_len, 1, lbs[l], ng[l:l + 1], s0t[:, l])
        new_s.append(jnp.swapaxes(st_ctx, -1, -2))

        x, h2, logits_t = _outproj((fn_ctx, fn_lat), (a_ctx, a_lat), (hg_ctx, hg_lat), xs, mod,
                                   norm_g[l, 1:2], w_out_b[l], w_router_t[l], n_ctx_tok, lat_len)
        gates_t, rank_t, counts = _route(logits_t, bias_mm[l])
        xs = _moe_sparse(h2, gates_t, rank_t, counts.reshape(-1).astype(jnp.int32), wgu_b, wd_b,
                         ws_gate, ws_up, ws_down, l, x, mod, final_g.reshape(1, D_MODEL), l == DEPTH - 1,
                         n_ctx_tok, lat_len)

    y_prompt = xs[0].reshape(n_ctx, ctx_len, D_MODEL)
    y_sample = xs[1].reshape(n_lat, lat_len, D_MODEL)
    new_k = caches[0].reshape(n_ctx, DEPTH, ctx_len, DA_HEADS, DA_VDIM)
    new_v = caches[1].reshape(n_ctx, DEPTH, ctx_len, DA_HEADS, DA_VDIM)
    return (y_prompt, y_sample, new_k, new_v, jnp.stack(new_s, axis=1))
```

```python
import functools
import math

import numpy as np
import jax
import jax.numpy as jnp
from jax import lax
from jax.experimental import pallas as pl
from jax.experimental.pallas import tpu as pltpu

F32 = jnp.float32
BF16 = jnp.bfloat16

D_MODEL = 1024
DEPTH = 2
GRID_W = 64
FN_WIDTH = 256
FN_GROUPS = 4
FN_GROUP_DIM = 64
DA_WIDTH = 512
DA_HEADS = 4
DA_VDIM = 128
DA_HALF = 64
HG_WIDTH = 256
HG_HEADS = 4
HG_DK = 64
PROJ_WIDTH = 3072
CHUNK = 64
ROPE_THETA = 10000.0
N_EXPERTS = 64
TOP_K = 8
N_GROUPS = 8
TOPK_GROUPS = 4
D_EXPERT = 128
ROUTED_SCALE = 2.5
EPS = 1e-6

COL_Q, COL_K, COL_V, COL_FN = 0, 512, 1024, 1536
PROJ_A_WIDTH = 1792
COL_HQ, COL_HI, COL_HF, COL_HB, COL_HG = 0, 256, 512, 768, 1024
PROJ_H_WIDTH = 1280

TOKEN_TILE = 512
MOE_TILE = 2048
EXPERTS_PER_STEP = 4
ROUTE_TILE = 512
CHUNKS_PER_GROUP = 4
VMEM_LIMIT = 56 * 1024 * 1024


def _dot(a, b):
    return jnp.dot(a, b, preferred_element_type=F32)


def _dot_nt(a, b):
    return lax.dot_general(a, b, (((1,), (1,)), ((), ())), preferred_element_type=F32)


def _dot_tn(a, b):
    return lax.dot_general(a, b, (((0,), (0,)), ((), ())), preferred_element_type=F32)


def _split(x, n):
    parts = []
    r = x
    for i in range(n):
        p = r.astype(BF16)
        parts.append(p)
        if i + 1 < n:
            r = r - p.astype(F32)
    return parts


def _sigmoid(x):
    return 1.0 / (1.0 + jnp.exp(-x))


def _silu(x):
    return x * _sigmoid(x)


def _params(n_axes):
    return pltpu.CompilerParams(dimension_semantics=("arbitrary",) * n_axes,
                                vmem_limit_bytes=VMEM_LIMIT)


def _ada_body(c_ref, w_ref, b_ref, o_ref):
    a = _silu(c_ref[...])
    a_hi, a_lo = _split(a, 2)
    w_hi, w_lo = _split(w_ref[...], 2)
    o_ref[...] = _dot(a_hi, w_hi) + _dot(a_lo, w_hi) + _dot(a_hi, w_lo) + b_ref[...]


def _ada_mods(c8, w_ada, b_ada):
    tn = 1536
    return pl.pallas_call(
        _ada_body,
        grid=(DEPTH, 6 * D_MODEL // tn),
        in_specs=[
            pl.BlockSpec((8, D_MODEL), lambda l, j: (0, 0)),
            pl.BlockSpec((None, D_MODEL, tn), lambda l, j: (l, 0, j)),
            pl.BlockSpec((None, 1, tn), lambda l, j: (l, 0, j)),
        ],
        out_specs=pl.BlockSpec((None, 8, tn), lambda l, j: (l, 0, j)),
        out_shape=jax.ShapeDtypeStruct((DEPTH, 8, 6 * D_MODEL), F32),
        compiler_params=_params(2),
        name="ada_mods",
    )(c8, w_ada, b_ada.reshape(DEPTH, 1, 6 * D_MODEL))


def _modnorm(x, g, shift, scale):
    ms = jnp.mean(x * x, axis=-1, keepdims=True)
    return (x * lax.rsqrt(ms + EPS) * g) * (1.0 + scale) + shift


def _mod_row(n_ctx_tiles, tiles_per_latent):
    def f(i):
        return jnp.where(i < n_ctx_tiles, 0, 1 + (i - n_ctx_tiles) // tiles_per_latent)
    return f


def _row_specs(n_parts, tm, width, n_ctx_tiles):
    if n_parts == 1:
        return [pl.BlockSpec((tm, width), lambda i, *_: (i, 0))]
    return [pl.BlockSpec((tm, width), lambda i, *_: (jnp.minimum(i, n_ctx_tiles - 1), 0)),
            pl.BlockSpec((tm, width), lambda i, *_: (jnp.maximum(i - n_ctx_tiles, 0), 0))]


def _pick(is_ctx, refs):
    if len(refs) == 1:
        return refs[0][...]
    return jnp.where(is_ctx, refs[0][...], refs[1][...])


def _inproj_body(n_x, n_prev, n_ctx_tiles, ctx_len, *refs):
    x_refs = refs[:n_x]
    mod_ref, g_ref, w_ref = refs[n_x:n_x + 3]
    prev_refs = refs[n_x + 3:n_x + 3 + 2 * n_prev]
    oa_ref, oh_ref, k_ref, v_ref = refs[-4:]
    is_ctx = pl.program_id(0) < n_ctx_tiles
    x = _pick(is_ctx, x_refs)
    h = _modnorm(x, g_ref[...], mod_ref[0:1, :], mod_ref[1:2, :])
    proj = _dot(h.astype(BF16), w_ref[...])
    oa_ref[...] = proj[:, :PROJ_A_WIDTH].astype(BF16)
    oh_ref[...] = proj[:, PROJ_A_WIDTH:]

    @pl.when(is_ctx)
    def _():
        per_layer = ctx_len * DA_HEADS
        for s in range(proj.shape[0] // ctx_len):
            base = s * (n_prev + 1) * per_layer
            for l in range(n_prev):
                dst = slice(base + l * per_layer, base + (l + 1) * per_layer)
                src = slice(s * per_layer, (s + 1) * per_layer)
                k_ref[dst, :] = prev_refs[2 * l][src, :]
                v_ref[dst, :] = prev_refs[2 * l + 1][src, :]
            tok = slice(s * ctx_len, (s + 1) * ctx_len)
            for h in range(DA_HEADS):
                rows = pl.ds(base + n_prev * per_layer + h, ctx_len, stride=DA_HEADS)
                k_ref[rows, :] = proj[tok, COL_K + h * DA_VDIM:COL_K + (h + 1) * DA_VDIM]
                v_ref[rows, :] = proj[tok, COL_V + h * DA_VDIM:COL_V + (h + 1) * DA_VDIM]


def _inproj(xs, mod, g, w, prev_caches, n_ctx, ctx_len, lat_len):
    t = sum(x.shape[0] for x in xs)
    tm = TOKEN_TILE
    seqs = tm // ctx_len
    n_ctx_tiles = n_ctx // seqs
    n_prev = len(prev_caches) // 2
    row = _mod_row(n_ctx_tiles, lat_len // tm)
    tile_rows = seqs * ctx_len * DA_HEADS
    tile_block = lambda i: (jnp.minimum(i, n_ctx_tiles - 1), 0)
    cache_spec = pl.BlockSpec(((n_prev + 1) * tile_rows, DA_VDIM), tile_block)
    cache_shape = jax.ShapeDtypeStruct((n_ctx_tiles * (n_prev + 1) * tile_rows, DA_VDIM), F32)
    in_specs = _row_specs(len(xs), tm, D_MODEL, n_ctx_tiles) + [
        pl.BlockSpec((None, 6, D_MODEL), lambda i: (row(i), 0, 0)),
        pl.BlockSpec((1, D_MODEL), lambda i: (0, 0)),
        pl.BlockSpec((D_MODEL, PROJ_A_WIDTH + PROJ_H_WIDTH), lambda i: (0, 0)),
    ] + [pl.BlockSpec((tile_rows, DA_VDIM), tile_block)] * (2 * n_prev)
    return pl.pallas_call(
        functools.partial(_inproj_body, len(xs), n_prev, n_ctx_tiles, ctx_len),
        grid=(t // tm,),
        in_specs=in_specs,
        out_specs=[pl.BlockSpec((tm, PROJ_A_WIDTH), lambda i: (i, 0)),
                   pl.BlockSpec((tm, PROJ_H_WIDTH), lambda i: (i, 0)), cache_spec, cache_spec],
        out_shape=[jax.ShapeDtypeStruct((t, PROJ_A_WIDTH), BF16), jax.ShapeDtypeStruct((t, PROJ_H_WIDTH), F32),
                   cache_shape, cache_shape],
        compiler_params=_params(1),
        name="inproj",
    )(*xs, mod, g, w, *prev_caches)


def _fourier_body(u_ref, cl_ref, sl_ref, cc_ref, sc_ref, w_ref, o_ref):
    z = u_ref[...].astype(BF16)
    a = _dot(z, cc_ref[...]).astype(BF16)
    b = _dot(z, sc_ref[...]).astype(BF16)
    y = _dot(cl_ref[...], a) - _dot(sl_ref[...], b)
    o_ref[...] = _dot(y.astype(BF16), w_ref[...]).astype(o_ref.dtype)


def _dft_tables(n, block):
    i = np.arange(n)
    prod = (i[:, None] % block) * (i[None, :] % block) % block
    ang = prod.astype(np.float64) * (2.0 * math.pi / block)
    same = (i[:, None] // block) == (i[None, :] // block)
    scale = 1.0 / math.sqrt(block)
    c = np.where(same, np.cos(ang) * scale, 0.0).astype(np.float32)
    s = np.where(same, np.sin(ang) * scale, 0.0).astype(np.float32)
    return jnp.asarray(c).astype(BF16), jnp.asarray(s).astype(BF16)


def _fourier(proj, row0, n_seq, seq_len, w_f):
    cl, sl = _dft_tables(seq_len, seq_len)
    cc, sc = _dft_tables(FN_WIDTH, FN_GROUP_DIM)
    blk0 = row0 // seq_len
    full = lambda shape: pl.BlockSpec(shape, lambda i: (0, 0))
    return pl.pallas_call(
        _fourier_body,
        grid=(n_seq,),
        in_specs=[
            pl.BlockSpec((seq_len, FN_WIDTH), lambda i: (blk0 + i, COL_FN // FN_WIDTH)),
            full((seq_len, seq_len)), full((seq_len, seq_len)),
            full((FN_WIDTH, FN_WIDTH)), full((FN_WIDTH, FN_WIDTH)), full((FN_WIDTH, FN_WIDTH)),
        ],
        out_specs=pl.BlockSpec((seq_len, FN_WIDTH), lambda i: (i, 0)),
        out_shape=jax.ShapeDtypeStruct((n_seq * seq_len, FN_WIDTH), BF16),
        compiler_params=_params(1),
        name=f"fourier_{seq_len}",
    )(proj, cl, sl, cc, sc, w_f)


def _lambda_full(lmb, lam_init):
    a = jnp.sum(lmb[0:1, :] * lmb[1:2, :], axis=-1, keepdims=True)
    b = jnp.sum(lmb[2:3, :] * lmb[3:4, :], axis=-1, keepdims=True)
    return jnp.exp(a) - jnp.exp(b) + lam_init


def _softmax_parts(parts):
    m = parts[0].max(axis=-1, keepdims=True)
    for p in parts[1:]:
        m = jnp.maximum(m, p.max(axis=-1, keepdims=True))
    es = [jnp.exp(p - m) for p in parts]
    tot = es[0].sum(axis=-1, keepdims=True)
    for e in es[1:]:
        tot = tot + e.sum(axis=-1, keepdims=True)
    return es, 1.0 / tot


def _diff_head(q, ks, vs, lam, g, lam_init):
    vas = [jnp.concatenate([v.astype(BF16), jnp.ones(v.shape, BF16)], axis=1) for v in vs]
    outs = []
    for m in range(2):
        qm = q[:, m * DA_HALF:(m + 1) * DA_HALF].astype(BF16)
        parts = [_dot_nt(qm, k[:, m * DA_HALF:(m + 1) * DA_HALF].astype(BF16)) for k in ks]
        mx = parts[0].max(axis=-1, keepdims=True)
        for p in parts[1:]:
            mx = jnp.maximum(mx, p.max(axis=-1, keepdims=True))
        oa = _dot(jnp.exp(parts[0] - mx).astype(BF16), vas[0])
        for p, va in zip(parts[1:], vas[1:]):
            oa = oa + _dot(jnp.exp(p - mx).astype(BF16), va)
        outs.append(oa[:, :DA_VDIM] * (1.0 / oa[:, DA_VDIM:DA_VDIM + 1]))
    a = outs[0] - lam * outs[1]
    ms = jnp.mean(a * a, axis=-1, keepdims=True)
    return a * lax.rsqrt(ms + EPS) * g * (1.0 - lam_init)


def _attn_ctx_body(lam_init, q_ref, k_ref, v_ref, lmb_ref, g_ref, o_ref):
    lam = _lambda_full(lmb_ref[...], lam_init)
    scale = DA_HALF ** -0.5
    for h in range(DA_HEADS):
        sl = slice(h * DA_VDIM, (h + 1) * DA_VDIM)
        o = _diff_head(q_ref[:, sl] * scale, [k_ref[:, sl]], [v_ref[:, sl]], lam, g_ref[...], lam_init)
        o_ref[:, sl] = o.astype(o_ref.dtype)


def _attn_ctx(proj, n_seq, seq_len, lmb, g, lam_init):
    blk = lambda c: pl.BlockSpec((seq_len, DA_WIDTH), lambda i: (i, c))
    return pl.pallas_call(
        functools.partial(_attn_ctx_body, lam_init),
        grid=(n_seq,),
        in_specs=[
            blk(COL_Q // DA_WIDTH), blk(COL_K // DA_WIDTH), blk(COL_V // DA_WIDTH),
            pl.BlockSpec((4, DA_HALF), lambda i: (0, 0)),
            pl.BlockSpec((1, DA_VDIM), lambda i: (0, 0)),
        ],
        out_specs=pl.BlockSpec((seq_len, DA_WIDTH), lambda i: (i, 0)),
        out_shape=jax.ShapeDtypeStruct((n_seq * seq_len, DA_WIDTH), BF16),
        compiler_params=_params(1),
        name="attn_ctx",
    )(proj, proj, proj, lmb, g)


def _rope(x, cos, sin):
    lane = lax.broadcasted_iota(jnp.int32, x.shape, 1)
    first = ((lane >> 4) & 1) == 0
    rot = jnp.where(first, -pltpu.roll(x, 128 - DA_HALF // 4, 1), pltpu.roll(x, DA_HALF // 4, 1))
    return x * cos + rot * sin


def _attn_lat_body(lam_init, q_ref, k_ref, v_ref, kc_ref, vc_ref, cq_ref, sq_ref, ck_ref, sk_ref,
                   lmb_ref, g_ref, o_ref):
    lam = _lambda_full(lmb_ref[...], lam_init)
    scale = DA_HALF ** -0.5
    for h in range(DA_HEADS):
        sl = slice(h * DA_VDIM, (h + 1) * DA_VDIM)
        q = _rope(q_ref[:, sl].astype(F32), cq_ref[...], sq_ref[...]) * scale
        k = _rope(k_ref[:, sl].astype(F32), ck_ref[...], sk_ref[...])
        o = _diff_head(q, [k, kc_ref[:, sl]], [v_ref[:, sl], vc_ref[:, sl]], lam, g_ref[...], lam_init)
        o_ref[:, sl] = o.astype(o_ref.dtype)


def _attn_lat(proj, row0, n_seq, seq_len, cache_k, cache_v, layer, cos, sin, lmb, g, lam_init):
    tq = 512
    nq = seq_len // tq
    past = cache_k.shape[2]
    qb0 = row0 // tq
    kb0 = row0 // seq_len
    cache_spec = pl.BlockSpec((None, None, past, DA_WIDTH), lambda b, j: (b, layer, 0, 0))
    kv_spec = lambda c: pl.BlockSpec((seq_len, DA_WIDTH), lambda b, j: (kb0 + b, c))
    return pl.pallas_call(
        functools.partial(_attn_lat_body, lam_init),
        grid=(n_seq, nq),
        in_specs=[
            pl.BlockSpec((tq, DA_WIDTH), lambda b, j: (qb0 + b * nq + j, COL_Q // DA_WIDTH)),
            kv_spec(COL_K // DA_WIDTH), kv_spec(COL_V // DA_WIDTH),
            cache_spec, cache_spec,
            pl.BlockSpec((tq, DA_VDIM), lambda b, j: (j, 0)),
            pl.BlockSpec((tq, DA_VDIM), lambda b, j: (j, 0)),
            pl.BlockSpec((seq_len, DA_VDIM), lambda b, j: (0, 0)),
            pl.BlockSpec((seq_len, DA_VDIM), lambda b, j: (0, 0)),
            pl.BlockSpec((4, DA_HALF), lambda b, j: (0, 0)),
            pl.BlockSpec((1, DA_VDIM), lambda b, j: (0, 0)),
        ],
        out_specs=pl.BlockSpec((tq, DA_WIDTH), lambda b, j: (b * nq + j, 0)),
        out_shape=jax.ShapeDtypeStruct((n_seq * seq_len, DA_WIDTH), BF16),
        compiler_params=_params(2),
        name="attn_lat",
    )(proj, proj, proj, cache_k, cache_v, cos, sin, cos, sin, lmb, g)


def _rope_tables(n_tokens):
    rows = n_tokens // GRID_W
    row = np.repeat(np.arange(rows, dtype=np.float64), GRID_W)
    col = np.tile(np.arange(GRID_W, dtype=np.float64), rows)
    axis_dim = DA_HALF // 2
    inv_freq = ROPE_THETA ** (-np.arange(0, axis_dim, 2, dtype=np.float64) / axis_dim)
    ang_r = row[:, None] * inv_freq[None, :]
    ang_c = col[:, None] * inv_freq[None, :]
    ang = np.concatenate([ang_r, ang_r, ang_c, ang_c] * 2, axis=-1)
    return jnp.asarray(np.cos(ang).astype(np.float32)), jnp.asarray(np.sin(ang).astype(np.float32))


def _hgrn_body(n_chunks, has_s0, *refs):
    if has_s0:
        (hq_ref, hi_ref, hf_ref, hb_ref, hg_ref, lb_ref, ng_ref, s0_ref,
         o_ref, oi_scr, qe_scr, u_scr, st_scr, dec_scr) = refs
        so_ref = None
    else:
        (hq_ref, hi_ref, hf_ref, hb_ref, hg_ref, lb_ref, ng_ref,
         o_ref, so_ref, oi_scr, qe_scr, u_scr, st_scr, dec_scr) = refs
    c = CHUNK
    w = HG_WIDTH
    gc = CHUNKS_PER_GROUP
    gl = gc * c
    c_bits = c.bit_length() - 1
    r_io = lax.broadcasted_iota(jnp.int32, (gl, gl), 0)
    c_io = lax.broadcasted_iota(jnp.int32, (gl, gl), 1)
    same_chunk = (r_io >> c_bits) == (c_io >> c_bits)
    tri_f = (same_chunk & (c_io <= r_io)).astype(BF16)
    tri_b = (same_chunk & (c_io >= r_io)).astype(BF16)
    lane = lax.broadcasted_iota(jnp.int32, (1, w), 1)
    dk_bits = HG_DK.bit_length() - 1
    head_masks = [((lane >> dk_bits) == h).astype(F32) for h in range(HG_HEADS)]
    head_masks_b = [hm.astype(BF16) for hm in head_masks]
    bd = ((lax.broadcasted_iota(jnp.int32, (w, w), 0) >> dk_bits)
          == (lax.broadcasted_iota(jnp.int32, (w, w), 1) >> dk_bits))
    t_of_row = lax.broadcasted_iota(jnp.int32, (HG_HEADS * gl, gl), 0) & (gl - 1)
    s_of_col = lax.broadcasted_iota(jnp.int32, (HG_HEADS * gl, gl), 1)
    same = (t_of_row >> c_bits) == (s_of_col >> c_bits)
    causal_f = same & (s_of_col <= t_of_row)
    causal_b = same & (s_of_col >= t_of_row)

    dirs = ((hf_ref, tri_f, causal_f, c - 1), (hb_ref, tri_b, causal_b, 0))

    def per_chunk_row(x, row):
        return jnp.concatenate([jnp.broadcast_to(x[k * c + row:k * c + row + 1, :], (c, w)) for k in range(gc)],
                               axis=0)

    def group_terms(gi):
        rows = pl.ds(gi * gl if isinstance(gi, int) else pl.multiple_of(gi * gl, gl), gl)
        q = _silu(hq_ref[rows, :]) * (HG_DK ** -0.5)
        v = hi_ref[rows, :].astype(BF16)
        for d, (fp_ref, tri, causal, edge_row) in enumerate(dirs):
            lb = lb_ref[d:d + 1, :]
            fp = fp_ref[rows, :]
            lsig = jnp.minimum(fp, 0.0) - jnp.log1p(jnp.exp(-jnp.abs(fp)))
            la = jnp.log(lb)
            lbb = jnp.log1p(-lb) + lsig
            logf = jnp.maximum(la, lbb) + jnp.log1p(jnp.exp(-jnp.abs(la - lbb)))
            kk = (1.0 - lb) * (1.0 / (1.0 + jnp.exp(fp)))
            g3 = _split(logf, 3)
            cum = _dot(tri, g3[0]) + _dot(tri, g3[1]) + _dot(tri, g3[2])
            total = per_chunk_row(cum, edge_row)
            ref = per_chunk_row(cum, c // 2)
            qc = (q * jnp.exp(cum - ref)).astype(BF16)
            kc = (kk * jnp.exp(ref - cum)).astype(BF16)
            ke = (kk * jnp.exp(total - cum)).astype(BF16)
            qe_scr[d, rows, :] = (q * jnp.exp(cum)).astype(BF16)
            lhs = jnp.concatenate([qc * hm for hm in head_masks_b], axis=0)
            a = jnp.where(causal, _dot_nt(lhs, kc), 0.0).astype(BF16)
            o_stack = _dot(a, v)
            o = o_stack[0:gl, :] * head_masks[0]
            for h in range(1, HG_HEADS):
                o = o + o_stack[h * gl:(h + 1) * gl, :] * head_masks[h]
            oi_scr[d, rows, :] = o
            for k in range(gc):
                ck = slice(k * c, (k + 1) * c)
                i = gi * gc + k
                dec_scr[d, pl.ds(i, 1), :] = jnp.exp(cum[k * c + edge_row:k * c + edge_row + 1, :])
                u_scr[d, i] = jnp.where(bd, _dot_tn(v[ck, :], ke[ck, :]), 0.0)

    if n_chunks == gc:
        group_terms(0)
    else:
        def terms_step(gi, carry):
            group_terms(gi)
            return carry

        lax.fori_loop(0, n_chunks // gc, terms_step, 0)

    slab = 32
    for d in range(2):
        order = range(n_chunks) if d == 0 else range(n_chunks - 1, -1, -1)
        for r in range(w // slab):
            rs = slice(r * slab, (r + 1) * slab)
            st = s0_ref[d, rs, :] if has_s0 else jnp.zeros((slab, w), F32)
            for i in order:
                st_scr[d, i, rs, :] = st.astype(BF16)
                st = st * dec_scr[d, i:i + 1, :] + u_scr[d, i, rs, :]
            if so_ref is not None:
                h = (r * slab) // HG_DK
                off = (r * slab) % HG_DK
                so_ref[d, h, off:off + slab, :] = st[:, h * HG_DK:(h + 1) * HG_DK]

    ones_bd = bd.astype(BF16)

    def finish(i, carry):
        rows = pl.ds(pl.multiple_of(i * c, c), c)
        o = (oi_scr[0, rows, :] + oi_scr[1, rows, :]
             + _dot_nt(qe_scr[0, rows, :], st_scr[0, i]) + _dot_nt(qe_scr[1, rows, :], st_scr[1, i]))
        sq = _split(o * o, 2)
        ms = (_dot(sq[0], ones_bd) + _dot(sq[1], ones_bd)) * (1.0 / HG_DK)
        y = o * lax.rsqrt(ms + EPS) * ng_ref[...]
        o_ref[rows, :] = (y * _silu(hg_ref[rows, :])).astype(o_ref.dtype)
        return carry

    lax.fori_loop(0, n_chunks, finish, 0, unroll=2)


def _hgrn(proj, row0, n_seq, seq_len, lb, ng, s0t):
    n_chunks = seq_len // CHUNK
    blk0 = row0 // seq_len
    col = lambda c0: pl.BlockSpec((seq_len, HG_WIDTH), lambda i: (blk0 + i, c0 // HG_WIDTH))
    in_specs = [col(COL_HQ), col(COL_HI), col(COL_HF), col(COL_HB), col(COL_HG),
                pl.BlockSpec((2, HG_WIDTH), lambda i: (0, 0)),
                pl.BlockSpec((1, HG_WIDTH), lambda i: (0, 0))]
    args = [proj, proj, proj, proj, proj, lb, ng]
    out_specs = [pl.BlockSpec((seq_len, HG_WIDTH), lambda i: (i, 0))]
    out_shape = [jax.ShapeDtypeStruct((n_seq * seq_len, HG_WIDTH), BF16)]
    if s0t is not None:
        in_specs.append(pl.BlockSpec((None, 2, HG_WIDTH, HG_WIDTH), lambda i: (i, 0, 0, 0)))
        args.append(s0t)
    else:
        out_specs.append(pl.BlockSpec((None, 2, HG_HEADS, HG_DK, HG_DK), lambda i: (i, 0, 0, 0, 0)))
        out_shape.append(jax.ShapeDtypeStruct((n_seq, 2, HG_HEADS, HG_DK, HG_DK), F32))
    return pl.pallas_call(
        functools.partial(_hgrn_body, n_chunks, s0t is not None),
        grid=(n_seq,),
        in_specs=in_specs,
        out_specs=out_specs,
        out_shape=out_shape,
        scratch_shapes=[pltpu.VMEM((2, seq_len, HG_WIDTH), F32),
                        pltpu.VMEM((2, seq_len, HG_WIDTH), BF16),
                        pltpu.VMEM((2, n_chunks, HG_WIDTH, HG_WIDTH), F32),
                        pltpu.VMEM((2, n_chunks, HG_WIDTH, HG_WIDTH), BF16),
                        pltpu.VMEM((2, max(n_chunks, 8), HG_WIDTH), F32)],
        compiler_params=_params(1),
        name=f"hgrn_{seq_len}",
    )(*args)


def _sigmoid_t(x):
    return 0.5 * jnp.tanh(0.5 * x) + 0.5


def _hgrn_tables():
    gl = CHUNKS_PER_GROUP * CHUNK
    t = np.arange(gl)
    same = (t[:, None] // CHUNK) == (t[None, :] // CHUNK)
    fwd = same & (t[None, :] <= t[:, None])
    bwd = same & (t[None, :] >= t[:, None])
    f = np.arange(HG_WIDTH)
    bd = (f[:, None] // HG_DK) == (f[None, :] // HG_DK)
    tri = jnp.asarray(np.stack([fwd, bwd]).astype(np.float32)).astype(BF16)
    causal = jnp.asarray(np.stack([fwd, bwd]).astype(np.float32))
    return tri, causal, jnp.asarray(bd.astype(np.float32))


def _hgrn2_body(n_seq, groups_per_seq, has_s0, *refs, between=None):
    (hq_ref, hi_ref, hf_ref, hb_ref, hg_ref, lb_ref, ng_ref, tri_ref, causal_ref, bd_ref) = refs[:10]
    if has_s0:
        s0_ref, o_ref, oi_scr, qe_scr, u_scr, st_scr, dec_scr = refs[10:]
        so_ref = None
    else:
        o_ref, so_ref, oi_scr, qe_scr, u_scr, st_scr, dec_scr = refs[10:]
        s0_ref = None
    c = CHUNK
    w = HG_WIDTH
    gc = CHUNKS_PER_GROUP
    gl = gc * c
    n_groups = n_seq * groups_per_seq
    chunks_per_seq = groups_per_seq * gc
    lane = lax.broadcasted_iota(jnp.int32, (1, w), 1)
    dk_bits = HG_DK.bit_length() - 1
    head_masks_b = [((lane >> dk_bits) == h).astype(BF16) for h in range(HG_HEADS)]
    bd = bd_ref[...] > 0.5
    edge_rows = (c - 1, 0)
    fp_refs = (hf_ref, hb_ref)

    def per_chunk_row(x, row):
        return jnp.concatenate([jnp.broadcast_to(x[k * c + row:k * c + row + 1, :], (c, w)) for k in range(gc)],
                               axis=0)

    def group_terms(gi):
        rows = pl.ds(gi * gl if isinstance(gi, int) else pl.multiple_of(gi * gl, gl), gl)
        hq = hq_ref[rows, :]
        q = hq * _sigmoid_t(hq) * (HG_DK ** -0.5)
        v = hi_ref[rows, :].astype(BF16)
        v_stack = jnp.concatenate([v * hm for hm in head_masks_b], axis=0)
        for d in range(2):
            lb = lb_ref[d:d + 1, :]
            fp = fp_refs[d][rows, :]
            lsig = jnp.minimum(fp, 0.0) - jnp.log(1.0 + jnp.exp(-jnp.abs(fp)))
            la = jnp.log(lb)
            lbb = jnp.log(1.0 - lb) + lsig
            logf = jnp.maximum(la, lbb) + jnp.log(1.0 + jnp.exp(-jnp.abs(la - lbb)))
            kk = (1.0 - lb) * _sigmoid_t(-fp)
            g_hi, g_lo = _split(logf, 2)
            cum = _dot(tri_ref[d], g_hi) + _dot(tri_ref[d], g_lo)
            total = per_chunk_row(cum, edge_rows[d])
            ref = per_chunk_row(cum, c // 2)
            qc = (q * jnp.exp(cum - ref)).astype(BF16)
            kc = (kk * jnp.exp(ref - cum)).astype(BF16)
            ke = (kk * jnp.exp(total - cum)).astype(BF16)
            qe_scr[d, rows, :] = (q * jnp.exp(cum)).astype(BF16)
            kc_stack = jnp.concatenate([kc * hm for hm in head_masks_b], axis=0)
            a = _dot_nt(qc, kc_stack)
            keep = causal_ref[d] > 0.5
            a = jnp.concatenate([jnp.where(keep, a[:, h * gl:(h + 1) * gl], 0.0) for h in range(HG_HEADS)],
                                axis=1).astype(BF16)
            oi_scr[d, rows, :] = _dot(a, v_stack)
            for k in range(gc):
                ck = slice(k * c, (k + 1) * c)
                i = gi * gc + k
                dec_scr[d, pl.ds(i, 1), :] = jnp.exp(cum[k * c + edge_rows[d]:k * c + edge_rows[d] + 1, :])
                u_scr[d, i] = jnp.where(bd, _dot_tn(v[ck, :], ke[ck, :]), 0.0)

    if n_groups <= 2:
        for gi in range(n_groups):
            group_terms(gi)
    else:
        def terms_step(gi, carry):
            group_terms(gi)
            return carry

        lax.fori_loop(0, n_groups, terms_step, 0)

    if between is not None:
        between()

    slab = 32
    for s in range(n_seq):
        first = s * chunks_per_seq
        for d in range(2):
            order = range(chunks_per_seq) if d == 0 else range(chunks_per_seq - 1, -1, -1)
            for r in range(w // slab):
                rs = slice(r * slab, (r + 1) * slab)
                st = s0_ref[s, d, rs, :] if has_s0 else jnp.zeros((slab, w), F32)
                for j in order:
                    i = first + j
                    st_scr[d, i, rs, :] = st.astype(BF16)
                    st = st * dec_scr[d, i:i + 1, :] + u_scr[d, i, rs, :]
                if so_ref is not None:
                    h = (r * slab) // HG_DK
                    off = (r * slab) % HG_DK
                    so_ref[s, d, h, off:off + slab, :] = st[:, h * HG_DK:(h + 1) * HG_DK]

    ones_bd = bd.astype(BF16)

    def finish(gi):
        rows = pl.ds(gi * gl if isinstance(gi, int) else pl.multiple_of(gi * gl, gl), gl)
        inter = []
        for k in range(gc):
            i = gi * gc + k
            ck = pl.ds(gi * gl + k * c if isinstance(gi, int) else pl.multiple_of(gi * gl + k * c, c), c)
            inter.append(_dot_nt(qe_scr[0, ck, :], st_scr[0, i]) + _dot_nt(qe_scr[1, ck, :], st_scr[1, i]))
        o = oi_scr[0, rows, :] + oi_scr[1, rows, :] + jnp.concatenate(inter, axis=0)
        sq = _split(o * o, 2)
        ms = (_dot(sq[0], ones_bd) + _dot(sq[1], ones_bd)) * (1.0 / HG_DK)
        y = o * lax.rsqrt(ms + EPS) * ng_ref[...]
        hg = hg_ref[rows, :]
        o_ref[rows, :] = (y * hg * _sigmoid_t(hg)).astype(o_ref.dtype)

    if n_groups <= 2:
        for gi in range(n_groups):
            finish(gi)
    else:
        def finish_step(gi, carry):
            finish(gi)
            return carry

        lax.fori_loop(0, n_groups, finish_step, 0)


def _hgrn2(proj_h, row0, n_seq, seq_len, seqs_per_step, lb, ng, s0t):
    gl = CHUNKS_PER_GROUP * CHUNK
    groups_per_seq = seq_len // gl
    n_chunks = seqs_per_step * seq_len // CHUNK
    rows = seqs_per_step * seq_len
    blk0 = row0 // rows
    tri, causal, bd = _hgrn_tables()
    col = lambda c0: pl.BlockSpec((rows, HG_WIDTH), lambda i: (blk0 + i, c0 // HG_WIDTH))
    const = lambda shape: pl.BlockSpec(shape, lambda i: (0,) * len(shape))
    in_specs = [col(COL_HQ), col(COL_HI), col(COL_HF), col(COL_HB), col(COL_HG),
                const((2, HG_WIDTH)), const((1, HG_WIDTH)),
                const((2, gl, gl)), const((2, gl, gl)), const((HG_WIDTH, HG_WIDTH))]
    args = [proj_h] * 5 + [lb, ng, tri, causal, bd]
    out_specs = [pl.BlockSpec((rows, HG_WIDTH), lambda i: (i, 0))]
    out_shape = [jax.ShapeDtypeStruct((n_seq * seq_len, HG_WIDTH), BF16)]
    if s0t is not None:
        in_specs.append(pl.BlockSpec((seqs_per_step, 2, HG_WIDTH, HG_WIDTH), lambda i: (i, 0, 0, 0)))
        args.append(s0t)
    else:
        out_specs.append(pl.BlockSpec((seqs_per_step, 2, HG_HEADS, HG_DK, HG_DK), lambda i: (i, 0, 0, 0, 0)))
        out_shape.append(jax.ShapeDtypeStruct((n_seq, 2, HG_HEADS, HG_DK, HG_DK), F32))
    return pl.pallas_call(
        functools.partial(_hgrn2_body, seqs_per_step, groups_per_seq, s0t is not None),
        grid=(n_seq // seqs_per_step,),
        in_specs=in_specs,
        out_specs=out_specs,
        out_shape=out_shape,
        scratch_shapes=[pltpu.VMEM((2, rows, HG_WIDTH), F32),
                        pltpu.VMEM((2, rows, HG_WIDTH), BF16),
                        pltpu.VMEM((2, n_chunks, HG_WIDTH, HG_WIDTH), F32),
                        pltpu.VMEM((2, n_chunks, HG_WIDTH, HG_WIDTH), BF16),
                        pltpu.VMEM((2, max(n_chunks, 8), HG_WIDTH), F32)],
        compiler_params=_params(1),
        name=f"hgrn_{seq_len}",
    )(*args)


def _ctx_mixers_body(n_seq, seq_len, lam_init, *refs):
    (q_ref, k_ref, v_ref, u_ref, hq_ref, hi_ref, hf_ref, hb_ref, hg_ref,
     cl_ref, sl_ref, cc_ref, sc_ref, wf_ref, lmb_ref, ag_ref,
     lb_ref, ng_ref, tri_ref, causal_ref, bd_ref,
     fn_ref, a_ref, ho_ref, so_ref) = refs[:25]
    scratch = refs[25:]

    def fourier_and_attention():
        z = u_ref[...].astype(BF16)
        za = _dot(z, cc_ref[...]).astype(BF16)
        zb = _dot(z, sc_ref[...]).astype(BF16)
        for s in range(n_seq):
            rs = slice(s * seq_len, (s + 1) * seq_len)
            y = _dot(cl_ref[...], za[rs, :]) - _dot(sl_ref[...], zb[rs, :])
            fn_ref[rs, :] = _dot(y.astype(BF16), wf_ref[...]).astype(fn_ref.dtype)

        lam = _lambda_full(lmb_ref[...], lam_init)
        scale = DA_HALF ** -0.5
        for s in range(n_seq):
            rs = slice(s * seq_len, (s + 1) * seq_len)
            for h in range(DA_HEADS):
                sl = slice(h * DA_VDIM, (h + 1) * DA_VDIM)
                o = _diff_head(q_ref[rs, sl] * scale, [k_ref[rs, sl]], [v_ref[rs, sl]], lam, ag_ref[...],
                               lam_init)
                a_ref[rs, sl] = o.astype(a_ref.dtype)

    fourier_and_attention()
    _hgrn2_body(n_seq, seq_len // (CHUNKS_PER_GROUP * CHUNK), False,
                hq_ref, hi_ref, hf_ref, hb_ref, hg_ref, lb_ref, ng_ref, tri_ref, causal_ref, bd_ref,
                ho_ref, so_ref, *scratch)


def _ctx_mixers(proj_a, proj_h, n_seq, seq_len, seqs_per_step, w_f, lmb, ag, lam_init, lb, ng):
    rows = seqs_per_step * seq_len
    n_chunks = rows // CHUNK
    gl = CHUNKS_PER_GROUP * CHUNK
    cl, sl = _dft_tables(seq_len, seq_len)
    cc, sc = _dft_tables(FN_WIDTH, FN_GROUP_DIM)
    tri, causal, bd = _hgrn_tables()
    col = lambda c0, width: pl.BlockSpec((rows, width), lambda i: (i, c0 // width))
    const = lambda shape: pl.BlockSpec(shape, lambda i: (0,) * len(shape))
    out_rows = lambda width: pl.BlockSpec((rows, width), lambda i: (i, 0))
    n_tok = n_seq * seq_len
    return pl.pallas_call(
        functools.partial(_ctx_mixers_body, seqs_per_step, seq_len, lam_init),
        grid=(n_seq // seqs_per_step,),
        in_specs=[col(COL_Q, DA_WIDTH), col(COL_K, DA_WIDTH), col(COL_V, DA_WIDTH), col(COL_FN, FN_WIDTH),
                  col(COL_HQ, HG_WIDTH), col(COL_HI, HG_WIDTH), col(COL_HF, HG_WIDTH), col(COL_HB, HG_WIDTH),
                  col(COL_HG, HG_WIDTH),
                  const((seq_len, seq_len)), const((seq_len, seq_len)),
                  const((FN_WIDTH, FN_WIDTH)), const((FN_WIDTH, FN_WIDTH)), const((FN_WIDTH, FN_WIDTH)),
                  const((4, DA_HALF)), const((1, DA_VDIM)),
                  const((2, HG_WIDTH)), const((1, HG_WIDTH)),
                  const((2, gl, gl)), const((2, gl, gl)), const((HG_WIDTH, HG_WIDTH))],
        out_specs=[out_rows(FN_WIDTH), out_rows(DA_WIDTH), out_rows(HG_WIDTH),
                   pl.BlockSpec((seqs_per_step, 2, HG_HEADS, HG_DK, HG_DK), lambda i: (i, 0, 0, 0, 0))],
        out_shape=[jax.ShapeDtypeStruct((n_tok, FN_WIDTH), BF16),
                   jax.ShapeDtypeStruct((n_tok, DA_WIDTH), BF16),
                   jax.ShapeDtypeStruct((n_tok, HG_WIDTH), BF16),
                   jax.ShapeDtypeStruct((n_seq, 2, HG_HEADS, HG_DK, HG_DK), F32)],
        scratch_shapes=[pltpu.VMEM((2, rows, HG_WIDTH), F32),
                        pltpu.VMEM((2, rows, HG_WIDTH), BF16),
                        pltpu.VMEM((2, n_chunks, HG_WIDTH, HG_WIDTH), F32),
                        pltpu.VMEM((2, n_chunks, HG_WIDTH, HG_WIDTH), BF16),
                        pltpu.VMEM((2, max(n_chunks, 8), HG_WIDTH), F32)],
        compiler_params=_params(1),
        name="ctx_mixers",
    )(*([proj_a] * 4 + [proj_h] * 5), cl, sl, cc, sc, w_f, lmb, ag, lb, ng, tri, causal, bd)


def _outproj_body(n_x, n_ctx_tiles, *refs):
    fn_refs, a_refs, hg_refs = refs[0:2], refs[2:4], refs[4:6]
    x_refs = refs[6:6 + n_x]
    mod_ref, g_ref, w_ref, wr_ref, xo_ref, h_ref, lt_ref = refs[6 + n_x:]
    is_ctx = pl.program_id(0) < n_ctx_tiles
    mix = (_dot(_pick(is_ctx, fn_refs), w_ref[0:FN_WIDTH, :])
           + _dot(_pick(is_ctx, a_refs), w_ref[FN_WIDTH:FN_WIDTH + DA_WIDTH, :])
           + _dot(_pick(is_ctx, hg_refs), w_ref[FN_WIDTH + DA_WIDTH:, :]))
    x = _pick(is_ctx, x_refs) + mod_ref[2:3, :] * mix
    xo_ref[...] = x
    h = _modnorm(x, g_ref[...], mod_ref[3:4, :], mod_ref[4:5, :])
    h_ref[...] = h.astype(BF16)
    h_hi, h_lo = _split(h, 2)
    w_hi, w_lo = _split(wr_ref[...], 2)
    lt_ref[...] = _dot_nt(w_hi, h_hi) + _dot_nt(w_lo, h_hi) + _dot_nt(w_hi, h_lo)


def _outproj(fn, a, hg, xs, mod, g, w_out, w_router_t, n_ctx_tok, lat_len):
    t = sum(x.shape[0] for x in xs)
    tm = TOKEN_TILE
    n_ctx_tiles = n_ctx_tok // tm
    row = _mod_row(n_ctx_tiles, lat_len // tm)
    rows = lambda width: pl.BlockSpec((tm, width), lambda i: (i, 0))
    parts = lambda n, width: _row_specs(n, tm, width, n_ctx_tiles)
    return pl.pallas_call(
        functools.partial(_outproj_body, len(xs), n_ctx_tiles),
        grid=(t // tm,),
        in_specs=parts(2, FN_WIDTH) + parts(2, DA_WIDTH) + parts(2, HG_WIDTH) + parts(len(xs), D_MODEL) + [
            pl.BlockSpec((None, 6, D_MODEL), lambda i: (row(i), 0, 0)),
            pl.BlockSpec((1, D_MODEL), lambda i: (0, 0)),
            pl.BlockSpec((D_MODEL, D_MODEL), lambda i: (0, 0)),
            pl.BlockSpec((N_EXPERTS, D_MODEL), lambda i: (0, 0)),
        ],
        out_specs=[rows(D_MODEL), rows(D_MODEL), pl.BlockSpec((N_EXPERTS, tm), lambda i: (0, i))],
        out_shape=[jax.ShapeDtypeStruct((t, D_MODEL), F32),
                   jax.ShapeDtypeStruct((t, D_MODEL), BF16),
                   jax.ShapeDtypeStruct((N_EXPERTS, t), F32)],
        compiler_params=_params(1),
        name="outproj",
    )(*fn, *a, *hg, *xs, mod, g, w_out, w_router_t)


def _route_body(lt_ref, bias_ref, o_ref):
    per = N_EXPERTS // N_GROUPS
    tt = lt_ref.shape[1]
    neg = -jnp.inf
    assert per == N_GROUPS == 8
    gi = lax.broadcasted_iota(jnp.int32, (N_GROUPS, tt), 0).astype(F32)
    s_j, b_j = [], []
    for j in range(per):
        s = _sigmoid(lt_ref[j * N_GROUPS:(j + 1) * N_GROUPS, :])
        s_j.append(s)
        b_j.append(s + bias_ref[j * N_GROUPS:(j + 1) * N_GROUPS, :])
    m1 = functools.reduce(jnp.maximum, b_j)
    i1 = functools.reduce(jnp.minimum, [jnp.where(b_j[j] == m1, float(j), float(per)) for j in range(per)])
    m2 = functools.reduce(jnp.maximum, [jnp.where(i1 == float(j), neg, b_j[j]) for j in range(per)])
    gs = m1 + m2
    gsel = jnp.zeros((N_GROUPS, tt), jnp.bool_)
    for _ in range(TOPK_GROUPS):
        m = gs.max(axis=0, keepdims=True)
        idx = jnp.where(gs == m, gi, float(N_GROUPS)).min(axis=0, keepdims=True)
        hit = gi == idx
        gsel = gsel | hit
        gs = jnp.where(hit, neg, gs)
    x_j = [jnp.where(gsel, b_j[j], neg) for j in range(per)]
    e_j = [gi * per + j for j in range(per)]
    sel_j = [jnp.zeros((N_GROUPS, tt), jnp.bool_) for _ in range(per)]
    for _ in range(TOP_K):
        m = functools.reduce(jnp.maximum, x_j).max(axis=0, keepdims=True)
        idx = functools.reduce(jnp.minimum, [jnp.where(x_j[j] == m, e_j[j], float(N_EXPERTS))
                                             for j in range(per)]).min(axis=0, keepdims=True)
        for j in range(per):
            hit = e_j[j] == idx
            sel_j[j] = sel_j[j] | hit
            x_j[j] = jnp.where(hit, neg, x_j[j])
    w_j = [jnp.where(sel_j[j], s_j[j], 0.0) for j in range(per)]
    denom = functools.reduce(lambda a, b: a + b, w_j).sum(axis=0, keepdims=True)
    gates_t = jnp.concatenate([w / denom * ROUTED_SCALE for w in w_j], axis=0)
    r_io = lax.broadcasted_iota(jnp.int32, (N_EXPERTS, N_EXPERTS), 0)
    e_io = lax.broadcasted_iota(jnp.int32, (N_EXPERTS, N_EXPERTS), 1)
    eye = (e_io == (r_io & (N_GROUPS - 1)) * per + (r_io >> 3)).astype(BF16)
    p = _split(gates_t, 3)
    o_ref[...] = _dot_tn(p[0], eye) + _dot_tn(p[1], eye) + _dot_tn(p[2], eye)


def _route(logits_t, bias):
    t = logits_t.shape[1]
    tt = ROUTE_TILE
    return pl.pallas_call(
        _route_body,
        grid=(t // tt,),
        in_specs=[pl.BlockSpec((N_EXPERTS, tt), lambda i: (0, i)),
                  pl.BlockSpec((N_EXPERTS, 1), lambda i: (0, 0))],
        out_specs=pl.BlockSpec((tt, N_EXPERTS), lambda i: (i, 0)),
        out_shape=jax.ShapeDtypeStruct((t, N_EXPERTS), F32),
        compiler_params=_params(1),
        name="route",
    )(logits_t, bias)


def _moe_body(final_norm, n_ctx_tiles, lat_len, h_ref, gate_ref, wg_ref, wu_ref, wd_ref, sg_ref, su_ref, sd_ref,
              x_ref, mod_ref, fg_ref, o_ref):
    acc_ref = o_ref
    i = pl.program_id(0)
    j = pl.program_id(1)
    h = h_ref[...]

    def act_of(wg, wu, gate):
        gu = _dot(h, jnp.concatenate([wg.astype(BF16), wu.astype(BF16)], axis=1))
        a = _silu(gu[:, :D_EXPERT]) * gu[:, D_EXPERT:]
        return a if gate is None else a * gate

    @pl.when(j == 0)
    def _():
        acc_ref[...] = _dot(act_of(sg_ref[...], su_ref[...], None).astype(BF16), sd_ref[...].astype(BF16))

    gates = gate_ref[...]
    expert_of_lane = lax.broadcasted_iota(jnp.int32, gates.shape, 1)

    def gate_col(p):
        e = j * EXPERTS_PER_STEP + p
        return jnp.sum(jnp.where(expert_of_lane == e, gates, 0.0), axis=1, keepdims=True)

    for p in range(EXPERTS_PER_STEP // 2):
        a0 = act_of(wg_ref[2 * p], wu_ref[2 * p], gate_col(2 * p))
        a1 = act_of(wg_ref[2 * p + 1], wu_ref[2 * p + 1], gate_col(2 * p + 1))
        pair = jnp.concatenate([a0, a1], axis=1).astype(BF16)
        acc_ref[...] += _dot(pair, wd_ref[p].astype(BF16))

    rb = 256
    tm = h_ref.shape[0]

    @pl.when(j == pl.num_programs(1) - 1)
    def _():
        def finish(b, carry):
            rows = pl.ds(pl.multiple_of(b * rb, rb), rb)
            lat = (i - n_ctx_tiles) * tm + b * rb
            seq = jnp.where(i < n_ctx_tiles, 0, 1 + lat // lat_len)
            x = x_ref[rows, :] + mod_ref[seq, 5:6, :] * acc_ref[rows, :]
            if final_norm:
                ms = jnp.mean(x * x, axis=-1, keepdims=True)
                x = x * lax.rsqrt(ms + EPS) * fg_ref[...]
            o_ref[rows, :] = x
            return carry

        lax.fori_loop(0, tm // rb, finish, 0)


def _moe(h, gates, w_gate, w_up, w_down2, ws_gate, ws_up, ws_down, layer, x, mods, final_g, final_norm,
         n_ctx_tok, lat_len):
    t = x.shape[0]
    tm = MOE_TILE
    eps_ = EXPERTS_PER_STEP
    n_ctx_tiles = n_ctx_tok // tm
    once = pl.Buffered(1)
    out_specs = pl.BlockSpec((tm, D_MODEL), lambda i, j: (i, 0), pipeline_mode=once)
    out_shape = jax.ShapeDtypeStruct((t, D_MODEL), F32)
    return pl.pallas_call(
        functools.partial(_moe_body, final_norm, n_ctx_tiles, lat_len),
        grid=(t // tm, N_EXPERTS // eps_),
        in_specs=[
            pl.BlockSpec((tm, D_MODEL), lambda i, j: (i, 0)),
            pl.BlockSpec((tm, N_EXPERTS), lambda i, j: (i, 0)),
            pl.BlockSpec((None, eps_, D_MODEL, D_EXPERT), lambda i, j: (layer, j, 0, 0)),
            pl.BlockSpec((None, eps_, D_MODEL, D_EXPERT), lambda i, j: (layer, j, 0, 0)),
            pl.BlockSpec((None, eps_ // 2, 2 * D_EXPERT, D_MODEL), lambda i, j: (layer, j, 0, 0)),
            pl.BlockSpec((None, D_MODEL, D_EXPERT), lambda i, j: (layer, 0, 0), pipeline_mode=once),
            pl.BlockSpec((None, D_MODEL, D_EXPERT), lambda i, j: (layer, 0, 0), pipeline_mode=once),
            pl.BlockSpec((None, D_EXPERT, D_MODEL), lambda i, j: (layer, 0, 0), pipeline_mode=once),
            pl.BlockSpec((tm, D_MODEL), lambda i, j: (i, 0), pipeline_mode=once),
            pl.BlockSpec((8, 6, D_MODEL), lambda i, j: (0, 0, 0)),
            pl.BlockSpec((1, D_MODEL), lambda i, j: (0, 0)),
        ],
        out_specs=out_specs,
        out_shape=out_shape,
        compiler_params=_params(2),
        name="moe",
    )(h, gates, w_gate, w_up, w_down2, ws_gate, ws_up, ws_down, x, mods, final_g)


def _expert_cast_body(wg_ref, wu_ref, wd_ref, gu_ref, d_ref):
    gu_ref[:, :D_EXPERT] = wg_ref[...].astype(BF16)
    gu_ref[:, D_EXPERT:] = wu_ref[...].astype(BF16)
    d_ref[...] = wd_ref[...].astype(BF16)


def _expert_cast(w_gate, w_up, w_down):
    per = N_EXPERTS // N_GROUPS
    src = lambda l, r: (l, (r % N_GROUPS) * per + r // N_GROUPS, 0, 0)
    dst = lambda l, r: (l, r, 0, 0)
    return pl.pallas_call(
        _expert_cast_body,
        grid=(DEPTH, N_EXPERTS),
        in_specs=[pl.BlockSpec((None, None, D_MODEL, D_EXPERT), src),
                  pl.BlockSpec((None, None, D_MODEL, D_EXPERT), src),
                  pl.BlockSpec((None, None, D_EXPERT, D_MODEL), src)],
        out_specs=[pl.BlockSpec((None, None, D_MODEL, 2 * D_EXPERT), dst),
                   pl.BlockSpec((None, None, D_EXPERT, D_MODEL), dst)],
        out_shape=[jax.ShapeDtypeStruct((DEPTH, N_EXPERTS, D_MODEL, 2 * D_EXPERT), BF16),
                   jax.ShapeDtypeStruct((DEPTH, N_EXPERTS, D_EXPERT, D_MODEL), BF16)],
        compiler_params=_params(2),
        name="expert_cast",
    )(w_gate, w_up, w_down)


SLOT_ALIGN = 16
WIN = 128
ROW_BLOCK = 512
ZERO_TAIL = 640


def _slot_capacity(tt):
    most = TOP_K * tt + N_EXPERTS * (SLOT_ALIGN - 1)
    return -(-(most + ZERO_TAIL) // ROW_BLOCK) * ROW_BLOCK + WIN


def _moe_sparse_body(final_norm, n_ctx_tiles, cnt_ref, h_ref, gate_ref, rank_ref, wgu_ref, wd_ref,
                     sg_ref, su_ref, sd_ref, x_ref, mod_ref, fg_ref, *rest):
    o_refs = rest[:-4]
    pg_scr, xs_scr, acc_scr, offs = rest[-4:]
    i = pl.program_id(0)
    c = pl.program_id(1)
    is_ctx = i < n_ctx_tiles
    tt = h_ref.shape[0]
    eps_ = wgu_ref.shape[0]

    def count(r):
        return cnt_ref[i * N_EXPERTS + r]

    def n_windows(n):
        return (n + (WIN - 1)) // WIN

    @pl.when(c == 0)
    def _():
        def set_offset(r, run):
            offs[r] = run
            return run + ((count(r) + (SLOT_ALIGN - 1)) // SLOT_ALIGN) * SLOT_ALIGN

        total = lax.fori_loop(0, N_EXPERTS, set_offset, 0)
        offs[N_EXPERTS] = total

        slot_io = lax.broadcasted_iota(jnp.int32, (WIN, tt), 0)

        def build(r, carry):
            off = offs[r]
            rank_r = rank_ref[pl.ds(r, 1), :]
            gate_r = gate_ref[pl.ds(r, 1), :]
            live = gate_r > 0.0

            def window(b, carry2):
                mine = ((slot_io + b * WIN).astype(F32) == rank_r) & live
                rows = pl.ds(pl.multiple_of(off + b * WIN, SLOT_ALIGN), WIN)
                pg_scr[rows, :] = jnp.where(mine, gate_r, 0.0).astype(BF16)
                return carry2

            lax.fori_loop(0, n_windows(count(r)), window, 0)
            return carry

        lax.fori_loop(0, N_EXPERTS, build, 0)
        tail = pl.ds(pl.multiple_of(total, SLOT_ALIGN), ZERO_TAIL)
        pg_scr[tail, :] = jnp.zeros((ZERO_TAIL, tt), BF16)

        def gather(kb, carry):
            rows = pl.ds(pl.multiple_of(kb * ROW_BLOCK, ROW_BLOCK), ROW_BLOCK)
            one_hot = jnp.where(pg_scr[rows, :].astype(F32) > 0.0, 1.0, 0.0).astype(BF16)
            xs_scr[rows, :] = _dot(one_hot, h_ref[...]).astype(BF16)
            return carry

        lax.fori_loop(0, (total + (ROW_BLOCK - 1)) // ROW_BLOCK, gather, 0)

        h = h_ref[...]
        gu = _dot(h, jnp.concatenate([sg_ref[...].astype(BF16), su_ref[...].astype(BF16)], axis=1))
        act = gu[:, :D_EXPERT] * _sigmoid_t(gu[:, :D_EXPERT]) * gu[:, D_EXPERT:]
        acc_scr[...] = _dot(act.astype(BF16), sd_ref[...].astype(BF16))

    row_io = lax.broadcasted_iota(jnp.int32, (WIN, D_MODEL), 0)
    def expert_rows(p, x):
        gu = _dot(x, wgu_ref[p])
        act = gu[:, :D_EXPERT] * _sigmoid_t(gu[:, :D_EXPERT]) * gu[:, D_EXPERT:]
        return _dot(act.astype(BF16), wd_ref[p]).astype(BF16)

    offsets = [offs[c * eps_ + p] for p in range(eps_)]
    counts = [count(c * eps_ + p) for p in range(eps_)]
    wins = [pl.ds(pl.multiple_of(offsets[p], SLOT_ALIGN), WIN) for p in range(eps_)]
    xs = [xs_scr[wins[p], :] for p in range(eps_)]
    ys = [expert_rows(p, xs[p]) for p in range(eps_)]
    for p in range(eps_):
        xs_scr[wins[p], :] = jnp.where(row_io < counts[p], ys[p], xs[p])

    for p in range(eps_):
        def window(b, carry, p=p):
            rows = pl.ds(pl.multiple_of(offsets[p] + b * WIN, SLOT_ALIGN), WIN)
            x = xs_scr[rows, :]
            xs_scr[rows, :] = jnp.where(row_io + b * WIN < counts[p], expert_rows(p, x), x)
            return carry

        lax.fori_loop(1, n_windows(counts[p]), window, 0)

    @pl.when(c == pl.num_programs(1) - 1)
    def _():
        def combine(kb, carry):
            rows = pl.ds(pl.multiple_of(kb * ROW_BLOCK, ROW_BLOCK), ROW_BLOCK)
            acc_scr[...] += _dot_tn(pg_scr[rows, :], xs_scr[rows, :])
            return carry

        lax.fori_loop(0, (offs[N_EXPERTS] + (ROW_BLOCK - 1)) // ROW_BLOCK, combine, 0)
        x = x_ref[...] + mod_ref[5:6, :] * acc_scr[...]
        if not final_norm:
            o_refs[0][...] = x
        else:
            ms = jnp.mean(x * x, axis=-1, keepdims=True)
            y = x * lax.rsqrt(ms + EPS) * fg_ref[...]

            @pl.when(is_ctx)
            def _():
                o_refs[0][...] = y

            @pl.when(jnp.logical_not(is_ctx))
            def _():
                o_refs[1][...] = y


def _moe_sparse(h, gates_t, rank_t, counts, wgu, wd, ws_gate, ws_up, ws_down, layer, x, mod, final_g,
                final_norm, n_ctx_tok, lat_len):
    t = x.shape[0]
    tt = ROUTE_TILE
    eps_ = EXPERTS_PER_STEP
    n_ctx_tiles = n_ctx_tok // tt
    row = _mod_row(n_ctx_tiles, lat_len // tt)
    cap = _slot_capacity(tt)
    if final_norm:
        out_specs = [pl.BlockSpec((tt, D_MODEL), lambda i, c, cnt: (jnp.minimum(i, n_ctx_tiles - 1), 0)),
                     pl.BlockSpec((tt, D_MODEL), lambda i, c, cnt: (jnp.maximum(i - n_ctx_tiles, 0), 0))]
        out_shape = [jax.ShapeDtypeStruct((n_ctx_tok, D_MODEL), F32),
                     jax.ShapeDtypeStruct((t - n_ctx_tok, D_MODEL), F32)]
    else:
        out_specs = [pl.BlockSpec((tt, D_MODEL), lambda i, c, cnt: (i, 0))]
        out_shape = [jax.ShapeDtypeStruct((t, D_MODEL), F32)]
    grid_spec = pltpu.PrefetchScalarGridSpec(
        num_scalar_prefetch=1,
        grid=(t // tt, N_EXPERTS // eps_),
        in_specs=[
            pl.BlockSpec((tt, D_MODEL), lambda i, c, cnt: (i, 0)),
            pl.BlockSpec((N_EXPERTS, tt), lambda i, c, cnt: (0, i)),
            pl.BlockSpec((N_EXPERTS, tt), lambda i, c, cnt: (0, i)),
            pl.BlockSpec((None, eps_, D_MODEL, 2 * D_EXPERT), lambda i, c, cnt: (layer, c, 0, 0)),
            pl.BlockSpec((None, eps_, D_EXPERT, D_MODEL), lambda i, c, cnt: (layer, c, 0, 0)),
            pl.BlockSpec((None, D_MODEL, D_EXPERT), lambda i, c, cnt: (layer, 0, 0)),
            pl.BlockSpec((None, D_MODEL, D_EXPERT), lambda i, c, cnt: (layer, 0, 0)),
            pl.BlockSpec((None, D_EXPERT, D_MODEL), lambda i, c, cnt: (layer, 0, 0)),
            pl.BlockSpec((tt, D_MODEL), lambda i, c, cnt: (i, 0)),
            pl.BlockSpec((None, 6, D_MODEL), lambda i, c, cnt: (row(i), 0, 0)),
            pl.BlockSpec((1, D_MODEL), lambda i, c, cnt: (0, 0)),
        ],
        out_specs=out_specs,
        scratch_shapes=[pltpu.VMEM((cap, tt), BF16),
                        pltpu.VMEM((cap, D_MODEL), BF16), pltpu.VMEM((tt, D_MODEL), F32),
                        pltpu.SMEM((N_EXPERTS + 1,), jnp.int32)],
    )
    return pl.pallas_call(
        functools.partial(_moe_sparse_body, final_norm, n_ctx_tiles),
        grid_spec=grid_spec,
        out_shape=out_shape,
        compiler_params=_params(2),
        name="moe_sparse",
    )(counts, h, gates_t, rank_t, wgu, wd, ws_gate, ws_up, ws_down, x, mod, final_g)


def _block_diag_t(s):
    eye = jnp.eye(HG_HEADS, dtype=s.dtype)
    out = jnp.einsum('...hkv,hg->...hvgk', s, eye)
    return out.reshape(s.shape[:-3] + (HG_WIDTH, HG_WIDTH))


def kernel(x_prompt, x_sample, cache_k, cache_v, state_hgrn, c, c_ctx, w_ada, b_ada, norm_g, w_in,
           w_fourier, lambdas, attn_norm_g, lower_bounds, hg_norm_g, w_out, w_router, router_bias,
           w_gate, w_up, w_down, ws_gate, ws_up, ws_down, final_g):
    n_ctx, ctx_len, _ = x_prompt.shape
    n_lat, lat_len, _ = x_sample.shape
    n_ctx_tok = n_ctx * ctx_len
    past = cache_k.shape[2]

    xs = (x_prompt.reshape(n_ctx_tok, D_MODEL), x_sample.reshape(n_lat * lat_len, D_MODEL))

    c8 = jnp.zeros((8, D_MODEL), F32).at[0].set(c_ctx).at[1:1 + n_lat].set(c)
    mods = _ada_mods(c8, w_ada, b_ada).reshape(DEPTH, 8, 6, D_MODEL)

    cs = jnp.cumsum(jax.nn.softmax(lower_bounds.astype(F32), axis=0), axis=0)
    lbs = cs - cs[0:1]

    cos, sin = _rope_tables(lat_len)
    cache_k4 = cache_k.reshape(n_lat, DEPTH, past, DA_WIDTH)
    cache_v4 = cache_v.reshape(n_lat, DEPTH, past, DA_WIDTH)
    s0t = _block_diag_t(state_hgrn.astype(F32))

    qkv_end = FN_WIDTH + 3 * DA_WIDTH
    w_in_b = jnp.concatenate([w_in[:, :, FN_WIDTH:qkv_end], w_in[:, :, :FN_WIDTH], w_in[:, :, qkv_end:]],
                             axis=-1).astype(BF16)
    w_f_b = w_fourier.astype(BF16)
    w_out_b = w_out.astype(BF16)
    per = N_EXPERTS // N_GROUPS
    w_router_t = (jnp.swapaxes(w_router, 1, 2).reshape(DEPTH, N_GROUPS, per, D_MODEL)
                  .swapaxes(1, 2).reshape(DEPTH, N_EXPERTS, D_MODEL))
    bias_mm = router_bias.reshape(DEPTH, N_GROUPS, per).swapaxes(1, 2).reshape(DEPTH, N_EXPERTS, 1)
    w_down2 = w_down.reshape(DEPTH, N_EXPERTS // 2, 2 * D_EXPERT, D_MODEL)
    ng = jnp.tile(hg_norm_g, (1, HG_HEADS))

    caches = []
    new_s = []
    for l in range(DEPTH):
        lam_init = 0.8 - 0.6 * math.exp(-0.3 * l)
        mod = mods[l]
        last = l == DEPTH - 1
        proj_a, proj_h, k_l, v_l = _inproj(xs, mod, norm_g[l, 0:1], w_in_b[l], tuple(caches) if last else (),
                                           n_ctx, ctx_len, lat_len)
        caches = [k_l, v_l] if last else caches + [k_l, v_l]

        ag = attn_norm_g[l].reshape(1, DA_VDIM)
        fn_ctx, a_ctx, hg_ctx, st_ctx = _ctx_mixers(proj_a, proj_h, n_ctx, ctx_len, 2, w_f_b[l], lambdas[l], ag,
                                                    lam_init, lbs[l], ng[l:l + 1])
        fn_lat = _fourier(proj_a, n_ctx_tok, n_lat, lat_len, w_f_b[l])
        a_lat = _attn_lat(proj_a, n_ctx_tok, n_lat, lat_len, cache_k4, cache_v4, l, cos, sin, lambdas[l], ag,
                          lam_init)
        (hg_lat,) = _hgrn2(proj_h, n_ctx_tok, n_lat, lat_len, 1, lbs[l], ng[l:l + 1], s0t[:, l])
        new_s.append(jnp.swapaxes(st_ctx, -1, -2))

        x, h2, logits_t = _outproj((fn_ctx, fn_lat), (a_ctx, a_lat), (hg_ctx, hg_lat), xs, mod,
                                   norm_g[l, 1:2], w_out_b[l], w_router_t[l], n_ctx_tok, lat_len)
        gates = _route(logits_t, bias_mm[l])
        xs = (_moe(h2, gates, w_gate, w_up, w_down2, ws_gate, ws_up, ws_down, l, x, mod,
                   final_g.reshape(1, D_MODEL), l == DEPTH - 1, n_ctx_tok, lat_len),)

    y_prompt = xs[0][:n_ctx_tok].reshape(n_ctx, ctx_len, D_MODEL)
    y_sample = xs[0][n_ctx_tok:].reshape(n_lat, lat_len, D_MODEL)
    new_k = caches[0].reshape(n_ctx, DEPTH, ctx_len, DA_HEADS, DA_VDIM)
    new_v = caches[1].reshape(n_ctx, DEPTH, ctx_len, DA_HEADS, DA_VDIM)
    return (y_prompt, y_sample, new_k, new_v, jnp.stack(new_s, axis=1))
```

```python
import functools
import math

import numpy as np
import jax
import jax.numpy as jnp
from jax import lax
from jax.experimental import pallas as pl
from jax.experimental.pallas import tpu as pltpu

F32 = jnp.float32
BF16 = jnp.bfloat16

D_MODEL = 1024
DEPTH = 2
GRID_W = 64
FN_WIDTH = 256
FN_GROUPS = 4
FN_GROUP_DIM = 64
DA_WIDTH = 512
DA_HEADS = 4
DA_VDIM = 128
DA_HALF = 64
HG_WIDTH = 256
HG_HEADS = 4
HG_DK = 64
PROJ_WIDTH = 3072
CHUNK = 64
ROPE_THETA = 10000.0
N_EXPERTS = 64
TOP_K = 8
N_GROUPS = 8
TOPK_GROUPS = 4
D_EXPERT = 128
ROUTED_SCALE = 2.5
EPS = 1e-6

COL_Q, COL_K, COL_V, COL_FN = 0, 512, 1024, 1536
PROJ_A_WIDTH = 1792
COL_HQ, COL_HI, COL_HF, COL_HB, COL_HG = 0, 256, 512, 768, 1024
PROJ_H_WIDTH = 1280

TOKEN_TILE = 512
MOE_TILE = 1024
EXPERTS_PER_STEP = 8
ROUTE_TILE = 512
CHUNKS_PER_GROUP = 4
VMEM_LIMIT = 56 * 1024 * 1024


def _dot(a, b):
    return jnp.dot(a, b, preferred_element_type=F32)


def _dot_nt(a, b):
    return lax.dot_general(a, b, (((1,), (1,)), ((), ())), preferred_element_type=F32)


def _dot_tn(a, b):
    return lax.dot_general(a, b, (((0,), (0,)), ((), ())), preferred_element_type=F32)


def _split(x, n):
    parts = []
    r = x
    for i in range(n):
        p = r.astype(BF16)
        parts.append(p)
        if i + 1 < n:
            r = r - p.astype(F32)
    return parts


def _sigmoid(x):
    return 1.0 / (1.0 + jnp.exp(-x))


def _silu(x):
    return x * _sigmoid(x)


def _params(n_axes):
    return pltpu.CompilerParams(dimension_semantics=("arbitrary",) * n_axes,
                                vmem_limit_bytes=VMEM_LIMIT)


def _ada_body(c_ref, w_ref, b_ref, o_ref):
    a = _silu(c_ref[...])
    a_hi, a_lo = _split(a, 2)
    w_hi, w_lo = _split(w_ref[...], 2)
    o_ref[...] = _dot(a_hi, w_hi) + _dot(a_lo, w_hi) + _dot(a_hi, w_lo) + b_ref[...]


def _ada_mods(c8, w_ada, b_ada):
    tn = 1536
    return pl.pallas_call(
        _ada_body,
        grid=(DEPTH, 6 * D_MODEL // tn),
        in_specs=[
            pl.BlockSpec((8, D_MODEL), lambda l, j: (0, 0)),
            pl.BlockSpec((None, D_MODEL, tn), lambda l, j: (l, 0, j)),
            pl.BlockSpec((None, 1, tn), lambda l, j: (l, 0, j)),
        ],
        out_specs=pl.BlockSpec((None, 8, tn), lambda l, j: (l, 0, j)),
        out_shape=jax.ShapeDtypeStruct((DEPTH, 8, 6 * D_MODEL), F32),
        compiler_params=_params(2),
        name="ada_mods",
    )(c8, w_ada, b_ada.reshape(DEPTH, 1, 6 * D_MODEL))


def _modnorm(x, g, shift, scale):
    ms = jnp.mean(x * x, axis=-1, keepdims=True)
    return (x * lax.rsqrt(ms + EPS) * g) * (1.0 + scale) + shift


def _mod_row(n_ctx_tiles, tiles_per_latent):
    def f(i):
        return jnp.where(i < n_ctx_tiles, 0, 1 + (i - n_ctx_tiles) // tiles_per_latent)
    return f


def _row_specs(n_parts, tm, width, n_ctx_tiles):
    if n_parts == 1:
        return [pl.BlockSpec((tm, width), lambda i, *_: (i, 0))]
    return [pl.BlockSpec((tm, width), lambda i, *_: (jnp.minimum(i, n_ctx_tiles - 1), 0)),
            pl.BlockSpec((tm, width), lambda i, *_: (jnp.maximum(i - n_ctx_tiles, 0), 0))]


def _pick(is_ctx, refs):
    if len(refs) == 1:
        return refs[0][...]
    return jnp.where(is_ctx, refs[0][...], refs[1][...])


def _inproj_body(n_x, n_prev, n_ctx_tiles, ctx_len, *refs):
    x_refs = refs[:n_x]
    mod_ref, g_ref, w_ref = refs[n_x:n_x + 3]
    prev_refs = refs[n_x + 3:n_x + 3 + 2 * n_prev]
    oa_ref, oh_ref, k_ref, v_ref = refs[-4:]
    is_ctx = pl.program_id(0) < n_ctx_tiles
    x = _pick(is_ctx, x_refs)
    h = _modnorm(x, g_ref[...], mod_ref[0:1, :], mod_ref[1:2, :])
    proj = _dot(h.astype(BF16), w_ref[...])
    oa_ref[...] = proj[:, :PROJ_A_WIDTH].astype(BF16)
    oh_ref[...] = proj[:, PROJ_A_WIDTH:]

    @pl.when(is_ctx)
    def _():
        per_layer = ctx_len * DA_HEADS
        for s in range(proj.shape[0] // ctx_len):
            base = s * (n_prev + 1) * per_layer
            for l in range(n_prev):
                dst = slice(base + l * per_layer, base + (l + 1) * per_layer)
                src = slice(s * per_layer, (s + 1) * per_layer)
                k_ref[dst, :] = prev_refs[2 * l][src, :]
                v_ref[dst, :] = prev_refs[2 * l + 1][src, :]
            tok = slice(s * ctx_len, (s + 1) * ctx_len)
            for h in range(DA_HEADS):
                rows = pl.ds(base + n_prev * per_layer + h, ctx_len, stride=DA_HEADS)
                k_ref[rows, :] = proj[tok, COL_K + h * DA_VDIM:COL_K + (h + 1) * DA_VDIM]
                v_ref[rows, :] = proj[tok, COL_V + h * DA_VDIM:COL_V + (h + 1) * DA_VDIM]


def _inproj(xs, mod, g, w, prev_caches, n_ctx, ctx_len, lat_len):
    t = sum(x.shape[0] for x in xs)
    tm = TOKEN_TILE
    seqs = tm // ctx_len
    n_ctx_tiles = n_ctx // seqs
    n_prev = len(prev_caches) // 2
    row = _mod_row(n_ctx_tiles, lat_len // tm)
    tile_rows = seqs * ctx_len * DA_HEADS
    tile_block = lambda i: (jnp.minimum(i, n_ctx_tiles - 1), 0)
    cache_spec = pl.BlockSpec(((n_prev + 1) * tile_rows, DA_VDIM), tile_block)
    cache_shape = jax.ShapeDtypeStruct((n_ctx_tiles * (n_prev + 1) * tile_rows, DA_VDIM), F32)
    in_specs = _row_specs(len(xs), tm, D_MODEL, n_ctx_tiles) + [
        pl.BlockSpec((None, 6, D_MODEL), lambda i: (row(i), 0, 0)),
        pl.BlockSpec((1, D_MODEL), lambda i: (0, 0)),
        pl.BlockSpec((D_MODEL, PROJ_A_WIDTH + PROJ_H_WIDTH), lambda i: (0, 0)),
    ] + [pl.BlockSpec((tile_rows, DA_VDIM), tile_block)] * (2 * n_prev)
    return pl.pallas_call(
        functools.partial(_inproj_body, len(xs), n_prev, n_ctx_tiles, ctx_len),
        grid=(t // tm,),
        in_specs=in_specs,
        out_specs=[pl.BlockSpec((tm, PROJ_A_WIDTH), lambda i: (i, 0)),
                   pl.BlockSpec((tm, PROJ_H_WIDTH), lambda i: (i, 0)), cache_spec, cache_spec],
        out_shape=[jax.ShapeDtypeStruct((t, PROJ_A_WIDTH), BF16), jax.ShapeDtypeStruct((t, PROJ_H_WIDTH), F32),
                   cache_shape, cache_shape],
        compiler_params=_params(1),
        name="inproj",
    )(*xs, mod, g, w, *prev_caches)


def _fourier_body(u_ref, cl_ref, sl_ref, cc_ref, sc_ref, w_ref, o_ref):
    z = u_ref[...].astype(BF16)
    a = _dot(z, cc_ref[...]).astype(BF16)
    b = _dot(z, sc_ref[...]).astype(BF16)
    y = _dot(cl_ref[...], a) - _dot(sl_ref[...], b)
    o_ref[...] = _dot(y.astype(BF16), w_ref[...]).astype(o_ref.dtype)


def _dft_tables(n, block):
    i = np.arange(n)
    prod = (i[:, None] % block) * (i[None, :] % block) % block
    ang = prod.astype(np.float64) * (2.0 * math.pi / block)
    same = (i[:, None] // block) == (i[None, :] // block)
    scale = 1.0 / math.sqrt(block)
    c = np.where(same, np.cos(ang) * scale, 0.0).astype(np.float32)
    s = np.where(same, np.sin(ang) * scale, 0.0).astype(np.float32)
    return jnp.asarray(c).astype(BF16), jnp.asarray(s).astype(BF16)


def _fourier(proj, row0, n_seq, seq_len, w_f):
    cl, sl = _dft_tables(seq_len, seq_len)
    cc, sc = _dft_tables(FN_WIDTH, FN_GROUP_DIM)
    blk0 = row0 // seq_len
    full = lambda shape: pl.BlockSpec(shape, lambda i: (0, 0))
    return pl.pallas_call(
        _fourier_body,
        grid=(n_seq,),
        in_specs=[
            pl.BlockSpec((seq_len, FN_WIDTH), lambda i: (blk0 + i, COL_FN // FN_WIDTH)),
            full((seq_len, seq_len)), full((seq_len, seq_len)),
            full((FN_WIDTH, FN_WIDTH)), full((FN_WIDTH, FN_WIDTH)), full((FN_WIDTH, FN_WIDTH)),
        ],
        out_specs=pl.BlockSpec((seq_len, FN_WIDTH), lambda i: (i, 0)),
        out_shape=jax.ShapeDtypeStruct((n_seq * seq_len, FN_WIDTH), BF16),
        compiler_params=_params(1),
        name=f"fourier_{seq_len}",
    )(proj, cl, sl, cc, sc, w_f)


def _lambda_full(lmb, lam_init):
    a = jnp.sum(lmb[0:1, :] * lmb[1:2, :], axis=-1, keepdims=True)
    b = jnp.sum(lmb[2:3, :] * lmb[3:4, :], axis=-1, keepdims=True)
    return jnp.exp(a) - jnp.exp(b) + lam_init


def _softmax_parts(parts):
    m = parts[0].max(axis=-1, keepdims=True)
    for p in parts[1:]:
        m = jnp.maximum(m, p.max(axis=-1, keepdims=True))
    es = [jnp.exp(p - m) for p in parts]
    tot = es[0].sum(axis=-1, keepdims=True)
    for e in es[1:]:
        tot = tot + e.sum(axis=-1, keepdims=True)
    return es, 1.0 / tot


def _diff_head(q, ks, vs, lam, g, lam_init):
    vas = [jnp.concatenate([v.astype(BF16), jnp.ones(v.shape, BF16)], axis=1) for v in vs]
    outs = []
    for m in range(2):
        qm = q[:, m * DA_HALF:(m + 1) * DA_HALF].astype(BF16)
        parts = [_dot_nt(qm, k[:, m * DA_HALF:(m + 1) * DA_HALF].astype(BF16)) for k in ks]
        mx = parts[0].max(axis=-1, keepdims=True)
        for p in parts[1:]:
            mx = jnp.maximum(mx, p.max(axis=-1, keepdims=True))
        oa = _dot(jnp.exp(parts[0] - mx).astype(BF16), vas[0])
        for p, va in zip(parts[1:], vas[1:]):
            oa = oa + _dot(jnp.exp(p - mx).astype(BF16), va)
        outs.append(oa[:, :DA_VDIM] * (1.0 / oa[:, DA_VDIM:DA_VDIM + 1]))
    a = outs[0] - lam * outs[1]
    ms = jnp.mean(a * a, axis=-1, keepdims=True)
    return a * lax.rsqrt(ms + EPS) * g * (1.0 - lam_init)


def _attn_ctx_body(lam_init, q_ref, k_ref, v_ref, lmb_ref, g_ref, o_ref):
    lam = _lambda_full(lmb_ref[...], lam_init)
    scale = DA_HALF ** -0.5
    for h in range(DA_HEADS):
        sl = slice(h * DA_VDIM, (h + 1) * DA_VDIM)
        o = _diff_head(q_ref[:, sl] * scale, [k_ref[:, sl]], [v_ref[:, sl]], lam, g_ref[...], lam_init)
        o_ref[:, sl] = o.astype(o_ref.dtype)


def _attn_ctx(proj, n_seq, seq_len, lmb, g, lam_init):
    blk = lambda c: pl.BlockSpec((seq_len, DA_WIDTH), lambda i: (i, c))
    return pl.pallas_call(
        functools.partial(_attn_ctx_body, lam_init),
        grid=(n_seq,),
        in_specs=[
            blk(COL_Q // DA_WIDTH), blk(COL_K // DA_WIDTH), blk(COL_V // DA_WIDTH),
            pl.BlockSpec((4, DA_HALF), lambda i: (0, 0)),
            pl.BlockSpec((1, DA_VDIM), lambda i: (0, 0)),
        ],
        out_specs=pl.BlockSpec((seq_len, DA_WIDTH), lambda i: (i, 0)),
        out_shape=jax.ShapeDtypeStruct((n_seq * seq_len, DA_WIDTH), BF16),
        compiler_params=_params(1),
        name="attn_ctx",
    )(proj, proj, proj, lmb, g)


def _rope(x, cos, sin):
    lane = lax.broadcasted_iota(jnp.int32, x.shape, 1)
    first = ((lane >> 4) & 1) == 0
    rot = jnp.where(first, -pltpu.roll(x, 128 - DA_HALF // 4, 1), pltpu.roll(x, DA_HALF // 4, 1))
    return x * cos + rot * sin


def _attn_lat_body(lam_init, q_ref, k_ref, v_ref, kc_ref, vc_ref, cq_ref, sq_ref, ck_ref, sk_ref,
                   lmb_ref, g_ref, o_ref):
    lam = _lambda_full(lmb_ref[...], lam_init)
    scale = DA_HALF ** -0.5
    for h in range(DA_HEADS):
        sl = slice(h * DA_VDIM, (h + 1) * DA_VDIM)
        q = _rope(q_ref[:, sl].astype(F32), cq_ref[...], sq_ref[...]) * scale
        k = _rope(k_ref[:, sl].astype(F32), ck_ref[...], sk_ref[...])
        o = _diff_head(q, [k, kc_ref[:, sl]], [v_ref[:, sl], vc_ref[:, sl]], lam, g_ref[...], lam_init)
        o_ref[:, sl] = o.astype(o_ref.dtype)


def _attn_lat(proj, row0, n_seq, seq_len, cache_k, cache_v, layer, cos, sin, lmb, g, lam_init):
    tq = 512
    nq = seq_len // tq
    past = cache_k.shape[2]
    qb0 = row0 // tq
    kb0 = row0 // seq_len
    cache_spec = pl.BlockSpec((None, None, past, DA_WIDTH), lambda b, j: (b, layer, 0, 0))
    kv_spec = lambda c: pl.BlockSpec((seq_len, DA_WIDTH), lambda b, j: (kb0 + b, c))
    return pl.pallas_call(
        functools.partial(_attn_lat_body, lam_init),
        grid=(n_seq, nq),
        in_specs=[
            pl.BlockSpec((tq, DA_WIDTH), lambda b, j: (qb0 + b * nq + j, COL_Q // DA_WIDTH)),
            kv_spec(COL_K // DA_WIDTH), kv_spec(COL_V // DA_WIDTH),
            cache_spec, cache_spec,
            pl.BlockSpec((tq, DA_VDIM), lambda b, j: (j, 0)),
            pl.BlockSpec((tq, DA_VDIM), lambda b, j: (j, 0)),
            pl.BlockSpec((seq_len, DA_VDIM), lambda b, j: (0, 0)),
            pl.BlockSpec((seq_len, DA_VDIM), lambda b, j: (0, 0)),
            pl.BlockSpec((4, DA_HALF), lambda b, j: (0, 0)),
            pl.BlockSpec((1, DA_VDIM), lambda b, j: (0, 0)),
        ],
        out_specs=pl.BlockSpec((tq, DA_WIDTH), lambda b, j: (b * nq + j, 0)),
        out_shape=jax.ShapeDtypeStruct((n_seq * seq_len, DA_WIDTH), BF16),
        compiler_params=_params(2),
        name="attn_lat",
    )(proj, proj, proj, cache_k, cache_v, cos, sin, cos, sin, lmb, g)


def _rope_tables(n_tokens):
    rows = n_tokens // GRID_W
    row = np.repeat(np.arange(rows, dtype=np.float64), GRID_W)
    col = np.tile(np.arange(GRID_W, dtype=np.float64), rows)
    axis_dim = DA_HALF // 2
    inv_freq = ROPE_THETA ** (-np.arange(0, axis_dim, 2, dtype=np.float64) / axis_dim)
    ang_r = row[:, None] * inv_freq[None, :]
    ang_c = col[:, None] * inv_freq[None, :]
    ang = np.concatenate([ang_r, ang_r, ang_c, ang_c] * 2, axis=-1)
    return jnp.asarray(np.cos(ang).astype(np.float32)), jnp.asarray(np.sin(ang).astype(np.float32))


def _hgrn_body(n_chunks, has_s0, *refs):
    if has_s0:
        (hq_ref, hi_ref, hf_ref, hb_ref, hg_ref, lb_ref, ng_ref, s0_ref,
         o_ref, oi_scr, qe_scr, u_scr, st_scr, dec_scr) = refs
        so_ref = None
    else:
        (hq_ref, hi_ref, hf_ref, hb_ref, hg_ref, lb_ref, ng_ref,
         o_ref, so_ref, oi_scr, qe_scr, u_scr, st_scr, dec_scr) = refs
    c = CHUNK
    w = HG_WIDTH
    gc = CHUNKS_PER_GROUP
    gl = gc * c
    c_bits = c.bit_length() - 1
    r_io = lax.broadcasted_iota(jnp.int32, (gl, gl), 0)
    c_io = lax.broadcasted_iota(jnp.int32, (gl, gl), 1)
    same_chunk = (r_io >> c_bits) == (c_io >> c_bits)
    tri_f = (same_chunk & (c_io <= r_io)).astype(BF16)
    tri_b = (same_chunk & (c_io >= r_io)).astype(BF16)
    lane = lax.broadcasted_iota(jnp.int32, (1, w), 1)
    dk_bits = HG_DK.bit_length() - 1
    head_masks = [((lane >> dk_bits) == h).astype(F32) for h in range(HG_HEADS)]
    head_masks_b = [hm.astype(BF16) for hm in head_masks]
    bd = ((lax.broadcasted_iota(jnp.int32, (w, w), 0) >> dk_bits)
          == (lax.broadcasted_iota(jnp.int32, (w, w), 1) >> dk_bits))
    t_of_row = lax.broadcasted_iota(jnp.int32, (HG_HEADS * gl, gl), 0) & (gl - 1)
    s_of_col = lax.broadcasted_iota(jnp.int32, (HG_HEADS * gl, gl), 1)
    same = (t_of_row >> c_bits) == (s_of_col >> c_bits)
    causal_f = same & (s_of_col <= t_of_row)
    causal_b = same & (s_of_col >= t_of_row)

    dirs = ((hf_ref, tri_f, causal_f, c - 1), (hb_ref, tri_b, causal_b, 0))

    def per_chunk_row(x, row):
        return jnp.concatenate([jnp.broadcast_to(x[k * c + row:k * c + row + 1, :], (c, w)) for k in range(gc)],
                               axis=0)

    def group_terms(gi):
        rows = pl.ds(gi * gl if isinstance(gi, int) else pl.multiple_of(gi * gl, gl), gl)
        q = _silu(hq_ref[rows, :]) * (HG_DK ** -0.5)
        v = hi_ref[rows, :].astype(BF16)
        for d, (fp_ref, tri, causal, edge_row) in enumerate(dirs):
            lb = lb_ref[d:d + 1, :]
            fp = fp_ref[rows, :]
            lsig = jnp.minimum(fp, 0.0) - jnp.log1p(jnp.exp(-jnp.abs(fp)))
            la = jnp.log(lb)
            lbb = jnp.log1p(-lb) + lsig
            logf = jnp.maximum(la, lbb) + jnp.log1p(jnp.exp(-jnp.abs(la - lbb)))
            kk = (1.0 - lb) * (1.0 / (1.0 + jnp.exp(fp)))
            g3 = _split(logf, 3)
            cum = _dot(tri, g3[0]) + _dot(tri, g3[1]) + _dot(tri, g3[2])
            total = per_chunk_row(cum, edge_row)
            ref = per_chunk_row(cum, c // 2)
            qc = (q * jnp.exp(cum - ref)).astype(BF16)
            kc = (kk * jnp.exp(ref - cum)).astype(BF16)
            ke = (kk * jnp.exp(total - cum)).astype(BF16)
            qe_scr[d, rows, :] = (q * jnp.exp(cum)).astype(BF16)
            lhs = jnp.concatenate([qc * hm for hm in head_masks_b], axis=0)
            a = jnp.where(causal, _dot_nt(lhs, kc), 0.0).astype(BF16)
            o_stack = _dot(a, v)
            o = o_stack[0:gl, :] * head_masks[0]
            for h in range(1, HG_HEADS):
                o = o + o_stack[h * gl:(h + 1) * gl, :] * head_masks[h]
            oi_scr[d, rows, :] = o
            for k in range(gc):
                ck = slice(k * c, (k + 1) * c)
                i = gi * gc + k
                dec_scr[d, pl.ds(i, 1), :] = jnp.exp(cum[k * c + edge_row:k * c + edge_row + 1, :])
                u_scr[d, i] = jnp.where(bd, _dot_tn(v[ck, :], ke[ck, :]), 0.0)

    if n_chunks == gc:
        group_terms(0)
    else:
        def terms_step(gi, carry):
            group_terms(gi)
            return carry

        lax.fori_loop(0, n_chunks // gc, terms_step, 0)

    slab = 32
    for d in range(2):
        order = range(n_chunks) if d == 0 else range(n_chunks - 1, -1, -1)
        for r in range(w // slab):
            rs = slice(r * slab, (r + 1) * slab)
            st = s0_ref[d, rs, :] if has_s0 else jnp.zeros((slab, w), F32)
            for i in order:
                st_scr[d, i, rs, :] = st.astype(BF16)
                st = st * dec_scr[d, i:i + 1, :] + u_scr[d, i, rs, :]
            if so_ref is not None:
                h = (r * slab) // HG_DK
                off = (r * slab) % HG_DK
                so_ref[d, h, off:off + slab, :] = st[:, h * HG_DK:(h + 1) * HG_DK]

    ones_bd = bd.astype(BF16)

    def finish(i, carry):
        rows = pl.ds(pl.multiple_of(i * c, c), c)
        o = (oi_scr[0, rows, :] + oi_scr[1, rows, :]
             + _dot_nt(qe_scr[0, rows, :], st_scr[0, i]) + _dot_nt(qe_scr[1, rows, :], st_scr[1, i]))
        sq = _split(o * o, 2)
        ms = (_dot(sq[0], ones_bd) + _dot(sq[1], ones_bd)) * (1.0 / HG_DK)
        y = o * lax.rsqrt(ms + EPS) * ng_ref[...]
        o_ref[rows, :] = (y * _silu(hg_ref[rows, :])).astype(o_ref.dtype)
        return carry

    lax.fori_loop(0, n_chunks, finish, 0, unroll=2)


def _hgrn(proj, row0, n_seq, seq_len, lb, ng, s0t):
    n_chunks = seq_len // CHUNK
    blk0 = row0 // seq_len
    col = lambda c0: pl.BlockSpec((seq_len, HG_WIDTH), lambda i: (blk0 + i, c0 // HG_WIDTH))
    in_specs = [col(COL_HQ), col(COL_HI), col(COL_HF), col(COL_HB), col(COL_HG),
                pl.BlockSpec((2, HG_WIDTH), lambda i: (0, 0)),
                pl.BlockSpec((1, HG_WIDTH), lambda i: (0, 0))]
    args = [proj, proj, proj, proj, proj, lb, ng]
    out_specs = [pl.BlockSpec((seq_len, HG_WIDTH), lambda i: (i, 0))]
    out_shape = [jax.ShapeDtypeStruct((n_seq * seq_len, HG_WIDTH), BF16)]
    if s0t is not None:
        in_specs.append(pl.BlockSpec((None, 2, HG_WIDTH, HG_WIDTH), lambda i: (i, 0, 0, 0)))
        args.append(s0t)
    else:
        out_specs.append(pl.BlockSpec((None, 2, HG_HEADS, HG_DK, HG_DK), lambda i: (i, 0, 0, 0, 0)))
        out_shape.append(jax.ShapeDtypeStruct((n_seq, 2, HG_HEADS, HG_DK, HG_DK), F32))
    return pl.pallas_call(
        functools.partial(_hgrn_body, n_chunks, s0t is not None),
        grid=(n_seq,),
        in_specs=in_specs,
        out_specs=out_specs,
        out_shape=out_shape,
        scratch_shapes=[pltpu.VMEM((2, seq_len, HG_WIDTH), F32),
                        pltpu.VMEM((2, seq_len, HG_WIDTH), BF16),
                        pltpu.VMEM((2, n_chunks, HG_WIDTH, HG_WIDTH), F32),
                        pltpu.VMEM((2, n_chunks, HG_WIDTH, HG_WIDTH), BF16),
                        pltpu.VMEM((2, max(n_chunks, 8), HG_WIDTH), F32)],
        compiler_params=_params(1),
        name=f"hgrn_{seq_len}",
    )(*args)


def _sigmoid_t(x):
    return 0.5 * jnp.tanh(0.5 * x) + 0.5


def _hgrn_tables():
    gl = CHUNKS_PER_GROUP * CHUNK
    t = np.arange(gl)
    same = (t[:, None] // CHUNK) == (t[None, :] // CHUNK)
    fwd = same & (t[None, :] <= t[:, None])
    bwd = same & (t[None, :] >= t[:, None])
    f = np.arange(HG_WIDTH)
    bd = (f[:, None] // HG_DK) == (f[None, :] // HG_DK)
    tri = jnp.asarray(np.stack([fwd, bwd]).astype(np.float32)).astype(BF16)
    causal = jnp.asarray(np.stack([fwd, bwd]).astype(np.float32))
    return tri, causal, jnp.asarray(bd.astype(np.float32))


def _hgrn2_body(n_seq, groups_per_seq, has_s0, *refs, between=None):
    (hq_ref, hi_ref, hf_ref, hb_ref, hg_ref, lb_ref, ng_ref, tri_ref, causal_ref, bd_ref) = refs[:10]
    if has_s0:
        s0_ref, o_ref, oi_scr, qe_scr, u_scr, st_scr, dec_scr = refs[10:]
        so_ref = None
    else:
        o_ref, so_ref, oi_scr, qe_scr, u_scr, st_scr, dec_scr = refs[10:]
        s0_ref = None
    c = CHUNK
    w = HG_WIDTH
    gc = CHUNKS_PER_GROUP
    gl = gc * c
    n_groups = n_seq * groups_per_seq
    chunks_per_seq = groups_per_seq * gc
    lane = lax.broadcasted_iota(jnp.int32, (1, w), 1)
    dk_bits = HG_DK.bit_length() - 1
    head_masks_b = [((lane >> dk_bits) == h).astype(BF16) for h in range(HG_HEADS)]
    bd = bd_ref[...] > 0.5
    edge_rows = (c - 1, 0)
    fp_refs = (hf_ref, hb_ref)

    def per_chunk_row(x, row):
        return jnp.concatenate([jnp.broadcast_to(x[k * c + row:k * c + row + 1, :], (c, w)) for k in range(gc)],
                               axis=0)

    def group_terms(gi):
        rows = pl.ds(gi * gl if isinstance(gi, int) else pl.multiple_of(gi * gl, gl), gl)
        hq = hq_ref[rows, :]
        q = hq * _sigmoid_t(hq) * (HG_DK ** -0.5)
        v = hi_ref[rows, :].astype(BF16)
        v_stack = jnp.concatenate([v * hm for hm in head_masks_b], axis=0)
        for d in range(2):
            lb = lb_ref[d:d + 1, :]
            fp = fp_refs[d][rows, :]
            lsig = jnp.minimum(fp, 0.0) - jnp.log(1.0 + jnp.exp(-jnp.abs(fp)))
            la = jnp.log(lb)
            lbb = jnp.log(1.0 - lb) + lsig
            logf = jnp.maximum(la, lbb) + jnp.log(1.0 + jnp.exp(-jnp.abs(la - lbb)))
            kk = (1.0 - lb) * _sigmoid_t(-fp)
            g_hi, g_lo = _split(logf, 2)
            cum = _dot(tri_ref[d], g_hi) + _dot(tri_ref[d], g_lo)
            total = per_chunk_row(cum, edge_rows[d])
            ref = per_chunk_row(cum, c // 2)
            qc = (q * jnp.exp(cum - ref)).astype(BF16)
            kc = (kk * jnp.exp(ref - cum)).astype(BF16)
            ke = (kk * jnp.exp(total - cum)).astype(BF16)
            qe_scr[d, rows, :] = (q * jnp.exp(cum)).astype(BF16)
            kc_stack = jnp.concatenate([kc * hm for hm in head_masks_b], axis=0)
            a = _dot_nt(qc, kc_stack)
            keep = causal_ref[d] > 0.5
            a = jnp.concatenate([jnp.where(keep, a[:, h * gl:(h + 1) * gl], 0.0) for h in range(HG_HEADS)],
                                axis=1).astype(BF16)
            oi_scr[d, rows, :] = _dot(a, v_stack)
            for k in range(gc):
                ck = slice(k * c, (k + 1) * c)
                i = gi * gc + k
                dec_scr[d, pl.ds(i, 1), :] = jnp.exp(cum[k * c + edge_rows[d]:k * c + edge_rows[d] + 1, :])
                u_scr[d, i] = jnp.where(bd, _dot_tn(v[ck, :], ke[ck, :]), 0.0)

    if n_groups <= 2:
        for gi in range(n_groups):
            group_terms(gi)
    else:
        def terms_step(gi, carry):
            group_terms(gi)
            return carry

        lax.fori_loop(0, n_groups, terms_step, 0)

    if between is not None:
        between()

    slab = 32
    for s in range(n_seq):
        first = s * chunks_per_seq
        for d in range(2):
            order = range(chunks_per_seq) if d == 0 else range(chunks_per_seq - 1, -1, -1)
            for r in range(w // slab):
                rs = slice(r * slab, (r + 1) * slab)
                st = s0_ref[s, d, rs, :] if has_s0 else jnp.zeros((slab, w), F32)
                for j in order:
                    i = first + j
                    st_scr[d, i, rs, :] = st.astype(BF16)
                    st = st * dec_scr[d, i:i + 1, :] + u_scr[d, i, rs, :]
                if so_ref is not None:
                    h = (r * slab) // HG_DK
                    off = (r * slab) % HG_DK
                    so_ref[s, d, h, off:off + slab, :] = st[:, h * HG_DK:(h + 1) * HG_DK]

    ones_bd = bd.astype(BF16)

    def finish(gi):
        rows = pl.ds(gi * gl if isinstance(gi, int) else pl.multiple_of(gi * gl, gl), gl)
        inter = []
        for k in range(gc):
            i = gi * gc + k
            ck = pl.ds(gi * gl + k * c if isinstance(gi, int) else pl.multiple_of(gi * gl + k * c, c), c)
            inter.append(_dot_nt(qe_scr[0, ck, :], st_scr[0, i]) + _dot_nt(qe_scr[1, ck, :], st_scr[1, i]))
        o = oi_scr[0, rows, :] + oi_scr[1, rows, :] + jnp.concatenate(inter, axis=0)
        sq = _split(o * o, 2)
        ms = (_dot(sq[0], ones_bd) + _dot(sq[1], ones_bd)) * (1.0 / HG_DK)
        y = o * lax.rsqrt(ms + EPS) * ng_ref[...]
        hg = hg_ref[rows, :]
        o_ref[rows, :] = (y * hg * _sigmoid_t(hg)).astype(o_ref.dtype)

    if n_groups <= 2:
        for gi in range(n_groups):
            finish(gi)
    else:
        def finish_step(gi, carry):
            finish(gi)
            return carry

        lax.fori_loop(0, n_groups, finish_step, 0)


def _hgrn2(proj_h, row0, n_seq, seq_len, seqs_per_step, lb, ng, s0t):
    gl = CHUNKS_PER_GROUP * CHUNK
    groups_per_seq = seq_len // gl
    n_chunks = seqs_per_step * seq_len // CHUNK
    rows = seqs_per_step * seq_len
    blk0 = row0 // rows
    tri, causal, bd = _hgrn_tables()
    col = lambda c0: pl.BlockSpec((rows, HG_WIDTH), lambda i: (blk0 + i, c0 // HG_WIDTH))
    const = lambda shape: pl.BlockSpec(shape, lambda i: (0,) * len(shape))
    in_specs = [col(COL_HQ), col(COL_HI), col(COL_HF), col(COL_HB), col(COL_HG),
                const((2, HG_WIDTH)), const((1, HG_WIDTH)),
                const((2, gl, gl)), const((2, gl, gl)), const((HG_WIDTH, HG_WIDTH))]
    args = [proj_h] * 5 + [lb, ng, tri, causal, bd]
    out_specs = [pl.BlockSpec((rows, HG_WIDTH), lambda i: (i, 0))]
    out_shape = [jax.ShapeDtypeStruct((n_seq * seq_len, HG_WIDTH), BF16)]
    if s0t is not None:
        in_specs.append(pl.BlockSpec((seqs_per_step, 2, HG_WIDTH, HG_WIDTH), lambda i: (i, 0, 0, 0)))
        args.append(s0t)
    else:
        out_specs.append(pl.BlockSpec((seqs_per_step, 2, HG_HEADS, HG_DK, HG_DK), lambda i: (i, 0, 0, 0, 0)))
        out_shape.append(jax.ShapeDtypeStruct((n_seq, 2, HG_HEADS, HG_DK, HG_DK), F32))
    return pl.pallas_call(
        functools.partial(_hgrn2_body, seqs_per_step, groups_per_seq, s0t is not None),
        grid=(n_seq // seqs_per_step,),
        in_specs=in_specs,
        out_specs=out_specs,
        out_shape=out_shape,
        scratch_shapes=[pltpu.VMEM((2, rows, HG_WIDTH), F32),
                        pltpu.VMEM((2, rows, HG_WIDTH), BF16),
                        pltpu.VMEM((2, n_chunks, HG_WIDTH, HG_WIDTH), F32),
                        pltpu.VMEM((2, n_chunks, HG_WIDTH, HG_WIDTH), BF16),
                        pltpu.VMEM((2, max(n_chunks, 8), HG_WIDTH), F32)],
        compiler_params=_params(1),
        name=f"hgrn_{seq_len}",
    )(*args)


def _ctx_mixers_body(n_seq, seq_len, lam_init, *refs):
    (q_ref, k_ref, v_ref, u_ref, hq_ref, hi_ref, hf_ref, hb_ref, hg_ref,
     cl_ref, sl_ref, cc_ref, sc_ref, wf_ref, lmb_ref, ag_ref,
     lb_ref, ng_ref, tri_ref, causal_ref, bd_ref,
     fn_ref, a_ref, ho_ref, so_ref) = refs[:25]
    scratch = refs[25:]

    def fourier_and_attention():
        z = u_ref[...].astype(BF16)
        za = _dot(z, cc_ref[...]).astype(BF16)
        zb = _dot(z, sc_ref[...]).astype(BF16)
        for s in range(n_seq):
            rs = slice(s * seq_len, (s + 1) * seq_len)
            y = _dot(cl_ref[...], za[rs, :]) - _dot(sl_ref[...], zb[rs, :])
            fn_ref[rs, :] = _dot(y.astype(BF16), wf_ref[...]).astype(fn_ref.dtype)

        lam = _lambda_full(lmb_ref[...], lam_init)
        scale = DA_HALF ** -0.5
        for s in range(n_seq):
            rs = slice(s * seq_len, (s + 1) * seq_len)
            for h in range(DA_HEADS):
                sl = slice(h * DA_VDIM, (h + 1) * DA_VDIM)
                o = _diff_head(q_ref[rs, sl] * scale, [k_ref[rs, sl]], [v_ref[rs, sl]], lam, ag_ref[...],
                               lam_init)
                a_ref[rs, sl] = o.astype(a_ref.dtype)

    fourier_and_attention()
    _hgrn2_body(n_seq, seq_len // (CHUNKS_PER_GROUP * CHUNK), False,
                hq_ref, hi_ref, hf_ref, hb_ref, hg_ref, lb_ref, ng_ref, tri_ref, causal_ref, bd_ref,
                ho_ref, so_ref, *scratch)


def _ctx_mixers(proj_a, proj_h, n_seq, seq_len, seqs_per_step, w_f, lmb, ag, lam_init, lb, ng):
    rows = seqs_per_step * seq_len
    n_chunks = rows // CHUNK
    gl = CHUNKS_PER_GROUP * CHUNK
    cl, sl = _dft_tables(seq_len, seq_len)
    cc, sc = _dft_tables(FN_WIDTH, FN_GROUP_DIM)
    tri, causal, bd = _hgrn_tables()
    col = lambda c0, width: pl.BlockSpec((rows, width), lambda i: (i, c0 // width))
    const = lambda shape: pl.BlockSpec(shape, lambda i: (0,) * len(shape))
    out_rows = lambda width: pl.BlockSpec((rows, width), lambda i: (i, 0))
    n_tok = n_seq * seq_len
    return pl.pallas_call(
        functools.partial(_ctx_mixers_body, seqs_per_step, seq_len, lam_init),
        grid=(n_seq // seqs_per_step,),
        in_specs=[col(COL_Q, DA_WIDTH), col(COL_K, DA_WIDTH), col(COL_V, DA_WIDTH), col(COL_FN, FN_WIDTH),
                  col(COL_HQ, HG_WIDTH), col(COL_HI, HG_WIDTH), col(COL_HF, HG_WIDTH), col(COL_HB, HG_WIDTH),
                  col(COL_HG, HG_WIDTH),
                  const((seq_len, seq_len)), const((seq_len, seq_len)),
                  const((FN_WIDTH, FN_WIDTH)), const((FN_WIDTH, FN_WIDTH)), const((FN_WIDTH, FN_WIDTH)),
                  const((4, DA_HALF)), const((1, DA_VDIM)),
                  const((2, HG_WIDTH)), const((1, HG_WIDTH)),
                  const((2, gl, gl)), const((2, gl, gl)), const((HG_WIDTH, HG_WIDTH))],
        out_specs=[out_rows(FN_WIDTH), out_rows(DA_WIDTH), out_rows(HG_WIDTH),
                   pl.BlockSpec((seqs_per_step, 2, HG_HEADS, HG_DK, HG_DK), lambda i: (i, 0, 0, 0, 0))],
        out_shape=[jax.ShapeDtypeStruct((n_tok, FN_WIDTH), BF16),
                   jax.ShapeDtypeStruct((n_tok, DA_WIDTH), BF16),
                   jax.ShapeDtypeStruct((n_tok, HG_WIDTH), BF16),
                   jax.ShapeDtypeStruct((n_seq, 2, HG_HEADS, HG_DK, HG_DK), F32)],
        scratch_shapes=[pltpu.VMEM((2, rows, HG_WIDTH), F32),
                        pltpu.VMEM((2, rows, HG_WIDTH), BF16),
                        pltpu.VMEM((2, n_chunks, HG_WIDTH, HG_WIDTH), F32),
                        pltpu.VMEM((2, n_chunks, HG_WIDTH, HG_WIDTH), BF16),
                        pltpu.VMEM((2, max(n_chunks, 8), HG_WIDTH), F32)],
        compiler_params=_params(1),
        name="ctx_mixers",
    )(*([proj_a] * 4 + [proj_h] * 5), cl, sl, cc, sc, w_f, lmb, ag, lb, ng, tri, causal, bd)


def _outproj_body(n_x, n_ctx_tiles, *refs):
    fn_refs, a_refs, hg_refs = refs[0:2], refs[2:4], refs[4:6]
    x_refs = refs[6:6 + n_x]
    mod_ref, g_ref, w_ref, wr_ref, rb_ref, xo_ref, h_ref, gate_ref = refs[6 + n_x:]
    is_ctx = pl.program_id(0) < n_ctx_tiles
    mix = (_dot(_pick(is_ctx, fn_refs), w_ref[0:FN_WIDTH, :])
           + _dot(_pick(is_ctx, a_refs), w_ref[FN_WIDTH:FN_WIDTH + DA_WIDTH, :])
           + _dot(_pick(is_ctx, hg_refs), w_ref[FN_WIDTH + DA_WIDTH:, :]))
    x = _pick(is_ctx, x_refs) + mod_ref[2:3, :] * mix
    xo_ref[...] = x
    h = _modnorm(x, g_ref[...], mod_ref[3:4, :], mod_ref[4:5, :])
    h_ref[...] = h.astype(BF16)
    h_hi, h_lo = _split(h, 2)
    w_hi, w_lo = _split(wr_ref[...], 2)
    logits_t = _dot_nt(w_hi, h_hi) + _dot_nt(w_lo, h_hi) + _dot_nt(w_hi, h_lo)
    gate_ref[...] = _route_gates(logits_t, rb_ref[...])


def _outproj(fn, a, hg, xs, mod, g, w_out, w_router_t, router_bias, n_ctx_tok, lat_len):
    t = sum(x.shape[0] for x in xs)
    tm = TOKEN_TILE
    n_ctx_tiles = n_ctx_tok // tm
    row = _mod_row(n_ctx_tiles, lat_len // tm)
    rows = lambda width: pl.BlockSpec((tm, width), lambda i: (i, 0))
    parts = lambda n, width: _row_specs(n, tm, width, n_ctx_tiles)
    return pl.pallas_call(
        functools.partial(_outproj_body, len(xs), n_ctx_tiles),
        grid=(t // tm,),
        in_specs=parts(2, FN_WIDTH) + parts(2, DA_WIDTH) + parts(2, HG_WIDTH) + parts(len(xs), D_MODEL) + [
            pl.BlockSpec((None, 6, D_MODEL), lambda i: (row(i), 0, 0)),
            pl.BlockSpec((1, D_MODEL), lambda i: (0, 0)),
            pl.BlockSpec((D_MODEL, D_MODEL), lambda i: (0, 0)),
            pl.BlockSpec((N_EXPERTS, D_MODEL), lambda i: (0, 0)),
            pl.BlockSpec((N_EXPERTS, 1), lambda i: (0, 0)),
        ],
        out_specs=[rows(D_MODEL), rows(D_MODEL), rows(N_EXPERTS)],
        out_shape=[jax.ShapeDtypeStruct((t, D_MODEL), F32),
                   jax.ShapeDtypeStruct((t, D_MODEL), BF16),
                   jax.ShapeDtypeStruct((t, N_EXPERTS), F32)],
        compiler_params=_params(1),
        name="outproj",
    )(*fn, *a, *hg, *xs, mod, g, w_out, w_router_t, router_bias)


def _route_gates(lt, bias):
    per = N_EXPERTS // N_GROUPS
    tt = lt.shape[1]
    neg = -jnp.inf
    assert per == N_GROUPS == 8
    gi = lax.broadcasted_iota(jnp.int32, (N_GROUPS, tt), 0).astype(F32)
    s_j, b_j = [], []
    for j in range(per):
        s = _sigmoid(lt[j * N_GROUPS:(j + 1) * N_GROUPS, :])
        s_j.append(s)
        b_j.append(s + bias[j * N_GROUPS:(j + 1) * N_GROUPS, :])
    m1 = functools.reduce(jnp.maximum, b_j)
    i1 = functools.reduce(jnp.minimum, [jnp.where(b_j[j] == m1, float(j), float(per)) for j in range(per)])
    m2 = functools.reduce(jnp.maximum, [jnp.where(i1 == float(j), neg, b_j[j]) for j in range(per)])
    gs = m1 + m2
    gsel = jnp.zeros((N_GROUPS, tt), jnp.bool_)
    for _ in range(TOPK_GROUPS):
        m = gs.max(axis=0, keepdims=True)
        idx = jnp.where(gs == m, gi, float(N_GROUPS)).min(axis=0, keepdims=True)
        hit = gi == idx
        gsel = gsel | hit
        gs = jnp.where(hit, neg, gs)
    x_j = [jnp.where(gsel, b_j[j], neg) for j in range(per)]
    e_j = [gi * per + j for j in range(per)]
    sel_j = [jnp.zeros((N_GROUPS, tt), jnp.bool_) for _ in range(per)]
    for _ in range(TOP_K):
        m = functools.reduce(jnp.maximum, x_j).max(axis=0, keepdims=True)
        idx = functools.reduce(jnp.minimum, [jnp.where(x_j[j] == m, e_j[j], float(N_EXPERTS))
                                             for j in range(per)]).min(axis=0, keepdims=True)
        for j in range(per):
            hit = e_j[j] == idx
            sel_j[j] = sel_j[j] | hit
            x_j[j] = jnp.where(hit, neg, x_j[j])
    w_j = [jnp.where(sel_j[j], s_j[j], 0.0) for j in range(per)]
    denom = functools.reduce(lambda a, b: a + b, w_j).sum(axis=0, keepdims=True)
    gates_t = jnp.concatenate([w / denom * ROUTED_SCALE for w in w_j], axis=0)
    r_io = lax.broadcasted_iota(jnp.int32, (N_EXPERTS, N_EXPERTS), 0)
    e_io = lax.broadcasted_iota(jnp.int32, (N_EXPERTS, N_EXPERTS), 1)
    eye = (e_io == (r_io & (N_GROUPS - 1)) * per + (r_io >> 3)).astype(BF16)
    p = _split(gates_t, 3)
    return _dot_tn(p[0], eye) + _dot_tn(p[1], eye) + _dot_tn(p[2], eye)


def _moe_body(final_norm, n_ctx_tiles, h_ref, gate_ref, wg_ref, wu_ref, wd_ref, sg_ref, su_ref, sd_ref,
              x_ref, mod_ref, fg_ref, *out_and_scratch):
    acc_ref = out_and_scratch[-1]
    o_refs = out_and_scratch[:-1]
    is_ctx = pl.program_id(0) < n_ctx_tiles
    j = pl.program_id(1)
    h = h_ref[...]

    def act_of(wg, wu, gate):
        gu = _dot(h, jnp.concatenate([wg.astype(BF16), wu.astype(BF16)], axis=1))
        a = _silu(gu[:, :D_EXPERT]) * gu[:, D_EXPERT:]
        return a if gate is None else a * gate

    @pl.when(j == 0)
    def _():
        acc_ref[...] = _dot(act_of(sg_ref[...], su_ref[...], None).astype(BF16), sd_ref[...].astype(BF16))

    gates = gate_ref[...]
    expert_of_lane = lax.broadcasted_iota(jnp.int32, gates.shape, 1)

    def gate_col(p):
        e = j * EXPERTS_PER_STEP + p
        return jnp.sum(jnp.where(expert_of_lane == e, gates, 0.0), axis=1, keepdims=True)

    for p in range(EXPERTS_PER_STEP // 2):
        a0 = act_of(wg_ref[2 * p], wu_ref[2 * p], gate_col(2 * p))
        a1 = act_of(wg_ref[2 * p + 1], wu_ref[2 * p + 1], gate_col(2 * p + 1))
        pair = jnp.concatenate([a0, a1], axis=1).astype(BF16)
        acc_ref[...] += _dot(pair, wd_ref[p].astype(BF16))

    @pl.when(j == pl.num_programs(1) - 1)
    def _():
        x = x_ref[...] + mod_ref[5:6, :] * acc_ref[...]
        if not final_norm:
            o_refs[0][...] = x
        else:
            ms = jnp.mean(x * x, axis=-1, keepdims=True)
            y = x * lax.rsqrt(ms + EPS) * fg_ref[...]

            @pl.when(is_ctx)
            def _():
                o_refs[0][...] = y

            @pl.when(jnp.logical_not(is_ctx))
            def _():
                o_refs[1][...] = y


def _moe(h, gates, w_gate, w_up, w_down2, ws_gate, ws_up, ws_down, layer, x, mod, final_g, final_norm,
         n_ctx_tok, lat_len):
    t = x.shape[0]
    tm = MOE_TILE
    eps_ = EXPERTS_PER_STEP
    n_ctx_tiles = n_ctx_tok // tm
    row = _mod_row(n_ctx_tiles, lat_len // tm)
    once = pl.Buffered(1)
    if final_norm:
        last_ctx = n_ctx_tiles - 1
        out_specs = [pl.BlockSpec((tm, D_MODEL), lambda i, j: (jnp.minimum(i, last_ctx), 0), pipeline_mode=once),
                     pl.BlockSpec((tm, D_MODEL), lambda i, j: (jnp.maximum(i - n_ctx_tiles, 0), 0),
                                  pipeline_mode=once)]
        out_shape = [jax.ShapeDtypeStruct((n_ctx_tok, D_MODEL), F32),
                     jax.ShapeDtypeStruct((t - n_ctx_tok, D_MODEL), F32)]
    else:
        out_specs = [pl.BlockSpec((tm, D_MODEL), lambda i, j: (i, 0), pipeline_mode=once)]
        out_shape = [jax.ShapeDtypeStruct((t, D_MODEL), F32)]
    return pl.pallas_call(
        functools.partial(_moe_body, final_norm, n_ctx_tiles),
        grid=(t // tm, N_EXPERTS // eps_),
        in_specs=[
            pl.BlockSpec((tm, D_MODEL), lambda i, j: (i, 0)),
            pl.BlockSpec((tm, N_EXPERTS), lambda i, j: (i, 0)),
            pl.BlockSpec((None, eps_, D_MODEL, D_EXPERT), lambda i, j: (layer, j, 0, 0)),
            pl.BlockSpec((None, eps_, D_MODEL, D_EXPERT), lambda i, j: (layer, j, 0, 0)),
            pl.BlockSpec((None, eps_ // 2, 2 * D_EXPERT, D_MODEL), lambda i, j: (layer, j, 0, 0)),
            pl.BlockSpec((None, D_MODEL, D_EXPERT), lambda i, j: (layer, 0, 0), pipeline_mode=once),
            pl.BlockSpec((None, D_MODEL, D_EXPERT), lambda i, j: (layer, 0, 0), pipeline_mode=once),
            pl.BlockSpec((None, D_EXPERT, D_MODEL), lambda i, j: (layer, 0, 0), pipeline_mode=once),
            pl.BlockSpec((tm, D_MODEL), lambda i, j: (i, 0), pipeline_mode=once),
            pl.BlockSpec((None, 6, D_MODEL), lambda i, j: (row(i), 0, 0)),
            pl.BlockSpec((1, D_MODEL), lambda i, j: (0, 0)),
        ],
        out_specs=out_specs,
        out_shape=out_shape,
        scratch_shapes=[pltpu.VMEM((tm, D_MODEL), F32)],
        compiler_params=_params(2),
        name="moe",
    )(h, gates, w_gate, w_up, w_down2, ws_gate, ws_up, ws_down, x, mod, final_g)


def _expert_cast_body(wg_ref, wu_ref, wd_ref, gu_ref, d_ref):
    gu_ref[:, :D_EXPERT] = wg_ref[...].astype(BF16)
    gu_ref[:, D_EXPERT:] = wu_ref[...].astype(BF16)
    d_ref[...] = wd_ref[...].astype(BF16)


def _expert_cast(w_gate, w_up, w_down):
    per = N_EXPERTS // N_GROUPS
    src = lambda l, r: (l, (r % N_GROUPS) * per + r // N_GROUPS, 0, 0)
    dst = lambda l, r: (l, r, 0, 0)
    return pl.pallas_call(
        _expert_cast_body,
        grid=(DEPTH, N_EXPERTS),
        in_specs=[pl.BlockSpec((None, None, D_MODEL, D_EXPERT), src),
                  pl.BlockSpec((None, None, D_MODEL, D_EXPERT), src),
                  pl.BlockSpec((None, None, D_EXPERT, D_MODEL), src)],
        out_specs=[pl.BlockSpec((None, None, D_MODEL, 2 * D_EXPERT), dst),
                   pl.BlockSpec((None, None, D_EXPERT, D_MODEL), dst)],
        out_shape=[jax.ShapeDtypeStruct((DEPTH, N_EXPERTS, D_MODEL, 2 * D_EXPERT), BF16),
                   jax.ShapeDtypeStruct((DEPTH, N_EXPERTS, D_EXPERT, D_MODEL), BF16)],
        compiler_params=_params(2),
        name="expert_cast",
    )(w_gate, w_up, w_down)


SLOT_ALIGN = 16
WIN = 128
ROW_BLOCK = 512
ZERO_TAIL = 640


def _slot_capacity(tt):
    most = TOP_K * tt + N_EXPERTS * (SLOT_ALIGN - 1)
    return -(-(most + ZERO_TAIL) // ROW_BLOCK) * ROW_BLOCK + WIN


def _moe_sparse_body(final_norm, n_ctx_tiles, cnt_ref, h_ref, gate_ref, rank_ref, wgu_ref, wd_ref,
                     sg_ref, su_ref, sd_ref, x_ref, mod_ref, fg_ref, *rest):
    o_refs = rest[:-4]
    pg_scr, xs_scr, acc_scr, offs = rest[-4:]
    i = pl.program_id(0)
    c = pl.program_id(1)
    is_ctx = i < n_ctx_tiles
    tt = h_ref.shape[0]
    eps_ = wgu_ref.shape[0]

    def count(r):
        return cnt_ref[i * N_EXPERTS + r]

    def n_windows(n):
        return (n + (WIN - 1)) // WIN

    @pl.when(c == 0)
    def _():
        def set_offset(r, run):
            offs[r] = run
            return run + ((count(r) + (SLOT_ALIGN - 1)) // SLOT_ALIGN) * SLOT_ALIGN

        total = lax.fori_loop(0, N_EXPERTS, set_offset, 0)
        offs[N_EXPERTS] = total

        slot_io = lax.broadcasted_iota(jnp.int32, (WIN, tt), 0)

        def build(r, carry):
            off = offs[r]
            rank_r = rank_ref[pl.ds(r, 1), :]
            gate_r = gate_ref[pl.ds(r, 1), :]
            live = gate_r > 0.0

            def window(b, carry2):
                mine = ((slot_io + b * WIN).astype(F32) == rank_r) & live
                rows = pl.ds(pl.multiple_of(off + b * WIN, SLOT_ALIGN), WIN)
                pg_scr[rows, :] = jnp.where(mine, gate_r, 0.0).astype(BF16)
                return carry2

            lax.fori_loop(0, n_windows(count(r)), window, 0)
            return carry

        lax.fori_loop(0, N_EXPERTS, build, 0)
        tail = pl.ds(pl.multiple_of(total, SLOT_ALIGN), ZERO_TAIL)
        pg_scr[tail, :] = jnp.zeros((ZERO_TAIL, tt), BF16)

        def gather(kb, carry):
            rows = pl.ds(pl.multiple_of(kb * ROW_BLOCK, ROW_BLOCK), ROW_BLOCK)
            one_hot = jnp.where(pg_scr[rows, :].astype(F32) > 0.0, 1.0, 0.0).astype(BF16)
            xs_scr[rows, :] = _dot(one_hot, h_ref[...]).astype(BF16)
            return carry

        lax.fori_loop(0, (total + (ROW_BLOCK - 1)) // ROW_BLOCK, gather, 0)

        h = h_ref[...]
        gu = _dot(h, jnp.concatenate([sg_ref[...].astype(BF16), su_ref[...].astype(BF16)], axis=1))
        act = gu[:, :D_EXPERT] * _sigmoid_t(gu[:, :D_EXPERT]) * gu[:, D_EXPERT:]
        acc_scr[...] = _dot(act.astype(BF16), sd_ref[...].astype(BF16))

    row_io = lax.broadcasted_iota(jnp.int32, (WIN, D_MODEL), 0)
    def expert_rows(p, x):
        gu = _dot(x, wgu_ref[p])
        act = gu[:, :D_EXPERT] * _sigmoid_t(gu[:, :D_EXPERT]) * gu[:, D_EXPERT:]
        return _dot(act.astype(BF16), wd_ref[p]).astype(BF16)

    offsets = [offs[c * eps_ + p] for p in range(eps_)]
    counts = [count(c * eps_ + p) for p in range(eps_)]
    wins = [pl.ds(pl.multiple_of(offsets[p], SLOT_ALIGN), WIN) for p in range(eps_)]
    xs = [xs_scr[wins[p], :] for p in range(eps_)]
    ys = [expert_rows(p, xs[p]) for p in range(eps_)]
    for p in range(eps_):
        xs_scr[wins[p], :] = jnp.where(row_io < counts[p], ys[p], xs[p])

    for p in range(eps_):
        def window(b, carry, p=p):
            rows = pl.ds(pl.multiple_of(offsets[p] + b * WIN, SLOT_ALIGN), WIN)
            x = xs_scr[rows, :]
            xs_scr[rows, :] = jnp.where(row_io + b * WIN < counts[p], expert_rows(p, x), x)
            return carry

        lax.fori_loop(1, n_windows(counts[p]), window, 0)

    @pl.when(c == pl.num_programs(1) - 1)
    def _():
        def combine(kb, carry):
            rows = pl.ds(pl.multiple_of(kb * ROW_BLOCK, ROW_BLOCK), ROW_BLOCK)
            acc_scr[...] += _dot_tn(pg_scr[rows, :], xs_scr[rows, :])
            return carry

        lax.fori_loop(0, (offs[N_EXPERTS] + (ROW_BLOCK - 1)) // ROW_BLOCK, combine, 0)
        x = x_ref[...] + mod_ref[5:6, :] * acc_scr[...]
        if not final_norm:
            o_refs[0][...] = x
        else:
            ms = jnp.mean(x * x, axis=-1, keepdims=True)
            y = x * lax.rsqrt(ms + EPS) * fg_ref[...]

            @pl.when(is_ctx)
            def _():
                o_refs[0][...] = y

            @pl.when(jnp.logical_not(is_ctx))
            def _():
                o_refs[1][...] = y


def _moe_sparse(h, gates_t, rank_t, counts, wgu, wd, ws_gate, ws_up, ws_down, layer, x, mod, final_g,
                final_norm, n_ctx_tok, lat_len):
    t = x.shape[0]
    tt = ROUTE_TILE
    eps_ = EXPERTS_PER_STEP
    n_ctx_tiles = n_ctx_tok // tt
    row = _mod_row(n_ctx_tiles, lat_len // tt)
    cap = _slot_capacity(tt)
    if final_norm:
        out_specs = [pl.BlockSpec((tt, D_MODEL), lambda i, c, cnt: (jnp.minimum(i, n_ctx_tiles - 1), 0)),
                     pl.BlockSpec((tt, D_MODEL), lambda i, c, cnt: (jnp.maximum(i - n_ctx_tiles, 0), 0))]
        out_shape = [jax.ShapeDtypeStruct((n_ctx_tok, D_MODEL), F32),
                     jax.ShapeDtypeStruct((t - n_ctx_tok, D_MODEL), F32)]
    else:
        out_specs = [pl.BlockSpec((tt, D_MODEL), lambda i, c, cnt: (i, 0))]
        out_shape = [jax.ShapeDtypeStruct((t, D_MODEL), F32)]
    grid_spec = pltpu.PrefetchScalarGridSpec(
        num_scalar_prefetch=1,
        grid=(t // tt, N_EXPERTS // eps_),
        in_specs=[
            pl.BlockSpec((tt, D_MODEL), lambda i, c, cnt: (i, 0)),
            pl.BlockSpec((N_EXPERTS, tt), lambda i, c, cnt: (0, i)),
            pl.BlockSpec((N_EXPERTS, tt), lambda i, c, cnt: (0, i)),
            pl.BlockSpec((None, eps_, D_MODEL, 2 * D_EXPERT), lambda i, c, cnt: (layer, c, 0, 0)),
            pl.BlockSpec((None, eps_, D_EXPERT, D_MODEL), lambda i, c, cnt: (layer, c, 0, 0)),
            pl.BlockSpec((None, D_MODEL, D_EXPERT), lambda i, c, cnt: (layer, 0, 0)),
            pl.BlockSpec((None, D_MODEL, D_EXPERT), lambda i, c, cnt: (layer, 0, 0)),
            pl.BlockSpec((None, D_EXPERT, D_MODEL), lambda i, c, cnt: (layer, 0, 0)),
            pl.BlockSpec((tt, D_MODEL), lambda i, c, cnt: (i, 0)),
            pl.BlockSpec((None, 6, D_MODEL), lambda i, c, cnt: (row(i), 0, 0)),
            pl.BlockSpec((1, D_MODEL), lambda i, c, cnt: (0, 0)),
        ],
        out_specs=out_specs,
        scratch_shapes=[pltpu.VMEM((cap, tt), BF16),
                        pltpu.VMEM((cap, D_MODEL), BF16), pltpu.VMEM((tt, D_MODEL), F32),
                        pltpu.SMEM((N_EXPERTS + 1,), jnp.int32)],
    )
    return pl.pallas_call(
        functools.partial(_moe_sparse_body, final_norm, n_ctx_tiles),
        grid_spec=grid_spec,
        out_shape=out_shape,
        compiler_params=_params(2),
        name="moe_sparse",
    )(counts, h, gates_t, rank_t, wgu, wd, ws_gate, ws_up, ws_down, x, mod, final_g)


def _block_diag_t(s):
    eye = jnp.eye(HG_HEADS, dtype=s.dtype)
    out = jnp.einsum('...hkv,hg->...hvgk', s, eye)
    return out.reshape(s.shape[:-3] + (HG_WIDTH, HG_WIDTH))


def kernel(x_prompt, x_sample, cache_k, cache_v, state_hgrn, c, c_ctx, w_ada, b_ada, norm_g, w_in,
           w_fourier, lambdas, attn_norm_g, lower_bounds, hg_norm_g, w_out, w_router, router_bias,
           w_gate, w_up, w_down, ws_gate, ws_up, ws_down, final_g):
    n_ctx, ctx_len, _ = x_prompt.shape
    n_lat, lat_len, _ = x_sample.shape
    n_ctx_tok = n_ctx * ctx_len
    past = cache_k.shape[2]

    xs = (x_prompt.reshape(n_ctx_tok, D_MODEL), x_sample.reshape(n_lat * lat_len, D_MODEL))

    c8 = jnp.zeros((8, D_MODEL), F32).at[0].set(c_ctx).at[1:1 + n_lat].set(c)
    mods = _ada_mods(c8, w_ada, b_ada).reshape(DEPTH, 8, 6, D_MODEL)

    cs = jnp.cumsum(jax.nn.softmax(lower_bounds.astype(F32), axis=0), axis=0)
    lbs = cs - cs[0:1]

    cos, sin = _rope_tables(lat_len)
    cache_k4 = cache_k.reshape(n_lat, DEPTH, past, DA_WIDTH)
    cache_v4 = cache_v.reshape(n_lat, DEPTH, past, DA_WIDTH)
    s0t = _block_diag_t(state_hgrn.astype(F32))

    qkv_end = FN_WIDTH + 3 * DA_WIDTH
    w_in_b = jnp.concatenate([w_in[:, :, FN_WIDTH:qkv_end], w_in[:, :, :FN_WIDTH], w_in[:, :, qkv_end:]],
                             axis=-1).astype(BF16)
    w_f_b = w_fourier.astype(BF16)
    w_out_b = w_out.astype(BF16)
    per = N_EXPERTS // N_GROUPS
    w_router_t = (jnp.swapaxes(w_router, 1, 2).reshape(DEPTH, N_GROUPS, per, D_MODEL)
                  .swapaxes(1, 2).reshape(DEPTH, N_EXPERTS, D_MODEL))
    bias_mm = router_bias.reshape(DEPTH, N_GROUPS, per).swapaxes(1, 2).reshape(DEPTH, N_EXPERTS, 1)
    w_down2 = w_down.reshape(DEPTH, N_EXPERTS // 2, 2 * D_EXPERT, D_MODEL)
    ng = jnp.tile(hg_norm_g, (1, HG_HEADS))

    caches = []
    new_s = []
    for l in range(DEPTH):
        lam_init = 0.8 - 0.6 * math.exp(-0.3 * l)
        mod = mods[l]
        last = l == DEPTH - 1
        proj_a, proj_h, k_l, v_l = _inproj(xs, mod, norm_g[l, 0:1], w_in_b[l], tuple(caches) if last else (),
                                           n_ctx, ctx_len, lat_len)
        caches = [k_l, v_l] if last else caches + [k_l, v_l]

        ag = attn_norm_g[l].reshape(1, DA_VDIM)
        fn_ctx, a_ctx, hg_ctx, st_ctx = _ctx_mixers(proj_a, proj_h, n_ctx, ctx_len, 2, w_f_b[l], lambdas[l], ag,
                                                    lam_init, lbs[l], ng[l:l + 1])
        fn_lat = _fourier(proj_a, n_ctx_tok, n_lat, lat_len, w_f_b[l])
        a_lat = _attn_lat(proj_a, n_ctx_tok, n_lat, lat_len, cache_k4, cache_v4, l, cos, sin, lambdas[l], ag,
                          lam_init)
        (hg_lat,) = _hgrn2(proj_h, n_ctx_tok, n_lat, lat_len, 1, lbs[l], ng[l:l + 1], s0t[:, l])
        new_s.append(jnp.swapaxes(st_ctx, -1, -2))

        x, h2, gates = _outproj((fn_ctx, fn_lat), (a_ctx, a_lat), (hg_ctx, hg_lat), xs, mod,
                                norm_g[l, 1:2], w_out_b[l], w_router_t[l], bias_mm[l], n_ctx_tok, lat_len)
        xs = _moe(h2, gates, w_gate, w_up, w_down2, ws_gate, ws_up, ws_down, l, x, mod,
                  final_g.reshape(1, D_MODEL), l == DEPTH - 1, n_ctx_tok, lat_len)

    y_prompt = xs[0].reshape(n_ctx, ctx_len, D_MODEL)
    y_sample = xs[1].reshape(n_lat, lat_len, D_MODEL)
    new_k = caches[0].reshape(n_ctx, DEPTH, ctx_len, DA_HEADS, DA_VDIM)
    new_v = caches[1].reshape(n_ctx, DEPTH, ctx_len, DA_HEADS, DA_VDIM)
    return (y_prompt, y_sample, new_k, new_v, jnp.stack(new_s, axis=1))
```

```python
import functools
import math

import numpy as np
import jax
import jax.numpy as jnp
from jax import lax
from jax.experimental import pallas as pl
from jax.experimental.pallas import tpu as pltpu

F32 = jnp.float32
BF16 = jnp.bfloat16

D_MODEL = 1024
DEPTH = 2
GRID_W = 64
FN_WIDTH = 256
FN_GROUP_DIM = 64
DA_WIDTH = 512
DA_HEADS = 4
DA_VDIM = 128
DA_HALF = 64
HG_WIDTH = 256
HG_HEADS = 4
HG_DK = 64
CHUNK = 64
ROPE_THETA = 10000.0
N_EXPERTS = 64
TOP_K = 8
N_GROUPS = 8
TOPK_GROUPS = 4
D_EXPERT = 128
ROUTED_SCALE = 2.5
EPS = 1e-6

COL_Q, COL_K, COL_V, COL_FN = 0, 512, 1024, 1536
PROJ_A_WIDTH = 1792
COL_HQ, COL_HI, COL_HF, COL_HB, COL_HG = 0, 256, 512, 768, 1024
PROJ_H_WIDTH = 1280

TOKEN_TILE = 512
MOE_TILE = 1024
EXPERTS_PER_STEP = 8
CHUNKS_PER_GROUP = 4
VMEM_LIMIT = 56 * 1024 * 1024


def _dot(a, b):
    return jnp.dot(a, b, preferred_element_type=F32)


def _dot_nt(a, b):
    return lax.dot_general(a, b, (((1,), (1,)), ((), ())), preferred_element_type=F32)


def _dot_tn(a, b):
    return lax.dot_general(a, b, (((0,), (0,)), ((), ())), preferred_element_type=F32)


def _split(x, n):
    parts = []
    r = x
    for i in range(n):
        p = r.astype(BF16)
        parts.append(p)
        if i + 1 < n:
            r = r - p.astype(F32)
    return parts


def _sigmoid(x):
    return 1.0 / (1.0 + jnp.exp(-x))


def _sigmoid_t(x):
    return 0.5 * jnp.tanh(0.5 * x) + 0.5


def _silu(x):
    return x * _sigmoid(x)


def _params(n_axes):
    return pltpu.CompilerParams(dimension_semantics=("arbitrary",) * n_axes,
                                vmem_limit_bytes=VMEM_LIMIT)


def _ada_body(c_ref, w_ref, b_ref, o_ref):
    a = _silu(c_ref[...])
    a_hi, a_lo = _split(a, 2)
    w_hi, w_lo = _split(w_ref[...], 2)
    o_ref[...] = _dot(a_hi, w_hi) + _dot(a_lo, w_hi) + _dot(a_hi, w_lo) + b_ref[...]


def _ada_mods(c8, w_ada, b_ada):
    tn = 1536
    return pl.pallas_call(
        _ada_body,
        grid=(DEPTH, 6 * D_MODEL // tn),
        in_specs=[
            pl.BlockSpec((8, D_MODEL), lambda l, j: (0, 0)),
            pl.BlockSpec((None, D_MODEL, tn), lambda l, j: (l, 0, j)),
            pl.BlockSpec((None, 1, tn), lambda l, j: (l, 0, j)),
        ],
        out_specs=pl.BlockSpec((None, 8, tn), lambda l, j: (l, 0, j)),
        out_shape=jax.ShapeDtypeStruct((DEPTH, 8, 6 * D_MODEL), F32),
        compiler_params=_params(2),
        name="ada_mods",
    )(c8, w_ada, b_ada.reshape(DEPTH, 1, 6 * D_MODEL))


def _modnorm(x, g, shift, scale):
    ms = jnp.mean(x * x, axis=-1, keepdims=True)
    return (x * lax.rsqrt(ms + EPS) * g) * (1.0 + scale) + shift


def _mod_row(n_ctx_tiles, tiles_per_latent):
    def f(i):
        return jnp.where(i < n_ctx_tiles, 0, 1 + (i - n_ctx_tiles) // tiles_per_latent)
    return f


def _row_specs(n_parts, tm, width, n_ctx_tiles):
    if n_parts == 1:
        return [pl.BlockSpec((tm, width), lambda i, *_: (i, 0))]
    return [pl.BlockSpec((tm, width), lambda i, *_: (jnp.minimum(i, n_ctx_tiles - 1), 0)),
            pl.BlockSpec((tm, width), lambda i, *_: (jnp.maximum(i - n_ctx_tiles, 0), 0))]


def _pick(is_ctx, refs):
    if len(refs) == 1:
        return refs[0][...]
    return jnp.where(is_ctx, refs[0][...], refs[1][...])


def _inproj_body(n_x, n_prev, n_ctx_tiles, ctx_len, *refs):
    x_refs = refs[:n_x]
    mod_ref, g_ref, w_ref = refs[n_x:n_x + 3]
    prev_refs = refs[n_x + 3:n_x + 3 + 2 * n_prev]
    oa_ref, oh_ref, k_ref, v_ref = refs[-4:]
    is_ctx = pl.program_id(0) < n_ctx_tiles
    x = _pick(is_ctx, x_refs)
    h = _modnorm(x, g_ref[...], mod_ref[0:1, :], mod_ref[1:2, :])
    proj = _dot(h.astype(BF16), w_ref[...])
    oa_ref[...] = proj[:, :PROJ_A_WIDTH].astype(BF16)
    oh_ref[...] = proj[:, PROJ_A_WIDTH:]

    @pl.when(is_ctx)
    def _():
        per_layer = ctx_len * DA_HEADS
        for s in range(proj.shape[0] // ctx_len):
            base = s * (n_prev + 1) * per_layer
            for l in range(n_prev):
                dst = slice(base + l * per_layer, base + (l + 1) * per_layer)
                src = slice(s * per_layer, (s + 1) * per_layer)
                k_ref[dst, :] = prev_refs[2 * l][src, :]
                v_ref[dst, :] = prev_refs[2 * l + 1][src, :]
            tok = slice(s * ctx_len, (s + 1) * ctx_len)
            for h in range(DA_HEADS):
                rows = pl.ds(base + n_prev * per_layer + h, ctx_len, stride=DA_HEADS)
                k_ref[rows, :] = proj[tok, COL_K + h * DA_VDIM:COL_K + (h + 1) * DA_VDIM]
                v_ref[rows, :] = proj[tok, COL_V + h * DA_VDIM:COL_V + (h + 1) * DA_VDIM]


def _inproj(xs, mod, g, w, prev_caches, n_ctx, ctx_len, lat_len):
    t = sum(x.shape[0] for x in xs)
    tm = TOKEN_TILE
    seqs = tm // ctx_len
    n_ctx_tiles = n_ctx // seqs
    n_prev = len(prev_caches) // 2
    row = _mod_row(n_ctx_tiles, lat_len // tm)
    tile_rows = seqs * ctx_len * DA_HEADS
    tile_block = lambda i: (jnp.minimum(i, n_ctx_tiles - 1), 0)
    cache_spec = pl.BlockSpec(((n_prev + 1) * tile_rows, DA_VDIM), tile_block)
    cache_shape = jax.ShapeDtypeStruct((n_ctx_tiles * (n_prev + 1) * tile_rows, DA_VDIM), F32)
    in_specs = _row_specs(len(xs), tm, D_MODEL, n_ctx_tiles) + [
        pl.BlockSpec((None, 6, D_MODEL), lambda i: (row(i), 0, 0)),
        pl.BlockSpec((1, D_MODEL), lambda i: (0, 0)),
        pl.BlockSpec((D_MODEL, PROJ_A_WIDTH + PROJ_H_WIDTH), lambda i: (0, 0)),
    ] + [pl.BlockSpec((tile_rows, DA_VDIM), tile_block)] * (2 * n_prev)
    return pl.pallas_call(
        functools.partial(_inproj_body, len(xs), n_prev, n_ctx_tiles, ctx_len),
        grid=(t // tm,),
        in_specs=in_specs,
        out_specs=[pl.BlockSpec((tm, PROJ_A_WIDTH), lambda i: (i, 0)),
                   pl.BlockSpec((tm, PROJ_H_WIDTH), lambda i: (i, 0)), cache_spec, cache_spec],
        out_shape=[jax.ShapeDtypeStruct((t, PROJ_A_WIDTH), BF16), jax.ShapeDtypeStruct((t, PROJ_H_WIDTH), F32),
                   cache_shape, cache_shape],
        compiler_params=_params(1),
        name="inproj",
    )(*xs, mod, g, w, *prev_caches)


def _fourier_rows(za, zb, cl, sl, w_f):
    y = _dot(cl, za) - _dot(sl, zb)
    return _dot(y.astype(BF16), w_f)


def _fourier_body(u_ref, cl_ref, sl_ref, cc_ref, sc_ref, w_ref, o_ref):
    z = u_ref[...].astype(BF16)
    za = _dot(z, cc_ref[...]).astype(BF16)
    zb = _dot(z, sc_ref[...]).astype(BF16)
    o_ref[...] = _fourier_rows(za, zb, cl_ref[...], sl_ref[...], w_ref[...]).astype(o_ref.dtype)


def _dft_tables(n, block):
    i = np.arange(n)
    prod = (i[:, None] % block) * (i[None, :] % block) % block
    ang = prod.astype(np.float64) * (2.0 * math.pi / block)
    same = (i[:, None] // block) == (i[None, :] // block)
    scale = 1.0 / math.sqrt(block)
    c = np.where(same, np.cos(ang) * scale, 0.0).astype(np.float32)
    s = np.where(same, np.sin(ang) * scale, 0.0).astype(np.float32)
    return jnp.asarray(c).astype(BF16), jnp.asarray(s).astype(BF16)


def _fourier(proj, row0, n_seq, seq_len, w_f):
    cl, sl = _dft_tables(seq_len, seq_len)
    cc, sc = _dft_tables(FN_WIDTH, FN_GROUP_DIM)
    blk0 = row0 // seq_len
    full = lambda shape: pl.BlockSpec(shape, lambda i: (0, 0))
    return pl.pallas_call(
        _fourier_body,
        grid=(n_seq,),
        in_specs=[
            pl.BlockSpec((seq_len, FN_WIDTH), lambda i: (blk0 + i, COL_FN // FN_WIDTH)),
            full((seq_len, seq_len)), full((seq_len, seq_len)),
            full((FN_WIDTH, FN_WIDTH)), full((FN_WIDTH, FN_WIDTH)), full((FN_WIDTH, FN_WIDTH)),
        ],
        out_specs=pl.BlockSpec((seq_len, FN_WIDTH), lambda i: (i, 0)),
        out_shape=jax.ShapeDtypeStruct((n_seq * seq_len, FN_WIDTH), BF16),
        compiler_params=_params(1),
        name=f"fourier_{seq_len}",
    )(proj, cl, sl, cc, sc, w_f)


def _lambda_full(lmb, lam_init):
    a = jnp.sum(lmb[0:1, :] * lmb[1:2, :], axis=-1, keepdims=True)
    b = jnp.sum(lmb[2:3, :] * lmb[3:4, :], axis=-1, keepdims=True)
    return jnp.exp(a) - jnp.exp(b) + lam_init


def _diff_head(q, ks, vs, lam, g, lam_init):
    vas = [jnp.concatenate([v.astype(BF16), jnp.ones(v.shape, BF16)], axis=1) for v in vs]
    outs = []
    for m in range(2):
        qm = q[:, m * DA_HALF:(m + 1) * DA_HALF].astype(BF16)
        parts = [_dot_nt(qm, k[:, m * DA_HALF:(m + 1) * DA_HALF].astype(BF16)) for k in ks]
        mx = parts[0].max(axis=-1, keepdims=True)
        for p in parts[1:]:
            mx = jnp.maximum(mx, p.max(axis=-1, keepdims=True))
        oa = _dot(jnp.exp(parts[0] - mx).astype(BF16), vas[0])
        for p, va in zip(parts[1:], vas[1:]):
            oa = oa + _dot(jnp.exp(p - mx).astype(BF16), va)
        outs.append(oa[:, :DA_VDIM] * (1.0 / oa[:, DA_VDIM:DA_VDIM + 1]))
    a = outs[0] - lam * outs[1]
    ms = jnp.mean(a * a, axis=-1, keepdims=True)
    return a * lax.rsqrt(ms + EPS) * g * (1.0 - lam_init)


def _rope(x, cos, sin):
    lane = lax.broadcasted_iota(jnp.int32, x.shape, 1)
    first = ((lane >> 4) & 1) == 0
    rot = jnp.where(first, -pltpu.roll(x, 128 - DA_HALF // 4, 1), pltpu.roll(x, DA_HALF // 4, 1))
    return x * cos + rot * sin


def _attn_lat_body(lam_init, q_ref, k_ref, v_ref, kc_ref, vc_ref, cq_ref, sq_ref, ck_ref, sk_ref,
                   lmb_ref, g_ref, o_ref):
    lam = _lambda_full(lmb_ref[...], lam_init)
    scale = DA_HALF ** -0.5
    for h in range(DA_HEADS):
        sl = slice(h * DA_VDIM, (h + 1) * DA_VDIM)
        q = _rope(q_ref[:, sl].astype(F32), cq_ref[...], sq_ref[...]) * scale
        k = _rope(k_ref[:, sl].astype(F32), ck_ref[...], sk_ref[...])
        o = _diff_head(q, [k, kc_ref[:, sl]], [v_ref[:, sl], vc_ref[:, sl]], lam, g_ref[...], lam_init)
        o_ref[:, sl] = o.astype(o_ref.dtype)


def _attn_lat(proj, row0, n_seq, seq_len, cache_k, cache_v, layer, cos, sin, lmb, g, lam_init):
    tq = 512
    nq = seq_len // tq
    past = cache_k.shape[2]
    qb0 = row0 // tq
    kb0 = row0 // seq_len
    cache_spec = pl.BlockSpec((None, None, past, DA_WIDTH), lambda b, j: (b, layer, 0, 0))
    kv_spec = lambda c: pl.BlockSpec((seq_len, DA_WIDTH), lambda b, j: (kb0 + b, c))
    return pl.pallas_call(
        functools.partial(_attn_lat_body, lam_init),
        grid=(n_seq, nq),
        in_specs=[
            pl.BlockSpec((tq, DA_WIDTH), lambda b, j: (qb0 + b * nq + j, COL_Q // DA_WIDTH)),
            kv_spec(COL_K // DA_WIDTH), kv_spec(COL_V // DA_WIDTH),
            cache_spec, cache_spec,
            pl.BlockSpec((tq, DA_VDIM), lambda b, j: (j, 0)),
            pl.BlockSpec((tq, DA_VDIM), lambda b, j: (j, 0)),
            pl.BlockSpec((seq_len, DA_VDIM), lambda b, j: (0, 0)),
            pl.BlockSpec((seq_len, DA_VDIM), lambda b, j: (0, 0)),
            pl.BlockSpec((4, DA_HALF), lambda b, j: (0, 0)),
            pl.BlockSpec((1, DA_VDIM), lambda b, j: (0, 0)),
        ],
        out_specs=pl.BlockSpec((tq, DA_WIDTH), lambda b, j: (b * nq + j, 0)),
        out_shape=jax.ShapeDtypeStruct((n_seq * seq_len, DA_WIDTH), BF16),
        compiler_params=_params(2),
        name="attn_lat",
    )(proj, proj, proj, cache_k, cache_v, cos, sin, cos, sin, lmb, g)


def _rope_tables(n_tokens):
    rows = n_tokens // GRID_W
    row = np.repeat(np.arange(rows, dtype=np.float64), GRID_W)
    col = np.tile(np.arange(GRID_W, dtype=np.float64), rows)
    axis_dim = DA_HALF // 2
    inv_freq = ROPE_THETA ** (-np.arange(0, axis_dim, 2, dtype=np.float64) / axis_dim)
    ang_r = row[:, None] * inv_freq[None, :]
    ang_c = col[:, None] * inv_freq[None, :]
    ang = np.concatenate([ang_r, ang_r, ang_c, ang_c] * 2, axis=-1)
    return jnp.asarray(np.cos(ang).astype(np.float32)), jnp.asarray(np.sin(ang).astype(np.float32))


def _hgrn_tables():
    gl = CHUNKS_PER_GROUP * CHUNK
    t = np.arange(gl)
    same = (t[:, None] // CHUNK) == (t[None, :] // CHUNK)
    fwd = same & (t[None, :] <= t[:, None])
    bwd = same & (t[None, :] >= t[:, None])
    f = np.arange(HG_WIDTH)
    bd = (f[:, None] // HG_DK) == (f[None, :] // HG_DK)
    tri = jnp.asarray(np.stack([fwd, bwd]).astype(np.float32)).astype(BF16)
    causal = jnp.asarray(np.stack([fwd, bwd]).astype(np.float32))
    return tri, causal, jnp.asarray(bd.astype(np.float32))


def _hgrn2_body(n_seq, groups_per_seq, has_s0, *refs):
    (hq_ref, hi_ref, hf_ref, hb_ref, hg_ref, lb_ref, ng_ref, tri_ref, causal_ref, bd_ref) = refs[:10]
    if has_s0:
        s0_ref, o_ref, oi_scr, qe_scr, u_scr, st_scr, dec_scr = refs[10:]
        so_ref = None
    else:
        o_ref, so_ref, oi_scr, qe_scr, u_scr, st_scr, dec_scr = refs[10:]
        s0_ref = None
    c = CHUNK
    w = HG_WIDTH
    gc = CHUNKS_PER_GROUP
    gl = gc * c
    n_groups = n_seq * groups_per_seq
    chunks_per_seq = groups_per_seq * gc
    lane = lax.broadcasted_iota(jnp.int32, (1, w), 1)
    dk_bits = HG_DK.bit_length() - 1
    head_masks_b = [((lane >> dk_bits) == h).astype(BF16) for h in range(HG_HEADS)]
    bd = bd_ref[...] > 0.5
    edge_rows = (c - 1, 0)
    fp_refs = (hf_ref, hb_ref)

    def per_chunk_row(x, row):
        return jnp.concatenate([jnp.broadcast_to(x[k * c + row:k * c + row + 1, :], (c, w)) for k in range(gc)],
                               axis=0)

    def group_terms(gi):
        rows = pl.ds(gi * gl if isinstance(gi, int) else pl.multiple_of(gi * gl, gl), gl)
        hq = hq_ref[rows, :]
        q = hq * _sigmoid_t(hq) * (HG_DK ** -0.5)
        v = hi_ref[rows, :].astype(BF16)
        v_stack = jnp.concatenate([v * hm for hm in head_masks_b], axis=0)
        for d in range(2):
            lb = lb_ref[d:d + 1, :]
            fp = fp_refs[d][rows, :]
            lsig = jnp.minimum(fp, 0.0) - jnp.log(1.0 + jnp.exp(-jnp.abs(fp)))
            la = jnp.log(lb)
            lbb = jnp.log(1.0 - lb) + lsig
            logf = jnp.maximum(la, lbb) + jnp.log(1.0 + jnp.exp(-jnp.abs(la - lbb)))
            kk = (1.0 - lb) * _sigmoid_t(-fp)
            g_hi, g_lo = _split(logf, 2)
            cum = _dot(tri_ref[d], g_hi) + _dot(tri_ref[d], g_lo)
            total = per_chunk_row(cum, edge_rows[d])
            ref = per_chunk_row(cum, c // 2)
            qc = (q * jnp.exp(cum - ref)).astype(BF16)
            kc = (kk * jnp.exp(ref - cum)).astype(BF16)
            ke = (kk * jnp.exp(total - cum)).astype(BF16)
            qe_scr[d, rows, :] = (q * jnp.exp(cum)).astype(BF16)
            kc_stack = jnp.concatenate([kc * hm for hm in head_masks_b], axis=0)
            a = _dot_nt(qc, kc_stack)
            keep = causal_ref[d] > 0.5
            a = jnp.concatenate([jnp.where(keep, a[:, h * gl:(h + 1) * gl], 0.0) for h in range(HG_HEADS)],
                                axis=1).astype(BF16)
            oi_scr[d, rows, :] = _dot(a, v_stack)
            for k in range(gc):
                ck = slice(k * c, (k + 1) * c)
                i = gi * gc + k
                dec_scr[d, pl.ds(i, 1), :] = jnp.exp(cum[k * c + edge_rows[d]:k * c + edge_rows[d] + 1, :])
                u_scr[d, i] = jnp.where(bd, _dot_tn(v[ck, :], ke[ck, :]), 0.0)

    if n_groups <= 2:
        for gi in range(n_groups):
            group_terms(gi)
    else:
        def terms_step(gi, carry):
            group_terms(gi)
            return carry

        lax.fori_loop(0, n_groups, terms_step, 0)

    slab = 32
    for s in range(n_seq):
        first = s * chunks_per_seq
        for d in range(2):
            order = range(chunks_per_seq) if d == 0 else range(chunks_per_seq - 1, -1, -1)
            for r in range(w // slab):
                rs = slice(r * slab, (r + 1) * slab)
                st = s0_ref[s, d, rs, :] if has_s0 else jnp.zeros((slab, w), F32)
                for j in order:
                    i = first + j
                    st_scr[d, i, rs, :] = st.astype(BF16)
                    st = st * dec_scr[d, i:i + 1, :] + u_scr[d, i, rs, :]
                if so_ref is not None:
                    h = (r * slab) // HG_DK
                    off = (r * slab) % HG_DK
                    so_ref[s, d, h, off:off + slab, :] = st[:, h * HG_DK:(h + 1) * HG_DK]

    ones_bd = bd.astype(BF16)

    def finish(gi):
        rows = pl.ds(gi * gl if isinstance(gi, int) else pl.multiple_of(gi * gl, gl), gl)
        inter = []
        for k in range(gc):
            i = gi * gc + k
            ck = pl.ds(gi * gl + k * c if isinstance(gi, int) else pl.multiple_of(gi * gl + k * c, c), c)
            inter.append(_dot_nt(qe_scr[0, ck, :], st_scr[0, i]) + _dot_nt(qe_scr[1, ck, :], st_scr[1, i]))
        o = oi_scr[0, rows, :] + oi_scr[1, rows, :] + jnp.concatenate(inter, axis=0)
        sq = _split(o * o, 2)
        ms = (_dot(sq[0], ones_bd) + _dot(sq[1], ones_bd)) * (1.0 / HG_DK)
        y = o * lax.rsqrt(ms + EPS) * ng_ref[...]
        hg = hg_ref[rows, :]
        o_ref[rows, :] = (y * hg * _sigmoid_t(hg)).astype(o_ref.dtype)

    if n_groups <= 2:
        for gi in range(n_groups):
            finish(gi)
    else:
        def finish_step(gi, carry):
            finish(gi)
            return carry

        lax.fori_loop(0, n_groups, finish_step, 0)


def _hgrn_scratch(rows):
    n_chunks = rows // CHUNK
    return [pltpu.VMEM((2, rows, HG_WIDTH), F32),
            pltpu.VMEM((2, rows, HG_WIDTH), BF16),
            pltpu.VMEM((2, n_chunks, HG_WIDTH, HG_WIDTH), F32),
            pltpu.VMEM((2, n_chunks, HG_WIDTH, HG_WIDTH), BF16),
            pltpu.VMEM((2, max(n_chunks, 8), HG_WIDTH), F32)]


def _hgrn2(proj_h, row0, n_seq, seq_len, seqs_per_step, lb, ng, s0t):
    gl = CHUNKS_PER_GROUP * CHUNK
    groups_per_seq = seq_len // gl
    rows = seqs_per_step * seq_len
    blk0 = row0 // rows
    tri, causal, bd = _hgrn_tables()
    col = lambda c0: pl.BlockSpec((rows, HG_WIDTH), lambda i: (blk0 + i, c0 // HG_WIDTH))
    const = lambda shape: pl.BlockSpec(shape, lambda i: (0,) * len(shape))
    in_specs = [col(COL_HQ), col(COL_HI), col(COL_HF), col(COL_HB), col(COL_HG),
                const((2, HG_WIDTH)), const((1, HG_WIDTH)),
                const((2, gl, gl)), const((2, gl, gl)), const((HG_WIDTH, HG_WIDTH))]
    args = [proj_h] * 5 + [lb, ng, tri, causal, bd]
    out_specs = [pl.BlockSpec((rows, HG_WIDTH), lambda i: (i, 0))]
    out_shape = [jax.ShapeDtypeStruct((n_seq * seq_len, HG_WIDTH), BF16)]
    if s0t is not None:
        in_specs.append(pl.BlockSpec((seqs_per_step, 2, HG_WIDTH, HG_WIDTH), lambda i: (i, 0, 0, 0)))
        args.append(s0t)
    else:
        out_specs.append(pl.BlockSpec((seqs_per_step, 2, HG_HEADS, HG_DK, HG_DK), lambda i: (i, 0, 0, 0, 0)))
        out_shape.append(jax.ShapeDtypeStruct((n_seq, 2, HG_HEADS, HG_DK, HG_DK), F32))
    return pl.pallas_call(
        functools.partial(_hgrn2_body, seqs_per_step, groups_per_seq, s0t is not None),
        grid=(n_seq // seqs_per_step,),
        in_specs=in_specs,
        out_specs=out_specs,
        out_shape=out_shape,
        scratch_shapes=_hgrn_scratch(rows),
        compiler_params=_params(1),
        name=f"hgrn_{seq_len}",
    )(*args)


def _ctx_mixers_body(n_seq, seq_len, lam_init, *refs):
    (q_ref, k_ref, v_ref, u_ref, hq_ref, hi_ref, hf_ref, hb_ref, hg_ref,
     cl_ref, sl_ref, cc_ref, sc_ref, wf_ref, lmb_ref, ag_ref,
     lb_ref, ng_ref, tri_ref, causal_ref, bd_ref,
     fn_ref, a_ref, ho_ref, so_ref) = refs[:25]
    scratch = refs[25:]

    z = u_ref[...].astype(BF16)
    za = _dot(z, cc_ref[...]).astype(BF16)
    zb = _dot(z, sc_ref[...]).astype(BF16)
    for s in range(n_seq):
        rs = slice(s * seq_len, (s + 1) * seq_len)
        fn_ref[rs, :] = _fourier_rows(za[rs, :], zb[rs, :], cl_ref[...], sl_ref[...],
                                      wf_ref[...]).astype(fn_ref.dtype)

    lam = _lambda_full(lmb_ref[...], lam_init)
    scale = DA_HALF ** -0.5
    for s in range(n_seq):
        rs = slice(s * seq_len, (s + 1) * seq_len)
        for h in range(DA_HEADS):
            sl = slice(h * DA_VDIM, (h + 1) * DA_VDIM)
            o = _diff_head(q_ref[rs, sl] * scale, [k_ref[rs, sl]], [v_ref[rs, sl]], lam, ag_ref[...], lam_init)
            a_ref[rs, sl] = o.astype(a_ref.dtype)

    _hgrn2_body(n_seq, seq_len // (CHUNKS_PER_GROUP * CHUNK), False,
                hq_ref, hi_ref, hf_ref, hb_ref, hg_ref, lb_ref, ng_ref, tri_ref, causal_ref, bd_ref,
                ho_ref, so_ref, *scratch)


def _ctx_mixers(proj_a, proj_h, n_seq, seq_len, seqs_per_step, w_f, lmb, ag, lam_init, lb, ng):
    rows = seqs_per_step * seq_len
    gl = CHUNKS_PER_GROUP * CHUNK
    cl, sl = _dft_tables(seq_len, seq_len)
    cc, sc = _dft_tables(FN_WIDTH, FN_GROUP_DIM)
    tri, causal, bd = _hgrn_tables()
    col = lambda c0, width: pl.BlockSpec((rows, width), lambda i: (i, c0 // width))
    const = lambda shape: pl.BlockSpec(shape, lambda i: (0,) * len(shape))
    out_rows = lambda width: pl.BlockSpec((rows, width), lambda i: (i, 0))
    n_tok = n_seq * seq_len
    return pl.pallas_call(
        functools.partial(_ctx_mixers_body, seqs_per_step, seq_len, lam_init),
        grid=(n_seq // seqs_per_step,),
        in_specs=[col(COL_Q, DA_WIDTH), col(COL_K, DA_WIDTH), col(COL_V, DA_WIDTH), col(COL_FN, FN_WIDTH),
                  col(COL_HQ, HG_WIDTH), col(COL_HI, HG_WIDTH), col(COL_HF, HG_WIDTH), col(COL_HB, HG_WIDTH),
                  col(COL_HG, HG_WIDTH),
                  const((seq_len, seq_len)), const((seq_len, seq_len)),
                  const((FN_WIDTH, FN_WIDTH)), const((FN_WIDTH, FN_WIDTH)), const((FN_WIDTH, FN_WIDTH)),
                  const((4, DA_HALF)), const((1, DA_VDIM)),
                  const((2, HG_WIDTH)), const((1, HG_WIDTH)),
                  const((2, gl, gl)), const((2, gl, gl)), const((HG_WIDTH, HG_WIDTH))],
        out_specs=[out_rows(FN_WIDTH), out_rows(DA_WIDTH), out_rows(HG_WIDTH),
                   pl.BlockSpec((seqs_per_step, 2, HG_HEADS, HG_DK, HG_DK), lambda i: (i, 0, 0, 0, 0))],
        out_shape=[jax.ShapeDtypeStruct((n_tok, FN_WIDTH), BF16),
                   jax.ShapeDtypeStruct((n_tok, DA_WIDTH), BF16),
                   jax.ShapeDtypeStruct((n_tok, HG_WIDTH), BF16),
                   jax.ShapeDtypeStruct((n_seq, 2, HG_HEADS, HG_DK, HG_DK), F32)],
        scratch_shapes=_hgrn_scratch(rows),
        compiler_params=_params(1),
        name="ctx_mixers",
    )(*([proj_a] * 4 + [proj_h] * 5), cl, sl, cc, sc, w_f, lmb, ag, lb, ng, tri, causal, bd)


def _route_gates(lt, bias):
    per = N_EXPERTS // N_GROUPS
    tt = lt.shape[1]
    neg = -jnp.inf
    assert per == N_GROUPS == 8
    gi = lax.broadcasted_iota(jnp.int32, (N_GROUPS, tt), 0).astype(F32)
    s_j, b_j = [], []
    for j in range(per):
        s = _sigmoid(lt[j * N_GROUPS:(j + 1) * N_GROUPS, :])
        s_j.append(s)
        b_j.append(s + bias[j * N_GROUPS:(j + 1) * N_GROUPS, :])
    m1 = functools.reduce(jnp.maximum, b_j)
    i1 = functools.reduce(jnp.minimum, [jnp.where(b_j[j] == m1, float(j), float(per)) for j in range(per)])
    m2 = functools.reduce(jnp.maximum, [jnp.where(i1 == float(j), neg, b_j[j]) for j in range(per)])
    gs = m1 + m2
    gsel = jnp.zeros((N_GROUPS, tt), jnp.bool_)
    for _ in range(TOPK_GROUPS):
        m = gs.max(axis=0, keepdims=True)
        idx = jnp.where(gs == m, gi, float(N_GROUPS)).min(axis=0, keepdims=True)
        hit = gi == idx
        gsel = gsel | hit
        gs = jnp.where(hit, neg, gs)
    x_j = [jnp.where(gsel, b_j[j], neg) for j in range(per)]
    e_j = [gi * per + j for j in range(per)]
    sel_j = [jnp.zeros((N_GROUPS, tt), jnp.bool_) for _ in range(per)]
    for _ in range(TOP_K):
        m = functools.reduce(jnp.maximum, x_j).max(axis=0, keepdims=True)
        idx = functools.reduce(jnp.minimum, [jnp.where(x_j[j] == m, e_j[j], float(N_EXPERTS))
                                             for j in range(per)]).min(axis=0, keepdims=True)
        for j in range(per):
            hit = e_j[j] == idx
            sel_j[j] = sel_j[j] | hit
            x_j[j] = jnp.where(hit, neg, x_j[j])
    w_j = [jnp.where(sel_j[j], s_j[j], 0.0) for j in range(per)]
    denom = functools.reduce(lambda a, b: a + b, w_j).sum(axis=0, keepdims=True)
    gates_t = jnp.concatenate([w / denom * ROUTED_SCALE for w in w_j], axis=0)
    r_io = lax.broadcasted_iota(jnp.int32, (N_EXPERTS, N_EXPERTS), 0)
    e_io = lax.broadcasted_iota(jnp.int32, (N_EXPERTS, N_EXPERTS), 1)
    eye = (e_io == (r_io & (N_GROUPS - 1)) * per + (r_io >> 3)).astype(BF16)
    p = _split(gates_t, 3)
    return _dot_tn(p[0], eye) + _dot_tn(p[1], eye) + _dot_tn(p[2], eye)


def _outproj_body(n_x, n_ctx_tiles, *refs):
    fn_refs, a_refs, hg_refs = refs[0:2], refs[2:4], refs[4:6]
    x_refs = refs[6:6 + n_x]
    mod_ref, g_ref, w_ref, wr_ref, rb_ref, xo_ref, h_ref, gate_ref = refs[6 + n_x:]
    is_ctx = pl.program_id(0) < n_ctx_tiles
    mix = (_dot(_pick(is_ctx, fn_refs), w_ref[0:FN_WIDTH, :])
           + _dot(_pick(is_ctx, a_refs), w_ref[FN_WIDTH:FN_WIDTH + DA_WIDTH, :])
           + _dot(_pick(is_ctx, hg_refs), w_ref[FN_WIDTH + DA_WIDTH:, :]))
    x = _pick(is_ctx, x_refs) + mod_ref[2:3, :] * mix
    xo_ref[...] = x
    h = _modnorm(x, g_ref[...], mod_ref[3:4, :], mod_ref[4:5, :])
    h_ref[...] = h.astype(BF16)
    h_hi, h_lo = _split(h, 2)
    w_hi, w_lo = _split(wr_ref[...], 2)
    logits_t = _dot_nt(w_hi, h_hi) + _dot_nt(w_lo, h_hi) + _dot_nt(w_hi, h_lo)
    gate_ref[...] = _route_gates(logits_t, rb_ref[...])


def _outproj(fn, a, hg, xs, mod, g, w_out, w_router_t, router_bias, n_ctx_tok, lat_len):
    t = sum(x.shape[0] for x in xs)
    tm = TOKEN_TILE
    n_ctx_tiles = n_ctx_tok // tm
    row = _mod_row(n_ctx_tiles, lat_len // tm)
    rows = lambda width: pl.BlockSpec((tm, width), lambda i: (i, 0))
    parts = lambda n, width: _row_specs(n, tm, width, n_ctx_tiles)
    return pl.pallas_call(
        functools.partial(_outproj_body, len(xs), n_ctx_tiles),
        grid=(t // tm,),
        in_specs=parts(2, FN_WIDTH) + parts(2, DA_WIDTH) + parts(2, HG_WIDTH) + parts(len(xs), D_MODEL) + [
            pl.BlockSpec((None, 6, D_MODEL), lambda i: (row(i), 0, 0)),
            pl.BlockSpec((1, D_MODEL), lambda i: (0, 0)),
            pl.BlockSpec((D_MODEL, D_MODEL), lambda i: (0, 0)),
            pl.BlockSpec((N_EXPERTS, D_MODEL), lambda i: (0, 0)),
            pl.BlockSpec((N_EXPERTS, 1), lambda i: (0, 0)),
        ],
        out_specs=[rows(D_MODEL), rows(D_MODEL), rows(N_EXPERTS)],
        out_shape=[jax.ShapeDtypeStruct((t, D_MODEL), F32),
                   jax.ShapeDtypeStruct((t, D_MODEL), BF16),
                   jax.ShapeDtypeStruct((t, N_EXPERTS), F32)],
        compiler_params=_params(1),
        name="outproj",
    )(*fn, *a, *hg, *xs, mod, g, w_out, w_router_t, router_bias)


def _moe_body(final_norm, n_ctx_tiles, h_ref, gate_ref, wg_ref, wu_ref, wd_ref, sg_ref, su_ref, sd_ref,
              x_ref, mod_ref, fg_ref, *out_and_scratch):
    acc_ref = out_and_scratch[-1]
    o_refs = out_and_scratch[:-1]
    is_ctx = pl.program_id(0) < n_ctx_tiles
    j = pl.program_id(1)
    h = h_ref[...]

    def act_of(wg, wu, gate):
        gu = _dot(h, jnp.concatenate([wg.astype(BF16), wu.astype(BF16)], axis=1))
        g = gu[:, :D_EXPERT]
        a = g * _sigmoid_t(g) * gu[:, D_EXPERT:]
        return a if gate is None else a * gate

    @pl.when(j == 0)
    def _():
        acc_ref[...] = _dot(act_of(sg_ref[...], su_ref[...], None).astype(BF16), sd_ref[...].astype(BF16))

    gates = gate_ref[...]
    expert_of_lane = lax.broadcasted_iota(jnp.int32, gates.shape, 1)

    def gate_col(p):
        e = j * EXPERTS_PER_STEP + p
        return jnp.sum(jnp.where(expert_of_lane == e, gates, 0.0), axis=1, keepdims=True)

    acts = [act_of(wg_ref[p], wu_ref[p], gate_col(p)).astype(BF16) for p in range(EXPERTS_PER_STEP)]
    wd = wd_ref[...].astype(BF16).reshape(EXPERTS_PER_STEP * D_EXPERT, D_MODEL)
    acc_ref[...] += _dot(jnp.concatenate(acts, axis=1), wd)

    @pl.when(j == pl.num_programs(1) - 1)
    def _():
        x = x_ref[...] + mod_ref[5:6, :] * acc_ref[...]
        if not final_norm:
            o_refs[0][...] = x
        else:
            ms = jnp.mean(x * x, axis=-1, keepdims=True)
            y = x * lax.rsqrt(ms + EPS) * fg_ref[...]

            @pl.when(is_ctx)
            def _():
                o_refs[0][...] = y

            @pl.when(jnp.logical_not(is_ctx))
            def _():
                o_refs[1][...] = y


def _moe(h, gates, w_gate, w_up, w_down2, ws_gate, ws_up, ws_down, layer, x, mod, final_g, final_norm,
         n_ctx_tok, lat_len):
    t = x.shape[0]
    tm = MOE_TILE
    eps_ = EXPERTS_PER_STEP
    n_ctx_tiles = n_ctx_tok // tm
    row = _mod_row(n_ctx_tiles, lat_len // tm)
    once = pl.Buffered(1)
    if final_norm:
        last_ctx = n_ctx_tiles - 1
        out_specs = [pl.BlockSpec((tm, D_MODEL), lambda i, j: (jnp.minimum(i, last_ctx), 0), pipeline_mode=once),
                     pl.BlockSpec((tm, D_MODEL), lambda i, j: (jnp.maximum(i - n_ctx_tiles, 0), 0),
                                  pipeline_mode=once)]
        out_shape = [jax.ShapeDtypeStruct((n_ctx_tok, D_MODEL), F32),
                     jax.ShapeDtypeStruct((t - n_ctx_tok, D_MODEL), F32)]
    else:
        out_specs = [pl.BlockSpec((tm, D_MODEL), lambda i, j: (i, 0), pipeline_mode=once)]
        out_shape = [jax.ShapeDtypeStruct((t, D_MODEL), F32)]
    return pl.pallas_call(
        functools.partial(_moe_body, final_norm, n_ctx_tiles),
        grid=(t // tm, N_EXPERTS // eps_),
        in_specs=[
            pl.BlockSpec((tm, D_MODEL), lambda i, j: (i, 0)),
            pl.BlockSpec((tm, N_EXPERTS), lambda i, j: (i, 0)),
            pl.BlockSpec((None, eps_, D_MODEL, D_EXPERT), lambda i, j: (layer, j, 0, 0)),
            pl.BlockSpec((None, eps_, D_MODEL, D_EXPERT), lambda i, j: (layer, j, 0, 0)),
            pl.BlockSpec((None, eps_ // 2, 2 * D_EXPERT, D_MODEL), lambda i, j: (layer, j, 0, 0)),
            pl.BlockSpec((None, D_MODEL, D_EXPERT), lambda i, j: (layer, 0, 0), pipeline_mode=once),
            pl.BlockSpec((None, D_MODEL, D_EXPERT), lambda i, j: (layer, 0, 0), pipeline_mode=once),
            pl.BlockSpec((None, D_EXPERT, D_MODEL), lambda i, j: (layer, 0, 0), pipeline_mode=once),
            pl.BlockSpec((tm, D_MODEL), lambda i, j: (i, 0), pipeline_mode=once),
            pl.BlockSpec((None, 6, D_MODEL), lambda i, j: (row(i), 0, 0)),
            pl.BlockSpec((1, D_MODEL), lambda i, j: (0, 0)),
        ],
        out_specs=out_specs,
        out_shape=out_shape,
        scratch_shapes=[pltpu.VMEM((tm, D_MODEL), F32)],
        compiler_params=_params(2),
        name="moe",
    )(h, gates, w_gate, w_up, w_down2, ws_gate, ws_up, ws_down, x, mod, final_g)


def _block_diag_t(s):
    eye = jnp.eye(HG_HEADS, dtype=s.dtype)
    out = jnp.einsum('...hkv,hg->...hvgk', s, eye)
    return out.reshape(s.shape[:-3] + (HG_WIDTH, HG_WIDTH))


def kernel(x_prompt, x_sample, cache_k, cache_v, state_hgrn, c, c_ctx, w_ada, b_ada, norm_g, w_in,
           w_fourier, lambdas, attn_norm_g, lower_bounds, hg_norm_g, w_out, w_router, router_bias,
           w_gate, w_up, w_down, ws_gate, ws_up, ws_down, final_g):
    n_ctx, ctx_len, _ = x_prompt.shape
    n_lat, lat_len, _ = x_sample.shape
    n_ctx_tok = n_ctx * ctx_len
    past = cache_k.shape[2]

    xs = (x_prompt.reshape(n_ctx_tok, D_MODEL), x_sample.reshape(n_lat * lat_len, D_MODEL))

    c8 = jnp.zeros((8, D_MODEL), F32).at[0].set(c_ctx).at[1:1 + n_lat].set(c)
    mods = _ada_mods(c8, w_ada, b_ada).reshape(DEPTH, 8, 6, D_MODEL)

    cs = jnp.cumsum(jax.nn.softmax(lower_bounds.astype(F32), axis=0), axis=0)
    lbs = cs - cs[0:1]

    cos, sin = _rope_tables(lat_len)
    cache_k4 = cache_k.reshape(n_lat, DEPTH, past, DA_WIDTH)
    cache_v4 = cache_v.reshape(n_lat, DEPTH, past, DA_WIDTH)
    s0t = _block_diag_t(state_hgrn.astype(F32))

    qkv_end = FN_WIDTH + 3 * DA_WIDTH
    w_in_b = jnp.concatenate([w_in[:, :, FN_WIDTH:qkv_end], w_in[:, :, :FN_WIDTH], w_in[:, :, qkv_end:]],
                             axis=-1).astype(BF16)
    w_f_b = w_fourier.astype(BF16)
    w_out_b = w_out.astype(BF16)
    per = N_EXPERTS // N_GROUPS
    w_router_t = (jnp.swapaxes(w_router, 1, 2).reshape(DEPTH, N_GROUPS, per, D_MODEL)
                  .swapaxes(1, 2).reshape(DEPTH, N_EXPERTS, D_MODEL))
    bias_mm = router_bias.reshape(DEPTH, N_GROUPS, per).swapaxes(1, 2).reshape(DEPTH, N_EXPERTS, 1)
    w_down2 = w_down.reshape(DEPTH, N_EXPERTS // 2, 2 * D_EXPERT, D_MODEL)
    ng = jnp.tile(hg_norm_g, (1, HG_HEADS))

    caches = []
    new_s = []
    for l in range(DEPTH):
        lam_init = 0.8 - 0.6 * math.exp(-0.3 * l)
        mod = mods[l]
        last = l == DEPTH - 1
        proj_a, proj_h, k_l, v_l = _inproj(xs, mod, norm_g[l, 0:1], w_in_b[l], tuple(caches) if last else (),
                                           n_ctx, ctx_len, lat_len)
        caches = [k_l, v_l] if last else caches + [k_l, v_l]

        ag = attn_norm_g[l].reshape(1, DA_VDIM)
        fn_ctx, a_ctx, hg_ctx, st_ctx = _ctx_mixers(proj_a, proj_h, n_ctx, ctx_len, 2, w_f_b[l], lambdas[l], ag,
                                                    lam_init, lbs[l], ng[l:l + 1])
        fn_lat = _fourier(proj_a, n_ctx_tok, n_lat, lat_len, w_f_b[l])
        a_lat = _attn_lat(proj_a, n_ctx_tok, n_lat, lat_len, cache_k4, cache_v4, l, cos, sin, lambdas[l], ag,
                          lam_init)
        (hg_lat,) = _hgrn2(proj_h, n_ctx_tok, n_lat, lat_len, 1, lbs[l], ng[l:l + 1], s0t[:, l])
        new_s.append(jnp.swapaxes(st_ctx, -1, -2))

        x, h2, gates = _outproj((fn_ctx, fn_lat), (a_ctx, a_lat), (hg_ctx, hg_lat), xs, mod,
                                norm_g[l, 1:2], w_out_b[l], w_router_t[l], bias_mm[l], n_ctx_tok, lat_len)
        xs = _moe(h2, gates, w_gate, w_up, w_down2, ws_gate, ws_up, ws_down, l, x, mod,
                  final_g.reshape(1, D_MODEL), l == DEPTH - 1, n_ctx_tok, lat_len)

    y_prompt = xs[0].reshape(n_ctx, ctx_len, D_MODEL)
    y_sample = xs[1].reshape(n_lat, lat_len, D_MODEL)
    new_k = caches[0].reshape(n_ctx, DEPTH, ctx_len, DA_HEADS, DA_VDIM)
    new_v = caches[1].reshape(n_ctx, DEPTH, ctx_len, DA_HEADS, DA_VDIM)
    return (y_prompt, y_sample, new_k, new_v, jnp.stack(new_s, axis=1))
```

```python
import functools
import math

import numpy as np
import jax
import jax.numpy as jnp
from jax import lax
from jax.experimental import pallas as pl
from jax.experimental.pallas import tpu as pltpu

F32 = jnp.float32
BF16 = jnp.bfloat16

D_MODEL = 1024
DEPTH = 2
GRID_W = 64
FN_WIDTH = 256
FN_GROUP_DIM = 64
DA_WIDTH = 512
DA_HEADS = 4
DA_VDIM = 128
DA_HALF = 64
HG_WIDTH = 256
HG_HEADS = 4
HG_DK = 64
CHUNK = 64
ROPE_THETA = 10000.0
N_EXPERTS = 64
TOP_K = 8
N_GROUPS = 8
TOPK_GROUPS = 4
D_EXPERT = 128
ROUTED_SCALE = 2.5
EPS = 1e-6

COL_Q, COL_K, COL_V, COL_FN = 0, 512, 1024, 1536
PROJ_A_WIDTH = 1792
COL_HQ, COL_HI, COL_HF, COL_HB, COL_HG = 0, 256, 512, 768, 1024
PROJ_H_WIDTH = 1280

TOKEN_TILE = 512
MOE_TILE = 1024
EXPERTS_PER_STEP = 8
CHUNKS_PER_GROUP = 4
VMEM_LIMIT = 56 * 1024 * 1024


def _dot(a, b):
    return jnp.dot(a, b, preferred_element_type=F32)


def _dot_nt(a, b):
    return lax.dot_general(a, b, (((1,), (1,)), ((), ())), preferred_element_type=F32)


def _dot_tn(a, b):
    return lax.dot_general(a, b, (((0,), (0,)), ((), ())), preferred_element_type=F32)


def _split(x, n):
    parts = []
    r = x
    for i in range(n):
        p = r.astype(BF16)
        parts.append(p)
        if i + 1 < n:
            r = r - p.astype(F32)
    return parts


def _sigmoid(x):
    return 1.0 / (1.0 + jnp.exp(-x))


def _silu_t(x, scale=1.0):
    return (0.5 * scale) * x * (jnp.tanh(0.5 * x) + 1.0)


def _silu(x):
    return x * _sigmoid(x)


def _params(n_axes):
    return pltpu.CompilerParams(dimension_semantics=("arbitrary",) * n_axes,
                                vmem_limit_bytes=VMEM_LIMIT)


def _ada_body(c_ref, w_ref, b_ref, o_ref):
    a = _silu(c_ref[...])
    a_hi, a_lo = _split(a, 2)
    w_hi, w_lo = _split(w_ref[...], 2)
    o_ref[...] = _dot(a_hi, w_hi) + _dot(a_lo, w_hi) + _dot(a_hi, w_lo) + b_ref[...]


def _ada_mods(c8, w_ada, b_ada):
    tn = 1536
    return pl.pallas_call(
        _ada_body,
        grid=(DEPTH, 6 * D_MODEL // tn),
        in_specs=[
            pl.BlockSpec((8, D_MODEL), lambda l, j: (0, 0)),
            pl.BlockSpec((None, D_MODEL, tn), lambda l, j: (l, 0, j)),
            pl.BlockSpec((None, 1, tn), lambda l, j: (l, 0, j)),
        ],
        out_specs=pl.BlockSpec((None, 8, tn), lambda l, j: (l, 0, j)),
        out_shape=jax.ShapeDtypeStruct((DEPTH, 8, 6 * D_MODEL), F32),
        compiler_params=_params(2),
        name="ada_mods",
    )(c8, w_ada, b_ada.reshape(DEPTH, 1, 6 * D_MODEL))


def _modnorm(x, g, shift, scale):
    ms = jnp.mean(x * x, axis=-1, keepdims=True)
    return (x * lax.rsqrt(ms + EPS) * g) * (1.0 + scale) + shift


def _mod_row(n_ctx_tiles, tiles_per_latent):
    def f(i):
        return jnp.where(i < n_ctx_tiles, 0, 1 + (i - n_ctx_tiles) // tiles_per_latent)
    return f


def _row_specs(n_parts, tm, width, n_ctx_tiles):
    if n_parts == 1:
        return [pl.BlockSpec((tm, width), lambda i, *_: (i, 0))]
    return [pl.BlockSpec((tm, width), lambda i, *_: (jnp.minimum(i, n_ctx_tiles - 1), 0)),
            pl.BlockSpec((tm, width), lambda i, *_: (jnp.maximum(i - n_ctx_tiles, 0), 0))]


def _pick(is_ctx, refs):
    if len(refs) == 1:
        return refs[0][...]
    return jnp.where(is_ctx, refs[0][...], refs[1][...])


def _inproj_body(n_x, n_prev, n_ctx_tiles, ctx_len, *refs):
    x_refs = refs[:n_x]
    mod_ref, g_ref, w_ref = refs[n_x:n_x + 3]
    prev_refs = refs[n_x + 3:n_x + 3 + 2 * n_prev]
    oa_ref, oh_ref, k_ref, v_ref = refs[-4:]
    is_ctx = pl.program_id(0) < n_ctx_tiles
    x = _pick(is_ctx, x_refs)
    h = _modnorm(x, g_ref[...], mod_ref[0:1, :], mod_ref[1:2, :])
    proj = _dot(h.astype(BF16), w_ref[...])
    oa_ref[...] = proj[:, :PROJ_A_WIDTH].astype(BF16)
    oh_ref[...] = proj[:, PROJ_A_WIDTH:]

    @pl.when(is_ctx)
    def _():
        per_layer = ctx_len * DA_HEADS
        for s in range(proj.shape[0] // ctx_len):
            base = s * (n_prev + 1) * per_layer
            for l in range(n_prev):
                dst = slice(base + l * per_layer, base + (l + 1) * per_layer)
                src = slice(s * per_layer, (s + 1) * per_layer)
                k_ref[dst, :] = prev_refs[2 * l][src, :]
                v_ref[dst, :] = prev_refs[2 * l + 1][src, :]
            tok = slice(s * ctx_len, (s + 1) * ctx_len)
            for h in range(DA_HEADS):
                rows = pl.ds(base + n_prev * per_layer + h, ctx_len, stride=DA_HEADS)
                k_ref[rows, :] = proj[tok, COL_K + h * DA_VDIM:COL_K + (h + 1) * DA_VDIM]
                v_ref[rows, :] = proj[tok, COL_V + h * DA_VDIM:COL_V + (h + 1) * DA_VDIM]


def _inproj(xs, mod, g, w, prev_caches, n_ctx, ctx_len, lat_len):
    t = sum(x.shape[0] for x in xs)
    tm = TOKEN_TILE
    seqs = tm // ctx_len
    n_ctx_tiles = n_ctx // seqs
    n_prev = len(prev_caches) // 2
    row = _mod_row(n_ctx_tiles, lat_len // tm)
    tile_rows = seqs * ctx_len * DA_HEADS
    tile_block = lambda i: (jnp.minimum(i, n_ctx_tiles - 1), 0)
    cache_spec = pl.BlockSpec(((n_prev + 1) * tile_rows, DA_VDIM), tile_block)
    cache_shape = jax.ShapeDtypeStruct((n_ctx_tiles * (n_prev + 1) * tile_rows, DA_VDIM), F32)
    in_specs = _row_specs(len(xs), tm, D_MODEL, n_ctx_tiles) + [
        pl.BlockSpec((None, 6, D_MODEL), lambda i: (row(i), 0, 0)),
        pl.BlockSpec((1, D_MODEL), lambda i: (0, 0)),
        pl.BlockSpec((D_MODEL, PROJ_A_WIDTH + PROJ_H_WIDTH), lambda i: (0, 0)),
    ] + [pl.BlockSpec((tile_rows, DA_VDIM), tile_block)] * (2 * n_prev)
    return pl.pallas_call(
        functools.partial(_inproj_body, len(xs), n_prev, n_ctx_tiles, ctx_len),
        grid=(t // tm,),
        in_specs=in_specs,
        out_specs=[pl.BlockSpec((tm, PROJ_A_WIDTH), lambda i: (i, 0)),
                   pl.BlockSpec((tm, PROJ_H_WIDTH), lambda i: (i, 0)), cache_spec, cache_spec],
        out_shape=[jax.ShapeDtypeStruct((t, PROJ_A_WIDTH), BF16), jax.ShapeDtypeStruct((t, PROJ_H_WIDTH), F32),
                   cache_shape, cache_shape],
        compiler_params=_params(1),
        name="inproj",
    )(*xs, mod, g, w, *prev_caches)


def _fourier_rows(za, zb, cl, sl, w_f):
    y = _dot(cl, za) - _dot(sl, zb)
    return _dot(y.astype(BF16), w_f)


def _fourier_body(u_ref, cl_ref, sl_ref, cc_ref, sc_ref, w_ref, o_ref):
    z = u_ref[...].astype(BF16)
    za = _dot(z, cc_ref[...]).astype(BF16)
    zb = _dot(z, sc_ref[...]).astype(BF16)
    o_ref[...] = _fourier_rows(za, zb, cl_ref[...], sl_ref[...], w_ref[...]).astype(o_ref.dtype)


def _dft_tables(n, block):
    i = np.arange(n)
    prod = (i[:, None] % block) * (i[None, :] % block) % block
    ang = prod.astype(np.float64) * (2.0 * math.pi / block)
    same = (i[:, None] // block) == (i[None, :] // block)
    scale = 1.0 / math.sqrt(block)
    c = np.where(same, np.cos(ang) * scale, 0.0).astype(np.float32)
    s = np.where(same, np.sin(ang) * scale, 0.0).astype(np.float32)
    return jnp.asarray(c).astype(BF16), jnp.asarray(s).astype(BF16)


def _fourier(proj, row0, n_seq, seq_len, w_f):
    cl, sl = _dft_tables(seq_len, seq_len)
    cc, sc = _dft_tables(FN_WIDTH, FN_GROUP_DIM)
    blk0 = row0 // seq_len
    full = lambda shape: pl.BlockSpec(shape, lambda i: (0, 0))
    return pl.pallas_call(
        _fourier_body,
        grid=(n_seq,),
        in_specs=[
            pl.BlockSpec((seq_len, FN_WIDTH), lambda i: (blk0 + i, COL_FN // FN_WIDTH)),
            full((seq_len, seq_len)), full((seq_len, seq_len)),
            full((FN_WIDTH, FN_WIDTH)), full((FN_WIDTH, FN_WIDTH)), full((FN_WIDTH, FN_WIDTH)),
        ],
        out_specs=pl.BlockSpec((seq_len, FN_WIDTH), lambda i: (i, 0)),
        out_shape=jax.ShapeDtypeStruct((n_seq * seq_len, FN_WIDTH), BF16),
        compiler_params=_params(1),
        name=f"fourier_{seq_len}",
    )(proj, cl, sl, cc, sc, w_f)


def _lambda_full(lmb, lam_init):
    a = jnp.sum(lmb[0:1, :] * lmb[1:2, :], axis=-1, keepdims=True)
    b = jnp.sum(lmb[2:3, :] * lmb[3:4, :], axis=-1, keepdims=True)
    return jnp.exp(a) - jnp.exp(b) + lam_init


def _diff_head(q, ks, vs, lam, g, lam_init):
    vas = [jnp.concatenate([v.astype(BF16), jnp.ones(v.shape, BF16)], axis=1) for v in vs]
    outs = []
    for m in range(2):
        qm = q[:, m * DA_HALF:(m + 1) * DA_HALF].astype(BF16)
        parts = [_dot_nt(qm, k[:, m * DA_HALF:(m + 1) * DA_HALF].astype(BF16)) for k in ks]
        mx = parts[0].max(axis=-1, keepdims=True)
        for p in parts[1:]:
            mx = jnp.maximum(mx, p.max(axis=-1, keepdims=True))
        oa = _dot(jnp.exp(parts[0] - mx).astype(BF16), vas[0])
        for p, va in zip(parts[1:], vas[1:]):
            oa = oa + _dot(jnp.exp(p - mx).astype(BF16), va)
        outs.append(oa[:, :DA_VDIM] * (1.0 / oa[:, DA_VDIM:DA_VDIM + 1]))
    a = outs[0] - lam * outs[1]
    ms = jnp.mean(a * a, axis=-1, keepdims=True)
    return a * lax.rsqrt(ms + EPS) * g * (1.0 - lam_init)


def _rope(x, cos, sin):
    lane = lax.broadcasted_iota(jnp.int32, x.shape, 1)
    first = ((lane >> 4) & 1) == 0
    rot = jnp.where(first, -pltpu.roll(x, 128 - DA_HALF // 4, 1), pltpu.roll(x, DA_HALF // 4, 1))
    return x * cos + rot * sin


def _attn_lat_body(lam_init, q_ref, k_ref, v_ref, kc_ref, vc_ref, cq_ref, sq_ref, ck_ref, sk_ref,
                   lmb_ref, g_ref, o_ref):
    lam = _lambda_full(lmb_ref[...], lam_init)
    scale = DA_HALF ** -0.5
    for h in range(DA_HEADS):
        sl = slice(h * DA_VDIM, (h + 1) * DA_VDIM)
        q = _rope(q_ref[:, sl].astype(F32), cq_ref[...], sq_ref[...]) * scale
        k = _rope(k_ref[:, sl].astype(F32), ck_ref[...], sk_ref[...])
        o = _diff_head(q, [k, kc_ref[:, sl]], [v_ref[:, sl], vc_ref[:, sl]], lam, g_ref[...], lam_init)
        o_ref[:, sl] = o.astype(o_ref.dtype)


def _attn_lat(proj, row0, n_seq, seq_len, cache_k, cache_v, layer, cos, sin, lmb, g, lam_init):
    tq = 512
    nq = seq_len // tq
    past = cache_k.shape[2]
    qb0 = row0 // tq
    kb0 = row0 // seq_len
    cache_spec = pl.BlockSpec((None, None, past, DA_WIDTH), lambda b, j: (b, layer, 0, 0))
    kv_spec = lambda c: pl.BlockSpec((seq_len, DA_WIDTH), lambda b, j: (kb0 + b, c))
    return pl.pallas_call(
        functools.partial(_attn_lat_body, lam_init),
        grid=(n_seq, nq),
        in_specs=[
            pl.BlockSpec((tq, DA_WIDTH), lambda b, j: (qb0 + b * nq + j, COL_Q // DA_WIDTH)),
            kv_spec(COL_K // DA_WIDTH), kv_spec(COL_V // DA_WIDTH),
            cache_spec, cache_spec,
            pl.BlockSpec((tq, DA_VDIM), lambda b, j: (j, 0)),
            pl.BlockSpec((tq, DA_VDIM), lambda b, j: (j, 0)),
            pl.BlockSpec((seq_len, DA_VDIM), lambda b, j: (0, 0)),
            pl.BlockSpec((seq_len, DA_VDIM), lambda b, j: (0, 0)),
            pl.BlockSpec((4, DA_HALF), lambda b, j: (0, 0)),
            pl.BlockSpec((1, DA_VDIM), lambda b, j: (0, 0)),
        ],
        out_specs=pl.BlockSpec((tq, DA_WIDTH), lambda b, j: (b * nq + j, 0)),
        out_shape=jax.ShapeDtypeStruct((n_seq * seq_len, DA_WIDTH), BF16),
        compiler_params=_params(2),
        name="attn_lat",
    )(proj, proj, proj, cache_k, cache_v, cos, sin, cos, sin, lmb, g)


def _rope_tables(n_tokens):
    rows = n_tokens // GRID_W
    row = np.repeat(np.arange(rows, dtype=np.float64), GRID_W)
    col = np.tile(np.arange(GRID_W, dtype=np.float64), rows)
    axis_dim = DA_HALF // 2
    inv_freq = ROPE_THETA ** (-np.arange(0, axis_dim, 2, dtype=np.float64) / axis_dim)
    ang_r = row[:, None] * inv_freq[None, :]
    ang_c = col[:, None] * inv_freq[None, :]
    ang = np.concatenate([ang_r, ang_r, ang_c, ang_c] * 2, axis=-1)
    return jnp.asarray(np.cos(ang).astype(np.float32)), jnp.asarray(np.sin(ang).astype(np.float32))


def _hgrn_tables():
    gl = CHUNKS_PER_GROUP * CHUNK
    t = np.arange(gl)
    same = (t[:, None] // CHUNK) == (t[None, :] // CHUNK)
    fwd = same & (t[None, :] <= t[:, None])
    bwd = same & (t[None, :] >= t[:, None])
    f = np.arange(HG_WIDTH)
    bd = (f[:, None] // HG_DK) == (f[None, :] // HG_DK)
    tri = jnp.asarray(np.stack([fwd, bwd]).astype(np.float32)).astype(BF16)
    causal = jnp.asarray(np.stack([fwd, bwd]).astype(np.float32))
    return tri, causal, jnp.asarray(bd.astype(np.float32))


def _hgrn2_body(n_seq, groups_per_seq, has_s0, *refs):
    (hq_ref, hi_ref, hf_ref, hb_ref, hg_ref, lb_ref, ng_ref, tri_ref, causal_ref, bd_ref) = refs[:10]
    if has_s0:
        s0_ref, o_ref, oi_scr, qe_scr, u_scr, st_scr, dec_scr = refs[10:]
        so_ref = None
    else:
        o_ref, so_ref, oi_scr, qe_scr, u_scr, st_scr, dec_scr = refs[10:]
        s0_ref = None
    c = CHUNK
    w = HG_WIDTH
    gc = CHUNKS_PER_GROUP
    gl = gc * c
    n_groups = n_seq * groups_per_seq
    chunks_per_seq = groups_per_seq * gc
    lane = lax.broadcasted_iota(jnp.int32, (1, w), 1)
    dk_bits = HG_DK.bit_length() - 1
    head_masks_b = [((lane >> dk_bits) == h).astype(BF16) for h in range(HG_HEADS)]
    bd = bd_ref[...] > 0.5
    edge_rows = (c - 1, 0)
    fp_refs = (hf_ref, hb_ref)

    def per_chunk_row(x, row):
        return jnp.concatenate([jnp.broadcast_to(x[k * c + row:k * c + row + 1, :], (c, w)) for k in range(gc)],
                               axis=0)

    def group_terms(gi):
        rows = pl.ds(gi * gl if isinstance(gi, int) else pl.multiple_of(gi * gl, gl), gl)
        hq = hq_ref[rows, :]
        q = _silu_t(hq, HG_DK ** -0.5)
        v = hi_ref[rows, :].astype(BF16)
        v_stack = jnp.concatenate([v * hm for hm in head_masks_b], axis=0)
        for d in range(2):
            lb = lb_ref[d:d + 1, :]
            fp = fp_refs[d][rows, :]
            lsig = jnp.minimum(fp, 0.0) - jnp.log(1.0 + jnp.exp(-jnp.abs(fp)))
            la = jnp.log(lb)
            lbb = jnp.log(1.0 - lb) + lsig
            logf = jnp.maximum(la, lbb) + jnp.log(1.0 + jnp.exp(-jnp.abs(la - lbb)))
            kk = (0.5 * (1.0 - lb)) * (1.0 - jnp.tanh(0.5 * fp))
            g_hi, g_lo = _split(logf, 2)
            cum = _dot(tri_ref[d], g_hi) + _dot(tri_ref[d], g_lo)
            total = per_chunk_row(cum, edge_rows[d])
            ref = per_chunk_row(cum, c // 2)
            qc = (q * jnp.exp(cum - ref)).astype(BF16)
            kc = (kk * jnp.exp(ref - cum)).astype(BF16)
            ke = (kk * jnp.exp(total - cum)).astype(BF16)
            qe_scr[d, rows, :] = (q * jnp.exp(cum)).astype(BF16)
            kc_stack = jnp.concatenate([kc * hm for hm in head_masks_b], axis=0)
            a = _dot_nt(qc, kc_stack)
            keep = causal_ref[d] > 0.5
            a = jnp.concatenate([jnp.where(keep, a[:, h * gl:(h + 1) * gl], 0.0) for h in range(HG_HEADS)],
                                axis=1).astype(BF16)
            oi_scr[d, rows, :] = _dot(a, v_stack)
            for k in range(gc):
                ck = slice(k * c, (k + 1) * c)
                i = gi * gc + k
                dec_scr[d, pl.ds(i, 1), :] = jnp.exp(cum[k * c + edge_rows[d]:k * c + edge_rows[d] + 1, :])
                u_scr[d, i] = jnp.where(bd, _dot_tn(v[ck, :], ke[ck, :]), 0.0)

    if n_groups <= 2:
        for gi in range(n_groups):
            group_terms(gi)
    else:
        def terms_step(gi, carry):
            group_terms(gi)
            return carry

        lax.fori_loop(0, n_groups, terms_step, 0)

    slab = 32
    for s in range(n_seq):
        first = s * chunks_per_seq
        for d in range(2):
            order = range(chunks_per_seq) if d == 0 else range(chunks_per_seq - 1, -1, -1)
            for r in range(w // slab):
                rs = slice(r * slab, (r + 1) * slab)
                st = s0_ref[s, d, rs, :] if has_s0 else jnp.zeros((slab, w), F32)
                for j in order:
                    i = first + j
                    st_scr[d, i, rs, :] = st.astype(BF16)
                    st = st * dec_scr[d, i:i + 1, :] + u_scr[d, i, rs, :]
                if so_ref is not None:
                    h = (r * slab) // HG_DK
                    off = (r * slab) % HG_DK
                    so_ref[s, d, h, off:off + slab, :] = st[:, h * HG_DK:(h + 1) * HG_DK]

    ones_bd = bd.astype(BF16)

    def finish(gi):
        rows = pl.ds(gi * gl if isinstance(gi, int) else pl.multiple_of(gi * gl, gl), gl)
        inter = []
        for k in range(gc):
            i = gi * gc + k
            ck = pl.ds(gi * gl + k * c if isinstance(gi, int) else pl.multiple_of(gi * gl + k * c, c), c)
            inter.append(_dot_nt(qe_scr[0, ck, :], st_scr[0, i]) + _dot_nt(qe_scr[1, ck, :], st_scr[1, i]))
        o = oi_scr[0, rows, :] + oi_scr[1, rows, :] + jnp.concatenate(inter, axis=0)
        sq = _split(o * o, 2)
        ms = (_dot(sq[0], ones_bd) + _dot(sq[1], ones_bd)) * (1.0 / HG_DK)
        y = o * lax.rsqrt(ms + EPS) * ng_ref[...]
        hg = hg_ref[rows, :]
        o_ref[rows, :] = (y * _silu_t(hg)).astype(o_ref.dtype)

    if n_groups <= 2:
        for gi in range(n_groups):
            finish(gi)
    else:
        def finish_step(gi, carry):
            finish(gi)
            return carry

        lax.fori_loop(0, n_groups, finish_step, 0)


def _hgrn_scratch(rows):
    n_chunks = rows // CHUNK
    return [pltpu.VMEM((2, rows, HG_WIDTH), F32),
            pltpu.VMEM((2, rows, HG_WIDTH), BF16),
            pltpu.VMEM((2, n_chunks, HG_WIDTH, HG_WIDTH), F32),
            pltpu.VMEM((2, n_chunks, HG_WIDTH, HG_WIDTH), BF16),
            pltpu.VMEM((2, max(n_chunks, 8), HG_WIDTH), F32)]


def _hgrn2(proj_h, row0, n_seq, seq_len, seqs_per_step, lb, ng, s0t):
    gl = CHUNKS_PER_GROUP * CHUNK
    groups_per_seq = seq_len // gl
    rows = seqs_per_step * seq_len
    blk0 = row0 // rows
    tri, causal, bd = _hgrn_tables()
    col = lambda c0: pl.BlockSpec((rows, HG_WIDTH), lambda i: (blk0 + i, c0 // HG_WIDTH))
    const = lambda shape: pl.BlockSpec(shape, lambda i: (0,) * len(shape))
    in_specs = [col(COL_HQ), col(COL_HI), col(COL_HF), col(COL_HB), col(COL_HG),
                const((2, HG_WIDTH)), const((1, HG_WIDTH)),
                const((2, gl, gl)), const((2, gl, gl)), const((HG_WIDTH, HG_WIDTH))]
    args = [proj_h] * 5 + [lb, ng, tri, causal, bd]
    out_specs = [pl.BlockSpec((rows, HG_WIDTH), lambda i: (i, 0))]
    out_shape = [jax.ShapeDtypeStruct((n_seq * seq_len, HG_WIDTH), BF16)]
    if s0t is not None:
        in_specs.append(pl.BlockSpec((seqs_per_step, 2, HG_WIDTH, HG_WIDTH), lambda i: (i, 0, 0, 0)))
        args.append(s0t)
    else:
        out_specs.append(pl.BlockSpec((seqs_per_step, 2, HG_HEADS, HG_DK, HG_DK), lambda i: (i, 0, 0, 0, 0)))
        out_shape.append(jax.ShapeDtypeStruct((n_seq, 2, HG_HEADS, HG_DK, HG_DK), F32))
    return pl.pallas_call(
        functools.partial(_hgrn2_body, seqs_per_step, groups_per_seq, s0t is not None),
        grid=(n_seq // seqs_per_step,),
        in_specs=in_specs,
        out_specs=out_specs,
        out_shape=out_shape,
        scratch_shapes=_hgrn_scratch(rows),
        compiler_params=_params(1),
        name=f"hgrn_{seq_len}",
    )(*args)


def _ctx_mixers_body(n_seq, seq_len, lam_init, *refs):
    (q_ref, k_ref, v_ref, u_ref, hq_ref, hi_ref, hf_ref, hb_ref, hg_ref,
     cl_ref, sl_ref, cc_ref, sc_ref, wf_ref, lmb_ref, ag_ref,
     lb_ref, ng_ref, tri_ref, causal_ref, bd_ref,
     fn_ref, a_ref, ho_ref, so_ref) = refs[:25]
    scratch = refs[25:]

    z = u_ref[...].astype(BF16)
    za = _dot(z, cc_ref[...]).astype(BF16)
    zb = _dot(z, sc_ref[...]).astype(BF16)
    for s in range(n_seq):
        rs = slice(s * seq_len, (s + 1) * seq_len)
        fn_ref[rs, :] = _fourier_rows(za[rs, :], zb[rs, :], cl_ref[...], sl_ref[...],
                                      wf_ref[...]).astype(fn_ref.dtype)

    lam = _lambda_full(lmb_ref[...], lam_init)
    scale = DA_HALF ** -0.5
    for s in range(n_seq):
        rs = slice(s * seq_len, (s + 1) * seq_len)
        for h in range(DA_HEADS):
            sl = slice(h * DA_VDIM, (h + 1) * DA_VDIM)
            o = _diff_head(q_ref[rs, sl] * scale, [k_ref[rs, sl]], [v_ref[rs, sl]], lam, ag_ref[...], lam_init)
            a_ref[rs, sl] = o.astype(a_ref.dtype)

    _hgrn2_body(n_seq, seq_len // (CHUNKS_PER_GROUP * CHUNK), False,
                hq_ref, hi_ref, hf_ref, hb_ref, hg_ref, lb_ref, ng_ref, tri_ref, causal_ref, bd_ref,
                ho_ref, so_ref, *scratch)


def _ctx_mixers(proj_a, proj_h, n_seq, seq_len, seqs_per_step, w_f, lmb, ag, lam_init, lb, ng):
    rows = seqs_per_step * seq_len
    gl = CHUNKS_PER_GROUP * CHUNK
    cl, sl = _dft_tables(seq_len, seq_len)
    cc, sc = _dft_tables(FN_WIDTH, FN_GROUP_DIM)
    tri, causal, bd = _hgrn_tables()
    col = lambda c0, width: pl.BlockSpec((rows, width), lambda i: (i, c0 // width))
    const = lambda shape: pl.BlockSpec(shape, lambda i: (0,) * len(shape))
    out_rows = lambda width: pl.BlockSpec((rows, width), lambda i: (i, 0))
    n_tok = n_seq * seq_len
    return pl.pallas_call(
        functools.partial(_ctx_mixers_body, seqs_per_step, seq_len, lam_init),
        grid=(n_seq // seqs_per_step,),
        in_specs=[col(COL_Q, DA_WIDTH), col(COL_K, DA_WIDTH), col(COL_V, DA_WIDTH), col(COL_FN, FN_WIDTH),
                  col(COL_HQ, HG_WIDTH), col(COL_HI, HG_WIDTH), col(COL_HF, HG_WIDTH), col(COL_HB, HG_WIDTH),
                  col(COL_HG, HG_WIDTH),
                  const((seq_len, seq_len)), const((seq_len, seq_len)),
                  const((FN_WIDTH, FN_WIDTH)), const((FN_WIDTH, FN_WIDTH)), const((FN_WIDTH, FN_WIDTH)),
                  const((4, DA_HALF)), const((1, DA_VDIM)),
                  const((2, HG_WIDTH)), const((1, HG_WIDTH)),
                  const((2, gl, gl)), const((2, gl, gl)), const((HG_WIDTH, HG_WIDTH))],
        out_specs=[out_rows(FN_WIDTH), out_rows(DA_WIDTH), out_rows(HG_WIDTH),
                   pl.BlockSpec((seqs_per_step, 2, HG_HEADS, HG_DK, HG_DK), lambda i: (i, 0, 0, 0, 0))],
        out_shape=[jax.ShapeDtypeStruct((n_tok, FN_WIDTH), BF16),
                   jax.ShapeDtypeStruct((n_tok, DA_WIDTH), BF16),
                   jax.ShapeDtypeStruct((n_tok, HG_WIDTH), BF16),
                   jax.ShapeDtypeStruct((n_seq, 2, HG_HEADS, HG_DK, HG_DK), F32)],
        scratch_shapes=_hgrn_scratch(rows),
        compiler_params=_params(1),
        name="ctx_mixers",
    )(*([proj_a] * 4 + [proj_h] * 5), cl, sl, cc, sc, w_f, lmb, ag, lb, ng, tri, causal, bd)


def _route_gates(lt, bias):
    per = N_EXPERTS // N_GROUPS
    tt = lt.shape[1]
    neg = -jnp.inf
    assert per == N_GROUPS == 8
    gi = lax.broadcasted_iota(jnp.int32, (N_GROUPS, tt), 0).astype(F32)
    s_j, b_j = [], []
    for j in range(per):
        s = _sigmoid(lt[j * N_GROUPS:(j + 1) * N_GROUPS, :])
        s_j.append(s)
        b_j.append(s + bias[j * N_GROUPS:(j + 1) * N_GROUPS, :])
    m1 = functools.reduce(jnp.maximum, b_j)
    i1 = functools.reduce(jnp.minimum, [jnp.where(b_j[j] == m1, float(j), float(per)) for j in range(per)])
    m2 = functools.reduce(jnp.maximum, [jnp.where(i1 == float(j), neg, b_j[j]) for j in range(per)])
    gs = m1 + m2
    gsel = jnp.zeros((N_GROUPS, tt), jnp.bool_)
    for _ in range(TOPK_GROUPS):
        m = gs.max(axis=0, keepdims=True)
        idx = jnp.where(gs == m, gi, float(N_GROUPS)).min(axis=0, keepdims=True)
        hit = gi == idx
        gsel = gsel | hit
        gs = jnp.where(hit, neg, gs)
    x_j = [jnp.where(gsel, b_j[j], neg) for j in range(per)]
    e_j = [gi * per + j for j in range(per)]
    sel_j = [jnp.zeros((N_GROUPS, tt), jnp.bool_) for _ in range(per)]
    for _ in range(TOP_K):
        m = functools.reduce(jnp.maximum, x_j).max(axis=0, keepdims=True)
        idx = functools.reduce(jnp.minimum, [jnp.where(x_j[j] == m, e_j[j], float(N_EXPERTS))
                                             for j in range(per)]).min(axis=0, keepdims=True)
        for j in range(per):
            hit = e_j[j] == idx
            sel_j[j] = sel_j[j] | hit
            x_j[j] = jnp.where(hit, neg, x_j[j])
    w_j = [jnp.where(sel_j[j], s_j[j], 0.0) for j in range(per)]
    denom = functools.reduce(lambda a, b: a + b, w_j).sum(axis=0, keepdims=True)
    gates_t = jnp.concatenate([w / denom * ROUTED_SCALE for w in w_j], axis=0)
    r_io = lax.broadcasted_iota(jnp.int32, (N_EXPERTS, N_EXPERTS), 0)
    e_io = lax.broadcasted_iota(jnp.int32, (N_EXPERTS, N_EXPERTS), 1)
    eye = (e_io == (r_io & (N_GROUPS - 1)) * per + (r_io >> 3)).astype(BF16)
    p = _split(gates_t, 3)
    return _dot_tn(p[0], eye) + _dot_tn(p[1], eye) + _dot_tn(p[2], eye)


def _outproj_body(n_x, n_ctx_tiles, *refs):
    fn_refs, a_refs, hg_refs = refs[0:2], refs[2:4], refs[4:6]
    x_refs = refs[6:6 + n_x]
    mod_ref, g_ref, w_ref, wr_ref, rb_ref, xo_ref, h_ref, gate_ref = refs[6 + n_x:]
    is_ctx = pl.program_id(0) < n_ctx_tiles
    mix = (_dot(_pick(is_ctx, fn_refs), w_ref[0:FN_WIDTH, :])
           + _dot(_pick(is_ctx, a_refs), w_ref[FN_WIDTH:FN_WIDTH + DA_WIDTH, :])
           + _dot(_pick(is_ctx, hg_refs), w_ref[FN_WIDTH + DA_WIDTH:, :]))
    x = _pick(is_ctx, x_refs) + mod_ref[2:3, :] * mix
    xo_ref[...] = x
    h = _modnorm(x, g_ref[...], mod_ref[3:4, :], mod_ref[4:5, :])
    h_ref[...] = h.astype(BF16)
    h_hi, h_lo = _split(h, 2)
    w_hi, w_lo = _split(wr_ref[...], 2)
    logits_t = _dot_nt(w_hi, h_hi) + _dot_nt(w_lo, h_hi) + _dot_nt(w_hi, h_lo)
    gate_ref[...] = _route_gates(logits_t, rb_ref[...])


def _outproj(fn, a, hg, xs, mod, g, w_out, w_router_t, router_bias, n_ctx_tok, lat_len):
    t = sum(x.shape[0] for x in xs)
    tm = TOKEN_TILE
    n_ctx_tiles = n_ctx_tok // tm
    row = _mod_row(n_ctx_tiles, lat_len // tm)
    rows = lambda width: pl.BlockSpec((tm, width), lambda i: (i, 0))
    parts = lambda n, width: _row_specs(n, tm, width, n_ctx_tiles)
    return pl.pallas_call(
        functools.partial(_outproj_body, len(xs), n_ctx_tiles),
        grid=(t // tm,),
        in_specs=parts(2, FN_WIDTH) + parts(2, DA_WIDTH) + parts(2, HG_WIDTH) + parts(len(xs), D_MODEL) + [
            pl.BlockSpec((None, 6, D_MODEL), lambda i: (row(i), 0, 0)),
            pl.BlockSpec((1, D_MODEL), lambda i: (0, 0)),
            pl.BlockSpec((D_MODEL, D_MODEL), lambda i: (0, 0)),
            pl.BlockSpec((N_EXPERTS, D_MODEL), lambda i: (0, 0)),
            pl.BlockSpec((N_EXPERTS, 1), lambda i: (0, 0)),
        ],
        out_specs=[rows(D_MODEL), rows(D_MODEL), rows(N_EXPERTS)],
        out_shape=[jax.ShapeDtypeStruct((t, D_MODEL), F32),
                   jax.ShapeDtypeStruct((t, D_MODEL), BF16),
                   jax.ShapeDtypeStruct((t, N_EXPERTS), F32)],
        compiler_params=_params(1),
        name="outproj",
    )(*fn, *a, *hg, *xs, mod, g, w_out, w_router_t, router_bias)


def _moe_body(final_norm, n_ctx_tiles, h_ref, gate_ref, wg_ref, wu_ref, wd_ref, sg_ref, su_ref, sd_ref,
              x_ref, mod_ref, fg_ref, *out_and_scratch):
    acc_ref = out_and_scratch[-1]
    o_refs = out_and_scratch[:-1]
    is_ctx = pl.program_id(0) < n_ctx_tiles
    j = pl.program_id(1)
    h = h_ref[...]

    def act_of(wg, wu, gate):
        gu = _dot(h, jnp.concatenate([wg.astype(BF16), wu.astype(BF16)], axis=1))
        a = _silu_t(gu[:, :D_EXPERT]) * gu[:, D_EXPERT:]
        return a if gate is None else a * gate

    @pl.when(j == 0)
    def _():
        acc_ref[...] = _dot(act_of(sg_ref[...], su_ref[...], None).astype(BF16), sd_ref[...].astype(BF16))

    gates = gate_ref[...]
    expert_of_lane = lax.broadcasted_iota(jnp.int32, gates.shape, 1)

    def gate_col(p):
        e = j * EXPERTS_PER_STEP + p
        return jnp.sum(jnp.where(expert_of_lane == e, gates, 0.0), axis=1, keepdims=True)

    acts = [act_of(wg_ref[p], wu_ref[p], gate_col(p)).astype(BF16) for p in range(EXPERTS_PER_STEP)]
    wd = wd_ref[...].astype(BF16).reshape(EXPERTS_PER_STEP * D_EXPERT, D_MODEL)
    acc_ref[...] += _dot(jnp.concatenate(acts, axis=1), wd)

    @pl.when(j == pl.num_programs(1) - 1)
    def _():
        x = x_ref[...] + mod_ref[5:6, :] * acc_ref[...]
        if not final_norm:
            o_refs[0][...] = x
        else:
            ms = jnp.mean(x * x, axis=-1, keepdims=True)
            y = x * lax.rsqrt(ms + EPS) * fg_ref[...]

            @pl.when(is_ctx)
            def _():
                o_refs[0][...] = y

            @pl.when(jnp.logical_not(is_ctx))
            def _():
                o_refs[1][...] = y


def _moe(h, gates, w_gate, w_up, w_down2, ws_gate, ws_up, ws_down, layer, x, mod, final_g, final_norm,
         n_ctx_tok, lat_len):
    t = x.shape[0]
    tm = MOE_TILE
    eps_ = EXPERTS_PER_STEP
    n_ctx_tiles = n_ctx_tok // tm
    row = _mod_row(n_ctx_tiles, lat_len // tm)
    once = pl.Buffered(1)
    if final_norm:
        last_ctx = n_ctx_tiles - 1
        out_specs = [pl.BlockSpec((tm, D_MODEL), lambda i, j: (jnp.minimum(i, last_ctx), 0), pipeline_mode=once),
                     pl.BlockSpec((tm, D_MODEL), lambda i, j: (jnp.maximum(i - n_ctx_tiles, 0), 0),
                                  pipeline_mode=once)]
        out_shape = [jax.ShapeDtypeStruct((n_ctx_tok, D_MODEL), F32),
                     jax.ShapeDtypeStruct((t - n_ctx_tok, D_MODEL), F32)]
    else:
        out_specs = [pl.BlockSpec((tm, D_MODEL), lambda i, j: (i, 0), pipeline_mode=once)]
        out_shape = [jax.ShapeDtypeStruct((t, D_MODEL), F32)]
    return pl.pallas_call(
        functools.partial(_moe_body, final_norm, n_ctx_tiles),
        grid=(t // tm, N_EXPERTS // eps_),
        in_specs=[
            pl.BlockSpec((tm, D_MODEL), lambda i, j: (i, 0)),
            pl.BlockSpec((tm, N_EXPERTS), lambda i, j: (i, 0)),
            pl.BlockSpec((None, eps_, D_MODEL, D_EXPERT), lambda i, j: (layer, j, 0, 0)),
            pl.BlockSpec((None, eps_, D_MODEL, D_EXPERT), lambda i, j: (layer, j, 0, 0)),
            pl.BlockSpec((None, eps_ // 2, 2 * D_EXPERT, D_MODEL), lambda i, j: (layer, j, 0, 0)),
            pl.BlockSpec((None, D_MODEL, D_EXPERT), lambda i, j: (layer, 0, 0), pipeline_mode=once),
            pl.BlockSpec((None, D_MODEL, D_EXPERT), lambda i, j: (layer, 0, 0), pipeline_mode=once),
            pl.BlockSpec((None, D_EXPERT, D_MODEL), lambda i, j: (layer, 0, 0), pipeline_mode=once),
            pl.BlockSpec((tm, D_MODEL), lambda i, j: (i, 0), pipeline_mode=once),
            pl.BlockSpec((None, 6, D_MODEL), lambda i, j: (row(i), 0, 0)),
            pl.BlockSpec((1, D_MODEL), lambda i, j: (0, 0)),
        ],
        out_specs=out_specs,
        out_shape=out_shape,
        scratch_shapes=[pltpu.VMEM((tm, D_MODEL), F32)],
        compiler_params=_params(2),
        name="moe",
    )(h, gates, w_gate, w_up, w_down2, ws_gate, ws_up, ws_down, x, mod, final_g)


def _block_diag_t(s):
    eye = jnp.eye(HG_HEADS, dtype=s.dtype)
    out = jnp.einsum('...hkv,hg->...hvgk', s, eye)
    return out.reshape(s.shape[:-3] + (HG_WIDTH, HG_WIDTH))


def kernel(x_prompt, x_sample, cache_k, cache_v, state_hgrn, c, c_ctx, w_ada, b_ada, norm_g, w_in,
           w_fourier, lambdas, attn_norm_g, lower_bounds, hg_norm_g, w_out, w_router, router_bias,
           w_gate, w_up, w_down, ws_gate, ws_up, ws_down, final_g):
    n_ctx, ctx_len, _ = x_prompt.shape
    n_lat, lat_len, _ = x_sample.shape
    n_ctx_tok = n_ctx * ctx_len
    past = cache_k.shape[2]

    xs = (x_prompt.reshape(n_ctx_tok, D_MODEL), x_sample.reshape(n_lat * lat_len, D_MODEL))

    c8 = jnp.zeros((8, D_MODEL), F32).at[0].set(c_ctx).at[1:1 + n_lat].set(c)
    mods = _ada_mods(c8, w_ada, b_ada).reshape(DEPTH, 8, 6, D_MODEL)

    cs = jnp.cumsum(jax.nn.softmax(lower_bounds.astype(F32), axis=0), axis=0)
    lbs = cs - cs[0:1]

    cos, sin = _rope_tables(lat_len)
    cache_k4 = cache_k.reshape(n_lat, DEPTH, past, DA_WIDTH)
    cache_v4 = cache_v.reshape(n_lat, DEPTH, past, DA_WIDTH)
    s0t = _block_diag_t(state_hgrn.astype(F32))

    qkv_end = FN_WIDTH + 3 * DA_WIDTH
    w_in_b = jnp.concatenate([w_in[:, :, FN_WIDTH:qkv_end], w_in[:, :, :FN_WIDTH], w_in[:, :, qkv_end:]],
                             axis=-1).astype(BF16)
    w_f_b = w_fourier.astype(BF16)
    w_out_b = w_out.astype(BF16)
    per = N_EXPERTS // N_GROUPS
    w_router_t = (jnp.swapaxes(w_router, 1, 2).reshape(DEPTH, N_GROUPS, per, D_MODEL)
                  .swapaxes(1, 2).reshape(DEPTH, N_EXPERTS, D_MODEL))
    bias_mm = router_bias.reshape(DEPTH, N_GROUPS, per).swapaxes(1, 2).reshape(DEPTH, N_EXPERTS, 1)
    w_down2 = w_down.reshape(DEPTH, N_EXPERTS // 2, 2 * D_EXPERT, D_MODEL)
    ng = jnp.tile(hg_norm_g, (1, HG_HEADS))

    caches = []
    new_s = []
    for l in range(DEPTH):
        lam_init = 0.8 - 0.6 * math.exp(-0.3 * l)
        mod = mods[l]
        last = l == DEPTH - 1
        proj_a, proj_h, k_l, v_l = _inproj(xs, mod, norm_g[l, 0:1], w_in_b[l], tuple(caches) if last else (),
                                           n_ctx, ctx_len, lat_len)
        caches = [k_l, v_l] if last else caches + [k_l, v_l]

        ag = attn_norm_g[l].reshape(1, DA_VDIM)
        fn_ctx, a_ctx, hg_ctx, st_ctx = _ctx_mixers(proj_a, proj_h, n_ctx, ctx_len, 2, w_f_b[l], lambdas[l], ag,
                                                    lam_init, lbs[l], ng[l:l + 1])
        fn_lat = _fourier(proj_a, n_ctx_tok, n_lat, lat_len, w_f_b[l])
        a_lat = _attn_lat(proj_a, n_ctx_tok, n_lat, lat_len, cache_k4, cache_v4, l, cos, sin, lambdas[l], ag,
                          lam_init)
        (hg_lat,) = _hgrn2(proj_h, n_ctx_tok, n_lat, lat_len, 1, lbs[l], ng[l:l + 1], s0t[:, l])
        new_s.append(jnp.swapaxes(st_ctx, -1, -2))

        x, h2, gates = _outproj((fn_ctx, fn_lat), (a_ctx, a_lat), (hg_ctx, hg_lat), xs, mod,
                                norm_g[l, 1:2], w_out_b[l], w_router_t[l], bias_mm[l], n_ctx_tok, lat_len)
        xs = _moe(h2, gates, w_gate, w_up, w_down2, ws_gate, ws_up, ws_down, l, x, mod,
                  final_g.reshape(1, D_MODEL), l == DEPTH - 1, n_ctx_tok, lat_len)

    y_prompt = xs[0].reshape(n_ctx, ctx_len, D_MODEL)
    y_sample = xs[1].reshape(n_lat, lat_len, D_MODEL)
    new_k = caches[0].reshape(n_ctx, DEPTH, ctx_len, DA_HEADS, DA_VDIM)
    new_v = caches[1].reshape(n_ctx, DEPTH, ctx_len, DA_HEADS, DA_VDIM)
    return (y_prompt, y_sample, new_k, new_v, jnp.stack(new_s, axis=1))
```

```python
import functools
import math

import numpy as np
import jax
import jax.numpy as jnp
from jax import lax
from jax.experimental import pallas as pl
from jax.experimental.pallas import tpu as pltpu

F32 = jnp.float32
BF16 = jnp.bfloat16

D_MODEL = 1024
DEPTH = 2
GRID_W = 64
FN_WIDTH = 256
FN_GROUP_DIM = 64
DA_WIDTH = 512
DA_HEADS = 4
DA_VDIM = 128
DA_HALF = 64
HG_WIDTH = 256
HG_HEADS = 4
HG_DK = 64
CHUNK = 64
ROPE_THETA = 10000.0
N_EXPERTS = 64
TOP_K = 8
N_GROUPS = 8
TOPK_GROUPS = 4
D_EXPERT = 128
ROUTED_SCALE = 2.5
EPS = 1e-6

COL_Q, COL_K, COL_V, COL_FN = 0, 512, 1024, 1536
PROJ_A_WIDTH = 1792
COL_HQ, COL_HI, COL_HF, COL_HB, COL_HG = 0, 256, 512, 768, 1024
PROJ_H_WIDTH = 1280

TOKEN_TILE = 512
MOE_TILE = 1024
EXPERTS_PER_STEP = 8
CHUNKS_PER_GROUP = 4
VMEM_LIMIT = 56 * 1024 * 1024


def _dot(a, b):
    return jnp.dot(a, b, preferred_element_type=F32)


def _dot_nt(a, b):
    return lax.dot_general(a, b, (((1,), (1,)), ((), ())), preferred_element_type=F32)


def _dot_tn(a, b):
    return lax.dot_general(a, b, (((0,), (0,)), ((), ())), preferred_element_type=F32)


def _split(x, n):
    parts = []
    r = x
    for i in range(n):
        p = r.astype(BF16)
        parts.append(p)
        if i + 1 < n:
            r = r - p.astype(F32)
    return parts


def _sigmoid(x):
    return 1.0 / (1.0 + jnp.exp(-x))


def _silu_t(x, scale=1.0):
    return (0.5 * scale) * x * (jnp.tanh(0.5 * x) + 1.0)


def _silu(x):
    return x * _sigmoid(x)


def _params(n_axes):
    return pltpu.CompilerParams(dimension_semantics=("arbitrary",) * n_axes,
                                vmem_limit_bytes=VMEM_LIMIT)


def _ada_body(c_ref, w_ref, b_ref, o_ref):
    a = _silu(c_ref[...])
    a_hi, a_lo = _split(a, 2)
    w_hi, w_lo = _split(w_ref[...], 2)
    o_ref[...] = _dot(a_hi, w_hi) + _dot(a_lo, w_hi) + _dot(a_hi, w_lo) + b_ref[...]


def _ada_mods(c8, w_ada, b_ada):
    tn = 1536
    return pl.pallas_call(
        _ada_body,
        grid=(DEPTH, 6 * D_MODEL // tn),
        in_specs=[
            pl.BlockSpec((8, D_MODEL), lambda l, j: (0, 0)),
            pl.BlockSpec((None, D_MODEL, tn), lambda l, j: (l, 0, j)),
            pl.BlockSpec((None, 1, tn), lambda l, j: (l, 0, j)),
        ],
        out_specs=pl.BlockSpec((None, 8, tn), lambda l, j: (l, 0, j)),
        out_shape=jax.ShapeDtypeStruct((DEPTH, 8, 6 * D_MODEL), F32),
        compiler_params=_params(2),
        name="ada_mods",
    )(c8, w_ada, b_ada.reshape(DEPTH, 1, 6 * D_MODEL))


def _modnorm(x, g, shift, scale):
    ms = jnp.mean(x * x, axis=-1, keepdims=True)
    return (x * lax.rsqrt(ms + EPS) * g) * (1.0 + scale) + shift


def _mod_row(n_ctx_tiles, tiles_per_latent):
    def f(i):
        return jnp.where(i < n_ctx_tiles, 0, 1 + (i - n_ctx_tiles) // tiles_per_latent)
    return f


def _row_specs(n_parts, tm, width, n_ctx_tiles):
    if n_parts == 1:
        return [pl.BlockSpec((tm, width), lambda i, *_: (i, 0))]
    return [pl.BlockSpec((tm, width), lambda i, *_: (jnp.minimum(i, n_ctx_tiles - 1), 0)),
            pl.BlockSpec((tm, width), lambda i, *_: (jnp.maximum(i - n_ctx_tiles, 0), 0))]


def _pick(is_ctx, refs):
    if len(refs) == 1:
        return refs[0][...]
    return jnp.where(is_ctx, refs[0][...], refs[1][...])


def _inproj_body(n_x, n_prev, n_ctx_tiles, ctx_len, *refs):
    x_refs = refs[:n_x]
    mod_ref, g_ref, w_ref = refs[n_x:n_x + 3]
    prev_refs = refs[n_x + 3:n_x + 3 + 2 * n_prev]
    oa_ref, oh_ref, k_ref, v_ref = refs[-4:]
    is_ctx = pl.program_id(0) < n_ctx_tiles
    x = _pick(is_ctx, x_refs)
    h = _modnorm(x, g_ref[...], mod_ref[0:1, :], mod_ref[1:2, :])
    proj = _dot(h.astype(BF16), w_ref[...])
    oa_ref[...] = proj[:, :PROJ_A_WIDTH].astype(BF16)
    oh_ref[...] = proj[:, PROJ_A_WIDTH:]

    @pl.when(is_ctx)
    def _():
        per_layer = ctx_len * DA_HEADS
        for s in range(proj.shape[0] // ctx_len):
            base = s * (n_prev + 1) * per_layer
            for l in range(n_prev):
                dst = slice(base + l * per_layer, base + (l + 1) * per_layer)
                src = slice(s * per_layer, (s + 1) * per_layer)
                k_ref[dst, :] = prev_refs[2 * l][src, :]
                v_ref[dst, :] = prev_refs[2 * l + 1][src, :]
            tok = slice(s * ctx_len, (s + 1) * ctx_len)
            for h in range(DA_HEADS):
                rows = pl.ds(base + n_prev * per_layer + h, ctx_len, stride=DA_HEADS)
                k_ref[rows, :] = proj[tok, COL_K + h * DA_VDIM:COL_K + (h + 1) * DA_VDIM]
                v_ref[rows, :] = proj[tok, COL_V + h * DA_VDIM:COL_V + (h + 1) * DA_VDIM]


def _inproj(xs, mod, g, w, prev_caches, n_ctx, ctx_len, lat_len):
    t = sum(x.shape[0] for x in xs)
    tm = TOKEN_TILE
    seqs = tm // ctx_len
    n_ctx_tiles = n_ctx // seqs
    n_prev = len(prev_caches) // 2
    row = _mod_row(n_ctx_tiles, lat_len // tm)
    tile_rows = seqs * ctx_len * DA_HEADS
    tile_block = lambda i: (jnp.minimum(i, n_ctx_tiles - 1), 0)
    cache_spec = pl.BlockSpec(((n_prev + 1) * tile_rows, DA_VDIM), tile_block)
    cache_shape = jax.ShapeDtypeStruct((n_ctx_tiles * (n_prev + 1) * tile_rows, DA_VDIM), F32)
    in_specs = _row_specs(len(xs), tm, D_MODEL, n_ctx_tiles) + [
        pl.BlockSpec((None, 6, D_MODEL), lambda i: (row(i), 0, 0)),
        pl.BlockSpec((1, D_MODEL), lambda i: (0, 0)),
        pl.BlockSpec((D_MODEL, PROJ_A_WIDTH + PROJ_H_WIDTH), lambda i: (0, 0)),
    ] + [pl.BlockSpec((tile_rows, DA_VDIM), tile_block)] * (2 * n_prev)
    return pl.pallas_call(
        functools.partial(_inproj_body, len(xs), n_prev, n_ctx_tiles, ctx_len),
        grid=(t // tm,),
        in_specs=in_specs,
        out_specs=[pl.BlockSpec((tm, PROJ_A_WIDTH), lambda i: (i, 0)),
                   pl.BlockSpec((tm, PROJ_H_WIDTH), lambda i: (i, 0)), cache_spec, cache_spec],
        out_shape=[jax.ShapeDtypeStruct((t, PROJ_A_WIDTH), BF16), jax.ShapeDtypeStruct((t, PROJ_H_WIDTH), F32),
                   cache_shape, cache_shape],
        compiler_params=_params(1),
        name="inproj",
    )(*xs, mod, g, w, *prev_caches)


def _fourier_rows(za, zb, cl, sl, w_f):
    y = _dot(cl, za) - _dot(sl, zb)
    return _dot(y.astype(BF16), w_f)


def _fourier_body(u_ref, cl_ref, sl_ref, cc_ref, sc_ref, w_ref, o_ref):
    z = u_ref[...].astype(BF16)
    za = _dot(z, cc_ref[...]).astype(BF16)
    zb = _dot(z, sc_ref[...]).astype(BF16)
    o_ref[...] = _fourier_rows(za, zb, cl_ref[...], sl_ref[...], w_ref[...]).astype(o_ref.dtype)


def _dft_tables(n, block):
    i = np.arange(n)
    prod = (i[:, None] % block) * (i[None, :] % block) % block
    ang = prod.astype(np.float64) * (2.0 * math.pi / block)
    same = (i[:, None] // block) == (i[None, :] // block)
    scale = 1.0 / math.sqrt(block)
    c = np.where(same, np.cos(ang) * scale, 0.0).astype(np.float32)
    s = np.where(same, np.sin(ang) * scale, 0.0).astype(np.float32)
    return jnp.asarray(c).astype(BF16), jnp.asarray(s).astype(BF16)


def _fourier(proj, row0, n_seq, seq_len, w_f):
    cl, sl = _dft_tables(seq_len, seq_len)
    cc, sc = _dft_tables(FN_WIDTH, FN_GROUP_DIM)
    blk0 = row0 // seq_len
    full = lambda shape: pl.BlockSpec(shape, lambda i: (0, 0))
    return pl.pallas_call(
        _fourier_body,
        grid=(n_seq,),
        in_specs=[
            pl.BlockSpec((seq_len, FN_WIDTH), lambda i: (blk0 + i, COL_FN // FN_WIDTH)),
            full((seq_len, seq_len)), full((seq_len, seq_len)),
            full((FN_WIDTH, FN_WIDTH)), full((FN_WIDTH, FN_WIDTH)), full((FN_WIDTH, FN_WIDTH)),
        ],
        out_specs=pl.BlockSpec((seq_len, FN_WIDTH), lambda i: (i, 0)),
        out_shape=jax.ShapeDtypeStruct((n_seq * seq_len, FN_WIDTH), BF16),
        compiler_params=_params(1),
        name=f"fourier_{seq_len}",
    )(proj, cl, sl, cc, sc, w_f)


def _lambda_full(lmb, lam_init):
    a = jnp.sum(lmb[0:1, :] * lmb[1:2, :], axis=-1, keepdims=True)
    b = jnp.sum(lmb[2:3, :] * lmb[3:4, :], axis=-1, keepdims=True)
    return jnp.exp(a) - jnp.exp(b) + lam_init


def _diff_head(q, ks, vs, lam, g, lam_init):
    vas = [jnp.concatenate([v.astype(BF16), jnp.ones(v.shape, BF16)], axis=1) for v in vs]
    outs = []
    for m in range(2):
        qm = q[:, m * DA_HALF:(m + 1) * DA_HALF].astype(BF16)
        parts = [_dot_nt(qm, k[:, m * DA_HALF:(m + 1) * DA_HALF].astype(BF16)) for k in ks]
        mx = parts[0].max(axis=-1, keepdims=True)
        for p in parts[1:]:
            mx = jnp.maximum(mx, p.max(axis=-1, keepdims=True))
        oa = _dot(jnp.exp(parts[0] - mx).astype(BF16), vas[0])
        for p, va in zip(parts[1:], vas[1:]):
            oa = oa + _dot(jnp.exp(p - mx).astype(BF16), va)
        outs.append(oa[:, :DA_VDIM] * (1.0 / oa[:, DA_VDIM:DA_VDIM + 1]))
    a = outs[0] - lam * outs[1]
    ms = jnp.mean(a * a, axis=-1, keepdims=True)
    return a * lax.rsqrt(ms + EPS) * g * (1.0 - lam_init)


def _rope(x, cos, sin):
    lane = lax.broadcasted_iota(jnp.int32, x.shape, 1)
    first = ((lane >> 4) & 1) == 0
    rot = jnp.where(first, -pltpu.roll(x, 128 - DA_HALF // 4, 1), pltpu.roll(x, DA_HALF // 4, 1))
    return x * cos + rot * sin


def _attn_lat_body(lam_init, q_ref, k_ref, v_ref, kc_ref, vc_ref, cq_ref, sq_ref, ck_ref, sk_ref,
                   lmb_ref, g_ref, o_ref):
    lam = _lambda_full(lmb_ref[...], lam_init)
    scale = DA_HALF ** -0.5
    for h in range(DA_HEADS):
        sl = slice(h * DA_VDIM, (h + 1) * DA_VDIM)
        q = _rope(q_ref[:, sl].astype(F32), cq_ref[...], sq_ref[...]) * scale
        k = _rope(k_ref[:, sl].astype(F32), ck_ref[...], sk_ref[...])
        o = _diff_head(q, [k, kc_ref[:, sl]], [v_ref[:, sl], vc_ref[:, sl]], lam, g_ref[...], lam_init)
        o_ref[:, sl] = o.astype(o_ref.dtype)


def _attn_lat(proj, row0, n_seq, seq_len, cache_k, cache_v, layer, cos, sin, lmb, g, lam_init):
    tq = seq_len
    nq = seq_len // tq
    past = cache_k.shape[2]
    qb0 = row0 // tq
    kb0 = row0 // seq_len
    cache_spec = pl.BlockSpec((None, None, past, DA_WIDTH), lambda b, j: (b, layer, 0, 0))
    kv_spec = lambda c: pl.BlockSpec((seq_len, DA_WIDTH), lambda b, j: (kb0 + b, c))
    return pl.pallas_call(
        functools.partial(_attn_lat_body, lam_init),
        grid=(n_seq, nq),
        in_specs=[
            pl.BlockSpec((tq, DA_WIDTH), lambda b, j: (qb0 + b * nq + j, COL_Q // DA_WIDTH)),
            kv_spec(COL_K // DA_WIDTH), kv_spec(COL_V // DA_WIDTH),
            cache_spec, cache_spec,
            pl.BlockSpec((tq, DA_VDIM), lambda b, j: (j, 0)),
            pl.BlockSpec((tq, DA_VDIM), lambda b, j: (j, 0)),
            pl.BlockSpec((seq_len, DA_VDIM), lambda b, j: (0, 0)),
            pl.BlockSpec((seq_len, DA_VDIM), lambda b, j: (0, 0)),
            pl.BlockSpec((4, DA_HALF), lambda b, j: (0, 0)),
            pl.BlockSpec((1, DA_VDIM), lambda b, j: (0, 0)),
        ],
        out_specs=pl.BlockSpec((tq, DA_WIDTH), lambda b, j: (b * nq + j, 0)),
        out_shape=jax.ShapeDtypeStruct((n_seq * seq_len, DA_WIDTH), BF16),
        compiler_params=_params(2),
        name="attn_lat",
    )(proj, proj, proj, cache_k, cache_v, cos, sin, cos, sin, lmb, g)


def _rope_tables(n_tokens):
    rows = n_tokens // GRID_W
    row = np.repeat(np.arange(rows, dtype=np.float64), GRID_W)
    col = np.tile(np.arange(GRID_W, dtype=np.float64), rows)
    axis_dim = DA_HALF // 2
    inv_freq = ROPE_THETA ** (-np.arange(0, axis_dim, 2, dtype=np.float64) / axis_dim)
    ang_r = row[:, None] * inv_freq[None, :]
    ang_c = col[:, None] * inv_freq[None, :]
    ang = np.concatenate([ang_r, ang_r, ang_c, ang_c] * 2, axis=-1)
    return jnp.asarray(np.cos(ang).astype(np.float32)), jnp.asarray(np.sin(ang).astype(np.float32))


def _hgrn_tables():
    gl = CHUNKS_PER_GROUP * CHUNK
    t = np.arange(gl)
    same = (t[:, None] // CHUNK) == (t[None, :] // CHUNK)
    fwd = same & (t[None, :] <= t[:, None])
    bwd = same & (t[None, :] >= t[:, None])
    f = np.arange(HG_WIDTH)
    bd = (f[:, None] // HG_DK) == (f[None, :] // HG_DK)
    tri = jnp.asarray(np.stack([fwd, bwd]).astype(np.float32)).astype(BF16)
    causal = jnp.asarray(np.stack([fwd, bwd]).astype(np.float32))
    return tri, causal, jnp.asarray(bd.astype(np.float32))


def _hgrn2_body(n_seq, groups_per_seq, has_s0, *refs):
    (hq_ref, hi_ref, hf_ref, hb_ref, hg_ref, lb_ref, ng_ref, tri_ref, causal_ref, bd_ref) = refs[:10]
    if has_s0:
        s0_ref, o_ref, oi_scr, qe_scr, u_scr, st_scr, dec_scr = refs[10:]
        so_ref = None
    else:
        o_ref, so_ref, oi_scr, qe_scr, u_scr, st_scr, dec_scr = refs[10:]
        s0_ref = None
    c = CHUNK
    w = HG_WIDTH
    gc = CHUNKS_PER_GROUP
    gl = gc * c
    n_groups = n_seq * groups_per_seq
    chunks_per_seq = groups_per_seq * gc
    lane = lax.broadcasted_iota(jnp.int32, (1, w), 1)
    dk_bits = HG_DK.bit_length() - 1
    head_masks_b = [((lane >> dk_bits) == h).astype(BF16) for h in range(HG_HEADS)]
    bd = bd_ref[...] > 0.5
    edge_rows = (c - 1, 0)
    fp_refs = (hf_ref, hb_ref)

    def per_chunk_row(x, row):
        return jnp.concatenate([jnp.broadcast_to(x[k * c + row:k * c + row + 1, :], (c, w)) for k in range(gc)],
                               axis=0)

    def group_terms(gi):
        rows = pl.ds(gi * gl if isinstance(gi, int) else pl.multiple_of(gi * gl, gl), gl)
        hq = hq_ref[rows, :]
        q = _silu_t(hq, HG_DK ** -0.5)
        v = hi_ref[rows, :].astype(BF16)
        v_stack = jnp.concatenate([v * hm for hm in head_masks_b], axis=0)
        for d in range(2):
            lb = lb_ref[d:d + 1, :]
            fp = fp_refs[d][rows, :]
            lsig = jnp.minimum(fp, 0.0) - jnp.log(1.0 + jnp.exp(-jnp.abs(fp)))
            la = jnp.log(lb)
            lbb = jnp.log(1.0 - lb) + lsig
            logf = jnp.maximum(la, lbb) + jnp.log(1.0 + jnp.exp(-jnp.abs(la - lbb)))
            kk = (0.5 * (1.0 - lb)) * (1.0 - jnp.tanh(0.5 * fp))
            g_hi, g_lo = _split(logf, 2)
            cum = _dot(tri_ref[d], g_hi) + _dot(tri_ref[d], g_lo)
            total = per_chunk_row(cum, edge_rows[d])
            ref = per_chunk_row(cum, c // 2)
            qc = (q * jnp.exp(cum - ref)).astype(BF16)
            kc = (kk * jnp.exp(ref - cum)).astype(BF16)
            ke = (kk * jnp.exp(total - cum)).astype(BF16)
            qe_scr[d, rows, :] = (q * jnp.exp(cum)).astype(BF16)
            kc_stack = jnp.concatenate([kc * hm for hm in head_masks_b], axis=0)
            a = _dot_nt(qc, kc_stack)
            keep = causal_ref[d] > 0.5
            a = jnp.concatenate([jnp.where(keep, a[:, h * gl:(h + 1) * gl], 0.0) for h in range(HG_HEADS)],
                                axis=1).astype(BF16)
            oi_scr[d, rows, :] = _dot(a, v_stack)
            for k in range(gc):
                ck = slice(k * c, (k + 1) * c)
                i = gi * gc + k
                dec_scr[d, pl.ds(i, 1), :] = jnp.exp(cum[k * c + edge_rows[d]:k * c + edge_rows[d] + 1, :])
                u_scr[d, i] = jnp.where(bd, _dot_tn(v[ck, :], ke[ck, :]), 0.0)

    if n_groups <= 4:
        for gi in range(n_groups):
            group_terms(gi)
    else:
        def terms_step(gi, carry):
            group_terms(gi)
            return carry

        lax.fori_loop(0, n_groups, terms_step, 0)

    slab = 32
    for s in range(n_seq):
        first = s * chunks_per_seq
        for d in range(2):
            order = range(chunks_per_seq) if d == 0 else range(chunks_per_seq - 1, -1, -1)
            for r in range(w // slab):
                rs = slice(r * slab, (r + 1) * slab)
                st = s0_ref[s, d, rs, :] if has_s0 else jnp.zeros((slab, w), F32)
                for j in order:
                    i = first + j
                    st_scr[d, i, rs, :] = st.astype(BF16)
                    st = st * dec_scr[d, i:i + 1, :] + u_scr[d, i, rs, :]
                if so_ref is not None:
                    h = (r * slab) // HG_DK
                    off = (r * slab) % HG_DK
                    so_ref[s, d, h, off:off + slab, :] = st[:, h * HG_DK:(h + 1) * HG_DK]

    ones_bd = bd.astype(BF16)

    def finish(gi):
        rows = pl.ds(gi * gl if isinstance(gi, int) else pl.multiple_of(gi * gl, gl), gl)
        inter = []
        for k in range(gc):
            i = gi * gc + k
            ck = pl.ds(gi * gl + k * c if isinstance(gi, int) else pl.multiple_of(gi * gl + k * c, c), c)
            inter.append(_dot_nt(qe_scr[0, ck, :], st_scr[0, i]) + _dot_nt(qe_scr[1, ck, :], st_scr[1, i]))
        o = oi_scr[0, rows, :] + oi_scr[1, rows, :] + jnp.concatenate(inter, axis=0)
        sq = _split(o * o, 2)
        ms = (_dot(sq[0], ones_bd) + _dot(sq[1], ones_bd)) * (1.0 / HG_DK)
        y = o * lax.rsqrt(ms + EPS) * ng_ref[...]
        hg = hg_ref[rows, :]
        o_ref[rows, :] = (y * _silu_t(hg)).astype(o_ref.dtype)

    if n_groups <= 4:
        for gi in range(n_groups):
            finish(gi)
    else:
        def finish_step(gi, carry):
            finish(gi)
            return carry

        lax.fori_loop(0, n_groups, finish_step, 0)


def _hgrn_scratch(rows):
    n_chunks = rows // CHUNK
    return [pltpu.VMEM((2, rows, HG_WIDTH), F32),
            pltpu.VMEM((2, rows, HG_WIDTH), BF16),
            pltpu.VMEM((2, n_chunks, HG_WIDTH, HG_WIDTH), F32),
            pltpu.VMEM((2, n_chunks, HG_WIDTH, HG_WIDTH), BF16),
            pltpu.VMEM((2, max(n_chunks, 8), HG_WIDTH), F32)]


def _hgrn2(proj_h, row0, n_seq, seq_len, seqs_per_step, lb, ng, s0t):
    gl = CHUNKS_PER_GROUP * CHUNK
    groups_per_seq = seq_len // gl
    rows = seqs_per_step * seq_len
    blk0 = row0 // rows
    tri, causal, bd = _hgrn_tables()
    col = lambda c0: pl.BlockSpec((rows, HG_WIDTH), lambda i: (blk0 + i, c0 // HG_WIDTH))
    const = lambda shape: pl.BlockSpec(shape, lambda i: (0,) * len(shape))
    in_specs = [col(COL_HQ), col(COL_HI), col(COL_HF), col(COL_HB), col(COL_HG),
                const((2, HG_WIDTH)), const((1, HG_WIDTH)),
                const((2, gl, gl)), const((2, gl, gl)), const((HG_WIDTH, HG_WIDTH))]
    args = [proj_h] * 5 + [lb, ng, tri, causal, bd]
    out_specs = [pl.BlockSpec((rows, HG_WIDTH), lambda i: (i, 0))]
    out_shape = [jax.ShapeDtypeStruct((n_seq * seq_len, HG_WIDTH), BF16)]
    if s0t is not None:
        in_specs.append(pl.BlockSpec((seqs_per_step, 2, HG_WIDTH, HG_WIDTH), lambda i: (i, 0, 0, 0)))
        args.append(s0t)
    else:
        out_specs.append(pl.BlockSpec((seqs_per_step, 2, HG_HEADS, HG_DK, HG_DK), lambda i: (i, 0, 0, 0, 0)))
        out_shape.append(jax.ShapeDtypeStruct((n_seq, 2, HG_HEADS, HG_DK, HG_DK), F32))
    return pl.pallas_call(
        functools.partial(_hgrn2_body, seqs_per_step, groups_per_seq, s0t is not None),
        grid=(n_seq // seqs_per_step,),
        in_specs=in_specs,
        out_specs=out_specs,
        out_shape=out_shape,
        scratch_shapes=_hgrn_scratch(rows),
        compiler_params=_params(1),
        name=f"hgrn_{seq_len}",
    )(*args)


def _ctx_mixers_body(n_seq, seq_len, lam_init, *refs):
    (q_ref, k_ref, v_ref, u_ref, hq_ref, hi_ref, hf_ref, hb_ref, hg_ref,
     cl_ref, sl_ref, cc_ref, sc_ref, wf_ref, lmb_ref, ag_ref,
     lb_ref, ng_ref, tri_ref, causal_ref, bd_ref,
     fn_ref, a_ref, ho_ref, so_ref) = refs[:25]
    scratch = refs[25:]

    z = u_ref[...].astype(BF16)
    za = _dot(z, cc_ref[...]).astype(BF16)
    zb = _dot(z, sc_ref[...]).astype(BF16)
    for s in range(n_seq):
        rs = slice(s * seq_len, (s + 1) * seq_len)
        fn_ref[rs, :] = _fourier_rows(za[rs, :], zb[rs, :], cl_ref[...], sl_ref[...],
                                      wf_ref[...]).astype(fn_ref.dtype)

    lam = _lambda_full(lmb_ref[...], lam_init)
    scale = DA_HALF ** -0.5
    for s in range(n_seq):
        rs = slice(s * seq_len, (s + 1) * seq_len)
        for h in range(DA_HEADS):
            sl = slice(h * DA_VDIM, (h + 1) * DA_VDIM)
            o = _diff_head(q_ref[rs, sl] * scale, [k_ref[rs, sl]], [v_ref[rs, sl]], lam, ag_ref[...], lam_init)
            a_ref[rs, sl] = o.astype(a_ref.dtype)

    _hgrn2_body(n_seq, seq_len // (CHUNKS_PER_GROUP * CHUNK), False,
                hq_ref, hi_ref, hf_ref, hb_ref, hg_ref, lb_ref, ng_ref, tri_ref, causal_ref, bd_ref,
                ho_ref, so_ref, *scratch)


def _ctx_mixers(proj_a, proj_h, n_seq, seq_len, seqs_per_step, w_f, lmb, ag, lam_init, lb, ng):
    rows = seqs_per_step * seq_len
    gl = CHUNKS_PER_GROUP * CHUNK
    cl, sl = _dft_tables(seq_len, seq_len)
    cc, sc = _dft_tables(FN_WIDTH, FN_GROUP_DIM)
    tri, causal, bd = _hgrn_tables()
    col = lambda c0, width: pl.BlockSpec((rows, width), lambda i: (i, c0 // width))
    const = lambda shape: pl.BlockSpec(shape, lambda i: (0,) * len(shape))
    out_rows = lambda width: pl.BlockSpec((rows, width), lambda i: (i, 0))
    n_tok = n_seq * seq_len
    return pl.pallas_call(
        functools.partial(_ctx_mixers_body, seqs_per_step, seq_len, lam_init),
        grid=(n_seq // seqs_per_step,),
        in_specs=[col(COL_Q, DA_WIDTH), col(COL_K, DA_WIDTH), col(COL_V, DA_WIDTH), col(COL_FN, FN_WIDTH),
                  col(COL_HQ, HG_WIDTH), col(COL_HI, HG_WIDTH), col(COL_HF, HG_WIDTH), col(COL_HB, HG_WIDTH),
                  col(COL_HG, HG_WIDTH),
                  const((seq_len, seq_len)), const((seq_len, seq_len)),
                  const((FN_WIDTH, FN_WIDTH)), const((FN_WIDTH, FN_WIDTH)), const((FN_WIDTH, FN_WIDTH)),
                  const((4, DA_HALF)), const((1, DA_VDIM)),
                  const((2, HG_WIDTH)), const((1, HG_WIDTH)),
                  const((2, gl, gl)), const((2, gl, gl)), const((HG_WIDTH, HG_WIDTH))],
        out_specs=[out_rows(FN_WIDTH), out_rows(DA_WIDTH), out_rows(HG_WIDTH),
                   pl.BlockSpec((seqs_per_step, 2, HG_HEADS, HG_DK, HG_DK), lambda i: (i, 0, 0, 0, 0))],
        out_shape=[jax.ShapeDtypeStruct((n_tok, FN_WIDTH), BF16),
                   jax.ShapeDtypeStruct((n_tok, DA_WIDTH), BF16),
                   jax.ShapeDtypeStruct((n_tok, HG_WIDTH), BF16),
                   jax.ShapeDtypeStruct((n_seq, 2, HG_HEADS, HG_DK, HG_DK), F32)],
        scratch_shapes=_hgrn_scratch(rows),
        compiler_params=_params(1),
        name="ctx_mixers",
    )(*([proj_a] * 4 + [proj_h] * 5), cl, sl, cc, sc, w_f, lmb, ag, lb, ng, tri, causal, bd)


def _route_gates(lt, bias):
    per = N_EXPERTS // N_GROUPS
    tt = lt.shape[1]
    neg = -jnp.inf
    assert per == N_GROUPS == 8
    gi = lax.broadcasted_iota(jnp.int32, (N_GROUPS, tt), 0).astype(F32)
    s_j, b_j = [], []
    for j in range(per):
        s = _sigmoid(lt[j * N_GROUPS:(j + 1) * N_GROUPS, :])
        s_j.append(s)
        b_j.append(s + bias[j * N_GROUPS:(j + 1) * N_GROUPS, :])
    m1 = functools.reduce(jnp.maximum, b_j)
    i1 = functools.reduce(jnp.minimum, [jnp.where(b_j[j] == m1, float(j), float(per)) for j in range(per)])
    m2 = functools.reduce(jnp.maximum, [jnp.where(i1 == float(j), neg, b_j[j]) for j in range(per)])
    gs = m1 + m2
    gsel = jnp.zeros((N_GROUPS, tt), jnp.bool_)
    for _ in range(TOPK_GROUPS):
        m = gs.max(axis=0, keepdims=True)
        idx = jnp.where(gs == m, gi, float(N_GROUPS)).min(axis=0, keepdims=True)
        hit = gi == idx
        gsel = gsel | hit
        gs = jnp.where(hit, neg, gs)
    x_j = [jnp.where(gsel, b_j[j], neg) for j in range(per)]
    e_j = [gi * per + j for j in range(per)]
    sel_j = [jnp.zeros((N_GROUPS, tt), jnp.bool_) for _ in range(per)]
    for _ in range(TOP_K):
        m = functools.reduce(jnp.maximum, x_j).max(axis=0, keepdims=True)
        idx = functools.reduce(jnp.minimum, [jnp.where(x_j[j] == m, e_j[j], float(N_EXPERTS))
                                             for j in range(per)]).min(axis=0, keepdims=True)
        for j in range(per):
            hit = e_j[j] == idx
            sel_j[j] = sel_j[j] | hit
            x_j[j] = jnp.where(hit, neg, x_j[j])
    w_j = [jnp.where(sel_j[j], s_j[j], 0.0) for j in range(per)]
    denom = functools.reduce(lambda a, b: a + b, w_j).sum(axis=0, keepdims=True)
    gates_t = jnp.concatenate([w / denom * ROUTED_SCALE for w in w_j], axis=0)
    r_io = lax.broadcasted_iota(jnp.int32, (N_EXPERTS, N_EXPERTS), 0)
    e_io = lax.broadcasted_iota(jnp.int32, (N_EXPERTS, N_EXPERTS), 1)
    eye = (e_io == (r_io & (N_GROUPS - 1)) * per + (r_io >> 3)).astype(BF16)
    p = _split(gates_t, 3)
    return _dot_tn(p[0], eye) + _dot_tn(p[1], eye) + _dot_tn(p[2], eye)


def _outproj_body(n_x, n_ctx_tiles, *refs):
    fn_refs, a_refs, hg_refs = refs[0:2], refs[2:4], refs[4:6]
    x_refs = refs[6:6 + n_x]
    mod_ref, g_ref, w_ref, wr_ref, rb_ref, xo_ref, h_ref, gate_ref = refs[6 + n_x:]
    is_ctx = pl.program_id(0) < n_ctx_tiles
    mix = (_dot(_pick(is_ctx, fn_refs), w_ref[0:FN_WIDTH, :])
           + _dot(_pick(is_ctx, a_refs), w_ref[FN_WIDTH:FN_WIDTH + DA_WIDTH, :])
           + _dot(_pick(is_ctx, hg_refs), w_ref[FN_WIDTH + DA_WIDTH:, :]))
    x = _pick(is_ctx, x_refs) + mod_ref[2:3, :] * mix
    xo_ref[...] = x
    h = _modnorm(x, g_ref[...], mod_ref[3:4, :], mod_ref[4:5, :])
    h_ref[...] = h.astype(BF16)
    h_hi, h_lo = _split(h, 2)
    w_hi, w_lo = _split(wr_ref[...], 2)
    logits_t = _dot_nt(w_hi, h_hi) + _dot_nt(w_lo, h_hi) + _dot_nt(w_hi, h_lo)
    gate_ref[...] = _route_gates(logits_t, rb_ref[...])


def _outproj(fn, a, hg, xs, mod, g, w_out, w_router_t, router_bias, n_ctx_tok, lat_len):
    t = sum(x.shape[0] for x in xs)
    tm = TOKEN_TILE
    n_ctx_tiles = n_ctx_tok // tm
    row = _mod_row(n_ctx_tiles, lat_len // tm)
    rows = lambda width: pl.BlockSpec((tm, width), lambda i: (i, 0))
    parts = lambda n, width: _row_specs(n, tm, width, n_ctx_tiles)
    return pl.pallas_call(
        functools.partial(_outproj_body, len(xs), n_ctx_tiles),
        grid=(t // tm,),
        in_specs=parts(2, FN_WIDTH) + parts(2, DA_WIDTH) + parts(2, HG_WIDTH) + parts(len(xs), D_MODEL) + [
            pl.BlockSpec((None, 6, D_MODEL), lambda i: (row(i), 0, 0)),
            pl.BlockSpec((1, D_MODEL), lambda i: (0, 0)),
            pl.BlockSpec((D_MODEL, D_MODEL), lambda i: (0, 0)),
            pl.BlockSpec((N_EXPERTS, D_MODEL), lambda i: (0, 0)),
            pl.BlockSpec((N_EXPERTS, 1), lambda i: (0, 0)),
        ],
        out_specs=[rows(D_MODEL), rows(D_MODEL), rows(N_EXPERTS)],
        out_shape=[jax.ShapeDtypeStruct((t, D_MODEL), F32),
                   jax.ShapeDtypeStruct((t, D_MODEL), BF16),
                   jax.ShapeDtypeStruct((t, N_EXPERTS), F32)],
        compiler_params=_params(1),
        name="outproj",
    )(*fn, *a, *hg, *xs, mod, g, w_out, w_router_t, router_bias)


def _moe_body(final_norm, n_ctx_tiles, h_ref, gate_ref, wg_ref, wu_ref, wd_ref, sg_ref, su_ref, sd_ref,
              x_ref, mod_ref, fg_ref, *out_and_scratch):
    acc_ref = out_and_scratch[-1]
    o_refs = out_and_scratch[:-1]
    is_ctx = pl.program_id(0) < n_ctx_tiles
    j = pl.program_id(1)
    h = h_ref[...]

    def act_of(wg, wu, gate):
        gu = _dot(h, jnp.concatenate([wg.astype(BF16), wu.astype(BF16)], axis=1))
        a = _silu_t(gu[:, :D_EXPERT]) * gu[:, D_EXPERT:]
        return a if gate is None else a * gate

    @pl.when(j == 0)
    def _():
        acc_ref[...] = _dot(act_of(sg_ref[...], su_ref[...], None).astype(BF16), sd_ref[...].astype(BF16))

    gates = gate_ref[...]
    expert_of_lane = lax.broadcasted_iota(jnp.int32, gates.shape, 1)

    def gate_col(p):
        e = j * EXPERTS_PER_STEP + p
        return jnp.sum(jnp.where(expert_of_lane == e, gates, 0.0), axis=1, keepdims=True)

    acts = [act_of(wg_ref[p], wu_ref[p], gate_col(p)).astype(BF16) for p in range(EXPERTS_PER_STEP)]
    wd = wd_ref[...].astype(BF16).reshape(EXPERTS_PER_STEP * D_EXPERT, D_MODEL)
    acc_ref[...] += _dot(jnp.concatenate(acts, axis=1), wd)

    @pl.when(j == pl.num_programs(1) - 1)
    def _():
        x = x_ref[...] + mod_ref[5:6, :] * acc_ref[...]
        if not final_norm:
            o_refs[0][...] = x
        else:
            ms = jnp.mean(x * x, axis=-1, keepdims=True)
            y = x * lax.rsqrt(ms + EPS) * fg_ref[...]

            @pl.when(is_ctx)
            def _():
                o_refs[0][...] = y

            @pl.when(jnp.logical_not(is_ctx))
            def _():
                o_refs[1][...] = y


def _moe(h, gates, w_gate, w_up, w_down2, ws_gate, ws_up, ws_down, layer, x, mod, final_g, final_norm,
         n_ctx_tok, lat_len):
    t = x.shape[0]
    tm = MOE_TILE
    eps_ = EXPERTS_PER_STEP
    n_ctx_tiles = n_ctx_tok // tm
    row = _mod_row(n_ctx_tiles, lat_len // tm)
    once = pl.Buffered(1)
    if final_norm:
        last_ctx = n_ctx_tiles - 1
        out_specs = [pl.BlockSpec((tm, D_MODEL), lambda i, j: (jnp.minimum(i, last_ctx), 0), pipeline_mode=once),
                     pl.BlockSpec((tm, D_MODEL), lambda i, j: (jnp.maximum(i - n_ctx_tiles, 0), 0),
                                  pipeline_mode=once)]
        out_shape = [jax.ShapeDtypeStruct((n_ctx_tok, D_MODEL), F32),
                     jax.ShapeDtypeStruct((t - n_ctx_tok, D_MODEL), F32)]
    else:
        out_specs = [pl.BlockSpec((tm, D_MODEL), lambda i, j: (i, 0), pipeline_mode=once)]
        out_shape = [jax.ShapeDtypeStruct((t, D_MODEL), F32)]
    return pl.pallas_call(
        functools.partial(_moe_body, final_norm, n_ctx_tiles),
        grid=(t // tm, N_EXPERTS // eps_),
        in_specs=[
            pl.BlockSpec((tm, D_MODEL), lambda i, j: (i, 0)),
            pl.BlockSpec((tm, N_EXPERTS), lambda i, j: (i, 0)),
            pl.BlockSpec((None, eps_, D_MODEL, D_EXPERT), lambda i, j: (layer, j, 0, 0)),
            pl.BlockSpec((None, eps_, D_MODEL, D_EXPERT), lambda i, j: (layer, j, 0, 0)),
            pl.BlockSpec((None, eps_ // 2, 2 * D_EXPERT, D_MODEL), lambda i, j: (layer, j, 0, 0)),
            pl.BlockSpec((None, D_MODEL, D_EXPERT), lambda i, j: (layer, 0, 0), pipeline_mode=once),
            pl.BlockSpec((None, D_MODEL, D_EXPERT), lambda i, j: (layer, 0, 0), pipeline_mode=once),
            pl.BlockSpec((None, D_EXPERT, D_MODEL), lambda i, j: (layer, 0, 0), pipeline_mode=once),
            pl.BlockSpec((tm, D_MODEL), lambda i, j: (i, 0), pipeline_mode=once),
            pl.BlockSpec((None, 6, D_MODEL), lambda i, j: (row(i), 0, 0)),
            pl.BlockSpec((1, D_MODEL), lambda i, j: (0, 0)),
        ],
        out_specs=out_specs,
        out_shape=out_shape,
        scratch_shapes=[pltpu.VMEM((tm, D_MODEL), F32)],
        compiler_params=_params(2),
        name="moe",
    )(h, gates, w_gate, w_up, w_down2, ws_gate, ws_up, ws_down, x, mod, final_g)


def _block_diag_t(s):
    eye = jnp.eye(HG_HEADS, dtype=s.dtype)
    out = jnp.einsum('...hkv,hg->...hvgk', s, eye)
    return out.reshape(s.shape[:-3] + (HG_WIDTH, HG_WIDTH))


def kernel(x_prompt, x_sample, cache_k, cache_v, state_hgrn, c, c_ctx, w_ada, b_ada, norm_g, w_in,
           w_fourier, lambdas, attn_norm_g, lower_bounds, hg_norm_g, w_out, w_router, router_bias,
           w_gate, w_up, w_down, ws_gate, ws_up, ws_down, final_g):
    n_ctx, ctx_len, _ = x_prompt.shape
    n_lat, lat_len, _ = x_sample.shape
    n_ctx_tok = n_ctx * ctx_len
    past = cache_k.shape[2]

    xs = (x_prompt.reshape(n_ctx_tok, D_MODEL), x_sample.reshape(n_lat * lat_len, D_MODEL))

    c8 = jnp.zeros((8, D_MODEL), F32).at[0].set(c_ctx).at[1:1 + n_lat].set(c)
    mods = _ada_mods(c8, w_ada, b_ada).reshape(DEPTH, 8, 6, D_MODEL)

    cs = jnp.cumsum(jax.nn.softmax(lower_bounds.astype(F32), axis=0), axis=0)
    lbs = cs - cs[0:1]

    cos, sin = _rope_tables(lat_len)
    cache_k4 = cache_k.reshape(n_lat, DEPTH, past, DA_WIDTH)
    cache_v4 = cache_v.reshape(n_lat, DEPTH, past, DA_WIDTH)
    s0t = _block_diag_t(state_hgrn.astype(F32))

    qkv_end = FN_WIDTH + 3 * DA_WIDTH
    w_in_b = jnp.concatenate([w_in[:, :, FN_WIDTH:qkv_end], w_in[:, :, :FN_WIDTH], w_in[:, :, qkv_end:]],
                             axis=-1).astype(BF16)
    w_f_b = w_fourier.astype(BF16)
    w_out_b = w_out.astype(BF16)
    per = N_EXPERTS // N_GROUPS
    w_router_t = (jnp.swapaxes(w_router, 1, 2).reshape(DEPTH, N_GROUPS, per, D_MODEL)
                  .swapaxes(1, 2).reshape(DEPTH, N_EXPERTS, D_MODEL))
    bias_mm = router_bias.reshape(DEPTH, N_GROUPS, per).swapaxes(1, 2).reshape(DEPTH, N_EXPERTS, 1)
    w_down2 = w_down.reshape(DEPTH, N_EXPERTS // 2, 2 * D_EXPERT, D_MODEL)
    ng = jnp.tile(hg_norm_g, (1, HG_HEADS))

    caches = []
    new_s = []
    for l in range(DEPTH):
        lam_init = 0.8 - 0.6 * math.exp(-0.3 * l)
        mod = mods[l]
        last = l == DEPTH - 1
        proj_a, proj_h, k_l, v_l = _inproj(xs, mod, norm_g[l, 0:1], w_in_b[l], tuple(caches) if last else (),
                                           n_ctx, ctx_len, lat_len)
        caches = [k_l, v_l] if last else caches + [k_l, v_l]

        ag = attn_norm_g[l].reshape(1, DA_VDIM)
        fn_ctx, a_ctx, hg_ctx, st_ctx = _ctx_mixers(proj_a, proj_h, n_ctx, ctx_len, 2, w_f_b[l], lambdas[l], ag,
                                                    lam_init, lbs[l], ng[l:l + 1])
        fn_lat = _fourier(proj_a, n_ctx_tok, n_lat, lat_len, w_f_b[l])
        a_lat = _attn_lat(proj_a, n_ctx_tok, n_lat, lat_len, cache_k4, cache_v4, l, cos, sin, lambdas[l], ag,
                          lam_init)
        (hg_lat,) = _hgrn2(proj_h, n_ctx_tok, n_lat, lat_len, 1, lbs[l], ng[l:l + 1], s0t[:, l])
        new_s.append(jnp.swapaxes(st_ctx, -1, -2))

        x, h2, gates = _outproj((fn_ctx, fn_lat), (a_ctx, a_lat), (hg_ctx, hg_lat), xs, mod,
                                norm_g[l, 1:2], w_out_b[l], w_router_t[l], bias_mm[l], n_ctx_tok, lat_len)
        xs = _moe(h2, gates, w_gate, w_up, w_down2, ws_gate, ws_up, ws_down, l, x, mod,
                  final_g.reshape(1, D_MODEL), l == DEPTH - 1, n_ctx_tok, lat_len)

    y_prompt = xs[0].reshape(n_ctx, ctx_len, D_MODEL)
    y_sample = xs[1].reshape(n_lat, lat_len, D_MODEL)
    new_k = caches[0].reshape(n_ctx, DEPTH, ctx_len, DA_HEADS, DA_VDIM)
    new_v = caches[1].reshape(n_ctx, DEPTH, ctx_len, DA_HEADS, DA_VDIM)
    return (y_prompt, y_sample, new_k, new_v, jnp.stack(new_s, axis=1))
```

```python
import functools
import math

import numpy as np
import jax
import jax.numpy as jnp
from jax import lax
from jax.experimental import pallas as pl
from jax.experimental.pallas import tpu as pltpu

F32 = jnp.float32
BF16 = jnp.bfloat16

D_MODEL = 1024
DEPTH = 2
GRID_W = 64
FN_WIDTH = 256
FN_GROUP_DIM = 64
DA_WIDTH = 512
DA_HEADS = 4
DA_VDIM = 128
DA_HALF = 64
HG_WIDTH = 256
HG_HEADS = 4
HG_DK = 64
CHUNK = 64
ROPE_THETA = 10000.0
N_EXPERTS = 64
TOP_K = 8
N_GROUPS = 8
TOPK_GROUPS = 4
D_EXPERT = 128
ROUTED_SCALE = 2.5
EPS = 1e-6

COL_Q, COL_K, COL_V, COL_FN = 0, 512, 1024, 1536
PROJ_A_WIDTH = 1792
COL_HQ, COL_HI, COL_HF, COL_HB, COL_HG = 0, 256, 512, 768, 1024
PROJ_H_WIDTH = 1280

TOKEN_TILE = 512
MOE_TILE = 1024
EXPERTS_PER_STEP = 8
CHUNKS_PER_GROUP = 4
VMEM_LIMIT = 56 * 1024 * 1024


def _dot(a, b):
    return jnp.dot(a, b, preferred_element_type=F32)


def _dot_nt(a, b):
    return lax.dot_general(a, b, (((1,), (1,)), ((), ())), preferred_element_type=F32)


def _dot_tn(a, b):
    return lax.dot_general(a, b, (((0,), (0,)), ((), ())), preferred_element_type=F32)


def _split(x, n):
    parts = []
    r = x
    for i in range(n):
        p = r.astype(BF16)
        parts.append(p)
        if i + 1 < n:
            r = r - p.astype(F32)
    return parts


def _sigmoid(x):
    return 1.0 / (1.0 + jnp.exp(-x))


def _silu_t(x, scale=1.0):
    return (0.5 * scale) * x * (jnp.tanh(0.5 * x) + 1.0)


def _silu(x):
    return x * _sigmoid(x)


def _params(n_axes):
    return pltpu.CompilerParams(dimension_semantics=("arbitrary",) * n_axes,
                                vmem_limit_bytes=VMEM_LIMIT)


def _ada_body(c_ref, w_ref, b_ref, o_ref):
    a = _silu(c_ref[...])
    a_hi, a_lo = _split(a, 2)
    w_hi, w_lo = _split(w_ref[...], 2)
    o_ref[...] = _dot(a_hi, w_hi) + _dot(a_lo, w_hi) + _dot(a_hi, w_lo) + b_ref[...]


def _ada_mods(c8, w_ada, b_ada):
    tn = 1536
    return pl.pallas_call(
        _ada_body,
        grid=(DEPTH, 6 * D_MODEL // tn),
        in_specs=[
            pl.BlockSpec((8, D_MODEL), lambda l, j: (0, 0)),
            pl.BlockSpec((None, D_MODEL, tn), lambda l, j: (l, 0, j)),
            pl.BlockSpec((None, 1, tn), lambda l, j: (l, 0, j)),
        ],
        out_specs=pl.BlockSpec((None, 8, tn), lambda l, j: (l, 0, j)),
        out_shape=jax.ShapeDtypeStruct((DEPTH, 8, 6 * D_MODEL), F32),
        compiler_params=_params(2),
        name="ada_mods",
    )(c8, w_ada, b_ada.reshape(DEPTH, 1, 6 * D_MODEL))


def _modnorm(x, g, shift, scale):
    ms = jnp.mean(x * x, axis=-1, keepdims=True)
    return (x * lax.rsqrt(ms + EPS) * g) * (1.0 + scale) + shift


def _mod_row(n_ctx_tiles, tiles_per_latent):
    def f(i):
        return jnp.where(i < n_ctx_tiles, 0, 1 + (i - n_ctx_tiles) // tiles_per_latent)
    return f


def _row_specs(n_parts, tm, width, n_ctx_tiles):
    if n_parts == 1:
        return [pl.BlockSpec((tm, width), lambda i, *_: (i, 0))]
    return [pl.BlockSpec((tm, width), lambda i, *_: (jnp.minimum(i, n_ctx_tiles - 1), 0)),
            pl.BlockSpec((tm, width), lambda i, *_: (jnp.maximum(i - n_ctx_tiles, 0), 0))]


def _pick(is_ctx, refs):
    if len(refs) == 1:
        return refs[0][...]
    return jnp.where(is_ctx, refs[0][...], refs[1][...])


def _inproj_body(n_x, n_prev, n_ctx_tiles, ctx_len, *refs):
    x_refs = refs[:n_x]
    mod_ref, g_ref, w_ref = refs[n_x:n_x + 3]
    prev_refs = refs[n_x + 3:n_x + 3 + 2 * n_prev]
    oa_ref, oh_ref, k_ref, v_ref = refs[-4:]
    is_ctx = pl.program_id(0) < n_ctx_tiles
    x = _pick(is_ctx, x_refs)
    h = _modnorm(x, g_ref[...], mod_ref[0:1, :], mod_ref[1:2, :])
    proj = _dot(h.astype(BF16), w_ref[...])
    oa_ref[...] = proj[:, :PROJ_A_WIDTH].astype(BF16)
    oh_ref[...] = proj[:, PROJ_A_WIDTH:]

    @pl.when(is_ctx)
    def _():
        per_layer = ctx_len * DA_HEADS
        for s in range(proj.shape[0] // ctx_len):
            base = s * (n_prev + 1) * per_layer
            for l in range(n_prev):
                dst = slice(base + l * per_layer, base + (l + 1) * per_layer)
                src = slice(s * per_layer, (s + 1) * per_layer)
                k_ref[dst, :] = prev_refs[2 * l][src, :]
                v_ref[dst, :] = prev_refs[2 * l + 1][src, :]
            tok = slice(s * ctx_len, (s + 1) * ctx_len)
            for h in range(DA_HEADS):
                rows = pl.ds(base + n_prev * per_layer + h, ctx_len, stride=DA_HEADS)
                k_ref[rows, :] = proj[tok, COL_K + h * DA_VDIM:COL_K + (h + 1) * DA_VDIM]
                v_ref[rows, :] = proj[tok, COL_V + h * DA_VDIM:COL_V + (h + 1) * DA_VDIM]


def _inproj(xs, mod, g, w, prev_caches, n_ctx, ctx_len, lat_len):
    t = sum(x.shape[0] for x in xs)
    tm = TOKEN_TILE
    seqs = tm // ctx_len
    n_ctx_tiles = n_ctx // seqs
    n_prev = len(prev_caches) // 2
    row = _mod_row(n_ctx_tiles, lat_len // tm)
    tile_rows = seqs * ctx_len * DA_HEADS
    tile_block = lambda i: (jnp.minimum(i, n_ctx_tiles - 1), 0)
    cache_spec = pl.BlockSpec(((n_prev + 1) * tile_rows, DA_VDIM), tile_block)
    cache_shape = jax.ShapeDtypeStruct((n_ctx_tiles * (n_prev + 1) * tile_rows, DA_VDIM), F32)
    in_specs = _row_specs(len(xs), tm, D_MODEL, n_ctx_tiles) + [
        pl.BlockSpec((None, 6, D_MODEL), lambda i: (row(i), 0, 0)),
        pl.BlockSpec((1, D_MODEL), lambda i: (0, 0)),
        pl.BlockSpec((D_MODEL, PROJ_A_WIDTH + PROJ_H_WIDTH), lambda i: (0, 0)),
    ] + [pl.BlockSpec((tile_rows, DA_VDIM), tile_block)] * (2 * n_prev)
    return pl.pallas_call(
        functools.partial(_inproj_body, len(xs), n_prev, n_ctx_tiles, ctx_len),
        grid=(t // tm,),
        in_specs=in_specs,
        out_specs=[pl.BlockSpec((tm, PROJ_A_WIDTH), lambda i: (i, 0)),
                   pl.BlockSpec((tm, PROJ_H_WIDTH), lambda i: (i, 0)), cache_spec, cache_spec],
        out_shape=[jax.ShapeDtypeStruct((t, PROJ_A_WIDTH), BF16), jax.ShapeDtypeStruct((t, PROJ_H_WIDTH), F32),
                   cache_shape, cache_shape],
        compiler_params=_params(1),
        name="inproj",
    )(*xs, mod, g, w, *prev_caches)


def _fourier_rows(za, zb, cl, sl, w_f):
    y = _dot(cl, za) - _dot(sl, zb)
    return _dot(y.astype(BF16), w_f)


def _fourier_body(u_ref, cl_ref, sl_ref, cc_ref, sc_ref, w_ref, o_ref):
    z = u_ref[...].astype(BF16)
    za = _dot(z, cc_ref[...]).astype(BF16)
    zb = _dot(z, sc_ref[...]).astype(BF16)
    o_ref[...] = _fourier_rows(za, zb, cl_ref[...], sl_ref[...], w_ref[...]).astype(o_ref.dtype)


def _dft_tables(n, block):
    i = np.arange(n)
    prod = (i[:, None] % block) * (i[None, :] % block) % block
    ang = prod.astype(np.float64) * (2.0 * math.pi / block)
    same = (i[:, None] // block) == (i[None, :] // block)
    scale = 1.0 / math.sqrt(block)
    c = np.where(same, np.cos(ang) * scale, 0.0).astype(np.float32)
    s = np.where(same, np.sin(ang) * scale, 0.0).astype(np.float32)
    return jnp.asarray(c).astype(BF16), jnp.asarray(s).astype(BF16)


def _fourier(proj, row0, n_seq, seq_len, w_f):
    cl, sl = _dft_tables(seq_len, seq_len)
    cc, sc = _dft_tables(FN_WIDTH, FN_GROUP_DIM)
    blk0 = row0 // seq_len
    full = lambda shape: pl.BlockSpec(shape, lambda i: (0, 0))
    return pl.pallas_call(
        _fourier_body,
        grid=(n_seq,),
        in_specs=[
            pl.BlockSpec((seq_len, FN_WIDTH), lambda i: (blk0 + i, COL_FN // FN_WIDTH)),
            full((seq_len, seq_len)), full((seq_len, seq_len)),
            full((FN_WIDTH, FN_WIDTH)), full((FN_WIDTH, FN_WIDTH)), full((FN_WIDTH, FN_WIDTH)),
        ],
        out_specs=pl.BlockSpec((seq_len, FN_WIDTH), lambda i: (i, 0)),
        out_shape=jax.ShapeDtypeStruct((n_seq * seq_len, FN_WIDTH), BF16),
        compiler_params=_params(1),
        name=f"fourier_{seq_len}",
    )(proj, cl, sl, cc, sc, w_f)


def _lambda_full(lmb, lam_init):
    a = jnp.sum(lmb[0:1, :] * lmb[1:2, :], axis=-1, keepdims=True)
    b = jnp.sum(lmb[2:3, :] * lmb[3:4, :], axis=-1, keepdims=True)
    return jnp.exp(a) - jnp.exp(b) + lam_init


def _diff_head(q, ks, vs, lam, g, lam_init):
    vas = [jnp.concatenate([v.astype(BF16), jnp.ones(v.shape, BF16)], axis=1) for v in vs]
    lq = q.shape[0]
    first_map = (lax.broadcasted_iota(jnp.int32, (1, 2 * DA_HALF), 1) < DA_HALF).astype(BF16)
    qb = q.astype(BF16)
    q_stack = jnp.concatenate([qb * first_map, qb * (1 - first_map)], axis=0)
    parts = [_dot_nt(q_stack, k.astype(BF16)) for k in ks]
    mx = parts[0].max(axis=-1, keepdims=True)
    for p in parts[1:]:
        mx = jnp.maximum(mx, p.max(axis=-1, keepdims=True))
    oa = _dot(jnp.exp(parts[0] - mx).astype(BF16), vas[0])
    for p, va in zip(parts[1:], vas[1:]):
        oa = oa + _dot(jnp.exp(p - mx).astype(BF16), va)
    o = oa[:, :DA_VDIM] * (1.0 / oa[:, DA_VDIM:DA_VDIM + 1])
    a = o[:lq] - lam * o[lq:]
    ms = jnp.mean(a * a, axis=-1, keepdims=True)
    return a * lax.rsqrt(ms + EPS) * g * (1.0 - lam_init)


def _rope(x, cos, sin):
    lane = lax.broadcasted_iota(jnp.int32, x.shape, 1)
    first = ((lane >> 4) & 1) == 0
    rot = jnp.where(first, -pltpu.roll(x, 128 - DA_HALF // 4, 1), pltpu.roll(x, DA_HALF // 4, 1))
    return x * cos + rot * sin


def _attn_lat_body(lam_init, q_ref, k_ref, v_ref, kc_ref, vc_ref, cq_ref, sq_ref, ck_ref, sk_ref,
                   lmb_ref, g_ref, o_ref):
    lam = _lambda_full(lmb_ref[...], lam_init)
    scale = DA_HALF ** -0.5
    for h in range(DA_HEADS):
        sl = slice(h * DA_VDIM, (h + 1) * DA_VDIM)
        q = _rope(q_ref[:, sl].astype(F32), cq_ref[...], sq_ref[...]) * scale
        k = _rope(k_ref[:, sl].astype(F32), ck_ref[...], sk_ref[...])
        o = _diff_head(q, [k, kc_ref[:, sl]], [v_ref[:, sl], vc_ref[:, sl]], lam, g_ref[...], lam_init)
        o_ref[:, sl] = o.astype(o_ref.dtype)


def _attn_lat(proj, row0, n_seq, seq_len, cache_k, cache_v, layer, cos, sin, lmb, g, lam_init):
    tq = 512
    nq = seq_len // tq
    past = cache_k.shape[2]
    qb0 = row0 // tq
    kb0 = row0 // seq_len
    cache_spec = pl.BlockSpec((None, None, past, DA_WIDTH), lambda b, j: (b, layer, 0, 0))
    kv_spec = lambda c: pl.BlockSpec((seq_len, DA_WIDTH), lambda b, j: (kb0 + b, c))
    return pl.pallas_call(
        functools.partial(_attn_lat_body, lam_init),
        grid=(n_seq, nq),
        in_specs=[
            pl.BlockSpec((tq, DA_WIDTH), lambda b, j: (qb0 + b * nq + j, COL_Q // DA_WIDTH)),
            kv_spec(COL_K // DA_WIDTH), kv_spec(COL_V // DA_WIDTH),
            cache_spec, cache_spec,
            pl.BlockSpec((tq, DA_VDIM), lambda b, j: (j, 0)),
            pl.BlockSpec((tq, DA_VDIM), lambda b, j: (j, 0)),
            pl.BlockSpec((seq_len, DA_VDIM), lambda b, j: (0, 0)),
            pl.BlockSpec((seq_len, DA_VDIM), lambda b, j: (0, 0)),
            pl.BlockSpec((4, DA_HALF), lambda b, j: (0, 0)),
            pl.BlockSpec((1, DA_VDIM), lambda b, j: (0, 0)),
        ],
        out_specs=pl.BlockSpec((tq, DA_WIDTH), lambda b, j: (b * nq + j, 0)),
        out_shape=jax.ShapeDtypeStruct((n_seq * seq_len, DA_WIDTH), BF16),
        compiler_params=_params(2),
        name="attn_lat",
    )(proj, proj, proj, cache_k, cache_v, cos, sin, cos, sin, lmb, g)


def _rope_tables(n_tokens):
    rows = n_tokens // GRID_W
    row = np.repeat(np.arange(rows, dtype=np.float64), GRID_W)
    col = np.tile(np.arange(GRID_W, dtype=np.float64), rows)
    axis_dim = DA_HALF // 2
    inv_freq = ROPE_THETA ** (-np.arange(0, axis_dim, 2, dtype=np.float64) / axis_dim)
    ang_r = row[:, None] * inv_freq[None, :]
    ang_c = col[:, None] * inv_freq[None, :]
    ang = np.concatenate([ang_r, ang_r, ang_c, ang_c] * 2, axis=-1)
    return jnp.asarray(np.cos(ang).astype(np.float32)), jnp.asarray(np.sin(ang).astype(np.float32))


def _hgrn_tables():
    gl = CHUNKS_PER_GROUP * CHUNK
    t = np.arange(gl)
    same = (t[:, None] // CHUNK) == (t[None, :] // CHUNK)
    fwd = same & (t[None, :] <= t[:, None])
    bwd = same & (t[None, :] >= t[:, None])
    f = np.arange(HG_WIDTH)
    bd = (f[:, None] // HG_DK) == (f[None, :] // HG_DK)
    tri = jnp.asarray(np.stack([fwd, bwd]).astype(np.float32)).astype(BF16)
    causal = jnp.asarray(np.stack([fwd, bwd]).astype(np.float32))
    return tri, causal, jnp.asarray(bd.astype(np.float32))


def _hgrn2_body(n_seq, groups_per_seq, has_s0, *refs):
    (hq_ref, hi_ref, hf_ref, hb_ref, hg_ref, lb_ref, ng_ref, tri_ref, causal_ref, bd_ref) = refs[:10]
    if has_s0:
        s0_ref, o_ref, oi_scr, qe_scr, u_scr, st_scr, dec_scr = refs[10:]
        so_ref = None
    else:
        o_ref, so_ref, oi_scr, qe_scr, u_scr, st_scr, dec_scr = refs[10:]
        s0_ref = None
    c = CHUNK
    w = HG_WIDTH
    gc = CHUNKS_PER_GROUP
    gl = gc * c
    n_groups = n_seq * groups_per_seq
    chunks_per_seq = groups_per_seq * gc
    lane = lax.broadcasted_iota(jnp.int32, (1, w), 1)
    dk_bits = HG_DK.bit_length() - 1
    head_masks_b = [((lane >> dk_bits) == h).astype(BF16) for h in range(HG_HEADS)]
    bd = bd_ref[...] > 0.5
    edge_rows = (c - 1, 0)
    fp_refs = (hf_ref, hb_ref)

    def per_chunk_row(x, row):
        return jnp.concatenate([jnp.broadcast_to(x[k * c + row:k * c + row + 1, :], (c, w)) for k in range(gc)],
                               axis=0)

    def group_terms(gi):
        rows = pl.ds(gi * gl if isinstance(gi, int) else pl.multiple_of(gi * gl, gl), gl)
        hq = hq_ref[rows, :]
        q = _silu_t(hq, HG_DK ** -0.5)
        v = hi_ref[rows, :].astype(BF16)
        v_stack = jnp.concatenate([v * hm for hm in head_masks_b], axis=0)
        for d in range(2):
            lb = lb_ref[d:d + 1, :]
            fp = fp_refs[d][rows, :]
            lsig = jnp.minimum(fp, 0.0) - jnp.log(1.0 + jnp.exp(-jnp.abs(fp)))
            la = jnp.log(lb)
            lbb = jnp.log(1.0 - lb) + lsig
            logf = jnp.maximum(la, lbb) + jnp.log(1.0 + jnp.exp(-jnp.abs(la - lbb)))
            kk = (0.5 * (1.0 - lb)) * (1.0 - jnp.tanh(0.5 * fp))
            g_hi, g_lo = _split(logf, 2)
            cum = _dot(tri_ref[d], g_hi) + _dot(tri_ref[d], g_lo)
            total = per_chunk_row(cum, edge_rows[d])
            ref = per_chunk_row(cum, c // 2)
            qc = (q * jnp.exp(cum - ref)).astype(BF16)
            kc = (kk * jnp.exp(ref - cum)).astype(BF16)
            ke = (kk * jnp.exp(total - cum)).astype(BF16)
            qe_scr[d, rows, :] = (q * jnp.exp(cum)).astype(BF16)
            kc_stack = jnp.concatenate([kc * hm for hm in head_masks_b], axis=0)
            a = _dot_nt(qc, kc_stack)
            keep = causal_ref[d] > 0.5
            a = jnp.concatenate([jnp.where(keep, a[:, h * gl:(h + 1) * gl], 0.0) for h in range(HG_HEADS)],
                                axis=1).astype(BF16)
            oi_scr[d, rows, :] = _dot(a, v_stack)
            for k in range(gc):
                ck = slice(k * c, (k + 1) * c)
                i = gi * gc + k
                dec_scr[d, pl.ds(i, 1), :] = jnp.exp(cum[k * c + edge_rows[d]:k * c + edge_rows[d] + 1, :])
                u_scr[d, i] = jnp.where(bd, _dot_tn(v[ck, :], ke[ck, :]), 0.0)

    if n_groups <= 2:
        for gi in range(n_groups):
            group_terms(gi)
    else:
        def terms_step(gi, carry):
            group_terms(gi)
            return carry

        lax.fori_loop(0, n_groups, terms_step, 0)

    slab = 32
    for s in range(n_seq):
        first = s * chunks_per_seq
        for d in range(2):
            order = range(chunks_per_seq) if d == 0 else range(chunks_per_seq - 1, -1, -1)
            for r in range(w // slab):
                rs = slice(r * slab, (r + 1) * slab)
                st = s0_ref[s, d, rs, :] if has_s0 else jnp.zeros((slab, w), F32)
                for j in order:
                    i = first + j
                    st_scr[d, i, rs, :] = st.astype(BF16)
                    st = st * dec_scr[d, i:i + 1, :] + u_scr[d, i, rs, :]
                if so_ref is not None:
                    h = (r * slab) // HG_DK
                    off = (r * slab) % HG_DK
                    so_ref[s, d, h, off:off + slab, :] = st[:, h * HG_DK:(h + 1) * HG_DK]

    ones_bd = bd.astype(BF16)

    def finish(gi):
        rows = pl.ds(gi * gl if isinstance(gi, int) else pl.multiple_of(gi * gl, gl), gl)
        inter = []
        for k in range(gc):
            i = gi * gc + k
            ck = pl.ds(gi * gl + k * c if isinstance(gi, int) else pl.multiple_of(gi * gl + k * c, c), c)
            inter.append(_dot_nt(qe_scr[0, ck, :], st_scr[0, i]) + _dot_nt(qe_scr[1, ck, :], st_scr[1, i]))
        o = oi_scr[0, rows, :] + oi_scr[1, rows, :] + jnp.concatenate(inter, axis=0)
        sq = _split(o * o, 2)
        ms = (_dot(sq[0], ones_bd) + _dot(sq[1], ones_bd)) * (1.0 / HG_DK)
        y = o * lax.rsqrt(ms + EPS) * ng_ref[...]
        hg = hg_ref[rows, :]
        o_ref[rows, :] = (y * _silu_t(hg)).astype(o_ref.dtype)

    if n_groups <= 2:
        for gi in range(n_groups):
            finish(gi)
    else:
        def finish_step(gi, carry):
            finish(gi)
            return carry

        lax.fori_loop(0, n_groups, finish_step, 0)


def _hgrn_scratch(rows):
    n_chunks = rows // CHUNK
    return [pltpu.VMEM((2, rows, HG_WIDTH), F32),
            pltpu.VMEM((2, rows, HG_WIDTH), BF16),
            pltpu.VMEM((2, n_chunks, HG_WIDTH, HG_WIDTH), F32),
            pltpu.VMEM((2, n_chunks, HG_WIDTH, HG_WIDTH), BF16),
            pltpu.VMEM((2, max(n_chunks, 8), HG_WIDTH), F32)]


def _hgrn2(proj_h, row0, n_seq, seq_len, seqs_per_step, lb, ng, s0t):
    gl = CHUNKS_PER_GROUP * CHUNK
    groups_per_seq = seq_len // gl
    rows = seqs_per_step * seq_len
    blk0 = row0 // rows
    tri, causal, bd = _hgrn_tables()
    col = lambda c0: pl.BlockSpec((rows, HG_WIDTH), lambda i: (blk0 + i, c0 // HG_WIDTH))
    const = lambda shape: pl.BlockSpec(shape, lambda i: (0,) * len(shape))
    in_specs = [col(COL_HQ), col(COL_HI), col(COL_HF), col(COL_HB), col(COL_HG),
                const((2, HG_WIDTH)), const((1, HG_WIDTH)),
                const((2, gl, gl)), const((2, gl, gl)), const((HG_WIDTH, HG_WIDTH))]
    args = [proj_h] * 5 + [lb, ng, tri, causal, bd]
    out_specs = [pl.BlockSpec((rows, HG_WIDTH), lambda i: (i, 0))]
    out_shape = [jax.ShapeDtypeStruct((n_seq * seq_len, HG_WIDTH), BF16)]
    if s0t is not None:
        in_specs.append(pl.BlockSpec((seqs_per_step, 2, HG_WIDTH, HG_WIDTH), lambda i: (i, 0, 0, 0)))
        args.append(s0t)
    else:
        out_specs.append(pl.BlockSpec((seqs_per_step, 2, HG_HEADS, HG_DK, HG_DK), lambda i: (i, 0, 0, 0, 0)))
        out_shape.append(jax.ShapeDtypeStruct((n_seq, 2, HG_HEADS, HG_DK, HG_DK), F32))
    return pl.pallas_call(
        functools.partial(_hgrn2_body, seqs_per_step, groups_per_seq, s0t is not None),
        grid=(n_seq // seqs_per_step,),
        in_specs=in_specs,
        out_specs=out_specs,
        out_shape=out_shape,
        scratch_shapes=_hgrn_scratch(rows),
        compiler_params=_params(1),
        name=f"hgrn_{seq_len}",
    )(*args)


def _ctx_mixers_body(n_seq, seq_len, lam_init, *refs):
    (q_ref, k_ref, v_ref, u_ref, hq_ref, hi_ref, hf_ref, hb_ref, hg_ref,
     cl_ref, sl_ref, cc_ref, sc_ref, wf_ref, lmb_ref, ag_ref,
     lb_ref, ng_ref, tri_ref, causal_ref, bd_ref,
     fn_ref, a_ref, ho_ref, so_ref) = refs[:25]
    scratch = refs[25:]

    z = u_ref[...].astype(BF16)
    za = _dot(z, cc_ref[...]).astype(BF16)
    zb = _dot(z, sc_ref[...]).astype(BF16)
    for s in range(n_seq):
        rs = slice(s * seq_len, (s + 1) * seq_len)
        fn_ref[rs, :] = _fourier_rows(za[rs, :], zb[rs, :], cl_ref[...], sl_ref[...],
                                      wf_ref[...]).astype(fn_ref.dtype)

    lam = _lambda_full(lmb_ref[...], lam_init)
    scale = DA_HALF ** -0.5
    for s in range(n_seq):
        rs = slice(s * seq_len, (s + 1) * seq_len)
        for h in range(DA_HEADS):
            sl = slice(h * DA_VDIM, (h + 1) * DA_VDIM)
            o = _diff_head(q_ref[rs, sl] * scale, [k_ref[rs, sl]], [v_ref[rs, sl]], lam, ag_ref[...], lam_init)
            a_ref[rs, sl] = o.astype(a_ref.dtype)

    _hgrn2_body(n_seq, seq_len // (CHUNKS_PER_GROUP * CHUNK), False,
                hq_ref, hi_ref, hf_ref, hb_ref, hg_ref, lb_ref, ng_ref, tri_ref, causal_ref, bd_ref,
                ho_ref, so_ref, *scratch)


def _ctx_mixers(proj_a, proj_h, n_seq, seq_len, seqs_per_step, w_f, lmb, ag, lam_init, lb, ng):
    rows = seqs_per_step * seq_len
    gl = CHUNKS_PER_GROUP * CHUNK
    cl, sl = _dft_tables(seq_len, seq_len)
    cc, sc = _dft_tables(FN_WIDTH, FN_GROUP_DIM)
    tri, causal, bd = _hgrn_tables()
    col = lambda c0, width: pl.BlockSpec((rows, width), lambda i: (i, c0 // width))
    const = lambda shape: pl.BlockSpec(shape, lambda i: (0,) * len(shape))
    out_rows = lambda width: pl.BlockSpec((rows, width), lambda i: (i, 0))
    n_tok = n_seq * seq_len
    return pl.pallas_call(
        functools.partial(_ctx_mixers_body, seqs_per_step, seq_len, lam_init),
        grid=(n_seq // seqs_per_step,),
        in_specs=[col(COL_Q, DA_WIDTH), col(COL_K, DA_WIDTH), col(COL_V, DA_WIDTH), col(COL_FN, FN_WIDTH),
                  col(COL_HQ, HG_WIDTH), col(COL_HI, HG_WIDTH), col(COL_HF, HG_WIDTH), col(COL_HB, HG_WIDTH),
                  col(COL_HG, HG_WIDTH),
                  const((seq_len, seq_len)), const((seq_len, seq_len)),
                  const((FN_WIDTH, FN_WIDTH)), const((FN_WIDTH, FN_WIDTH)), const((FN_WIDTH, FN_WIDTH)),
                  const((4, DA_HALF)), const((1, DA_VDIM)),
                  const((2, HG_WIDTH)), const((1, HG_WIDTH)),
                  const((2, gl, gl)), const((2, gl, gl)), const((HG_WIDTH, HG_WIDTH))],
        out_specs=[out_rows(FN_WIDTH), out_rows(DA_WIDTH), out_rows(HG_WIDTH),
                   pl.BlockSpec((seqs_per_step, 2, HG_HEADS, HG_DK, HG_DK), lambda i: (i, 0, 0, 0, 0))],
        out_shape=[jax.ShapeDtypeStruct((n_tok, FN_WIDTH), BF16),
                   jax.ShapeDtypeStruct((n_tok, DA_WIDTH), BF16),
                   jax.ShapeDtypeStruct((n_tok, HG_WIDTH), BF16),
                   jax.ShapeDtypeStruct((n_seq, 2, HG_HEADS, HG_DK, HG_DK), F32)],
        scratch_shapes=_hgrn_scratch(rows),
        compiler_params=_params(1),
        name="ctx_mixers",
    )(*([proj_a] * 4 + [proj_h] * 5), cl, sl, cc, sc, w_f, lmb, ag, lb, ng, tri, causal, bd)


def _route_gates(lt, bias):
    per = N_EXPERTS // N_GROUPS
    tt = lt.shape[1]
    neg = -jnp.inf
    assert per == N_GROUPS == 8
    gi = lax.broadcasted_iota(jnp.int32, (N_GROUPS, tt), 0).astype(F32)
    s_j, b_j = [], []
    for j in range(per):
        s = _sigmoid(lt[j * N_GROUPS:(j + 1) * N_GROUPS, :])
        s_j.append(s)
        b_j.append(s + bias[j * N_GROUPS:(j + 1) * N_GROUPS, :])
    m1 = functools.reduce(jnp.maximum, b_j)
    i1 = functools.reduce(jnp.minimum, [jnp.where(b_j[j] == m1, float(j), float(per)) for j in range(per)])
    m2 = functools.reduce(jnp.maximum, [jnp.where(i1 == float(j), neg, b_j[j]) for j in range(per)])
    gs = m1 + m2
    gsel = jnp.zeros((N_GROUPS, tt), jnp.bool_)
    for _ in range(TOPK_GROUPS):
        m = gs.max(axis=0, keepdims=True)
        idx = jnp.where(gs == m, gi, float(N_GROUPS)).min(axis=0, keepdims=True)
        hit = gi == idx
        gsel = gsel | hit
        gs = jnp.where(hit, neg, gs)
    x_j = [jnp.where(gsel, b_j[j], neg) for j in range(per)]
    e_j = [gi * per + j for j in range(per)]
    sel_j = [jnp.zeros((N_GROUPS, tt), jnp.bool_) for _ in range(per)]
    for _ in range(TOP_K):
        m = functools.reduce(jnp.maximum, x_j).max(axis=0, keepdims=True)
        idx = functools.reduce(jnp.minimum, [jnp.where(x_j[j] == m, e_j[j], float(N_EXPERTS))
                                             for j in range(per)]).min(axis=0, keepdims=True)
        for j in range(per):
            hit = e_j[j] == idx
            sel_j[j] = sel_j[j] | hit
            x_j[j] = jnp.where(hit, neg, x_j[j])
    w_j = [jnp.where(sel_j[j], s_j[j], 0.0) for j in range(per)]
    denom = functools.reduce(lambda a, b: a + b, w_j).sum(axis=0, keepdims=True)
    gates_t = jnp.concatenate([w / denom * ROUTED_SCALE for w in w_j], axis=0)
    r_io = lax.broadcasted_iota(jnp.int32, (N_EXPERTS, N_EXPERTS), 0)
    e_io = lax.broadcasted_iota(jnp.int32, (N_EXPERTS, N_EXPERTS), 1)
    eye = (e_io == (r_io & (N_GROUPS - 1)) * per + (r_io >> 3)).astype(BF16)
    p = _split(gates_t, 3)
    return _dot_tn(p[0], eye) + _dot_tn(p[1], eye) + _dot_tn(p[2], eye)


def _outproj_body(n_x, n_ctx_tiles, *refs):
    fn_refs, a_refs, hg_refs = refs[0:2], refs[2:4], refs[4:6]
    x_refs = refs[6:6 + n_x]
    mod_ref, g_ref, w_ref, wr_ref, rb_ref, xo_ref, h_ref, gate_ref = refs[6 + n_x:]
    is_ctx = pl.program_id(0) < n_ctx_tiles
    mix = (_dot(_pick(is_ctx, fn_refs), w_ref[0:FN_WIDTH, :])
           + _dot(_pick(is_ctx, a_refs), w_ref[FN_WIDTH:FN_WIDTH + DA_WIDTH, :])
           + _dot(_pick(is_ctx, hg_refs), w_ref[FN_WIDTH + DA_WIDTH:, :]))
    x = _pick(is_ctx, x_refs) + mod_ref[2:3, :] * mix
    xo_ref[...] = x
    h = _modnorm(x, g_ref[...], mod_ref[3:4, :], mod_ref[4:5, :])
    h_ref[...] = h.astype(BF16)
    h_hi, h_lo = _split(h, 2)
    w_hi, w_lo = _split(wr_ref[...], 2)
    logits_t = _dot_nt(w_hi, h_hi) + _dot_nt(w_lo, h_hi) + _dot_nt(w_hi, h_lo)
    gate_ref[...] = _route_gates(logits_t, rb_ref[...])


def _outproj(fn, a, hg, xs, mod, g, w_out, w_router_t, router_bias, n_ctx_tok, lat_len):
    t = sum(x.shape[0] for x in xs)
    tm = TOKEN_TILE
    n_ctx_tiles = n_ctx_tok // tm
    row = _mod_row(n_ctx_tiles, lat_len // tm)
    rows = lambda width: pl.BlockSpec((tm, width), lambda i: (i, 0))
    parts = lambda n, width: _row_specs(n, tm, width, n_ctx_tiles)
    return pl.pallas_call(
        functools.partial(_outproj_body, len(xs), n_ctx_tiles),
        grid=(t // tm,),
        in_specs=parts(2, FN_WIDTH) + parts(2, DA_WIDTH) + parts(2, HG_WIDTH) + parts(len(xs), D_MODEL) + [
            pl.BlockSpec((None, 6, D_MODEL), lambda i: (row(i), 0, 0)),
            pl.BlockSpec((1, D_MODEL), lambda i: (0, 0)),
            pl.BlockSpec((D_MODEL, D_MODEL), lambda i: (0, 0)),
            pl.BlockSpec((N_EXPERTS, D_MODEL), lambda i: (0, 0)),
            pl.BlockSpec((N_EXPERTS, 1), lambda i: (0, 0)),
        ],
        out_specs=[rows(D_MODEL), rows(D_MODEL), rows(N_EXPERTS)],
        out_shape=[jax.ShapeDtypeStruct((t, D_MODEL), F32),
                   jax.ShapeDtypeStruct((t, D_MODEL), BF16),
                   jax.ShapeDtypeStruct((t, N_EXPERTS), F32)],
        compiler_params=_params(1),
        name="outproj",
    )(*fn, *a, *hg, *xs, mod, g, w_out, w_router_t, router_bias)


def _moe_body(final_norm, n_ctx_tiles, h_ref, gate_ref, wg_ref, wu_ref, wd_ref, sg_ref, su_ref, sd_ref,
              x_ref, mod_ref, fg_ref, *out_and_scratch):
    acc_ref = out_and_scratch[-1]
    o_refs = out_and_scratch[:-1]
    is_ctx = pl.program_id(0) < n_ctx_tiles
    j = pl.program_id(1)
    h = h_ref[...]

    def act_of(wg, wu, gate):
        gu = _dot(h, jnp.concatenate([wg.astype(BF16), wu.astype(BF16)], axis=1))
        a = _silu_t(gu[:, :D_EXPERT]) * gu[:, D_EXPERT:]
        return a if gate is None else a * gate

    @pl.when(j == 0)
    def _():
        acc_ref[...] = _dot(act_of(sg_ref[...], su_ref[...], None).astype(BF16), sd_ref[...].astype(BF16))

    gates = gate_ref[...]
    expert_of_lane = lax.broadcasted_iota(jnp.int32, gates.shape, 1)

    def gate_col(p):
        e = j * EXPERTS_PER_STEP + p
        return jnp.sum(jnp.where(expert_of_lane == e, gates, 0.0), axis=1, keepdims=True)

    acts = [act_of(wg_ref[p], wu_ref[p], gate_col(p)).astype(BF16) for p in range(EXPERTS_PER_STEP)]
    wd = wd_ref[...].astype(BF16).reshape(EXPERTS_PER_STEP * D_EXPERT, D_MODEL)
    acc_ref[...] += _dot(jnp.concatenate(acts, axis=1), wd)

    @pl.when(j == pl.num_programs(1) - 1)
    def _():
        x = x_ref[...] + mod_ref[5:6, :] * acc_ref[...]
        if not final_norm:
            o_refs[0][...] = x
        else:
            ms = jnp.mean(x * x, axis=-1, keepdims=True)
            y = x * lax.rsqrt(ms + EPS) * fg_ref[...]

            @pl.when(is_ctx)
            def _():
                o_refs[0][...] = y

            @pl.when(jnp.logical_not(is_ctx))
            def _():
                o_refs[1][...] = y


def _moe(h, gates, w_gate, w_up, w_down2, ws_gate, ws_up, ws_down, layer, x, mod, final_g, final_norm,
         n_ctx_tok, lat_len):
    t = x.shape[0]
    tm = MOE_TILE
    eps_ = EXPERTS_PER_STEP
    n_ctx_tiles = n_ctx_tok // tm
    row = _mod_row(n_ctx_tiles, lat_len // tm)
    once = pl.Buffered(1)
    if final_norm:
        last_ctx = n_ctx_tiles - 1
        out_specs = [pl.BlockSpec((tm, D_MODEL), lambda i, j: (jnp.minimum(i, last_ctx), 0), pipeline_mode=once),
                     pl.BlockSpec((tm, D_MODEL), lambda i, j: (jnp.maximum(i - n_ctx_tiles, 0), 0),
                                  pipeline_mode=once)]
        out_shape = [jax.ShapeDtypeStruct((n_ctx_tok, D_MODEL), F32),
                     jax.ShapeDtypeStruct((t - n_ctx_tok, D_MODEL), F32)]
    else:
        out_specs = [pl.BlockSpec((tm, D_MODEL), lambda i, j: (i, 0), pipeline_mode=once)]
        out_shape = [jax.ShapeDtypeStruct((t, D_MODEL), F32)]
    return pl.pallas_call(
        functools.partial(_moe_body, final_norm, n_ctx_tiles),
        grid=(t // tm, N_EXPERTS // eps_),
        in_specs=[
            pl.BlockSpec((tm, D_MODEL), lambda i, j: (i, 0)),
            pl.BlockSpec((tm, N_EXPERTS), lambda i, j: (i, 0)),
            pl.BlockSpec((None, eps_, D_MODEL, D_EXPERT), lambda i, j: (layer, j, 0, 0)),
            pl.BlockSpec((None, eps_, D_MODEL, D_EXPERT), lambda i, j: (layer, j, 0, 0)),
            pl.BlockSpec((None, eps_ // 2, 2 * D_EXPERT, D_MODEL), lambda i, j: (layer, j, 0, 0)),
            pl.BlockSpec((None, D_MODEL, D_EXPERT), lambda i, j: (layer, 0, 0), pipeline_mode=once),
            pl.BlockSpec((None, D_MODEL, D_EXPERT), lambda i, j: (layer, 0, 0), pipeline_mode=once),
            pl.BlockSpec((None, D_EXPERT, D_MODEL), lambda i, j: (layer, 0, 0), pipeline_mode=once),
            pl.BlockSpec((tm, D_MODEL), lambda i, j: (i, 0), pipeline_mode=once),
            pl.BlockSpec((None, 6, D_MODEL), lambda i, j: (row(i), 0, 0)),
            pl.BlockSpec((1, D_MODEL), lambda i, j: (0, 0)),
        ],
        out_specs=out_specs,
        out_shape=out_shape,
        scratch_shapes=[pltpu.VMEM((tm, D_MODEL), F32)],
        compiler_params=_params(2),
        name="moe",
    )(h, gates, w_gate, w_up, w_down2, ws_gate, ws_up, ws_down, x, mod, final_g)


def _block_diag_t(s):
    eye = jnp.eye(HG_HEADS, dtype=s.dtype)
    out = jnp.einsum('...hkv,hg->...hvgk', s, eye)
    return out.reshape(s.shape[:-3] + (HG_WIDTH, HG_WIDTH))


def kernel(x_prompt, x_sample, cache_k, cache_v, state_hgrn, c, c_ctx, w_ada, b_ada, norm_g, w_in,
           w_fourier, lambdas, attn_norm_g, lower_bounds, hg_norm_g, w_out, w_router, router_bias,
           w_gate, w_up, w_down, ws_gate, ws_up, ws_down, final_g):
    n_ctx, ctx_len, _ = x_prompt.shape
    n_lat, lat_len, _ = x_sample.shape
    n_ctx_tok = n_ctx * ctx_len
    past = cache_k.shape[2]

    xs = (x_prompt.reshape(n_ctx_tok, D_MODEL), x_sample.reshape(n_lat * lat_len, D_MODEL))

    c8 = jnp.zeros((8, D_MODEL), F32).at[0].set(c_ctx).at[1:1 + n_lat].set(c)
    mods = _ada_mods(c8, w_ada, b_ada).reshape(DEPTH, 8, 6, D_MODEL)

    cs = jnp.cumsum(jax.nn.softmax(lower_bounds.astype(F32), axis=0), axis=0)
    lbs = cs - cs[0:1]

    cos, sin = _rope_tables(lat_len)
    cache_k4 = cache_k.reshape(n_lat, DEPTH, past, DA_WIDTH)
    cache_v4 = cache_v.reshape(n_lat, DEPTH, past, DA_WIDTH)
    s0t = _block_diag_t(state_hgrn.astype(F32))

    qkv_end = FN_WIDTH + 3 * DA_WIDTH
    w_in_b = jnp.concatenate([w_in[:, :, FN_WIDTH:qkv_end], w_in[:, :, :FN_WIDTH], w_in[:, :, qkv_end:]],
                             axis=-1).astype(BF16)
    w_f_b = w_fourier.astype(BF16)
    w_out_b = w_out.astype(BF16)
    per = N_EXPERTS // N_GROUPS
    w_router_t = (jnp.swapaxes(w_router, 1, 2).reshape(DEPTH, N_GROUPS, per, D_MODEL)
                  .swapaxes(1, 2).reshape(DEPTH, N_EXPERTS, D_MODEL))
    bias_mm = router_bias.reshape(DEPTH, N_GROUPS, per).swapaxes(1, 2).reshape(DEPTH, N_EXPERTS, 1)
    w_down2 = w_down.reshape(DEPTH, N_EXPERTS // 2, 2 * D_EXPERT, D_MODEL)
    ng = jnp.tile(hg_norm_g, (1, HG_HEADS))

    caches = []
    new_s = []
    for l in range(DEPTH):
        lam_init = 0.8 - 0.6 * math.exp(-0.3 * l)
        mod = mods[l]
        last = l == DEPTH - 1
        proj_a, proj_h, k_l, v_l = _inproj(xs, mod, norm_g[l, 0:1], w_in_b[l], tuple(caches) if last else (),
                                           n_ctx, ctx_len, lat_len)
        caches = [k_l, v_l] if last else caches + [k_l, v_l]

        ag = attn_norm_g[l].reshape(1, DA_VDIM)
        fn_ctx, a_ctx, hg_ctx, st_ctx = _ctx_mixers(proj_a, proj_h, n_ctx, ctx_len, 2, w_f_b[l], lambdas[l], ag,
                                                    lam_init, lbs[l], ng[l:l + 1])
        fn_lat = _fourier(proj_a, n_ctx_tok, n_lat, lat_len, w_f_b[l])
        a_lat = _attn_lat(proj_a, n_ctx_tok, n_lat, lat_len, cache_k4, cache_v4, l, cos, sin, lambdas[l], ag,
                          lam_init)
        (hg_lat,) = _hgrn2(proj_h, n_ctx_tok, n_lat, lat_len, 1, lbs[l], ng[l:l + 1], s0t[:, l])
        new_s.append(jnp.swapaxes(st_ctx, -1, -2))

        x, h2, gates = _outproj((fn_ctx, fn_lat), (a_ctx, a_lat), (hg_ctx, hg_lat), xs, mod,
                                norm_g[l, 1:2], w_out_b[l], w_router_t[l], bias_mm[l], n_ctx_tok, lat_len)
        xs = _moe(h2, gates, w_gate, w_up, w_down2, ws_gate, ws_up, ws_down, l, x, mod,
                  final_g.reshape(1, D_MODEL), l == DEPTH - 1, n_ctx_tok, lat_len)

    y_prompt = xs[0].reshape(n_ctx, ctx_len, D_MODEL)
    y_sample = xs[1].reshape(n_lat, lat_len, D_MODEL)
    new_k = caches[0].reshape(n_ctx, DEPTH, ctx_len, DA_HEADS, DA_VDIM)
    new_v = caches[1].reshape(n_ctx, DEPTH, ctx_len, DA_HEADS, DA_VDIM)
    return (y_prompt, y_sample, new_k, new_v, jnp.stack(new_s, axis=1))
```

```python
import functools
import math

import numpy as np
import jax
import jax.numpy as jnp
from jax import lax
from jax.experimental import pallas as pl
from jax.experimental.pallas import tpu as pltpu

F32 = jnp.float32
BF16 = jnp.bfloat16

D_MODEL = 1024
DEPTH = 2
GRID_W = 64
FN_WIDTH = 256
FN_GROUP_DIM = 64
DA_WIDTH = 512
DA_HEADS = 4
DA_VDIM = 128
DA_HALF = 64
HG_WIDTH = 256
HG_HEADS = 4
HG_DK = 64
CHUNK = 64
ROPE_THETA = 10000.0
N_EXPERTS = 64
TOP_K = 8
N_GROUPS = 8
TOPK_GROUPS = 4
D_EXPERT = 128
ROUTED_SCALE = 2.5
EPS = 1e-6

COL_Q, COL_K, COL_V, COL_FN = 0, 512, 1024, 1536
PROJ_A_WIDTH = 1792
COL_HQ, COL_HI, COL_HF, COL_HB, COL_HG = 0, 256, 512, 768, 1024
PROJ_H_WIDTH = 1280

TOKEN_TILE = 512
MOE_TILE = 1024
EXPERTS_PER_STEP = 8
CHUNKS_PER_GROUP = 4
VMEM_LIMIT = 56 * 1024 * 1024


def _dot(a, b):
    return jnp.dot(a, b, preferred_element_type=F32)


def _dot_nt(a, b):
    return lax.dot_general(a, b, (((1,), (1,)), ((), ())), preferred_element_type=F32)


def _dot_tn(a, b):
    return lax.dot_general(a, b, (((0,), (0,)), ((), ())), preferred_element_type=F32)


def _split(x, n):
    parts = []
    r = x
    for i in range(n):
        p = r.astype(BF16)
        parts.append(p)
        if i + 1 < n:
            r = r - p.astype(F32)
    return parts


def _sigmoid(x):
    return 1.0 / (1.0 + jnp.exp(-x))


def _silu_t(x, scale=1.0):
    return (0.5 * scale) * x * (jnp.tanh(0.5 * x) + 1.0)


def _silu(x):
    return x * _sigmoid(x)


def _params(n_axes):
    return pltpu.CompilerParams(dimension_semantics=("arbitrary",) * n_axes,
                                vmem_limit_bytes=VMEM_LIMIT)


def _ada_body(c_ref, w_ref, b_ref, o_ref):
    a = _silu(c_ref[...])
    a_hi, a_lo = _split(a, 2)
    w_hi, w_lo = _split(w_ref[...], 2)
    o_ref[...] = _dot(a_hi, w_hi) + _dot(a_lo, w_hi) + _dot(a_hi, w_lo) + b_ref[...]


def _ada_mods(c8, w_ada, b_ada):
    tn = 1536
    return pl.pallas_call(
        _ada_body,
        grid=(DEPTH, 6 * D_MODEL // tn),
        in_specs=[
            pl.BlockSpec((8, D_MODEL), lambda l, j: (0, 0)),
            pl.BlockSpec((None, D_MODEL, tn), lambda l, j: (l, 0, j)),
            pl.BlockSpec((None, 1, tn), lambda l, j: (l, 0, j)),
        ],
        out_specs=pl.BlockSpec((None, 8, tn), lambda l, j: (l, 0, j)),
        out_shape=jax.ShapeDtypeStruct((DEPTH, 8, 6 * D_MODEL), F32),
        compiler_params=_params(2),
        name="ada_mods",
    )(c8, w_ada, b_ada.reshape(DEPTH, 1, 6 * D_MODEL))


def _modnorm(x, g, shift, scale):
    ms = jnp.mean(x * x, axis=-1, keepdims=True)
    return (x * lax.rsqrt(ms + EPS) * g) * (1.0 + scale) + shift


def _mod_row(n_ctx_tiles, tiles_per_latent):
    def f(i):
        return jnp.where(i < n_ctx_tiles, 0, 1 + (i - n_ctx_tiles) // tiles_per_latent)
    return f


def _row_specs(n_parts, tm, width, n_ctx_tiles):
    if n_parts == 1:
        return [pl.BlockSpec((tm, width), lambda i, *_: (i, 0))]
    return [pl.BlockSpec((tm, width), lambda i, *_: (jnp.minimum(i, n_ctx_tiles - 1), 0)),
            pl.BlockSpec((tm, width), lambda i, *_: (jnp.maximum(i - n_ctx_tiles, 0), 0))]


def _pick(is_ctx, refs):
    if len(refs) == 1:
        return refs[0][...]
    return jnp.where(is_ctx, refs[0][...], refs[1][...])


def _inproj_body(n_x, n_prev, n_ctx_tiles, ctx_len, *refs):
    x_refs = refs[:n_x]
    mod_ref, g_ref, w_ref = refs[n_x:n_x + 3]
    prev_refs = refs[n_x + 3:n_x + 3 + 2 * n_prev]
    oa_ref, oh_ref, k_ref, v_ref = refs[-4:]
    is_ctx = pl.program_id(0) < n_ctx_tiles
    x = _pick(is_ctx, x_refs)
    h = _modnorm(x, g_ref[...], mod_ref[0:1, :], mod_ref[1:2, :])
    proj = _dot(h.astype(BF16), w_ref[...])
    oa_ref[...] = proj[:, :PROJ_A_WIDTH].astype(BF16)
    oh_ref[...] = proj[:, PROJ_A_WIDTH:]

    @pl.when(is_ctx)
    def _():
        per_layer = ctx_len * DA_HEADS
        for s in range(proj.shape[0] // ctx_len):
            base = s * (n_prev + 1) * per_layer
            for l in range(n_prev):
                dst = slice(base + l * per_layer, base + (l + 1) * per_layer)
                src = slice(s * per_layer, (s + 1) * per_layer)
                k_ref[dst, :] = prev_refs[2 * l][src, :]
                v_ref[dst, :] = prev_refs[2 * l + 1][src, :]
            tok = slice(s * ctx_len, (s + 1) * ctx_len)
            for h in range(DA_HEADS):
                rows = pl.ds(base + n_prev * per_layer + h, ctx_len, stride=DA_HEADS)
                k_ref[rows, :] = proj[tok, COL_K + h * DA_VDIM:COL_K + (h + 1) * DA_VDIM]
                v_ref[rows, :] = proj[tok, COL_V + h * DA_VDIM:COL_V + (h + 1) * DA_VDIM]


def _inproj(xs, mod, g, w, prev_caches, n_ctx, ctx_len, lat_len):
    t = sum(x.shape[0] for x in xs)
    tm = TOKEN_TILE
    seqs = tm // ctx_len
    n_ctx_tiles = n_ctx // seqs
    n_prev = len(prev_caches) // 2
    row = _mod_row(n_ctx_tiles, lat_len // tm)
    tile_rows = seqs * ctx_len * DA_HEADS
    tile_block = lambda i: (jnp.minimum(i, n_ctx_tiles - 1), 0)
    cache_spec = pl.BlockSpec(((n_prev + 1) * tile_rows, DA_VDIM), tile_block)
    cache_shape = jax.ShapeDtypeStruct((n_ctx_tiles * (n_prev + 1) * tile_rows, DA_VDIM), F32)
    in_specs = _row_specs(len(xs), tm, D_MODEL, n_ctx_tiles) + [
        pl.BlockSpec((None, 6, D_MODEL), lambda i: (row(i), 0, 0)),
        pl.BlockSpec((1, D_MODEL), lambda i: (0, 0)),
        pl.BlockSpec((D_MODEL, PROJ_A_WIDTH + PROJ_H_WIDTH), lambda i: (0, 0)),
    ] + [pl.BlockSpec((tile_rows, DA_VDIM), tile_block)] * (2 * n_prev)
    return pl.pallas_call(
        functools.partial(_inproj_body, len(xs), n_prev, n_ctx_tiles, ctx_len),
        grid=(t // tm,),
        in_specs=in_specs,
        out_specs=[pl.BlockSpec((tm, PROJ_A_WIDTH), lambda i: (i, 0)),
                   pl.BlockSpec((tm, PROJ_H_WIDTH), lambda i: (i, 0)), cache_spec, cache_spec],
        out_shape=[jax.ShapeDtypeStruct((t, PROJ_A_WIDTH), BF16), jax.ShapeDtypeStruct((t, PROJ_H_WIDTH), F32),
                   cache_shape, cache_shape],
        compiler_params=_params(1),
        name="inproj",
    )(*xs, mod, g, w, *prev_caches)


def _channel_dft(z, cc, sc):
    return _dot(z, jnp.concatenate([cc, sc], axis=1)).astype(BF16)


def _fourier_rows(zab, cl, sl, w_f):
    y = _dot(jnp.concatenate([cl, -sl], axis=1),
             jnp.concatenate([zab[:, :FN_WIDTH], zab[:, FN_WIDTH:]], axis=0))
    return _dot(y.astype(BF16), w_f)


def _fourier_body(u_ref, cl_ref, sl_ref, cc_ref, sc_ref, w_ref, o_ref):
    zab = _channel_dft(u_ref[...].astype(BF16), cc_ref[...], sc_ref[...])
    o_ref[...] = _fourier_rows(zab, cl_ref[...], sl_ref[...], w_ref[...]).astype(o_ref.dtype)


def _dft_tables(n, block):
    i = np.arange(n)
    prod = (i[:, None] % block) * (i[None, :] % block) % block
    ang = prod.astype(np.float64) * (2.0 * math.pi / block)
    same = (i[:, None] // block) == (i[None, :] // block)
    scale = 1.0 / math.sqrt(block)
    c = np.where(same, np.cos(ang) * scale, 0.0).astype(np.float32)
    s = np.where(same, np.sin(ang) * scale, 0.0).astype(np.float32)
    return jnp.asarray(c).astype(BF16), jnp.asarray(s).astype(BF16)


def _fourier(proj, row0, n_seq, seq_len, w_f):
    cl, sl = _dft_tables(seq_len, seq_len)
    cc, sc = _dft_tables(FN_WIDTH, FN_GROUP_DIM)
    blk0 = row0 // seq_len
    full = lambda shape: pl.BlockSpec(shape, lambda i: (0, 0))
    return pl.pallas_call(
        _fourier_body,
        grid=(n_seq,),
        in_specs=[
            pl.BlockSpec((seq_len, FN_WIDTH), lambda i: (blk0 + i, COL_FN // FN_WIDTH)),
            full((seq_len, seq_len)), full((seq_len, seq_len)),
            full((FN_WIDTH, FN_WIDTH)), full((FN_WIDTH, FN_WIDTH)), full((FN_WIDTH, FN_WIDTH)),
        ],
        out_specs=pl.BlockSpec((seq_len, FN_WIDTH), lambda i: (i, 0)),
        out_shape=jax.ShapeDtypeStruct((n_seq * seq_len, FN_WIDTH), BF16),
        compiler_params=_params(1),
        name=f"fourier_{seq_len}",
    )(proj, cl, sl, cc, sc, w_f)


def _lambda_full(lmb, lam_init):
    a = jnp.sum(lmb[0:1, :] * lmb[1:2, :], axis=-1, keepdims=True)
    b = jnp.sum(lmb[2:3, :] * lmb[3:4, :], axis=-1, keepdims=True)
    return jnp.exp(a) - jnp.exp(b) + lam_init


def _diff_head(q, ks, vs, lam, g, lam_init):
    vas = [jnp.concatenate([v.astype(BF16), jnp.ones(v.shape, BF16)], axis=1) for v in vs]
    lq = q.shape[0]
    first_map = (lax.broadcasted_iota(jnp.int32, (1, 2 * DA_HALF), 1) < DA_HALF).astype(BF16)
    qb = q.astype(BF16)
    q_stack = jnp.concatenate([qb * first_map, qb * (1 - first_map)], axis=0)
    parts = [_dot_nt(q_stack, k.astype(BF16)) for k in ks]
    mx = parts[0].max(axis=-1, keepdims=True)
    for p in parts[1:]:
        mx = jnp.maximum(mx, p.max(axis=-1, keepdims=True))
    oa = _dot(jnp.exp(parts[0] - mx).astype(BF16), vas[0])
    for p, va in zip(parts[1:], vas[1:]):
        oa = oa + _dot(jnp.exp(p - mx).astype(BF16), va)
    o = oa[:, :DA_VDIM] * (1.0 / oa[:, DA_VDIM:DA_VDIM + 1])
    a = o[:lq] - lam * o[lq:]
    ms = jnp.mean(a * a, axis=-1, keepdims=True)
    return a * lax.rsqrt(ms + EPS) * g * (1.0 - lam_init)


def _rope(x, cos, sin):
    lane = lax.broadcasted_iota(jnp.int32, x.shape, 1)
    first = ((lane >> 4) & 1) == 0
    rot = jnp.where(first, -pltpu.roll(x, 128 - DA_HALF // 4, 1), pltpu.roll(x, DA_HALF // 4, 1))
    return x * cos + rot * sin


def _attn_lat_body(lam_init, q_ref, k_ref, v_ref, kc_ref, vc_ref, cq_ref, sq_ref, ck_ref, sk_ref,
                   lmb_ref, g_ref, o_ref):
    lam = _lambda_full(lmb_ref[...], lam_init)
    scale = DA_HALF ** -0.5
    for h in range(DA_HEADS):
        sl = slice(h * DA_VDIM, (h + 1) * DA_VDIM)
        q = _rope(q_ref[:, sl].astype(F32), cq_ref[...], sq_ref[...]) * scale
        k = _rope(k_ref[:, sl].astype(F32), ck_ref[...], sk_ref[...])
        o = _diff_head(q, [k, kc_ref[:, sl]], [v_ref[:, sl], vc_ref[:, sl]], lam, g_ref[...], lam_init)
        o_ref[:, sl] = o.astype(o_ref.dtype)


def _attn_lat(proj, row0, n_seq, seq_len, cache_k, cache_v, layer, cos, sin, lmb, g, lam_init):
    tq = 512
    nq = seq_len // tq
    past = cache_k.shape[2]
    qb0 = row0 // tq
    kb0 = row0 // seq_len
    cache_spec = pl.BlockSpec((None, None, past, DA_WIDTH), lambda b, j: (b, layer, 0, 0))
    kv_spec = lambda c: pl.BlockSpec((seq_len, DA_WIDTH), lambda b, j: (kb0 + b, c))
    return pl.pallas_call(
        functools.partial(_attn_lat_body, lam_init),
        grid=(n_seq, nq),
        in_specs=[
            pl.BlockSpec((tq, DA_WIDTH), lambda b, j: (qb0 + b * nq + j, COL_Q // DA_WIDTH)),
            kv_spec(COL_K // DA_WIDTH), kv_spec(COL_V // DA_WIDTH),
            cache_spec, cache_spec,
            pl.BlockSpec((tq, DA_VDIM), lambda b, j: (j, 0)),
            pl.BlockSpec((tq, DA_VDIM), lambda b, j: (j, 0)),
            pl.BlockSpec((seq_len, DA_VDIM), lambda b, j: (0, 0)),
            pl.BlockSpec((seq_len, DA_VDIM), lambda b, j: (0, 0)),
            pl.BlockSpec((4, DA_HALF), lambda b, j: (0, 0)),
            pl.BlockSpec((1, DA_VDIM), lambda b, j: (0, 0)),
        ],
        out_specs=pl.BlockSpec((tq, DA_WIDTH), lambda b, j: (b * nq + j, 0)),
        out_shape=jax.ShapeDtypeStruct((n_seq * seq_len, DA_WIDTH), BF16),
        compiler_params=_params(2),
        name="attn_lat",
    )(proj, proj, proj, cache_k, cache_v, cos, sin, cos, sin, lmb, g)


def _rope_tables(n_tokens):
    rows = n_tokens // GRID_W
    row = np.repeat(np.arange(rows, dtype=np.float64), GRID_W)
    col = np.tile(np.arange(GRID_W, dtype=np.float64), rows)
    axis_dim = DA_HALF // 2
    inv_freq = ROPE_THETA ** (-np.arange(0, axis_dim, 2, dtype=np.float64) / axis_dim)
    ang_r = row[:, None] * inv_freq[None, :]
    ang_c = col[:, None] * inv_freq[None, :]
    ang = np.concatenate([ang_r, ang_r, ang_c, ang_c] * 2, axis=-1)
    return jnp.asarray(np.cos(ang).astype(np.float32)), jnp.asarray(np.sin(ang).astype(np.float32))


def _hgrn_tables():
    gl = CHUNKS_PER_GROUP * CHUNK
    t = np.arange(gl)
    same = (t[:, None] // CHUNK) == (t[None, :] // CHUNK)
    fwd = same & (t[None, :] <= t[:, None])
    bwd = same & (t[None, :] >= t[:, None])
    f = np.arange(HG_WIDTH)
    bd = (f[:, None] // HG_DK) == (f[None, :] // HG_DK)
    tri = jnp.asarray(np.stack([fwd, bwd]).astype(np.float32)).astype(BF16)
    causal = jnp.asarray(np.stack([fwd, bwd]).astype(np.float32))
    return tri, causal, jnp.asarray(bd.astype(np.float32))


def _hgrn2_body(n_seq, groups_per_seq, has_s0, *refs):
    (hq_ref, hi_ref, hf_ref, hb_ref, hg_ref, lb_ref, ng_ref, tri_ref, causal_ref, bd_ref) = refs[:10]
    if has_s0:
        s0_ref, o_ref, oi_scr, qe_scr, u_scr, st_scr, dec_scr = refs[10:]
        so_ref = None
    else:
        o_ref, so_ref, oi_scr, qe_scr, u_scr, st_scr, dec_scr = refs[10:]
        s0_ref = None
    c = CHUNK
    w = HG_WIDTH
    gc = CHUNKS_PER_GROUP
    gl = gc * c
    n_groups = n_seq * groups_per_seq
    chunks_per_seq = groups_per_seq * gc
    lane = lax.broadcasted_iota(jnp.int32, (1, w), 1)
    dk_bits = HG_DK.bit_length() - 1
    head_masks_b = [((lane >> dk_bits) == h).astype(BF16) for h in range(HG_HEADS)]
    bd = bd_ref[...] > 0.5
    edge_rows = (c - 1, 0)
    fp_refs = (hf_ref, hb_ref)

    def per_chunk_row(x, row):
        return jnp.concatenate([jnp.broadcast_to(x[k * c + row:k * c + row + 1, :], (c, w)) for k in range(gc)],
                               axis=0)

    def group_terms(gi):
        rows = pl.ds(gi * gl if isinstance(gi, int) else pl.multiple_of(gi * gl, gl), gl)
        hq = hq_ref[rows, :]
        q = _silu_t(hq, HG_DK ** -0.5)
        v = hi_ref[rows, :].astype(BF16)
        v_stack = jnp.concatenate([v * hm for hm in head_masks_b], axis=0)
        for d in range(2):
            lb = lb_ref[d:d + 1, :]
            fp = fp_refs[d][rows, :]
            lsig = jnp.minimum(fp, 0.0) - jnp.log(1.0 + jnp.exp(-jnp.abs(fp)))
            la = jnp.log(lb)
            lbb = jnp.log(1.0 - lb) + lsig
            logf = jnp.maximum(la, lbb) + jnp.log(1.0 + jnp.exp(-jnp.abs(la - lbb)))
            kk = (0.5 * (1.0 - lb)) * (1.0 - jnp.tanh(0.5 * fp))
            g_hi, g_lo = _split(logf, 2)
            cum = _dot(tri_ref[d], g_hi) + _dot(tri_ref[d], g_lo)
            total = per_chunk_row(cum, edge_rows[d])
            ref = per_chunk_row(cum, c // 2)
            qc = (q * jnp.exp(cum - ref)).astype(BF16)
            kc = (kk * jnp.exp(ref - cum)).astype(BF16)
            ke = (kk * jnp.exp(total - cum)).astype(BF16)
            qe_scr[d, rows, :] = (q * jnp.exp(cum)).astype(BF16)
            kc_stack = jnp.concatenate([kc * hm for hm in head_masks_b], axis=0)
            a = _dot_nt(qc, kc_stack)
            keep = causal_ref[d] > 0.5
            a = jnp.concatenate([jnp.where(keep, a[:, h * gl:(h + 1) * gl], 0.0) for h in range(HG_HEADS)],
                                axis=1).astype(BF16)
            oi_scr[d, rows, :] = _dot(a, v_stack)
            for k in range(gc):
                ck = slice(k * c, (k + 1) * c)
                i = gi * gc + k
                dec_scr[d, pl.ds(i, 1), :] = jnp.exp(cum[k * c + edge_rows[d]:k * c + edge_rows[d] + 1, :])
                u_scr[d, i] = jnp.where(bd, _dot_tn(v[ck, :], ke[ck, :]), 0.0)

    if n_groups <= 2:
        for gi in range(n_groups):
            group_terms(gi)
    else:
        def terms_step(gi, carry):
            group_terms(gi)
            return carry

        lax.fori_loop(0, n_groups, terms_step, 0)

    slab = 32
    for s in range(n_seq):
        first = s * chunks_per_seq
        for d in range(2):
            order = range(chunks_per_seq) if d == 0 else range(chunks_per_seq - 1, -1, -1)
            for r in range(w // slab):
                rs = slice(r * slab, (r + 1) * slab)
                st = s0_ref[s, d, rs, :] if has_s0 else jnp.zeros((slab, w), F32)
                for j in order:
                    i = first + j
                    st_scr[d, i, rs, :] = st.astype(BF16)
                    st = st * dec_scr[d, i:i + 1, :] + u_scr[d, i, rs, :]
                if so_ref is not None:
                    h = (r * slab) // HG_DK
                    off = (r * slab) % HG_DK
                    so_ref[s, d, h, off:off + slab, :] = st[:, h * HG_DK:(h + 1) * HG_DK]

    ones_bd = bd.astype(BF16)

    def finish(gi):
        rows = pl.ds(gi * gl if isinstance(gi, int) else pl.multiple_of(gi * gl, gl), gl)
        inter = []
        for k in range(gc):
            i = gi * gc + k
            ck = pl.ds(gi * gl + k * c if isinstance(gi, int) else pl.multiple_of(gi * gl + k * c, c), c)
            inter.append(_dot_nt(qe_scr[0, ck, :], st_scr[0, i]) + _dot_nt(qe_scr[1, ck, :], st_scr[1, i]))
        o = oi_scr[0, rows, :] + oi_scr[1, rows, :] + jnp.concatenate(inter, axis=0)
        sq = _split(o * o, 2)
        ms = (_dot(sq[0], ones_bd) + _dot(sq[1], ones_bd)) * (1.0 / HG_DK)
        y = o * lax.rsqrt(ms + EPS) * ng_ref[...]
        hg = hg_ref[rows, :]
        o_ref[rows, :] = (y * _silu_t(hg)).astype(o_ref.dtype)

    if n_groups <= 2:
        for gi in range(n_groups):
            finish(gi)
    else:
        def finish_step(gi, carry):
            finish(gi)
            return carry

        lax.fori_loop(0, n_groups, finish_step, 0)


def _hgrn_scratch(rows):
    n_chunks = rows // CHUNK
    return [pltpu.VMEM((2, rows, HG_WIDTH), F32),
            pltpu.VMEM((2, rows, HG_WIDTH), BF16),
            pltpu.VMEM((2, n_chunks, HG_WIDTH, HG_WIDTH), F32),
            pltpu.VMEM((2, n_chunks, HG_WIDTH, HG_WIDTH), BF16),
            pltpu.VMEM((2, max(n_chunks, 8), HG_WIDTH), F32)]


def _hgrn2(proj_h, row0, n_seq, seq_len, seqs_per_step, lb, ng, s0t):
    gl = CHUNKS_PER_GROUP * CHUNK
    groups_per_seq = seq_len // gl
    rows = seqs_per_step * seq_len
    blk0 = row0 // rows
    tri, causal, bd = _hgrn_tables()
    col = lambda c0: pl.BlockSpec((rows, HG_WIDTH), lambda i: (blk0 + i, c0 // HG_WIDTH))
    const = lambda shape: pl.BlockSpec(shape, lambda i: (0,) * len(shape))
    in_specs = [col(COL_HQ), col(COL_HI), col(COL_HF), col(COL_HB), col(COL_HG),
                const((2, HG_WIDTH)), const((1, HG_WIDTH)),
                const((2, gl, gl)), const((2, gl, gl)), const((HG_WIDTH, HG_WIDTH))]
    args = [proj_h] * 5 + [lb, ng, tri, causal, bd]
    out_specs = [pl.BlockSpec((rows, HG_WIDTH), lambda i: (i, 0))]
    out_shape = [jax.ShapeDtypeStruct((n_seq * seq_len, HG_WIDTH), BF16)]
    if s0t is not None:
        in_specs.append(pl.BlockSpec((seqs_per_step, 2, HG_WIDTH, HG_WIDTH), lambda i: (i, 0, 0, 0)))
        args.append(s0t)
    else:
        out_specs.append(pl.BlockSpec((seqs_per_step, 2, HG_HEADS, HG_DK, HG_DK), lambda i: (i, 0, 0, 0, 0)))
        out_shape.append(jax.ShapeDtypeStruct((n_seq, 2, HG_HEADS, HG_DK, HG_DK), F32))
    return pl.pallas_call(
        functools.partial(_hgrn2_body, seqs_per_step, groups_per_seq, s0t is not None),
        grid=(n_seq // seqs_per_step,),
        in_specs=in_specs,
        out_specs=out_specs,
        out_shape=out_shape,
        scratch_shapes=_hgrn_scratch(rows),
        compiler_params=_params(1),
        name=f"hgrn_{seq_len}",
    )(*args)


def _ctx_mixers_body(n_seq, seq_len, lam_init, *refs):
    (q_ref, k_ref, v_ref, u_ref, hq_ref, hi_ref, hf_ref, hb_ref, hg_ref,
     cl_ref, sl_ref, cc_ref, sc_ref, wf_ref, lmb_ref, ag_ref,
     lb_ref, ng_ref, tri_ref, causal_ref, bd_ref,
     fn_ref, a_ref, ho_ref, so_ref) = refs[:25]
    scratch = refs[25:]

    zab = _channel_dft(u_ref[...].astype(BF16), cc_ref[...], sc_ref[...])
    for s in range(n_seq):
        rs = slice(s * seq_len, (s + 1) * seq_len)
        fn_ref[rs, :] = _fourier_rows(zab[rs, :], cl_ref[...], sl_ref[...], wf_ref[...]).astype(fn_ref.dtype)

    lam = _lambda_full(lmb_ref[...], lam_init)
    scale = DA_HALF ** -0.5
    for s in range(n_seq):
        rs = slice(s * seq_len, (s + 1) * seq_len)
        for h in range(DA_HEADS):
            sl = slice(h * DA_VDIM, (h + 1) * DA_VDIM)
            o = _diff_head(q_ref[rs, sl] * scale, [k_ref[rs, sl]], [v_ref[rs, sl]], lam, ag_ref[...], lam_init)
            a_ref[rs, sl] = o.astype(a_ref.dtype)

    _hgrn2_body(n_seq, seq_len // (CHUNKS_PER_GROUP * CHUNK), False,
                hq_ref, hi_ref, hf_ref, hb_ref, hg_ref, lb_ref, ng_ref, tri_ref, causal_ref, bd_ref,
                ho_ref, so_ref, *scratch)


def _ctx_mixers(proj_a, proj_h, n_seq, seq_len, seqs_per_step, w_f, lmb, ag, lam_init, lb, ng):
    rows = seqs_per_step * seq_len
    gl = CHUNKS_PER_GROUP * CHUNK
    cl, sl = _dft_tables(seq_len, seq_len)
    cc, sc = _dft_tables(FN_WIDTH, FN_GROUP_DIM)
    tri, causal, bd = _hgrn_tables()
    col = lambda c0, width: pl.BlockSpec((rows, width), lambda i: (i, c0 // width))
    const = lambda shape: pl.BlockSpec(shape, lambda i: (0,) * len(shape))
    out_rows = lambda width: pl.BlockSpec((rows, width), lambda i: (i, 0))
    n_tok = n_seq * seq_len
    return pl.pallas_call(
        functools.partial(_ctx_mixers_body, seqs_per_step, seq_len, lam_init),
        grid=(n_seq // seqs_per_step,),
        in_specs=[col(COL_Q, DA_WIDTH), col(COL_K, DA_WIDTH), col(COL_V, DA_WIDTH), col(COL_FN, FN_WIDTH),
                  col(COL_HQ, HG_WIDTH), col(COL_HI, HG_WIDTH), col(COL_HF, HG_WIDTH), col(COL_HB, HG_WIDTH),
                  col(COL_HG, HG_WIDTH),
                  const((seq_len, seq_len)), const((seq_len, seq_len)),
                  const((FN_WIDTH, FN_WIDTH)), const((FN_WIDTH, FN_WIDTH)), const((FN_WIDTH, FN_WIDTH)),
                  const((4, DA_HALF)), const((1, DA_VDIM)),
                  const((2, HG_WIDTH)), const((1, HG_WIDTH)),
                  const((2, gl, gl)), const((2, gl, gl)), const((HG_WIDTH, HG_WIDTH))],
        out_specs=[out_rows(FN_WIDTH), out_rows(DA_WIDTH), out_rows(HG_WIDTH),
                   pl.BlockSpec((seqs_per_step, 2, HG_HEADS, HG_DK, HG_DK), lambda i: (i, 0, 0, 0, 0))],
        out_shape=[jax.ShapeDtypeStruct((n_tok, FN_WIDTH), BF16),
                   jax.ShapeDtypeStruct((n_tok, DA_WIDTH), BF16),
                   jax.ShapeDtypeStruct((n_tok, HG_WIDTH), BF16),
                   jax.ShapeDtypeStruct((n_seq, 2, HG_HEADS, HG_DK, HG_DK), F32)],
        scratch_shapes=_hgrn_scratch(rows),
        compiler_params=_params(1),
        name="ctx_mixers",
    )(*([proj_a] * 4 + [proj_h] * 5), cl, sl, cc, sc, w_f, lmb, ag, lb, ng, tri, causal, bd)


def _route_gates(lt, bias):
    per = N_EXPERTS // N_GROUPS
    tt = lt.shape[1]
    neg = -jnp.inf
    assert per == N_GROUPS == 8
    gi = lax.broadcasted_iota(jnp.int32, (N_GROUPS, tt), 0).astype(F32)
    s_j, b_j = [], []
    for j in range(per):
        s = _sigmoid(lt[j * N_GROUPS:(j + 1) * N_GROUPS, :])
        s_j.append(s)
        b_j.append(s + bias[j * N_GROUPS:(j + 1) * N_GROUPS, :])
    m1 = functools.reduce(jnp.maximum, b_j)
    i1 = functools.reduce(jnp.minimum, [jnp.where(b_j[j] == m1, float(j), float(per)) for j in range(per)])
    m2 = functools.reduce(jnp.maximum, [jnp.where(i1 == float(j), neg, b_j[j]) for j in range(per)])
    gs = m1 + m2
    gsel = jnp.zeros((N_GROUPS, tt), jnp.bool_)
    for _ in range(TOPK_GROUPS):
        m = gs.max(axis=0, keepdims=True)
        idx = jnp.where(gs == m, gi, float(N_GROUPS)).min(axis=0, keepdims=True)
        hit = gi == idx
        gsel = gsel | hit
        gs = jnp.where(hit, neg, gs)
    x_j = [jnp.where(gsel, b_j[j], neg) for j in range(per)]
    e_j = [gi * per + j for j in range(per)]
    sel_j = [jnp.zeros((N_GROUPS, tt), jnp.bool_) for _ in range(per)]
    for _ in range(TOP_K):
        m = functools.reduce(jnp.maximum, x_j).max(axis=0, keepdims=True)
        idx = functools.reduce(jnp.minimum, [jnp.where(x_j[j] == m, e_j[j], float(N_EXPERTS))
                                             for j in range(per)]).min(axis=0, keepdims=True)
        for j in range(per):
            hit = e_j[j] == idx
            sel_j[j] = sel_j[j] | hit
            x_j[j] = jnp.where(hit, neg, x_j[j])
    w_j = [jnp.where(sel_j[j], s_j[j], 0.0) for j in range(per)]
    denom = functools.reduce(lambda a, b: a + b, w_j).sum(axis=0, keepdims=True)
    gates_t = jnp.concatenate([w / denom * ROUTED_SCALE for w in w_j], axis=0)
    r_io = lax.broadcasted_iota(jnp.int32, (N_EXPERTS, N_EXPERTS), 0)
    e_io = lax.broadcasted_iota(jnp.int32, (N_EXPERTS, N_EXPERTS), 1)
    eye = (e_io == (r_io & (N_GROUPS - 1)) * per + (r_io >> 3)).astype(BF16)
    p = _split(gates_t, 3)
    return _dot_tn(p[0], eye) + _dot_tn(p[1], eye) + _dot_tn(p[2], eye)


def _outproj_body(n_x, n_ctx_tiles, *refs):
    fn_refs, a_refs, hg_refs = refs[0:2], refs[2:4], refs[4:6]
    x_refs = refs[6:6 + n_x]
    mod_ref, g_ref, w_ref, wr_ref, rb_ref, xo_ref, h_ref, gate_ref = refs[6 + n_x:]
    is_ctx = pl.program_id(0) < n_ctx_tiles
    mix = _dot(jnp.concatenate([_pick(is_ctx, fn_refs), _pick(is_ctx, a_refs), _pick(is_ctx, hg_refs)], axis=1),
               w_ref[...])
    x = _pick(is_ctx, x_refs) + mod_ref[2:3, :] * mix
    xo_ref[...] = x
    h = _modnorm(x, g_ref[...], mod_ref[3:4, :], mod_ref[4:5, :])
    h_ref[...] = h.astype(BF16)
    h_hi, h_lo = _split(h, 2)
    w_hi, w_lo = _split(wr_ref[...], 2)
    logits_t = _dot_nt(w_hi, h_hi) + _dot_nt(w_lo, h_hi) + _dot_nt(w_hi, h_lo)
    gate_ref[...] = _route_gates(logits_t, rb_ref[...])


def _outproj(fn, a, hg, xs, mod, g, w_out, w_router_t, router_bias, n_ctx_tok, lat_len):
    t = sum(x.shape[0] for x in xs)
    tm = TOKEN_TILE
    n_ctx_tiles = n_ctx_tok // tm
    row = _mod_row(n_ctx_tiles, lat_len // tm)
    rows = lambda width: pl.BlockSpec((tm, width), lambda i: (i, 0))
    parts = lambda n, width: _row_specs(n, tm, width, n_ctx_tiles)
    return pl.pallas_call(
        functools.partial(_outproj_body, len(xs), n_ctx_tiles),
        grid=(t // tm,),
        in_specs=parts(2, FN_WIDTH) + parts(2, DA_WIDTH) + parts(2, HG_WIDTH) + parts(len(xs), D_MODEL) + [
            pl.BlockSpec((None, 6, D_MODEL), lambda i: (row(i), 0, 0)),
            pl.BlockSpec((1, D_MODEL), lambda i: (0, 0)),
            pl.BlockSpec((D_MODEL, D_MODEL), lambda i: (0, 0)),
            pl.BlockSpec((N_EXPERTS, D_MODEL), lambda i: (0, 0)),
            pl.BlockSpec((N_EXPERTS, 1), lambda i: (0, 0)),
        ],
        out_specs=[rows(D_MODEL), rows(D_MODEL), rows(N_EXPERTS)],
        out_shape=[jax.ShapeDtypeStruct((t, D_MODEL), F32),
                   jax.ShapeDtypeStruct((t, D_MODEL), BF16),
                   jax.ShapeDtypeStruct((t, N_EXPERTS), F32)],
        compiler_params=_params(1),
        name="outproj",
    )(*fn, *a, *hg, *xs, mod, g, w_out, w_router_t, router_bias)


def _moe_body(final_norm, n_ctx_tiles, h_ref, gate_ref, wg_ref, wu_ref, wd_ref, sg_ref, su_ref, sd_ref,
              x_ref, mod_ref, fg_ref, *out_and_scratch):
    acc_ref = out_and_scratch[-1]
    o_refs = out_and_scratch[:-1]
    is_ctx = pl.program_id(0) < n_ctx_tiles
    j = pl.program_id(1)
    h = h_ref[...]

    def act_of(wg, wu, gate):
        gu = _dot(h, jnp.concatenate([wg.astype(BF16), wu.astype(BF16)], axis=1))
        a = _silu_t(gu[:, :D_EXPERT]) * gu[:, D_EXPERT:]
        return a if gate is None else a * gate

    @pl.when(j == 0)
    def _():
        acc_ref[...] = _dot(act_of(sg_ref[...], su_ref[...], None).astype(BF16), sd_ref[...].astype(BF16))

    gates = gate_ref[...]
    expert_of_lane = lax.broadcasted_iota(jnp.int32, gates.shape, 1)

    def gate_col(p):
        e = j * EXPERTS_PER_STEP + p
        return jnp.sum(jnp.where(expert_of_lane == e, gates, 0.0), axis=1, keepdims=True)

    acts = [act_of(wg_ref[p], wu_ref[p], gate_col(p)).astype(BF16) for p in range(EXPERTS_PER_STEP)]
    wd = wd_ref[...].astype(BF16).reshape(EXPERTS_PER_STEP * D_EXPERT, D_MODEL)
    acc_ref[...] += _dot(jnp.concatenate(acts, axis=1), wd)

    @pl.when(j == pl.num_programs(1) - 1)
    def _():
        x = x_ref[...] + mod_ref[5:6, :] * acc_ref[...]
        if not final_norm:
            o_refs[0][...] = x
        else:
            ms = jnp.mean(x * x, axis=-1, keepdims=True)
            y = x * lax.rsqrt(ms + EPS) * fg_ref[...]

            @pl.when(is_ctx)
            def _():
                o_refs[0][...] = y

            @pl.when(jnp.logical_not(is_ctx))
            def _():
                o_refs[1][...] = y


def _moe(h, gates, w_gate, w_up, w_down2, ws_gate, ws_up, ws_down, layer, x, mod, final_g, final_norm,
         n_ctx_tok, lat_len):
    t = x.shape[0]
    tm = MOE_TILE
    eps_ = EXPERTS_PER_STEP
    n_ctx_tiles = n_ctx_tok // tm
    row = _mod_row(n_ctx_tiles, lat_len // tm)
    once = pl.Buffered(1)
    if final_norm:
        last_ctx = n_ctx_tiles - 1
        out_specs = [pl.BlockSpec((tm, D_MODEL), lambda i, j: (jnp.minimum(i, last_ctx), 0), pipeline_mode=once),
                     pl.BlockSpec((tm, D_MODEL), lambda i, j: (jnp.maximum(i - n_ctx_tiles, 0), 0),
                                  pipeline_mode=once)]
        out_shape = [jax.ShapeDtypeStruct((n_ctx_tok, D_MODEL), F32),
                     jax.ShapeDtypeStruct((t - n_ctx_tok, D_MODEL), F32)]
    else:
        out_specs = [pl.BlockSpec((tm, D_MODEL), lambda i, j: (i, 0), pipeline_mode=once)]
        out_shape = [jax.ShapeDtypeStruct((t, D_MODEL), F32)]
    return pl.pallas_call(
        functools.partial(_moe_body, final_norm, n_ctx_tiles),
        grid=(t // tm, N_EXPERTS // eps_),
        in_specs=[
            pl.BlockSpec((tm, D_MODEL), lambda i, j: (i, 0)),
            pl.BlockSpec((tm, N_EXPERTS), lambda i, j: (i, 0)),
            pl.BlockSpec((None, eps_, D_MODEL, D_EXPERT), lambda i, j: (layer, j, 0, 0)),
            pl.BlockSpec((None, eps_, D_MODEL, D_EXPERT), lambda i, j: (layer, j, 0, 0)),
            pl.BlockSpec((None, eps_ // 2, 2 * D_EXPERT, D_MODEL), lambda i, j: (layer, j, 0, 0)),
            pl.BlockSpec((None, D_MODEL, D_EXPERT), lambda i, j: (layer, 0, 0), pipeline_mode=once),
            pl.BlockSpec((None, D_MODEL, D_EXPERT), lambda i, j: (layer, 0, 0), pipeline_mode=once),
            pl.BlockSpec((None, D_EXPERT, D_MODEL), lambda i, j: (layer, 0, 0), pipeline_mode=once),
            pl.BlockSpec((tm, D_MODEL), lambda i, j: (i, 0), pipeline_mode=once),
            pl.BlockSpec((None, 6, D_MODEL), lambda i, j: (row(i), 0, 0)),
            pl.BlockSpec((1, D_MODEL), lambda i, j: (0, 0)),
        ],
        out_specs=out_specs,
        out_shape=out_shape,
        scratch_shapes=[pltpu.VMEM((tm, D_MODEL), F32)],
        compiler_params=_params(2),
        name="moe",
    )(h, gates, w_gate, w_up, w_down2, ws_gate, ws_up, ws_down, x, mod, final_g)


def _block_diag_t(s):
    eye = jnp.eye(HG_HEADS, dtype=s.dtype)
    out = jnp.einsum('...hkv,hg->...hvgk', s, eye)
    return out.reshape(s.shape[:-3] + (HG_WIDTH, HG_WIDTH))


def kernel(x_prompt, x_sample, cache_k, cache_v, state_hgrn, c, c_ctx, w_ada, b_ada, norm_g, w_in,
           w_fourier, lambdas, attn_norm_g, lower_bounds, hg_norm_g, w_out, w_router, router_bias,
           w_gate, w_up, w_down, ws_gate, ws_up, ws_down, final_g):
    n_ctx, ctx_len, _ = x_prompt.shape
    n_lat, lat_len, _ = x_sample.shape
    n_ctx_tok = n_ctx * ctx_len
    past = cache_k.shape[2]

    xs = (x_prompt.reshape(n_ctx_tok, D_MODEL), x_sample.reshape(n_lat * lat_len, D_MODEL))

    c8 = jnp.zeros((8, D_MODEL), F32).at[0].set(c_ctx).at[1:1 + n_lat].set(c)
    mods = _ada_mods(c8, w_ada, b_ada).reshape(DEPTH, 8, 6, D_MODEL)

    cs = jnp.cumsum(jax.nn.softmax(lower_bounds.astype(F32), axis=0), axis=0)
    lbs = cs - cs[0:1]

    cos, sin = _rope_tables(lat_len)
    cache_k4 = cache_k.reshape(n_lat, DEPTH, past, DA_WIDTH)
    cache_v4 = cache_v.reshape(n_lat, DEPTH, past, DA_WIDTH)
    s0t = _block_diag_t(state_hgrn.astype(F32))

    qkv_end = FN_WIDTH + 3 * DA_WIDTH
    w_in_b = jnp.concatenate([w_in[:, :, FN_WIDTH:qkv_end], w_in[:, :, :FN_WIDTH], w_in[:, :, qkv_end:]],
                             axis=-1).astype(BF16)
    w_f_b = w_fourier.astype(BF16)
    w_out_b = w_out.astype(BF16)
    per = N_EXPERTS // N_GROUPS
    w_router_t = (jnp.swapaxes(w_router, 1, 2).reshape(DEPTH, N_GROUPS, per, D_MODEL)
                  .swapaxes(1, 2).reshape(DEPTH, N_EXPERTS, D_MODEL))
    bias_mm = router_bias.reshape(DEPTH, N_GROUPS, per).swapaxes(1, 2).reshape(DEPTH, N_EXPERTS, 1)
    w_down2 = w_down.reshape(DEPTH, N_EXPERTS // 2, 2 * D_EXPERT, D_MODEL)
    ng = jnp.tile(hg_norm_g, (1, HG_HEADS))

    caches = []
    new_s = []
    for l in range(DEPTH):
        lam_init = 0.8 - 0.6 * math.exp(-0.3 * l)
        mod = mods[l]
        last = l == DEPTH - 1
        proj_a, proj_h, k_l, v_l = _inproj(xs, mod, norm_g[l, 0:1], w_in_b[l], tuple(caches) if last else (),
                                           n_ctx, ctx_len, lat_len)
        caches = [k_l, v_l] if last else caches + [k_l, v_l]

        ag = attn_norm_g[l].reshape(1, DA_VDIM)
        fn_ctx, a_ctx, hg_ctx, st_ctx = _ctx_mixers(proj_a, proj_h, n_ctx, ctx_len, 2, w_f_b[l], lambdas[l], ag,
                                                    lam_init, lbs[l], ng[l:l + 1])
        fn_lat = _fourier(proj_a, n_ctx_tok, n_lat, lat_len, w_f_b[l])
        a_lat = _attn_lat(proj_a, n_ctx_tok, n_lat, lat_len, cache_k4, cache_v4, l, cos, sin, lambdas[l], ag,
                          lam_init)
        (hg_lat,) = _hgrn2(proj_h, n_ctx_tok, n_lat, lat_len, 1, lbs[l], ng[l:l + 1], s0t[:, l])
        new_s.append(jnp.swapaxes(st_ctx, -1, -2))

        x, h2, gates = _outproj((fn_ctx, fn_lat), (a_ctx, a_lat), (hg_ctx, hg_lat), xs, mod,
                                norm_g[l, 1:2], w_out_b[l], w_router_t[l], bias_mm[l], n_ctx_tok, lat_len)
        xs = _moe(h2, gates, w_gate, w_up, w_down2, ws_gate, ws_up, ws_down, l, x, mod,
                  final_g.reshape(1, D_MODEL), l == DEPTH - 1, n_ctx_tok, lat_len)

    y_prompt = xs[0].reshape(n_ctx, ctx_len, D_MODEL)
    y_sample = xs[1].reshape(n_lat, lat_len, D_MODEL)
    new_k = caches[0].reshape(n_ctx, DEPTH, ctx_len, DA_HEADS, DA_VDIM)
    new_v = caches[1].reshape(n_ctx, DEPTH, ctx_len, DA_HEADS, DA_VDIM)
    return (y_prompt, y_sample, new_k, new_v, jnp.stack(new_s, axis=1))
```

```python
import functools
import math

import numpy as np
import jax
import jax.numpy as jnp
from jax import lax
from jax.experimental import pallas as pl
from jax.experimental.pallas import tpu as pltpu

F32 = jnp.float32
BF16 = jnp.bfloat16

D_MODEL = 1024
DEPTH = 2
GRID_W = 64
FN_WIDTH = 256
FN_GROUP_DIM = 64
DA_WIDTH = 512
DA_HEADS = 4
DA_VDIM = 128
DA_HALF = 64
HG_WIDTH = 256
HG_HEADS = 4
HG_DK = 64
CHUNK = 64
ROPE_THETA = 10000.0
N_EXPERTS = 64
TOP_K = 8
N_GROUPS = 8
TOPK_GROUPS = 4
D_EXPERT = 128
ROUTED_SCALE = 2.5
EPS = 1e-6

COL_Q, COL_K, COL_V, COL_FN = 0, 512, 1024, 1536
PROJ_A_WIDTH = 1792
COL_HQ, COL_HI, COL_HF, COL_HB, COL_HG = 0, 256, 512, 768, 1024
PROJ_H_WIDTH = 1280

TOKEN_TILE = 512
MOE_TILE = 1024
EXPERTS_PER_STEP = 8
CHUNKS_PER_GROUP = 4
VMEM_LIMIT = 56 * 1024 * 1024


def _dot(a, b):
    return jnp.dot(a, b, preferred_element_type=F32)


def _dot_nt(a, b):
    return lax.dot_general(a, b, (((1,), (1,)), ((), ())), preferred_element_type=F32)


def _dot_tn(a, b):
    return lax.dot_general(a, b, (((0,), (0,)), ((), ())), preferred_element_type=F32)


def _split(x, n):
    parts = []
    r = x
    for i in range(n):
        p = r.astype(BF16)
        parts.append(p)
        if i + 1 < n:
            r = r - p.astype(F32)
    return parts


def _sigmoid(x):
    return 1.0 / (1.0 + jnp.exp(-x))


def _silu_t(x, scale=1.0):
    return (0.5 * scale) * x * (jnp.tanh(0.5 * x) + 1.0)


def _silu(x):
    return x * _sigmoid(x)


def _params(n_axes):
    return pltpu.CompilerParams(dimension_semantics=("arbitrary",) * n_axes,
                                vmem_limit_bytes=VMEM_LIMIT)


def _ada_body(c_ref, w_ref, b_ref, o_ref):
    a = _silu(c_ref[...])
    a_hi, a_lo = _split(a, 2)
    w_hi, w_lo = _split(w_ref[...], 2)
    o_ref[...] = _dot(a_hi, w_hi) + _dot(a_lo, w_hi) + _dot(a_hi, w_lo) + b_ref[...]


def _ada_mods(c8, w_ada, b_ada):
    tn = 1536
    return pl.pallas_call(
        _ada_body,
        grid=(DEPTH, 6 * D_MODEL // tn),
        in_specs=[
            pl.BlockSpec((8, D_MODEL), lambda l, j: (0, 0)),
            pl.BlockSpec((None, D_MODEL, tn), lambda l, j: (l, 0, j)),
            pl.BlockSpec((None, 1, tn), lambda l, j: (l, 0, j)),
        ],
        out_specs=pl.BlockSpec((None, 8, tn), lambda l, j: (l, 0, j)),
        out_shape=jax.ShapeDtypeStruct((DEPTH, 8, 6 * D_MODEL), F32),
        compiler_params=_params(2),
        name="ada_mods",
    )(c8, w_ada, b_ada.reshape(DEPTH, 1, 6 * D_MODEL))


def _modnorm(x, g, shift, scale):
    ms = jnp.mean(x * x, axis=-1, keepdims=True)
    return (x * lax.rsqrt(ms + EPS) * g) * (1.0 + scale) + shift


def _mod_row(n_ctx_tiles, tiles_per_latent):
    def f(i):
        return jnp.where(i < n_ctx_tiles, 0, 1 + (i - n_ctx_tiles) // tiles_per_latent)
    return f


def _row_specs(n_parts, tm, width, n_ctx_tiles):
    if n_parts == 1:
        return [pl.BlockSpec((tm, width), lambda i, *_: (i, 0))]
    return [pl.BlockSpec((tm, width), lambda i, *_: (jnp.minimum(i, n_ctx_tiles - 1), 0)),
            pl.BlockSpec((tm, width), lambda i, *_: (jnp.maximum(i - n_ctx_tiles, 0), 0))]


def _pick(is_ctx, refs):
    if len(refs) == 1:
        return refs[0][...]
    return jnp.where(is_ctx, refs[0][...], refs[1][...])


def _inproj_body(n_x, n_prev, n_ctx_tiles, ctx_len, *refs):
    x_refs = refs[:n_x]
    mod_ref, g_ref, w_ref = refs[n_x:n_x + 3]
    prev_refs = refs[n_x + 3:n_x + 3 + 2 * n_prev]
    oa_ref, oh_ref, k_ref, v_ref = refs[-4:]
    is_ctx = pl.program_id(0) < n_ctx_tiles
    x = _pick(is_ctx, x_refs)
    h = _modnorm(x, g_ref[...], mod_ref[0:1, :], mod_ref[1:2, :])
    proj = _dot(h.astype(BF16), w_ref[...])
    oa_ref[...] = proj[:, :PROJ_A_WIDTH].astype(BF16)
    oh_ref[...] = proj[:, PROJ_A_WIDTH:]

    @pl.when(is_ctx)
    def _():
        per_layer = ctx_len * DA_HEADS
        for s in range(proj.shape[0] // ctx_len):
            base = s * (n_prev + 1) * per_layer
            for l in range(n_prev):
                dst = slice(base + l * per_layer, base + (l + 1) * per_layer)
                src = slice(s * per_layer, (s + 1) * per_layer)
                k_ref[dst, :] = prev_refs[2 * l][src, :]
                v_ref[dst, :] = prev_refs[2 * l + 1][src, :]
            tok = slice(s * ctx_len, (s + 1) * ctx_len)
            for h in range(DA_HEADS):
                rows = pl.ds(base + n_prev * per_layer + h, ctx_len, stride=DA_HEADS)
                k_ref[rows, :] = proj[tok, COL_K + h * DA_VDIM:COL_K + (h + 1) * DA_VDIM]
                v_ref[rows, :] = proj[tok, COL_V + h * DA_VDIM:COL_V + (h + 1) * DA_VDIM]


def _inproj(xs, mod, g, w, prev_caches, n_ctx, ctx_len, lat_len):
    t = sum(x.shape[0] for x in xs)
    tm = TOKEN_TILE
    seqs = tm // ctx_len
    n_ctx_tiles = n_ctx // seqs
    n_prev = len(prev_caches) // 2
    row = _mod_row(n_ctx_tiles, lat_len // tm)
    tile_rows = seqs * ctx_len * DA_HEADS
    tile_block = lambda i: (jnp.minimum(i, n_ctx_tiles - 1), 0)
    cache_spec = pl.BlockSpec(((n_prev + 1) * tile_rows, DA_VDIM), tile_block)
    cache_shape = jax.ShapeDtypeStruct((n_ctx_tiles * (n_prev + 1) * tile_rows, DA_VDIM), F32)
    in_specs = _row_specs(len(xs), tm, D_MODEL, n_ctx_tiles) + [
        pl.BlockSpec((None, 6, D_MODEL), lambda i: (row(i), 0, 0)),
        pl.BlockSpec((1, D_MODEL), lambda i: (0, 0)),
        pl.BlockSpec((D_MODEL, PROJ_A_WIDTH + PROJ_H_WIDTH), lambda i: (0, 0)),
    ] + [pl.BlockSpec((tile_rows, DA_VDIM), tile_block)] * (2 * n_prev)
    return pl.pallas_call(
        functools.partial(_inproj_body, len(xs), n_prev, n_ctx_tiles, ctx_len),
        grid=(t // tm,),
        in_specs=in_specs,
        out_specs=[pl.BlockSpec((tm, PROJ_A_WIDTH), lambda i: (i, 0)),
                   pl.BlockSpec((tm, PROJ_H_WIDTH), lambda i: (i, 0)), cache_spec, cache_spec],
        out_shape=[jax.ShapeDtypeStruct((t, PROJ_A_WIDTH), BF16), jax.ShapeDtypeStruct((t, PROJ_H_WIDTH), F32),
                   cache_shape, cache_shape],
        compiler_params=_params(1),
        name="inproj",
    )(*xs, mod, g, w, *prev_caches)


def _channel_dft(z, cc, sc):
    return _dot(z, jnp.concatenate([cc, sc], axis=1)).astype(BF16)


def _fourier_rows(zab, cl, sl, w_f):
    y = _dot(jnp.concatenate([cl, -sl], axis=1),
             jnp.concatenate([zab[:, :FN_WIDTH], zab[:, FN_WIDTH:]], axis=0))
    return _dot(y.astype(BF16), w_f)


def _fourier_body(u_ref, cl_ref, sl_ref, cc_ref, sc_ref, w_ref, o_ref):
    zab = _channel_dft(u_ref[...].astype(BF16), cc_ref[...], sc_ref[...])
    o_ref[...] = _fourier_rows(zab, cl_ref[...], sl_ref[...], w_ref[...]).astype(o_ref.dtype)


def _dft_tables(n, block):
    i = np.arange(n)
    prod = (i[:, None] % block) * (i[None, :] % block) % block
    ang = prod.astype(np.float64) * (2.0 * math.pi / block)
    same = (i[:, None] // block) == (i[None, :] // block)
    scale = 1.0 / math.sqrt(block)
    c = np.where(same, np.cos(ang) * scale, 0.0).astype(np.float32)
    s = np.where(same, np.sin(ang) * scale, 0.0).astype(np.float32)
    return jnp.asarray(c).astype(BF16), jnp.asarray(s).astype(BF16)


def _fourier(proj, row0, n_seq, seq_len, w_f):
    cl, sl = _dft_tables(seq_len, seq_len)
    cc, sc = _dft_tables(FN_WIDTH, FN_GROUP_DIM)
    blk0 = row0 // seq_len
    full = lambda shape: pl.BlockSpec(shape, lambda i: (0, 0))
    return pl.pallas_call(
        _fourier_body,
        grid=(n_seq,),
        in_specs=[
            pl.BlockSpec((seq_len, FN_WIDTH), lambda i: (blk0 + i, COL_FN // FN_WIDTH)),
            full((seq_len, seq_len)), full((seq_len, seq_len)),
            full((FN_WIDTH, FN_WIDTH)), full((FN_WIDTH, FN_WIDTH)), full((FN_WIDTH, FN_WIDTH)),
        ],
        out_specs=pl.BlockSpec((seq_len, FN_WIDTH), lambda i: (i, 0)),
        out_shape=jax.ShapeDtypeStruct((n_seq * seq_len, FN_WIDTH), BF16),
        compiler_params=_params(1),
        name=f"fourier_{seq_len}",
    )(proj, cl, sl, cc, sc, w_f)


def _lambda_full(lmb, lam_init):
    a = jnp.sum(lmb[0:1, :] * lmb[1:2, :], axis=-1, keepdims=True)
    b = jnp.sum(lmb[2:3, :] * lmb[3:4, :], axis=-1, keepdims=True)
    return jnp.exp(a) - jnp.exp(b) + lam_init


def _diff_head(q, ks, vs, lam, g, lam_init):
    vas = [jnp.concatenate([v.astype(BF16), jnp.ones(v.shape, BF16)], axis=1) for v in vs]
    lq = q.shape[0]
    first_map = (lax.broadcasted_iota(jnp.int32, (1, 2 * DA_HALF), 1) < DA_HALF).astype(BF16)
    qb = q.astype(BF16)
    q_stack = jnp.concatenate([qb * first_map, qb * (1 - first_map)], axis=0)
    parts = [_dot_nt(q_stack, k.astype(BF16)) for k in ks]
    mx = parts[0].max(axis=-1, keepdims=True)
    for p in parts[1:]:
        mx = jnp.maximum(mx, p.max(axis=-1, keepdims=True))
    oa = _dot(jnp.exp(parts[0] - mx).astype(BF16), vas[0])
    for p, va in zip(parts[1:], vas[1:]):
        oa = oa + _dot(jnp.exp(p - mx).astype(BF16), va)
    o = oa[:, :DA_VDIM] * (1.0 / oa[:, DA_VDIM:DA_VDIM + 1])
    a = o[:lq] - lam * o[lq:]
    ms = jnp.mean(a * a, axis=-1, keepdims=True)
    return a * lax.rsqrt(ms + EPS) * g * (1.0 - lam_init)


def _rope(x, cos, sin):
    lane = lax.broadcasted_iota(jnp.int32, x.shape, 1)
    first = ((lane >> 4) & 1) == 0
    rot = jnp.where(first, -pltpu.roll(x, 128 - DA_HALF // 4, 1), pltpu.roll(x, DA_HALF // 4, 1))
    return x * cos + rot * sin


def _attn_lat_body(lam_init, q_ref, k_ref, v_ref, kc_ref, vc_ref, cq_ref, sq_ref, ck_ref, sk_ref,
                   lmb_ref, g_ref, o_ref):
    lam = _lambda_full(lmb_ref[...], lam_init)
    scale = DA_HALF ** -0.5
    for h in range(DA_HEADS):
        sl = slice(h * DA_VDIM, (h + 1) * DA_VDIM)
        q = _rope(q_ref[:, sl].astype(F32), cq_ref[...], sq_ref[...]) * scale
        k = _rope(k_ref[:, sl].astype(F32), ck_ref[...], sk_ref[...])
        o = _diff_head(q, [k, kc_ref[:, sl]], [v_ref[:, sl], vc_ref[:, sl]], lam, g_ref[...], lam_init)
        o_ref[:, sl] = o.astype(o_ref.dtype)


def _attn_lat(proj, row0, n_seq, seq_len, cache_k, cache_v, layer, cos, sin, lmb, g, lam_init):
    tq = 512
    nq = seq_len // tq
    past = cache_k.shape[2]
    qb0 = row0 // tq
    kb0 = row0 // seq_len
    cache_spec = pl.BlockSpec((None, None, past, DA_WIDTH), lambda b, j: (b, layer, 0, 0))
    kv_spec = lambda c: pl.BlockSpec((seq_len, DA_WIDTH), lambda b, j: (kb0 + b, c))
    return pl.pallas_call(
        functools.partial(_attn_lat_body, lam_init),
        grid=(n_seq, nq),
        in_specs=[
            pl.BlockSpec((tq, DA_WIDTH), lambda b, j: (qb0 + b * nq + j, COL_Q // DA_WIDTH)),
            kv_spec(COL_K // DA_WIDTH), kv_spec(COL_V // DA_WIDTH),
            cache_spec, cache_spec,
            pl.BlockSpec((tq, DA_VDIM), lambda b, j: (j, 0)),
            pl.BlockSpec((tq, DA_VDIM), lambda b, j: (j, 0)),
            pl.BlockSpec((seq_len, DA_VDIM), lambda b, j: (0, 0)),
            pl.BlockSpec((seq_len, DA_VDIM), lambda b, j: (0, 0)),
            pl.BlockSpec((4, DA_HALF), lambda b, j: (0, 0)),
            pl.BlockSpec((1, DA_VDIM), lambda b, j: (0, 0)),
        ],
        out_specs=pl.BlockSpec((tq, DA_WIDTH), lambda b, j: (b * nq + j, 0)),
        out_shape=jax.ShapeDtypeStruct((n_seq * seq_len, DA_WIDTH), BF16),
        compiler_params=_params(2),
        name="attn_lat",
    )(proj, proj, proj, cache_k, cache_v, cos, sin, cos, sin, lmb, g)


def _rope_tables(n_tokens):
    rows = n_tokens // GRID_W
    row = np.repeat(np.arange(rows, dtype=np.float64), GRID_W)
    col = np.tile(np.arange(GRID_W, dtype=np.float64), rows)
    axis_dim = DA_HALF // 2
    inv_freq = ROPE_THETA ** (-np.arange(0, axis_dim, 2, dtype=np.float64) / axis_dim)
    ang_r = row[:, None] * inv_freq[None, :]
    ang_c = col[:, None] * inv_freq[None, :]
    ang = np.concatenate([ang_r, ang_r, ang_c, ang_c] * 2, axis=-1)
    return jnp.asarray(np.cos(ang).astype(np.float32)), jnp.asarray(np.sin(ang).astype(np.float32))


def _hgrn_tables():
    gl = CHUNKS_PER_GROUP * CHUNK
    t = np.arange(gl)
    same = (t[:, None] // CHUNK) == (t[None, :] // CHUNK)
    fwd = same & (t[None, :] <= t[:, None])
    bwd = same & (t[None, :] >= t[:, None])
    f = np.arange(HG_WIDTH)
    bd = (f[:, None] // HG_DK) == (f[None, :] // HG_DK)
    tri = jnp.asarray(np.stack([fwd, bwd]).astype(np.float32)).astype(BF16)
    causal = jnp.asarray(np.stack([fwd, bwd]).astype(np.float32))
    return tri, causal, jnp.asarray(bd.astype(np.float32))


def _hgrn2_body(n_seq, groups_per_seq, has_s0, *refs):
    (hq_ref, hi_ref, hf_ref, hb_ref, hg_ref, lb_ref, ng_ref, tri_ref, causal_ref, bd_ref) = refs[:10]
    if has_s0:
        s0_ref, o_ref, oi_scr, qe_scr, u_scr, st_scr, dec_scr = refs[10:]
        so_ref = None
    else:
        o_ref, so_ref, oi_scr, qe_scr, u_scr, st_scr, dec_scr = refs[10:]
        s0_ref = None
    c = CHUNK
    w = HG_WIDTH
    gc = CHUNKS_PER_GROUP
    gl = gc * c
    n_groups = n_seq * groups_per_seq
    chunks_per_seq = groups_per_seq * gc
    lane = lax.broadcasted_iota(jnp.int32, (1, w), 1)
    dk_bits = HG_DK.bit_length() - 1
    head_masks_b = [((lane >> dk_bits) == h).astype(BF16) for h in range(HG_HEADS)]
    bd = bd_ref[...] > 0.5
    edge_rows = (c - 1, 0)
    fp_refs = (hf_ref, hb_ref)

    def per_chunk_row(x, row):
        return jnp.concatenate([jnp.broadcast_to(x[k * c + row:k * c + row + 1, :], (c, w)) for k in range(gc)],
                               axis=0)

    def group_terms(gi):
        rows = pl.ds(gi * gl if isinstance(gi, int) else pl.multiple_of(gi * gl, gl), gl)
        hq = hq_ref[rows, :]
        q = _silu_t(hq, HG_DK ** -0.5)
        v = hi_ref[rows, :].astype(BF16)
        v_stack = jnp.concatenate([v * hm for hm in head_masks_b], axis=0)
        for d in range(2):
            lb = lb_ref[d:d + 1, :]
            fp = fp_refs[d][rows, :]
            lsig = jnp.minimum(fp, 0.0) - jnp.log(1.0 + jnp.exp(-jnp.abs(fp)))
            la = jnp.log(lb)
            lbb = jnp.log(1.0 - lb) + lsig
            logf = jnp.maximum(la, lbb) + jnp.log(1.0 + jnp.exp(-jnp.abs(la - lbb)))
            kk = (0.5 * (1.0 - lb)) * (1.0 - jnp.tanh(0.5 * fp))
            g_hi, g_lo = _split(logf, 2)
            cum = _dot(tri_ref[d], g_hi) + _dot(tri_ref[d], g_lo)
            total = per_chunk_row(cum, edge_rows[d])
            ref = per_chunk_row(cum, c // 2)
            qc = (q * jnp.exp(cum - ref)).astype(BF16)
            kc = (kk * jnp.exp(ref - cum)).astype(BF16)
            ke = (kk * jnp.exp(total - cum)).astype(BF16)
            qe_scr[d, rows, :] = (q * jnp.exp(cum)).astype(BF16)
            kc_stack = jnp.concatenate([kc * hm for hm in head_masks_b], axis=0)
            a = _dot_nt(qc, kc_stack)
            keep = causal_ref[d] > 0.5
            a = jnp.concatenate([jnp.where(keep, a[:, h * gl:(h + 1) * gl], 0.0) for h in range(HG_HEADS)],
                                axis=1).astype(BF16)
            oi_scr[d, rows, :] = _dot(a, v_stack)
            for k in range(gc):
                ck = slice(k * c, (k + 1) * c)
                i = gi * gc + k
                dec_scr[d, pl.ds(i, 1), :] = jnp.exp(cum[k * c + edge_rows[d]:k * c + edge_rows[d] + 1, :])
                u_scr[d, i] = jnp.where(bd, _dot_tn(v[ck, :], ke[ck, :]), 0.0)

    if n_groups <= 4:
        for gi in range(n_groups):
            group_terms(gi)
    else:
        def terms_step(gi, carry):
            group_terms(gi)
            return carry

        lax.fori_loop(0, n_groups, terms_step, 0)

    slab = 32
    for s in range(n_seq):
        first = s * chunks_per_seq
        for d in range(2):
            order = range(chunks_per_seq) if d == 0 else range(chunks_per_seq - 1, -1, -1)
            for r in range(w // slab):
                rs = slice(r * slab, (r + 1) * slab)
                st = s0_ref[s, d, rs, :] if has_s0 else jnp.zeros((slab, w), F32)
                for j in order:
                    i = first + j
                    st_scr[d, i, rs, :] = st.astype(BF16)
                    st = st * dec_scr[d, i:i + 1, :] + u_scr[d, i, rs, :]
                if so_ref is not None:
                    h = (r * slab) // HG_DK
                    off = (r * slab) % HG_DK
                    so_ref[s, d, h, off:off + slab, :] = st[:, h * HG_DK:(h + 1) * HG_DK]

    ones_bd = bd.astype(BF16)

    def finish(gi):
        rows = pl.ds(gi * gl if isinstance(gi, int) else pl.multiple_of(gi * gl, gl), gl)
        inter = []
        for k in range(gc):
            i = gi * gc + k
            ck = pl.ds(gi * gl + k * c if isinstance(gi, int) else pl.multiple_of(gi * gl + k * c, c), c)
            inter.append(_dot_nt(qe_scr[0, ck, :], st_scr[0, i]) + _dot_nt(qe_scr[1, ck, :], st_scr[1, i]))
        o = oi_scr[0, rows, :] + oi_scr[1, rows, :] + jnp.concatenate(inter, axis=0)
        sq = _split(o * o, 2)
        ms = (_dot(sq[0], ones_bd) + _dot(sq[1], ones_bd)) * (1.0 / HG_DK)
        y = o * lax.rsqrt(ms + EPS) * ng_ref[...]
        hg = hg_ref[rows, :]
        o_ref[rows, :] = (y * _silu_t(hg)).astype(o_ref.dtype)

    if n_groups <= 4:
        for gi in range(n_groups):
            finish(gi)
    else:
        def finish_step(gi, carry):
            finish(gi)
            return carry

        lax.fori_loop(0, n_groups, finish_step, 0)


def _hgrn_scratch(rows):
    n_chunks = rows // CHUNK
    return [pltpu.VMEM((2, rows, HG_WIDTH), F32),
            pltpu.VMEM((2, rows, HG_WIDTH), BF16),
            pltpu.VMEM((2, n_chunks, HG_WIDTH, HG_WIDTH), F32),
            pltpu.VMEM((2, n_chunks, HG_WIDTH, HG_WIDTH), BF16),
            pltpu.VMEM((2, max(n_chunks, 8), HG_WIDTH), F32)]


def _hgrn2(proj_h, row0, n_seq, seq_len, seqs_per_step, lb, ng, s0t):
    gl = CHUNKS_PER_GROUP * CHUNK
    groups_per_seq = seq_len // gl
    rows = seqs_per_step * seq_len
    blk0 = row0 // rows
    tri, causal, bd = _hgrn_tables()
    col = lambda c0: pl.BlockSpec((rows, HG_WIDTH), lambda i: (blk0 + i, c0 // HG_WIDTH))
    const = lambda shape: pl.BlockSpec(shape, lambda i: (0,) * len(shape))
    in_specs = [col(COL_HQ), col(COL_HI), col(COL_HF), col(COL_HB), col(COL_HG),
                const((2, HG_WIDTH)), const((1, HG_WIDTH)),
                const((2, gl, gl)), const((2, gl, gl)), const((HG_WIDTH, HG_WIDTH))]
    args = [proj_h] * 5 + [lb, ng, tri, causal, bd]
    out_specs = [pl.BlockSpec((rows, HG_WIDTH), lambda i: (i, 0))]
    out_shape = [jax.ShapeDtypeStruct((n_seq * seq_len, HG_WIDTH), BF16)]
    if s0t is not None:
        in_specs.append(pl.BlockSpec((seqs_per_step, 2, HG_WIDTH, HG_WIDTH), lambda i: (i, 0, 0, 0)))
        args.append(s0t)
    else:
        out_specs.append(pl.BlockSpec((seqs_per_step, 2, HG_HEADS, HG_DK, HG_DK), lambda i: (i, 0, 0, 0, 0)))
        out_shape.append(jax.ShapeDtypeStruct((n_seq, 2, HG_HEADS, HG_DK, HG_DK), F32))
    return pl.pallas_call(
        functools.partial(_hgrn2_body, seqs_per_step, groups_per_seq, s0t is not None),
        grid=(n_seq // seqs_per_step,),
        in_specs=in_specs,
        out_specs=out_specs,
        out_shape=out_shape,
        scratch_shapes=_hgrn_scratch(rows),
        compiler_params=_params(1),
        name=f"hgrn_{seq_len}",
    )(*args)


def _ctx_mixers_body(n_seq, seq_len, lam_init, *refs):
    (q_ref, k_ref, v_ref, u_ref, hq_ref, hi_ref, hf_ref, hb_ref, hg_ref,
     cl_ref, sl_ref, cc_ref, sc_ref, wf_ref, lmb_ref, ag_ref,
     lb_ref, ng_ref, tri_ref, causal_ref, bd_ref,
     fn_ref, a_ref, ho_ref, so_ref) = refs[:25]
    scratch = refs[25:]

    zab = _channel_dft(u_ref[...].astype(BF16), cc_ref[...], sc_ref[...])
    for s in range(n_seq):
        rs = slice(s * seq_len, (s + 1) * seq_len)
        fn_ref[rs, :] = _fourier_rows(zab[rs, :], cl_ref[...], sl_ref[...], wf_ref[...]).astype(fn_ref.dtype)

    lam = _lambda_full(lmb_ref[...], lam_init)
    scale = DA_HALF ** -0.5
    for s in range(n_seq):
        rs = slice(s * seq_len, (s + 1) * seq_len)
        for h in range(DA_HEADS):
            sl = slice(h * DA_VDIM, (h + 1) * DA_VDIM)
            o = _diff_head(q_ref[rs, sl] * scale, [k_ref[rs, sl]], [v_ref[rs, sl]], lam, ag_ref[...], lam_init)
            a_ref[rs, sl] = o.astype(a_ref.dtype)

    _hgrn2_body(n_seq, seq_len // (CHUNKS_PER_GROUP * CHUNK), False,
                hq_ref, hi_ref, hf_ref, hb_ref, hg_ref, lb_ref, ng_ref, tri_ref, causal_ref, bd_ref,
                ho_ref, so_ref, *scratch)


def _ctx_mixers(proj_a, proj_h, n_seq, seq_len, seqs_per_step, w_f, lmb, ag, lam_init, lb, ng):
    rows = seqs_per_step * seq_len
    gl = CHUNKS_PER_GROUP * CHUNK
    cl, sl = _dft_tables(seq_len, seq_len)
    cc, sc = _dft_tables(FN_WIDTH, FN_GROUP_DIM)
    tri, causal, bd = _hgrn_tables()
    col = lambda c0, width: pl.BlockSpec((rows, width), lambda i: (i, c0 // width))
    const = lambda shape: pl.BlockSpec(shape, lambda i: (0,) * len(shape))
    out_rows = lambda width: pl.BlockSpec((rows, width), lambda i: (i, 0))
    n_tok = n_seq * seq_len
    return pl.pallas_call(
        functools.partial(_ctx_mixers_body, seqs_per_step, seq_len, lam_init),
        grid=(n_seq // seqs_per_step,),
        in_specs=[col(COL_Q, DA_WIDTH), col(COL_K, DA_WIDTH), col(COL_V, DA_WIDTH), col(COL_FN, FN_WIDTH),
                  col(COL_HQ, HG_WIDTH), col(COL_HI, HG_WIDTH), col(COL_HF, HG_WIDTH), col(COL_HB, HG_WIDTH),
                  col(COL_HG, HG_WIDTH),
                  const((seq_len, seq_len)), const((seq_len, seq_len)),
                  const((FN_WIDTH, FN_WIDTH)), const((FN_WIDTH, FN_WIDTH)), const((FN_WIDTH, FN_WIDTH)),
                  const((4, DA_HALF)), const((1, DA_VDIM)),
                  const((2, HG_WIDTH)), const((1, HG_WIDTH)),
                  const((2, gl, gl)), const((2, gl, gl)), const((HG_WIDTH, HG_WIDTH))],
        out_specs=[out_rows(FN_WIDTH), out_rows(DA_WIDTH), out_rows(HG_WIDTH),
                   pl.BlockSpec((seqs_per_step, 2, HG_HEADS, HG_DK, HG_DK), lambda i: (i, 0, 0, 0, 0))],
        out_shape=[jax.ShapeDtypeStruct((n_tok, FN_WIDTH), BF16),
                   jax.ShapeDtypeStruct((n_tok, DA_WIDTH), BF16),
                   jax.ShapeDtypeStruct((n_tok, HG_WIDTH), BF16),
                   jax.ShapeDtypeStruct((n_seq, 2, HG_HEADS, HG_DK, HG_DK), F32)],
        scratch_shapes=_hgrn_scratch(rows),
        compiler_params=_params(1),
        name="ctx_mixers",
    )(*([proj_a] * 4 + [proj_h] * 5), cl, sl, cc, sc, w_f, lmb, ag, lb, ng, tri, causal, bd)


def _route_gates(lt, bias):
    per = N_EXPERTS // N_GROUPS
    tt = lt.shape[1]
    neg = -jnp.inf
    assert per == N_GROUPS == 8
    gi = lax.broadcasted_iota(jnp.int32, (N_GROUPS, tt), 0).astype(F32)
    s_j, b_j = [], []
    for j in range(per):
        s = _sigmoid(lt[j * N_GROUPS:(j + 1) * N_GROUPS, :])
        s_j.append(s)
        b_j.append(s + bias[j * N_GROUPS:(j + 1) * N_GROUPS, :])
    m1 = functools.reduce(jnp.maximum, b_j)
    i1 = functools.reduce(jnp.minimum, [jnp.where(b_j[j] == m1, float(j), float(per)) for j in range(per)])
    m2 = functools.reduce(jnp.maximum, [jnp.where(i1 == float(j), neg, b_j[j]) for j in range(per)])
    gs = m1 + m2
    gsel = jnp.zeros((N_GROUPS, tt), jnp.bool_)
    for _ in range(TOPK_GROUPS):
        m = gs.max(axis=0, keepdims=True)
        idx = jnp.where(gs == m, gi, float(N_GROUPS)).min(axis=0, keepdims=True)
        hit = gi == idx
        gsel = gsel | hit
        gs = jnp.where(hit, neg, gs)
    x_j = [jnp.where(gsel, b_j[j], neg) for j in range(per)]
    e_j = [gi * per + j for j in range(per)]
    sel_j = [jnp.zeros((N_GROUPS, tt), jnp.bool_) for _ in range(per)]
    for _ in range(TOP_K):
        m = functools.reduce(jnp.maximum, x_j).max(axis=0, keepdims=True)
        idx = functools.reduce(jnp.minimum, [jnp.where(x_j[j] == m, e_j[j], float(N_EXPERTS))
                                             for j in range(per)]).min(axis=0, keepdims=True)
        for j in range(per):
            hit = e_j[j] == idx
            sel_j[j] = sel_j[j] | hit
            x_j[j] = jnp.where(hit, neg, x_j[j])
    w_j = [jnp.where(sel_j[j], s_j[j], 0.0) for j in range(per)]
    denom = functools.reduce(lambda a, b: a + b, w_j).sum(axis=0, keepdims=True)
    gates_t = jnp.concatenate([w / denom * ROUTED_SCALE for w in w_j], axis=0)
    r_io = lax.broadcasted_iota(jnp.int32, (N_EXPERTS, N_EXPERTS), 0)
    e_io = lax.broadcasted_iota(jnp.int32, (N_EXPERTS, N_EXPERTS), 1)
    eye = (e_io == (r_io & (N_GROUPS - 1)) * per + (r_io >> 3)).astype(BF16)
    p = _split(gates_t, 3)
    return _dot_tn(p[0], eye) + _dot_tn(p[1], eye) + _dot_tn(p[2], eye)


def _outproj_body(n_x, n_ctx_tiles, *refs):
    fn_refs, a_refs, hg_refs = refs[0:2], refs[2:4], refs[4:6]
    x_refs = refs[6:6 + n_x]
    mod_ref, g_ref, w_ref, wr_ref, rb_ref, xo_ref, h_ref, gate_ref = refs[6 + n_x:]
    is_ctx = pl.program_id(0) < n_ctx_tiles
    mix = _dot(jnp.concatenate([_pick(is_ctx, fn_refs), _pick(is_ctx, a_refs), _pick(is_ctx, hg_refs)], axis=1),
               w_ref[...])
    x = _pick(is_ctx, x_refs) + mod_ref[2:3, :] * mix
    xo_ref[...] = x
    h = _modnorm(x, g_ref[...], mod_ref[3:4, :], mod_ref[4:5, :])
    h_ref[...] = h.astype(BF16)
    h_hi, h_lo = _split(h, 2)
    w_hi, w_lo = _split(wr_ref[...], 2)
    logits_t = _dot_nt(w_hi, h_hi) + _dot_nt(w_lo, h_hi) + _dot_nt(w_hi, h_lo)
    gate_ref[...] = _route_gates(logits_t, rb_ref[...])


def _outproj(fn, a, hg, xs, mod, g, w_out, w_router_t, router_bias, n_ctx_tok, lat_len):
    t = sum(x.shape[0] for x in xs)
    tm = TOKEN_TILE
    n_ctx_tiles = n_ctx_tok // tm
    row = _mod_row(n_ctx_tiles, lat_len // tm)
    rows = lambda width: pl.BlockSpec((tm, width), lambda i: (i, 0))
    parts = lambda n, width: _row_specs(n, tm, width, n_ctx_tiles)
    return pl.pallas_call(
        functools.partial(_outproj_body, len(xs), n_ctx_tiles),
        grid=(t // tm,),
        in_specs=parts(2, FN_WIDTH) + parts(2, DA_WIDTH) + parts(2, HG_WIDTH) + parts(len(xs), D_MODEL) + [
            pl.BlockSpec((None, 6, D_MODEL), lambda i: (row(i), 0, 0)),
            pl.BlockSpec((1, D_MODEL), lambda i: (0, 0)),
            pl.BlockSpec((D_MODEL, D_MODEL), lambda i: (0, 0)),
            pl.BlockSpec((N_EXPERTS, D_MODEL), lambda i: (0, 0)),
            pl.BlockSpec((N_EXPERTS, 1), lambda i: (0, 0)),
        ],
        out_specs=[rows(D_MODEL), rows(D_MODEL), rows(N_EXPERTS)],
        out_shape=[jax.ShapeDtypeStruct((t, D_MODEL), F32),
                   jax.ShapeDtypeStruct((t, D_MODEL), BF16),
                   jax.ShapeDtypeStruct((t, N_EXPERTS), F32)],
        compiler_params=_params(1),
        name="outproj",
    )(*fn, *a, *hg, *xs, mod, g, w_out, w_router_t, router_bias)


def _moe_body(final_norm, n_ctx_tiles, h_ref, gate_ref, wg_ref, wu_ref, wd_ref, sg_ref, su_ref, sd_ref,
              x_ref, mod_ref, fg_ref, *out_and_scratch):
    acc_ref = out_and_scratch[-1]
    o_refs = out_and_scratch[:-1]
    is_ctx = pl.program_id(0) < n_ctx_tiles
    j = pl.program_id(1)
    h = h_ref[...]

    def act_of(wg, wu, gate):
        gu = _dot(h, jnp.concatenate([wg.astype(BF16), wu.astype(BF16)], axis=1))
        a = _silu_t(gu[:, :D_EXPERT]) * gu[:, D_EXPERT:]
        return a if gate is None else a * gate

    @pl.when(j == 0)
    def _():
        acc_ref[...] = _dot(act_of(sg_ref[...], su_ref[...], None).astype(BF16), sd_ref[...].astype(BF16))

    gates = gate_ref[...]
    expert_of_lane = lax.broadcasted_iota(jnp.int32, gates.shape, 1)

    def gate_col(p):
        e = j * EXPERTS_PER_STEP + p
        return jnp.sum(jnp.where(expert_of_lane == e, gates, 0.0), axis=1, keepdims=True)

    acts = [act_of(wg_ref[p], wu_ref[p], gate_col(p)).astype(BF16) for p in range(EXPERTS_PER_STEP)]
    wd = wd_ref[...].astype(BF16).reshape(EXPERTS_PER_STEP * D_EXPERT, D_MODEL)
    acc_ref[...] += _dot(jnp.concatenate(acts, axis=1), wd)

    @pl.when(j == pl.num_programs(1) - 1)
    def _():
        x = x_ref[...] + mod_ref[5:6, :] * acc_ref[...]
        if not final_norm:
            o_refs[0][...] = x
        else:
            ms = jnp.mean(x * x, axis=-1, keepdims=True)
            y = x * lax.rsqrt(ms + EPS) * fg_ref[...]

            @pl.when(is_ctx)
            def _():
                o_refs[0][...] = y

            @pl.when(jnp.logical_not(is_ctx))
            def _():
                o_refs[1][...] = y


def _moe(h, gates, w_gate, w_up, w_down2, ws_gate, ws_up, ws_down, layer, x, mod, final_g, final_norm,
         n_ctx_tok, lat_len):
    t = x.shape[0]
    tm = MOE_TILE
    eps_ = EXPERTS_PER_STEP
    n_ctx_tiles = n_ctx_tok // tm
    row = _mod_row(n_ctx_tiles, lat_len // tm)
    once = pl.Buffered(1)
    if final_norm:
        last_ctx = n_ctx_tiles - 1
        out_specs = [pl.BlockSpec((tm, D_MODEL), lambda i, j: (jnp.minimum(i, last_ctx), 0), pipeline_mode=once),
                     pl.BlockSpec((tm, D_MODEL), lambda i, j: (jnp.maximum(i - n_ctx_tiles, 0), 0),
                                  pipeline_mode=once)]
        out_shape = [jax.ShapeDtypeStruct((n_ctx_tok, D_MODEL), F32),
                     jax.ShapeDtypeStruct((t - n_ctx_tok, D_MODEL), F32)]
    else:
        out_specs = [pl.BlockSpec((tm, D_MODEL), lambda i, j: (i, 0), pipeline_mode=once)]
        out_shape = [jax.ShapeDtypeStruct((t, D_MODEL), F32)]
    return pl.pallas_call(
        functools.partial(_moe_body, final_norm, n_ctx_tiles),
        grid=(t // tm, N_EXPERTS // eps_),
        in_specs=[
            pl.BlockSpec((tm, D_MODEL), lambda i, j: (i, 0)),
            pl.BlockSpec((tm, N_EXPERTS), lambda i, j: (i, 0)),
            pl.BlockSpec((None, eps_, D_MODEL, D_EXPERT), lambda i, j: (layer, j, 0, 0)),
            pl.BlockSpec((None, eps_, D_MODEL, D_EXPERT), lambda i, j: (layer, j, 0, 0)),
            pl.BlockSpec((None, eps_ // 2, 2 * D_EXPERT, D_MODEL), lambda i, j: (layer, j, 0, 0)),
            pl.BlockSpec((None, D_MODEL, D_EXPERT), lambda i, j: (layer, 0, 0), pipeline_mode=once),
            pl.BlockSpec((None, D_MODEL, D_EXPERT), lambda i, j: (layer, 0, 0), pipeline_mode=once),
            pl.BlockSpec((None, D_EXPERT, D_MODEL), lambda i, j: (layer, 0, 0), pipeline_mode=once),
            pl.BlockSpec((tm, D_MODEL), lambda i, j: (i, 0), pipeline_mode=once),
            pl.BlockSpec((None, 6, D_MODEL), lambda i, j: (row(i), 0, 0)),
            pl.BlockSpec((1, D_MODEL), lambda i, j: (0, 0)),
        ],
        out_specs=out_specs,
        out_shape=out_shape,
        scratch_shapes=[pltpu.VMEM((tm, D_MODEL), F32)],
        compiler_params=_params(2),
        name="moe",
    )(h, gates, w_gate, w_up, w_down2, ws_gate, ws_up, ws_down, x, mod, final_g)


def _block_diag_t(s):
    eye = jnp.eye(HG_HEADS, dtype=s.dtype)
    out = jnp.einsum('...hkv,hg->...hvgk', s, eye)
    return out.reshape(s.shape[:-3] + (HG_WIDTH, HG_WIDTH))


def kernel(x_prompt, x_sample, cache_k, cache_v, state_hgrn, c, c_ctx, w_ada, b_ada, norm_g, w_in,
           w_fourier, lambdas, attn_norm_g, lower_bounds, hg_norm_g, w_out, w_router, router_bias,
           w_gate, w_up, w_down, ws_gate, ws_up, ws_down, final_g):
    n_ctx, ctx_len, _ = x_prompt.shape
    n_lat, lat_len, _ = x_sample.shape
    n_ctx_tok = n_ctx * ctx_len
    past = cache_k.shape[2]

    xs = (x_prompt.reshape(n_ctx_tok, D_MODEL), x_sample.reshape(n_lat * lat_len, D_MODEL))

    c8 = jnp.zeros((8, D_MODEL), F32).at[0].set(c_ctx).at[1:1 + n_lat].set(c)
    mods = _ada_mods(c8, w_ada, b_ada).reshape(DEPTH, 8, 6, D_MODEL)

    cs = jnp.cumsum(jax.nn.softmax(lower_bounds.astype(F32), axis=0), axis=0)
    lbs = cs - cs[0:1]

    cos, sin = _rope_tables(lat_len)
    cache_k4 = cache_k.reshape(n_lat, DEPTH, past, DA_WIDTH)
    cache_v4 = cache_v.reshape(n_lat, DEPTH, past, DA_WIDTH)
    s0t = _block_diag_t(state_hgrn.astype(F32))

    qkv_end = FN_WIDTH + 3 * DA_WIDTH
    w_in_b = jnp.concatenate([w_in[:, :, FN_WIDTH:qkv_end], w_in[:, :, :FN_WIDTH], w_in[:, :, qkv_end:]],
                             axis=-1).astype(BF16)
    w_f_b = w_fourier.astype(BF16)
    w_out_b = w_out.astype(BF16)
    per = N_EXPERTS // N_GROUPS
    w_router_t = (jnp.swapaxes(w_router, 1, 2).reshape(DEPTH, N_GROUPS, per, D_MODEL)
                  .swapaxes(1, 2).reshape(DEPTH, N_EXPERTS, D_MODEL))
    bias_mm = router_bias.reshape(DEPTH, N_GROUPS, per).swapaxes(1, 2).reshape(DEPTH, N_EXPERTS, 1)
    w_down2 = w_down.reshape(DEPTH, N_EXPERTS // 2, 2 * D_EXPERT, D_MODEL)
    ng = jnp.tile(hg_norm_g, (1, HG_HEADS))

    caches = []
    new_s = []
    for l in range(DEPTH):
        lam_init = 0.8 - 0.6 * math.exp(-0.3 * l)
        mod = mods[l]
        last = l == DEPTH - 1
        proj_a, proj_h, k_l, v_l = _inproj(xs, mod, norm_g[l, 0:1], w_in_b[l], tuple(caches) if last else (),
                                           n_ctx, ctx_len, lat_len)
        caches = [k_l, v_l] if last else caches + [k_l, v_l]

        ag = attn_norm_g[l].reshape(1, DA_VDIM)
        fn_ctx, a_ctx, hg_ctx, st_ctx = _ctx_mixers(proj_a, proj_h, n_ctx, ctx_len, 2, w_f_b[l], lambdas[l], ag,
                                                    lam_init, lbs[l], ng[l:l + 1])
        fn_lat = _fourier(proj_a, n_ctx_tok, n_lat, lat_len, w_f_b[l])
        a_lat = _attn_lat(proj_a, n_ctx_tok, n_lat, lat_len, cache_k4, cache_v4, l, cos, sin, lambdas[l], ag,
                          lam_init)
        (hg_lat,) = _hgrn2(proj_h, n_ctx_tok, n_lat, lat_len, 1, lbs[l], ng[l:l + 1], s0t[:, l])
        new_s.append(jnp.swapaxes(st_ctx, -1, -2))

        x, h2, gates = _outproj((fn_ctx, fn_lat), (a_ctx, a_lat), (hg_ctx, hg_lat), xs, mod,
                                norm_g[l, 1:2], w_out_b[l], w_router_t[l], bias_mm[l], n_ctx_tok, lat_len)
        xs = _moe(h2, gates, w_gate, w_up, w_down2, ws_gate, ws_up, ws_down, l, x, mod,
                  final_g.reshape(1, D_MODEL), l == DEPTH - 1, n_ctx_tok, lat_len)

    y_prompt = xs[0].reshape(n_ctx, ctx_len, D_MODEL)
    y_sample = xs[1].reshape(n_lat, lat_len, D_MODEL)
    new_k = caches[0].reshape(n_ctx, DEPTH, ctx_len, DA_HEADS, DA_VDIM)
    new_v = caches[1].reshape(n_ctx, DEPTH, ctx_len, DA_HEADS, DA_VDIM)
    return (y_prompt, y_sample, new_k, new_v, jnp.stack(new_s, axis=1))
```

```python
import functools
import math

import numpy as np
import jax
import jax.numpy as jnp
from jax import lax
from jax.experimental import pallas as pl
from jax.experimental.pallas import tpu as pltpu

F32 = jnp.float32
BF16 = jnp.bfloat16

D_MODEL = 1024
DEPTH = 2
GRID_W = 64
FN_WIDTH = 256
FN_GROUP_DIM = 64
DA_WIDTH = 512
DA_HEADS = 4
DA_VDIM = 128
DA_HALF = 64
HG_WIDTH = 256
HG_HEADS = 4
HG_DK = 64
CHUNK = 64
ROPE_THETA = 10000.0
N_EXPERTS = 64
TOP_K = 8
N_GROUPS = 8
TOPK_GROUPS = 4
D_EXPERT = 128
ROUTED_SCALE = 2.5
EPS = 1e-6

COL_Q, COL_K, COL_V, COL_FN = 0, 512, 1024, 1536
PROJ_A_WIDTH = 1792
COL_HQ, COL_HI, COL_HF, COL_HB, COL_HG = 0, 256, 512, 768, 1024
PROJ_H_WIDTH = 1280

TOKEN_TILE = 512
MOE_TILE = 1024
EXPERTS_PER_STEP = 8
CHUNKS_PER_GROUP = 4
VMEM_LIMIT = 56 * 1024 * 1024


def _dot(a, b):
    return jnp.dot(a, b, preferred_element_type=F32)


def _dot_nt(a, b):
    return lax.dot_general(a, b, (((1,), (1,)), ((), ())), preferred_element_type=F32)


def _dot_tn(a, b):
    return lax.dot_general(a, b, (((0,), (0,)), ((), ())), preferred_element_type=F32)


def _split(x, n):
    parts = []
    r = x
    for i in range(n):
        p = r.astype(BF16)
        parts.append(p)
        if i + 1 < n:
            r = r - p.astype(F32)
    return parts


def _sigmoid(x):
    return 1.0 / (1.0 + jnp.exp(-x))


def _silu_t(x, scale=1.0):
    return (0.5 * scale) * x * (jnp.tanh(0.5 * x) + 1.0)


def _silu(x):
    return x * _sigmoid(x)


def _params(n_axes):
    return pltpu.CompilerParams(dimension_semantics=("arbitrary",) * n_axes,
                                vmem_limit_bytes=VMEM_LIMIT)


def _ada_body(c_ref, w_ref, b_ref, o_ref):
    a = _silu(c_ref[...])
    a_hi, a_lo = _split(a, 2)
    w_hi, w_lo = _split(w_ref[...], 2)
    o_ref[...] = _dot(a_hi, w_hi) + _dot(a_lo, w_hi) + _dot(a_hi, w_lo) + b_ref[...]


def _ada_mods(c8, w_ada, b_ada):
    tn = 1536
    return pl.pallas_call(
        _ada_body,
        grid=(DEPTH, 6 * D_MODEL // tn),
        in_specs=[
            pl.BlockSpec((8, D_MODEL), lambda l, j: (0, 0)),
            pl.BlockSpec((None, D_MODEL, tn), lambda l, j: (l, 0, j)),
            pl.BlockSpec((None, 1, tn), lambda l, j: (l, 0, j)),
        ],
        out_specs=pl.BlockSpec((None, 8, tn), lambda l, j: (l, 0, j)),
        out_shape=jax.ShapeDtypeStruct((DEPTH, 8, 6 * D_MODEL), F32),
        compiler_params=_params(2),
        name="ada_mods",
    )(c8, w_ada, b_ada.reshape(DEPTH, 1, 6 * D_MODEL))


def _modnorm(x, g, shift, scale):
    ms = jnp.mean(x * x, axis=-1, keepdims=True)
    return (x * lax.rsqrt(ms + EPS) * g) * (1.0 + scale) + shift


def _mod_row(n_ctx_tiles, tiles_per_latent):
    def f(i):
        return jnp.where(i < n_ctx_tiles, 0, 1 + (i - n_ctx_tiles) // tiles_per_latent)
    return f


def _row_specs(n_parts, tm, width, n_ctx_tiles):
    if n_parts == 1:
        return [pl.BlockSpec((tm, width), lambda i, *_: (i, 0))]
    return [pl.BlockSpec((tm, width), lambda i, *_: (jnp.minimum(i, n_ctx_tiles - 1), 0)),
            pl.BlockSpec((tm, width), lambda i, *_: (jnp.maximum(i - n_ctx_tiles, 0), 0))]


def _pick(is_ctx, refs):
    if len(refs) == 1:
        return refs[0][...]
    return jnp.where(is_ctx, refs[0][...], refs[1][...])


def _inproj_body(n_x, n_prev, n_ctx_tiles, ctx_len, *refs):
    x_refs = refs[:n_x]
    mod_ref, g_ref, w_ref = refs[n_x:n_x + 3]
    prev_refs = refs[n_x + 3:n_x + 3 + 2 * n_prev]
    oa_ref, oh_ref, k_ref, v_ref = refs[-4:]
    is_ctx = pl.program_id(0) < n_ctx_tiles
    x = _pick(is_ctx, x_refs)
    h = _modnorm(x, g_ref[...], mod_ref[0:1, :], mod_ref[1:2, :])
    proj = _dot(h.astype(BF16), w_ref[...])
    oa_ref[...] = proj[:, :PROJ_A_WIDTH].astype(BF16)
    oh_ref[...] = proj[:, PROJ_A_WIDTH:]

    @pl.when(is_ctx)
    def _():
        per_layer = ctx_len * DA_HEADS
        for s in range(proj.shape[0] // ctx_len):
            base = s * (n_prev + 1) * per_layer
            for l in range(n_prev):
                dst = slice(base + l * per_layer, base + (l + 1) * per_layer)
                src = slice(s * per_layer, (s + 1) * per_layer)
                k_ref[dst, :] = prev_refs[2 * l][src, :]
                v_ref[dst, :] = prev_refs[2 * l + 1][src, :]
            tok = slice(s * ctx_len, (s + 1) * ctx_len)
            for h in range(DA_HEADS):
                rows = pl.ds(base + n_prev * per_layer + h, ctx_len, stride=DA_HEADS)
                k_ref[rows, :] = proj[tok, COL_K + h * DA_VDIM:COL_K + (h + 1) * DA_VDIM]
                v_ref[rows, :] = proj[tok, COL_V + h * DA_VDIM:COL_V + (h + 1) * DA_VDIM]


def _inproj(xs, mod, g, w, prev_caches, n_ctx, ctx_len, lat_len):
    t = sum(x.shape[0] for x in xs)
    tm = TOKEN_TILE
    seqs = tm // ctx_len
    n_ctx_tiles = n_ctx // seqs
    n_prev = len(prev_caches) // 2
    row = _mod_row(n_ctx_tiles, lat_len // tm)
    tile_rows = seqs * ctx_len * DA_HEADS
    tile_block = lambda i: (jnp.minimum(i, n_ctx_tiles - 1), 0)
    cache_spec = pl.BlockSpec(((n_prev + 1) * tile_rows, DA_VDIM), tile_block)
    cache_shape = jax.ShapeDtypeStruct((n_ctx_tiles * (n_prev + 1) * tile_rows, DA_VDIM), F32)
    in_specs = _row_specs(len(xs), tm, D_MODEL, n_ctx_tiles) + [
        pl.BlockSpec((None, 6, D_MODEL), lambda i: (row(i), 0, 0)),
        pl.BlockSpec((1, D_MODEL), lambda i: (0, 0)),
        pl.BlockSpec((D_MODEL, PROJ_A_WIDTH + PROJ_H_WIDTH), lambda i: (0, 0)),
    ] + [pl.BlockSpec((tile_rows, DA_VDIM), tile_block)] * (2 * n_prev)
    return pl.pallas_call(
        functools.partial(_inproj_body, len(xs), n_prev, n_ctx_tiles, ctx_len),
        grid=(t // tm,),
        in_specs=in_specs,
        out_specs=[pl.BlockSpec((tm, PROJ_A_WIDTH), lambda i: (i, 0)),
                   pl.BlockSpec((tm, PROJ_H_WIDTH), lambda i: (i, 0)), cache_spec, cache_spec],
        out_shape=[jax.ShapeDtypeStruct((t, PROJ_A_WIDTH), BF16), jax.ShapeDtypeStruct((t, PROJ_H_WIDTH), F32),
                   cache_shape, cache_shape],
        compiler_params=_params(1),
        name="inproj",
    )(*xs, mod, g, w, *prev_caches)


def _channel_dft(z, cc, sc):
    return _dot(z, jnp.concatenate([cc, sc], axis=1)).astype(BF16)


def _fourier_rows(zab, cl, sl, w_f):
    y = _dot(jnp.concatenate([cl, -sl], axis=1),
             jnp.concatenate([zab[:, :FN_WIDTH], zab[:, FN_WIDTH:]], axis=0))
    return _dot(y.astype(BF16), w_f)


def _fourier_body(u_ref, cl_ref, sl_ref, cc_ref, sc_ref, w_ref, o_ref):
    zab = _channel_dft(u_ref[...].astype(BF16), cc_ref[...], sc_ref[...])
    o_ref[...] = _fourier_rows(zab, cl_ref[...], sl_ref[...], w_ref[...]).astype(o_ref.dtype)


def _dft_tables(n, block):
    i = np.arange(n)
    prod = (i[:, None] % block) * (i[None, :] % block) % block
    ang = prod.astype(np.float64) * (2.0 * math.pi / block)
    same = (i[:, None] // block) == (i[None, :] // block)
    scale = 1.0 / math.sqrt(block)
    c = np.where(same, np.cos(ang) * scale, 0.0).astype(np.float32)
    s = np.where(same, np.sin(ang) * scale, 0.0).astype(np.float32)
    return jnp.asarray(c).astype(BF16), jnp.asarray(s).astype(BF16)


def _fourier(proj, row0, n_seq, seq_len, w_f):
    cl, sl = _dft_tables(seq_len, seq_len)
    cc, sc = _dft_tables(FN_WIDTH, FN_GROUP_DIM)
    blk0 = row0 // seq_len
    full = lambda shape: pl.BlockSpec(shape, lambda i: (0, 0))
    return pl.pallas_call(
        _fourier_body,
        grid=(n_seq,),
        in_specs=[
            pl.BlockSpec((seq_len, FN_WIDTH), lambda i: (blk0 + i, COL_FN // FN_WIDTH)),
            full((seq_len, seq_len)), full((seq_len, seq_len)),
            full((FN_WIDTH, FN_WIDTH)), full((FN_WIDTH, FN_WIDTH)), full((FN_WIDTH, FN_WIDTH)),
        ],
        out_specs=pl.BlockSpec((seq_len, FN_WIDTH), lambda i: (i, 0)),
        out_shape=jax.ShapeDtypeStruct((n_seq * seq_len, FN_WIDTH), BF16),
        compiler_params=_params(1),
        name=f"fourier_{seq_len}",
    )(proj, cl, sl, cc, sc, w_f)


def _lambda_full(lmb, lam_init):
    a = jnp.sum(lmb[0:1, :] * lmb[1:2, :], axis=-1, keepdims=True)
    b = jnp.sum(lmb[2:3, :] * lmb[3:4, :], axis=-1, keepdims=True)
    return jnp.exp(a) - jnp.exp(b) + lam_init


def _diff_head(q, ks, vs, lam, g, lam_init):
    vas = [jnp.concatenate([v.astype(BF16), jnp.ones(v.shape, BF16)], axis=1) for v in vs]
    lq = q.shape[0]
    first_map = (lax.broadcasted_iota(jnp.int32, (1, 2 * DA_HALF), 1) < DA_HALF).astype(BF16)
    qb = q.astype(BF16)
    q_stack = jnp.concatenate([qb * first_map, qb * (1 - first_map)], axis=0)
    parts = [_dot_nt(q_stack, k.astype(BF16)) for k in ks]
    mx = parts[0].max(axis=-1, keepdims=True)
    for p in parts[1:]:
        mx = jnp.maximum(mx, p.max(axis=-1, keepdims=True))
    oa = _dot(jnp.exp(parts[0] - mx).astype(BF16), vas[0])
    for p, va in zip(parts[1:], vas[1:]):
        oa = oa + _dot(jnp.exp(p - mx).astype(BF16), va)
    o = oa[:, :DA_VDIM] * (1.0 / oa[:, DA_VDIM:DA_VDIM + 1])
    a = o[:lq] - lam * o[lq:]
    ms = jnp.mean(a * a, axis=-1, keepdims=True)
    return a * lax.rsqrt(ms + EPS) * g * (1.0 - lam_init)


def _rope(x, cos, sin):
    lane = lax.broadcasted_iota(jnp.int32, x.shape, 1)
    first = ((lane >> 4) & 1) == 0
    rot = jnp.where(first, -pltpu.roll(x, 128 - DA_HALF // 4, 1), pltpu.roll(x, DA_HALF // 4, 1))
    return x * cos + rot * sin


def _attn_lat_body(lam_init, q_ref, k_ref, v_ref, kc_ref, vc_ref, cq_ref, sq_ref, ck_ref, sk_ref,
                   lmb_ref, g_ref, o_ref):
    lam = _lambda_full(lmb_ref[...], lam_init)
    scale = DA_HALF ** -0.5
    for h in range(DA_HEADS):
        sl = slice(h * DA_VDIM, (h + 1) * DA_VDIM)
        q = _rope(q_ref[:, sl].astype(F32), cq_ref[...], sq_ref[...]) * scale
        k = _rope(k_ref[:, sl].astype(F32), ck_ref[...], sk_ref[...])
        o = _diff_head(q, [k, kc_ref[:, sl]], [v_ref[:, sl], vc_ref[:, sl]], lam, g_ref[...], lam_init)
        o_ref[:, sl] = o.astype(o_ref.dtype)


def _attn_lat(proj, row0, n_seq, seq_len, cache_k, cache_v, layer, cos, sin, lmb, g, lam_init):
    tq = 512
    nq = seq_len // tq
    past = cache_k.shape[2]
    qb0 = row0 // tq
    kb0 = row0 // seq_len
    cache_spec = pl.BlockSpec((None, None, past, DA_WIDTH), lambda b, j: (b, layer, 0, 0))
    kv_spec = lambda c: pl.BlockSpec((seq_len, DA_WIDTH), lambda b, j: (kb0 + b, c))
    return pl.pallas_call(
        functools.partial(_attn_lat_body, lam_init),
        grid=(n_seq, nq),
        in_specs=[
            pl.BlockSpec((tq, DA_WIDTH), lambda b, j: (qb0 + b * nq + j, COL_Q // DA_WIDTH)),
            kv_spec(COL_K // DA_WIDTH), kv_spec(COL_V // DA_WIDTH),
            cache_spec, cache_spec,
            pl.BlockSpec((tq, DA_VDIM), lambda b, j: (j, 0)),
            pl.BlockSpec((tq, DA_VDIM), lambda b, j: (j, 0)),
            pl.BlockSpec((seq_len, DA_VDIM), lambda b, j: (0, 0)),
            pl.BlockSpec((seq_len, DA_VDIM), lambda b, j: (0, 0)),
            pl.BlockSpec((4, DA_HALF), lambda b, j: (0, 0)),
            pl.BlockSpec((1, DA_VDIM), lambda b, j: (0, 0)),
        ],
        out_specs=pl.BlockSpec((tq, DA_WIDTH), lambda b, j: (b * nq + j, 0)),
        out_shape=jax.ShapeDtypeStruct((n_seq * seq_len, DA_WIDTH), BF16),
        compiler_params=_params(2),
        name="attn_lat",
    )(proj, proj, proj, cache_k, cache_v, cos, sin, cos, sin, lmb, g)


def _rope_tables(n_tokens):
    rows = n_tokens // GRID_W
    row = np.repeat(np.arange(rows, dtype=np.float64), GRID_W)
    col = np.tile(np.arange(GRID_W, dtype=np.float64), rows)
    axis_dim = DA_HALF // 2
    inv_freq = ROPE_THETA ** (-np.arange(0, axis_dim, 2, dtype=np.float64) / axis_dim)
    ang_r = row[:, None] * inv_freq[None, :]
    ang_c = col[:, None] * inv_freq[None, :]
    ang = np.concatenate([ang_r, ang_r, ang_c, ang_c] * 2, axis=-1)
    return jnp.asarray(np.cos(ang).astype(np.float32)), jnp.asarray(np.sin(ang).astype(np.float32))


def _hgrn_tables():
    gl = CHUNKS_PER_GROUP * CHUNK
    t = np.arange(gl)
    same = (t[:, None] // CHUNK) == (t[None, :] // CHUNK)
    fwd = same & (t[None, :] <= t[:, None])
    bwd = same & (t[None, :] >= t[:, None])
    f = np.arange(HG_WIDTH)
    bd = (f[:, None] // HG_DK) == (f[None, :] // HG_DK)
    tri = jnp.asarray(np.stack([fwd, bwd]).astype(np.float32)).astype(BF16)
    causal = jnp.asarray(np.stack([fwd, bwd]).astype(np.float32))
    return tri, causal, jnp.asarray(bd.astype(np.float32))


def _hgrn2_body(n_seq, groups_per_seq, has_s0, *refs):
    (hq_ref, hi_ref, hf_ref, hb_ref, hg_ref, lb_ref, ng_ref, tri_ref, causal_ref, bd_ref) = refs[:10]
    if has_s0:
        s0_ref, o_ref, oi_scr, qe_scr, u_scr, st_scr, dec_scr = refs[10:]
        so_ref = None
    else:
        o_ref, so_ref, oi_scr, qe_scr, u_scr, st_scr, dec_scr = refs[10:]
        s0_ref = None
    c = CHUNK
    w = HG_WIDTH
    gc = CHUNKS_PER_GROUP
    gl = gc * c
    n_groups = n_seq * groups_per_seq
    chunks_per_seq = groups_per_seq * gc
    lane = lax.broadcasted_iota(jnp.int32, (1, w), 1)
    dk_bits = HG_DK.bit_length() - 1
    head_masks_b = [((lane >> dk_bits) == h).astype(BF16) for h in range(HG_HEADS)]
    bd = bd_ref[...] > 0.5
    edge_rows = (c - 1, 0)
    fp_refs = (hf_ref, hb_ref)

    def per_chunk_row(x, row):
        return jnp.concatenate([jnp.broadcast_to(x[k * c + row:k * c + row + 1, :], (c, w)) for k in range(gc)],
                               axis=0)

    def group_terms(gi):
        rows = pl.ds(gi * gl if isinstance(gi, int) else pl.multiple_of(gi * gl, gl), gl)
        hq = hq_ref[rows, :]
        q = _silu_t(hq, HG_DK ** -0.5)
        v = hi_ref[rows, :].astype(BF16)
        v_stack = jnp.concatenate([v * hm for hm in head_masks_b], axis=0)
        for d in range(2):
            lb = lb_ref[d:d + 1, :]
            fp = fp_refs[d][rows, :]
            lsig = jnp.minimum(fp, 0.0) - jnp.log(1.0 + jnp.exp(-jnp.abs(fp)))
            la = jnp.log(lb)
            lbb = jnp.log(1.0 - lb) + lsig
            logf = jnp.maximum(la, lbb) + jnp.log(1.0 + jnp.exp(-jnp.abs(la - lbb)))
            kk = (0.5 * (1.0 - lb)) * (1.0 - jnp.tanh(0.5 * fp))
            g_hi, g_lo = _split(logf, 2)
            cum = _dot(tri_ref[d], g_hi) + _dot(tri_ref[d], g_lo)
            total = per_chunk_row(cum, edge_rows[d])
            ref = per_chunk_row(cum, c // 2)
            qc = (q * jnp.exp(cum - ref)).astype(BF16)
            kc = (kk * jnp.exp(ref - cum)).astype(BF16)
            ke = (kk * jnp.exp(total - cum)).astype(BF16)
            qe_scr[d, rows, :] = (q * jnp.exp(cum)).astype(BF16)
            kc_stack = jnp.concatenate([kc * hm for hm in head_masks_b], axis=0)
            a = _dot_nt(qc, kc_stack)
            keep = causal_ref[d] > 0.5
            a = jnp.concatenate([jnp.where(keep, a[:, h * gl:(h + 1) * gl], 0.0) for h in range(HG_HEADS)],
                                axis=1).astype(BF16)
            oi_scr[d, rows, :] = _dot(a, v_stack)
            for k in range(gc):
                ck = slice(k * c, (k + 1) * c)
                i = gi * gc + k
                dec_scr[d, pl.ds(i, 1), :] = jnp.exp(cum[k * c + edge_rows[d]:k * c + edge_rows[d] + 1, :])
                u_scr[d, i] = jnp.where(bd, _dot_tn(v[ck, :], ke[ck, :]), 0.0)

    if n_groups <= 4:
        for gi in range(n_groups):
            group_terms(gi)
    else:
        def terms_step(gi, carry):
            group_terms(gi)
            return carry

        lax.fori_loop(0, n_groups, terms_step, 0)

    slab = 32
    for s in range(n_seq):
        first = s * chunks_per_seq
        for d in range(2):
            order = range(chunks_per_seq) if d == 0 else range(chunks_per_seq - 1, -1, -1)
            for r in range(w // slab):
                rs = slice(r * slab, (r + 1) * slab)
                st = s0_ref[s, d, rs, :] if has_s0 else jnp.zeros((slab, w), F32)
                for j in order:
                    i = first + j
                    st_scr[d, i, rs, :] = st.astype(BF16)
                    st = st * dec_scr[d, i:i + 1, :] + u_scr[d, i, rs, :]
                if so_ref is not None:
                    h = (r * slab) // HG_DK
                    off = (r * slab) % HG_DK
                    so_ref[s, d, h, off:off + slab, :] = st[:, h * HG_DK:(h + 1) * HG_DK]

    ones_bd = bd.astype(BF16)

    def finish(gi):
        rows = pl.ds(gi * gl if isinstance(gi, int) else pl.multiple_of(gi * gl, gl), gl)
        inter = []
        for k in range(gc):
            i = gi * gc + k
            ck = pl.ds(gi * gl + k * c if isinstance(gi, int) else pl.multiple_of(gi * gl + k * c, c), c)
            inter.append(_dot_nt(qe_scr[0, ck, :], st_scr[0, i]) + _dot_nt(qe_scr[1, ck, :], st_scr[1, i]))
        o = oi_scr[0, rows, :] + oi_scr[1, rows, :] + jnp.concatenate(inter, axis=0)
        sq = _split(o * o, 2)
        ms = (_dot(sq[0], ones_bd) + _dot(sq[1], ones_bd)) * (1.0 / HG_DK)
        y = o * lax.rsqrt(ms + EPS) * ng_ref[...]
        hg = hg_ref[rows, :]
        o_ref[rows, :] = (y * _silu_t(hg)).astype(o_ref.dtype)

    if n_groups <= 4:
        for gi in range(n_groups):
            finish(gi)
    else:
        def finish_step(gi, carry):
            finish(gi)
            return carry

        lax.fori_loop(0, n_groups, finish_step, 0)


def _hgrn_scratch(rows):
    n_chunks = rows // CHUNK
    return [pltpu.VMEM((2, rows, HG_WIDTH), F32),
            pltpu.VMEM((2, rows, HG_WIDTH), BF16),
            pltpu.VMEM((2, n_chunks, HG_WIDTH, HG_WIDTH), F32),
            pltpu.VMEM((2, n_chunks, HG_WIDTH, HG_WIDTH), BF16),
            pltpu.VMEM((2, max(n_chunks, 8), HG_WIDTH), F32)]


def _hgrn2(proj_h, row0, n_seq, seq_len, seqs_per_step, lb, ng, s0t):
    gl = CHUNKS_PER_GROUP * CHUNK
    groups_per_seq = seq_len // gl
    rows = seqs_per_step * seq_len
    blk0 = row0 // rows
    tri, causal, bd = _hgrn_tables()
    col = lambda c0: pl.BlockSpec((rows, HG_WIDTH), lambda i: (blk0 + i, c0 // HG_WIDTH))
    const = lambda shape: pl.BlockSpec(shape, lambda i: (0,) * len(shape))
    in_specs = [col(COL_HQ), col(COL_HI), col(COL_HF), col(COL_HB), col(COL_HG),
                const((2, HG_WIDTH)), const((1, HG_WIDTH)),
                const((2, gl, gl)), const((2, gl, gl)), const((HG_WIDTH, HG_WIDTH))]
    args = [proj_h] * 5 + [lb, ng, tri, causal, bd]
    out_specs = [pl.BlockSpec((rows, HG_WIDTH), lambda i: (i, 0))]
    out_shape = [jax.ShapeDtypeStruct((n_seq * seq_len, HG_WIDTH), BF16)]
    if s0t is not None:
        in_specs.append(pl.BlockSpec((seqs_per_step, 2, HG_WIDTH, HG_WIDTH), lambda i: (i, 0, 0, 0)))
        args.append(s0t)
    else:
        out_specs.append(pl.BlockSpec((seqs_per_step, 2, HG_HEADS, HG_DK, HG_DK), lambda i: (i, 0, 0, 0, 0)))
        out_shape.append(jax.ShapeDtypeStruct((n_seq, 2, HG_HEADS, HG_DK, HG_DK), F32))
    return pl.pallas_call(
        functools.partial(_hgrn2_body, seqs_per_step, groups_per_seq, s0t is not None),
        grid=(n_seq // seqs_per_step,),
        in_specs=in_specs,
        out_specs=out_specs,
        out_shape=out_shape,
        scratch_shapes=_hgrn_scratch(rows),
        compiler_params=_params(1),
        name=f"hgrn_{seq_len}",
    )(*args)


def _ctx_mixers_body(n_seq, seq_len, lam_init, *refs):
    (q_ref, k_ref, v_ref, u_ref, hq_ref, hi_ref, hf_ref, hb_ref, hg_ref,
     cl_ref, sl_ref, cc_ref, sc_ref, wf_ref, lmb_ref, ag_ref,
     lb_ref, ng_ref, tri_ref, causal_ref, bd_ref,
     fn_ref, a_ref, ho_ref, so_ref) = refs[:25]
    scratch = refs[25:]

    zab = _channel_dft(u_ref[...].astype(BF16), cc_ref[...], sc_ref[...])
    for s in range(n_seq):
        rs = slice(s * seq_len, (s + 1) * seq_len)
        fn_ref[rs, :] = _fourier_rows(zab[rs, :], cl_ref[...], sl_ref[...], wf_ref[...]).astype(fn_ref.dtype)

    lam = _lambda_full(lmb_ref[...], lam_init)
    scale = DA_HALF ** -0.5
    for s in range(n_seq):
        rs = slice(s * seq_len, (s + 1) * seq_len)
        for h in range(DA_HEADS):
            sl = slice(h * DA_VDIM, (h + 1) * DA_VDIM)
            o = _diff_head(q_ref[rs, sl] * scale, [k_ref[rs, sl]], [v_ref[rs, sl]], lam, ag_ref[...], lam_init)
            a_ref[rs, sl] = o.astype(a_ref.dtype)

    _hgrn2_body(n_seq, seq_len // (CHUNKS_PER_GROUP * CHUNK), False,
                hq_ref, hi_ref, hf_ref, hb_ref, hg_ref, lb_ref, ng_ref, tri_ref, causal_ref, bd_ref,
                ho_ref, so_ref, *scratch)


def _ctx_mixers(proj_a, proj_h, n_seq, seq_len, seqs_per_step, w_f, lmb, ag, lam_init, lb, ng):
    rows = seqs_per_step * seq_len
    gl = CHUNKS_PER_GROUP * CHUNK
    cl, sl = _dft_tables(seq_len, seq_len)
    cc, sc = _dft_tables(FN_WIDTH, FN_GROUP_DIM)
    tri, causal, bd = _hgrn_tables()
    col = lambda c0, width: pl.BlockSpec((rows, width), lambda i: (i, c0 // width))
    const = lambda shape: pl.BlockSpec(shape, lambda i: (0,) * len(shape))
    out_rows = lambda width: pl.BlockSpec((rows, width), lambda i: (i, 0))
    n_tok = n_seq * seq_len
    return pl.pallas_call(
        functools.partial(_ctx_mixers_body, seqs_per_step, seq_len, lam_init),
        grid=(n_seq // seqs_per_step,),
        in_specs=[col(COL_Q, DA_WIDTH), col(COL_K, DA_WIDTH), col(COL_V, DA_WIDTH), col(COL_FN, FN_WIDTH),
                  col(COL_HQ, HG_WIDTH), col(COL_HI, HG_WIDTH), col(COL_HF, HG_WIDTH), col(COL_HB, HG_WIDTH),
                  col(COL_HG, HG_WIDTH),
                  const((seq_len, seq_len)), const((seq_len, seq_len)),
                  const((FN_WIDTH, FN_WIDTH)), const((FN_WIDTH, FN_WIDTH)), const((FN_WIDTH, FN_WIDTH)),
                  const((4, DA_HALF)), const((1, DA_VDIM)),
                  const((2, HG_WIDTH)), const((1, HG_WIDTH)),
                  const((2, gl, gl)), const((2, gl, gl)), const((HG_WIDTH, HG_WIDTH))],
        out_specs=[out_rows(FN_WIDTH), out_rows(DA_WIDTH), out_rows(HG_WIDTH),
                   pl.BlockSpec((seqs_per_step, 2, HG_HEADS, HG_DK, HG_DK), lambda i: (i, 0, 0, 0, 0))],
        out_shape=[jax.ShapeDtypeStruct((n_tok, FN_WIDTH), BF16),
                   jax.ShapeDtypeStruct((n_tok, DA_WIDTH), BF16),
                   jax.ShapeDtypeStruct((n_tok, HG_WIDTH), BF16),
                   jax.ShapeDtypeStruct((n_seq, 2, HG_HEADS, HG_DK, HG_DK), F32)],
        scratch_shapes=_hgrn_scratch(rows),
        compiler_params=_params(1),
        name="ctx_mixers",
    )(*([proj_a] * 4 + [proj_h] * 5), cl, sl, cc, sc, w_f, lmb, ag, lb, ng, tri, causal, bd)


def _route_gates(lt, bias):
    per = N_EXPERTS // N_GROUPS
    tt = lt.shape[1]
    neg = -jnp.inf
    assert per == N_GROUPS == 8
    gi = lax.broadcasted_iota(jnp.int32, (N_GROUPS, tt), 0).astype(F32)
    s_j, b_j = [], []
    for j in range(per):
        s = _sigmoid(lt[j * N_GROUPS:(j + 1) * N_GROUPS, :])
        s_j.append(s)
        b_j.append(s + bias[j * N_GROUPS:(j + 1) * N_GROUPS, :])
    m1 = functools.reduce(jnp.maximum, b_j)
    i1 = functools.reduce(jnp.minimum, [jnp.where(b_j[j] == m1, float(j), float(per)) for j in range(per)])
    m2 = functools.reduce(jnp.maximum, [jnp.where(i1 == float(j), neg, b_j[j]) for j in range(per)])
    gs = m1 + m2
    gsel = jnp.zeros((N_GROUPS, tt), jnp.bool_)
    for _ in range(TOPK_GROUPS):
        m = gs.max(axis=0, keepdims=True)
        idx = jnp.where(gs == m, gi, float(N_GROUPS)).min(axis=0, keepdims=True)
        hit = gi == idx
        gsel = gsel | hit
        gs = jnp.where(hit, neg, gs)
    x_j = [jnp.where(gsel, b_j[j], neg) for j in range(per)]
    e_j = [gi * per + j for j in range(per)]
    sel_j = [jnp.zeros((N_GROUPS, tt), jnp.bool_) for _ in range(per)]
    for _ in range(TOP_K):
        m = functools.reduce(jnp.maximum, x_j).max(axis=0, keepdims=True)
        idx = functools.reduce(jnp.minimum, [jnp.where(x_j[j] == m, e_j[j], float(N_EXPERTS))
                                             for j in range(per)]).min(axis=0, keepdims=True)
        for j in range(per):
            hit = e_j[j] == idx
            sel_j[j] = sel_j[j] | hit
            x_j[j] = jnp.where(hit, neg, x_j[j])
    w_j = [jnp.where(sel_j[j], s_j[j], 0.0) for j in range(per)]
    denom = functools.reduce(lambda a, b: a + b, w_j).sum(axis=0, keepdims=True)
    gates_t = jnp.concatenate([w / denom * ROUTED_SCALE for w in w_j], axis=0)
    r_io = lax.broadcasted_iota(jnp.int32, (N_EXPERTS, N_EXPERTS), 0)
    e_io = lax.broadcasted_iota(jnp.int32, (N_EXPERTS, N_EXPERTS), 1)
    eye = (e_io == (r_io & (N_GROUPS - 1)) * per + (r_io >> 3)).astype(BF16)
    p = _split(gates_t, 3)
    return _dot_tn(p[0], eye) + _dot_tn(p[1], eye) + _dot_tn(p[2], eye)


def _outproj_body(n_x, n_ctx_tiles, *refs):
    fn_refs, a_refs, hg_refs = refs[0:2], refs[2:4], refs[4:6]
    x_refs = refs[6:6 + n_x]
    mod_ref, g_ref, w_ref, wr_ref, rb_ref, xo_ref, h_ref, gate_ref = refs[6 + n_x:]
    is_ctx = pl.program_id(0) < n_ctx_tiles
    mix = _dot(jnp.concatenate([_pick(is_ctx, fn_refs), _pick(is_ctx, a_refs), _pick(is_ctx, hg_refs)], axis=1),
               w_ref[...])
    x = _pick(is_ctx, x_refs) + mod_ref[2:3, :] * mix
    xo_ref[...] = x
    h = _modnorm(x, g_ref[...], mod_ref[3:4, :], mod_ref[4:5, :])
    h_ref[...] = h.astype(BF16)
    h_hi, h_lo = _split(h, 2)
    w_hi, w_lo = _split(wr_ref[...], 2)
    logits_t = _dot_nt(w_hi, h_hi) + _dot_nt(w_lo, h_hi) + _dot_nt(w_hi, h_lo)
    gate_ref[...] = _route_gates(logits_t, rb_ref[...])


def _outproj(fn, a, hg, xs, mod, g, w_out, w_router_t, router_bias, n_ctx_tok, lat_len):
    t = sum(x.shape[0] for x in xs)
    tm = TOKEN_TILE
    n_ctx_tiles = n_ctx_tok // tm
    row = _mod_row(n_ctx_tiles, lat_len // tm)
    rows = lambda width: pl.BlockSpec((tm, width), lambda i: (i, 0))
    parts = lambda n, width: _row_specs(n, tm, width, n_ctx_tiles)
    return pl.pallas_call(
        functools.partial(_outproj_body, len(xs), n_ctx_tiles),
        grid=(t // tm,),
        in_specs=parts(2, FN_WIDTH) + parts(2, DA_WIDTH) + parts(2, HG_WIDTH) + parts(len(xs), D_MODEL) + [
            pl.BlockSpec((None, 6, D_MODEL), lambda i: (row(i), 0, 0)),
            pl.BlockSpec((1, D_MODEL), lambda i: (0, 0)),
            pl.BlockSpec((D_MODEL, D_MODEL), lambda i: (0, 0)),
            pl.BlockSpec((N_EXPERTS, D_MODEL), lambda i: (0, 0)),
            pl.BlockSpec((N_EXPERTS, 1), lambda i: (0, 0)),
        ],
        out_specs=[rows(D_MODEL), rows(D_MODEL), rows(N_EXPERTS)],
        out_shape=[jax.ShapeDtypeStruct((t, D_MODEL), F32),
                   jax.ShapeDtypeStruct((t, D_MODEL), BF16),
                   jax.ShapeDtypeStruct((t, N_EXPERTS), F32)],
        compiler_params=_params(1),
        name="outproj",
    )(*fn, *a, *hg, *xs, mod, g, w_out, w_router_t, router_bias)


def _moe_body(final_norm, n_ctx_tiles, h_ref, gate_ref, wg_ref, wu_ref, wd_ref, sg_ref, su_ref, sd_ref,
              x_ref, mod_ref, fg_ref, *out_and_scratch):
    acc_ref = out_and_scratch[-1]
    o_refs = out_and_scratch[:-1]
    is_ctx = pl.program_id(0) < n_ctx_tiles
    j = pl.program_id(1)
    h = h_ref[...]

    def act_of(wg, wu, gate):
        gu = _dot(h, jnp.concatenate([wg.astype(BF16), wu.astype(BF16)], axis=1))
        a = _silu_t(gu[:, :D_EXPERT]) * gu[:, D_EXPERT:]
        return a if gate is None else a * gate

    @pl.when(j == 0)
    def _():
        acc_ref[...] = _dot(act_of(sg_ref[...], su_ref[...], None).astype(BF16), sd_ref[...].astype(BF16))

    gates = gate_ref[...]
    expert_of_lane = lax.broadcasted_iota(jnp.int32, gates.shape, 1)

    def gate_col(p):
        e = j * EXPERTS_PER_STEP + p
        return jnp.sum(jnp.where(expert_of_lane == e, gates, 0.0), axis=1, keepdims=True)

    acts = [act_of(wg_ref[p], wu_ref[p], gate_col(p)).astype(BF16) for p in range(EXPERTS_PER_STEP)]
    wd = wd_ref[...].astype(BF16).reshape(EXPERTS_PER_STEP * D_EXPERT, D_MODEL)
    acc_ref[...] += _dot(jnp.concatenate(acts, axis=1), wd)

    @pl.when(j == pl.num_programs(1) - 1)
    def _():
        x = x_ref[...] + mod_ref[5:6, :] * acc_ref[...]
        if not final_norm:
            o_refs[0][...] = x
        else:
            ms = jnp.mean(x * x, axis=-1, keepdims=True)
            y = x * lax.rsqrt(ms + EPS) * fg_ref[...]

            @pl.when(is_ctx)
            def _():
                o_refs[0][...] = y

            @pl.when(jnp.logical_not(is_ctx))
            def _():
                o_refs[1][...] = y


def _moe(h, gates, w_gate, w_up, w_down2, ws_gate, ws_up, ws_down, layer, x, mod, final_g, final_norm,
         n_ctx_tok, lat_len):
    t = x.shape[0]
    tm = MOE_TILE
    eps_ = EXPERTS_PER_STEP
    n_ctx_tiles = n_ctx_tok // tm
    row = _mod_row(n_ctx_tiles, lat_len // tm)
    once = pl.Buffered(1)
    if final_norm:
        last_ctx = n_ctx_tiles - 1
        out_specs = [pl.BlockSpec((tm, D_MODEL), lambda i, j: (jnp.minimum(i, last_ctx), 0), pipeline_mode=once),
                     pl.BlockSpec((tm, D_MODEL), lambda i, j: (jnp.maximum(i - n_ctx_tiles, 0), 0),
                                  pipeline_mode=once)]
        out_shape = [jax.ShapeDtypeStruct((n_ctx_tok, D_MODEL), F32),
                     jax.ShapeDtypeStruct((t - n_ctx_tok, D_MODEL), F32)]
    else:
        out_specs = [pl.BlockSpec((tm, D_MODEL), lambda i, j: (i, 0), pipeline_mode=once)]
        out_shape = [jax.ShapeDtypeStruct((t, D_MODEL), F32)]
    return pl.pallas_call(
        functools.partial(_moe_body, final_norm, n_ctx_tiles),
        grid=(t // tm, N_EXPERTS // eps_),
        in_specs=[
            pl.BlockSpec((tm, D_MODEL), lambda i, j: (i, 0)),
            pl.BlockSpec((tm, N_EXPERTS), lambda i, j: (i, 0)),
            pl.BlockSpec((None, eps_, D_MODEL, D_EXPERT), lambda i, j: (layer, j, 0, 0)),
            pl.BlockSpec((None, eps_, D_MODEL, D_EXPERT), lambda i, j: (layer, j, 0, 0)),
            pl.BlockSpec((None, eps_ // 2, 2 * D_EXPERT, D_MODEL), lambda i, j: (layer, j, 0, 0)),
            pl.BlockSpec((None, D_MODEL, D_EXPERT), lambda i, j: (layer, 0, 0), pipeline_mode=once),
            pl.BlockSpec((None, D_MODEL, D_EXPERT), lambda i, j: (layer, 0, 0), pipeline_mode=once),
            pl.BlockSpec((None, D_EXPERT, D_MODEL), lambda i, j: (layer, 0, 0), pipeline_mode=once),
            pl.BlockSpec((tm, D_MODEL), lambda i, j: (i, 0), pipeline_mode=once),
            pl.BlockSpec((None, 6, D_MODEL), lambda i, j: (row(i), 0, 0)),
            pl.BlockSpec((1, D_MODEL), lambda i, j: (0, 0)),
        ],
        out_specs=out_specs,
        out_shape=out_shape,
        scratch_shapes=[pltpu.VMEM((tm, D_MODEL), F32)],
        compiler_params=_params(2),
        name="moe",
    )(h, gates, w_gate, w_up, w_down2, ws_gate, ws_up, ws_down, x, mod, final_g)


def _block_diag_t(s):
    eye = jnp.eye(HG_HEADS, dtype=s.dtype)
    out = jnp.einsum('...hkv,hg->...hvgk', s, eye)
    return out.reshape(s.shape[:-3] + (HG_WIDTH, HG_WIDTH))


def kernel(x_prompt, x_sample, cache_k, cache_v, state_hgrn, c, c_ctx, w_ada, b_ada, norm_g, w_in,
           w_fourier, lambdas, attn_norm_g, lower_bounds, hg_norm_g, w_out, w_router, router_bias,
           w_gate, w_up, w_down, ws_gate, ws_up, ws_down, final_g):
    n_ctx, ctx_len, _ = x_prompt.shape
    n_lat, lat_len, _ = x_sample.shape
    n_ctx_tok = n_ctx * ctx_len
    past = cache_k.shape[2]

    xs = (x_prompt.reshape(n_ctx_tok, D_MODEL), x_sample.reshape(n_lat * lat_len, D_MODEL))

    c8 = jnp.zeros((8, D_MODEL), F32).at[0].set(c_ctx).at[1:1 + n_lat].set(c)
    mods = _ada_mods(c8, w_ada, b_ada).reshape(DEPTH, 8, 6, D_MODEL)

    cs = jnp.cumsum(jax.nn.softmax(lower_bounds.astype(F32), axis=0), axis=0)
    lbs = cs - cs[0:1]

    cos, sin = _rope_tables(lat_len)
    cache_k4 = cache_k.reshape(n_lat, DEPTH, past, DA_WIDTH).astype(BF16)
    cache_v4 = cache_v.reshape(n_lat, DEPTH, past, DA_WIDTH).astype(BF16)
    s0t = _block_diag_t(state_hgrn.astype(F32))

    qkv_end = FN_WIDTH + 3 * DA_WIDTH
    w_in_b = jnp.concatenate([w_in[:, :, FN_WIDTH:qkv_end], w_in[:, :, :FN_WIDTH], w_in[:, :, qkv_end:]],
                             axis=-1).astype(BF16)
    w_f_b = w_fourier.astype(BF16)
    w_out_b = w_out.astype(BF16)
    per = N_EXPERTS // N_GROUPS
    w_router_t = (jnp.swapaxes(w_router, 1, 2).reshape(DEPTH, N_GROUPS, per, D_MODEL)
                  .swapaxes(1, 2).reshape(DEPTH, N_EXPERTS, D_MODEL))
    bias_mm = router_bias.reshape(DEPTH, N_GROUPS, per).swapaxes(1, 2).reshape(DEPTH, N_EXPERTS, 1)
    w_down2 = w_down.reshape(DEPTH, N_EXPERTS // 2, 2 * D_EXPERT, D_MODEL)
    ng = jnp.tile(hg_norm_g, (1, HG_HEADS))

    caches = []
    new_s = []
    for l in range(DEPTH):
        lam_init = 0.8 - 0.6 * math.exp(-0.3 * l)
        mod = mods[l]
        last = l == DEPTH - 1
        proj_a, proj_h, k_l, v_l = _inproj(xs, mod, norm_g[l, 0:1], w_in_b[l], tuple(caches) if last else (),
                                           n_ctx, ctx_len, lat_len)
        caches = [k_l, v_l] if last else caches + [k_l, v_l]

        ag = attn_norm_g[l].reshape(1, DA_VDIM)
        fn_ctx, a_ctx, hg_ctx, st_ctx = _ctx_mixers(proj_a, proj_h, n_ctx, ctx_len, 2, w_f_b[l], lambdas[l], ag,
                                                    lam_init, lbs[l], ng[l:l + 1])
        fn_lat = _fourier(proj_a, n_ctx_tok, n_lat, lat_len, w_f_b[l])
        a_lat = _attn_lat(proj_a, n_ctx_tok, n_lat, lat_len, cache_k4, cache_v4, l, cos, sin, lambdas[l], ag,
                          lam_init)
        (hg_lat,) = _hgrn2(proj_h, n_ctx_tok, n_lat, lat_len, 1, lbs[l], ng[l:l + 1], s0t[:, l])
        new_s.append(jnp.swapaxes(st_ctx, -1, -2))

        x, h2, gates = _outproj((fn_ctx, fn_lat), (a_ctx, a_lat), (hg_ctx, hg_lat), xs, mod,
                                norm_g[l, 1:2], w_out_b[l], w_router_t[l], bias_mm[l], n_ctx_tok, lat_len)
        xs = _moe(h2, gates, w_gate, w_up, w_down2, ws_gate, ws_up, ws_down, l, x, mod,
                  final_g.reshape(1, D_MODEL), l == DEPTH - 1, n_ctx_tok, lat_len)

    y_prompt = xs[0].reshape(n_ctx, ctx_len, D_MODEL)
    y_sample = xs[1].reshape(n_lat, lat_len, D_MODEL)
    new_k = caches[0].reshape(n_ctx, DEPTH, ctx_len, DA_HEADS, DA_VDIM)
    new_v = caches[1].reshape(n_ctx, DEPTH, ctx_len, DA_HEADS, DA_VDIM)
    return (y_prompt, y_sample, new_k, new_v, jnp.stack(new_s, axis=1))
```

```python
import functools
import math

import numpy as np
import jax
import jax.numpy as jnp
from jax import lax
from jax.experimental import pallas as pl
from jax.experimental.pallas import tpu as pltpu

F32 = jnp.float32
BF16 = jnp.bfloat16

D_MODEL = 1024
DEPTH = 2
GRID_W = 64
FN_WIDTH = 256
FN_GROUP_DIM = 64
DA_WIDTH = 512
DA_HEADS = 4
DA_VDIM = 128
DA_HALF = 64
HG_WIDTH = 256
HG_HEADS = 4
HG_DK = 64
CHUNK = 64
ROPE_THETA = 10000.0
N_EXPERTS = 64
TOP_K = 8
N_GROUPS = 8
TOPK_GROUPS = 4
D_EXPERT = 128
ROUTED_SCALE = 2.5
EPS = 1e-6

COL_Q, COL_K, COL_V, COL_FN = 0, 512, 1024, 1536
PROJ_A_WIDTH = 1792
COL_HQ, COL_HI, COL_HF, COL_HB, COL_HG = 0, 256, 512, 768, 1024
PROJ_H_WIDTH = 1280

TOKEN_TILE = 512
MOE_TILE = 1024
EXPERTS_PER_STEP = 8
CHUNKS_PER_GROUP = 4
VMEM_LIMIT = 56 * 1024 * 1024


def _dot(a, b):
    return jnp.dot(a, b, preferred_element_type=F32)


def _dot_nt(a, b):
    return lax.dot_general(a, b, (((1,), (1,)), ((), ())), preferred_element_type=F32)


def _dot_tn(a, b):
    return lax.dot_general(a, b, (((0,), (0,)), ((), ())), preferred_element_type=F32)


def _split(x, n):
    parts = []
    r = x
    for i in range(n):
        p = r.astype(BF16)
        parts.append(p)
        if i + 1 < n:
            r = r - p.astype(F32)
    return parts


def _sigmoid(x):
    return 1.0 / (1.0 + jnp.exp(-x))


def _silu_t(x, scale=1.0):
    return (0.5 * scale) * x * (jnp.tanh(0.5 * x) + 1.0)


def _silu(x):
    return x * _sigmoid(x)


def _params(n_axes):
    return pltpu.CompilerParams(dimension_semantics=("arbitrary",) * n_axes,
                                vmem_limit_bytes=VMEM_LIMIT)


def _ada_body(c_ref, w_ref, b_ref, o_ref):
    a = _silu(c_ref[...])
    a_hi, a_lo = _split(a, 2)
    w_hi, w_lo = _split(w_ref[...], 2)
    o_ref[...] = _dot(a_hi, w_hi) + _dot(a_lo, w_hi) + _dot(a_hi, w_lo) + b_ref[...]


def _ada_mods(c8, w_ada, b_ada):
    tn = 1536
    return pl.pallas_call(
        _ada_body,
        grid=(DEPTH, 6 * D_MODEL // tn),
        in_specs=[
            pl.BlockSpec((8, D_MODEL), lambda l, j: (0, 0)),
            pl.BlockSpec((None, D_MODEL, tn), lambda l, j: (l, 0, j)),
            pl.BlockSpec((None, 1, tn), lambda l, j: (l, 0, j)),
        ],
        out_specs=pl.BlockSpec((None, 8, tn), lambda l, j: (l, 0, j)),
        out_shape=jax.ShapeDtypeStruct((DEPTH, 8, 6 * D_MODEL), F32),
        compiler_params=_params(2),
        name="ada_mods",
    )(c8, w_ada, b_ada.reshape(DEPTH, 1, 6 * D_MODEL))


def _modnorm(x, g, shift, scale):
    ms = jnp.mean(x * x, axis=-1, keepdims=True)
    return (x * lax.rsqrt(ms + EPS) * g) * (1.0 + scale) + shift


def _mod_row(n_ctx_tiles, tiles_per_latent):
    def f(i):
        return jnp.where(i < n_ctx_tiles, 0, 1 + (i - n_ctx_tiles) // tiles_per_latent)
    return f


def _row_specs(n_parts, tm, width, n_ctx_tiles):
    if n_parts == 1:
        return [pl.BlockSpec((tm, width), lambda i, *_: (i, 0))]
    return [pl.BlockSpec((tm, width), lambda i, *_: (jnp.minimum(i, n_ctx_tiles - 1), 0)),
            pl.BlockSpec((tm, width), lambda i, *_: (jnp.maximum(i - n_ctx_tiles, 0), 0))]


def _pick(is_ctx, refs):
    if len(refs) == 1:
        return refs[0][...]
    return jnp.where(is_ctx, refs[0][...], refs[1][...])


def _inproj_body(n_x, n_prev, n_ctx_tiles, ctx_len, *refs):
    x_refs = refs[:n_x]
    mod_ref, g_ref, w_ref = refs[n_x:n_x + 3]
    prev_refs = refs[n_x + 3:n_x + 3 + 2 * n_prev]
    oa_ref, oh_ref, k_ref, v_ref = refs[-4:]
    is_ctx = pl.program_id(0) < n_ctx_tiles
    x = _pick(is_ctx, x_refs)
    h = _modnorm(x, g_ref[...], mod_ref[0:1, :], mod_ref[1:2, :])
    proj = _dot(h.astype(BF16), w_ref[...])
    oa_ref[...] = proj[:, :PROJ_A_WIDTH].astype(BF16)
    oh_ref[...] = proj[:, PROJ_A_WIDTH:]

    @pl.when(is_ctx)
    def _():
        per_layer = ctx_len * DA_HEADS
        for s in range(proj.shape[0] // ctx_len):
            base = s * (n_prev + 1) * per_layer
            for l in range(n_prev):
                dst = slice(base + l * per_layer, base + (l + 1) * per_layer)
                src = slice(s * per_layer, (s + 1) * per_layer)
                k_ref[dst, :] = prev_refs[2 * l][src, :]
                v_ref[dst, :] = prev_refs[2 * l + 1][src, :]
            tok = slice(s * ctx_len, (s + 1) * ctx_len)
            for h in range(DA_HEADS):
                rows = pl.ds(base + n_prev * per_layer + h, ctx_len, stride=DA_HEADS)
                k_ref[rows, :] = proj[tok, COL_K + h * DA_VDIM:COL_K + (h + 1) * DA_VDIM]
                v_ref[rows, :] = proj[tok, COL_V + h * DA_VDIM:COL_V + (h + 1) * DA_VDIM]


def _inproj(xs, mod, g, w, prev_caches, n_ctx, ctx_len, lat_len):
    t = sum(x.shape[0] for x in xs)
    tm = TOKEN_TILE
    seqs = tm // ctx_len
    n_ctx_tiles = n_ctx // seqs
    n_prev = len(prev_caches) // 2
    row = _mod_row(n_ctx_tiles, lat_len // tm)
    tile_rows = seqs * ctx_len * DA_HEADS
    tile_block = lambda i: (jnp.minimum(i, n_ctx_tiles - 1), 0)
    cache_spec = pl.BlockSpec(((n_prev + 1) * tile_rows, DA_VDIM), tile_block)
    cache_shape = jax.ShapeDtypeStruct((n_ctx_tiles * (n_prev + 1) * tile_rows, DA_VDIM), F32)
    in_specs = _row_specs(len(xs), tm, D_MODEL, n_ctx_tiles) + [
        pl.BlockSpec((None, 6, D_MODEL), lambda i: (row(i), 0, 0)),
        pl.BlockSpec((1, D_MODEL), lambda i: (0, 0)),
        pl.BlockSpec((D_MODEL, PROJ_A_WIDTH + PROJ_H_WIDTH), lambda i: (0, 0)),
    ] + [pl.BlockSpec((tile_rows, DA_VDIM), tile_block)] * (2 * n_prev)
    return pl.pallas_call(
        functools.partial(_inproj_body, len(xs), n_prev, n_ctx_tiles, ctx_len),
        grid=(t // tm,),
        in_specs=in_specs,
        out_specs=[pl.BlockSpec((tm, PROJ_A_WIDTH), lambda i: (i, 0)),
                   pl.BlockSpec((tm, PROJ_H_WIDTH), lambda i: (i, 0)), cache_spec, cache_spec],
        out_shape=[jax.ShapeDtypeStruct((t, PROJ_A_WIDTH), BF16), jax.ShapeDtypeStruct((t, PROJ_H_WIDTH), F32),
                   cache_shape, cache_shape],
        compiler_params=_params(1),
        name="inproj",
    )(*xs, mod, g, w, *prev_caches)


def _channel_dft(z, cc, sc):
    return _dot(z, jnp.concatenate([cc, sc], axis=1)).astype(BF16)


def _fourier_rows(zab, cl, sl, w_f):
    y = _dot(jnp.concatenate([cl, -sl], axis=1),
             jnp.concatenate([zab[:, :FN_WIDTH], zab[:, FN_WIDTH:]], axis=0))
    return _dot(y.astype(BF16), w_f)


def _fourier_body(u_ref, cl_ref, sl_ref, cc_ref, sc_ref, w_ref, o_ref):
    zab = _channel_dft(u_ref[...].astype(BF16), cc_ref[...], sc_ref[...])
    o_ref[...] = _fourier_rows(zab, cl_ref[...], sl_ref[...], w_ref[...]).astype(o_ref.dtype)


def _dft_tables(n, block):
    i = np.arange(n)
    prod = (i[:, None] % block) * (i[None, :] % block) % block
    ang = prod.astype(np.float64) * (2.0 * math.pi / block)
    same = (i[:, None] // block) == (i[None, :] // block)
    scale = 1.0 / math.sqrt(block)
    c = np.where(same, np.cos(ang) * scale, 0.0).astype(np.float32)
    s = np.where(same, np.sin(ang) * scale, 0.0).astype(np.float32)
    return jnp.asarray(c).astype(BF16), jnp.asarray(s).astype(BF16)


def _fourier(proj, row0, n_seq, seq_len, w_f):
    cl, sl = _dft_tables(seq_len, seq_len)
    cc, sc = _dft_tables(FN_WIDTH, FN_GROUP_DIM)
    blk0 = row0 // seq_len
    full = lambda shape: pl.BlockSpec(shape, lambda i: (0, 0))
    return pl.pallas_call(
        _fourier_body,
        grid=(n_seq,),
        in_specs=[
            pl.BlockSpec((seq_len, FN_WIDTH), lambda i: (blk0 + i, COL_FN // FN_WIDTH)),
            full((seq_len, seq_len)), full((seq_len, seq_len)),
            full((FN_WIDTH, FN_WIDTH)), full((FN_WIDTH, FN_WIDTH)), full((FN_WIDTH, FN_WIDTH)),
        ],
        out_specs=pl.BlockSpec((seq_len, FN_WIDTH), lambda i: (i, 0)),
        out_shape=jax.ShapeDtypeStruct((n_seq * seq_len, FN_WIDTH), BF16),
        compiler_params=_params(1),
        name=f"fourier_{seq_len}",
    )(proj, cl, sl, cc, sc, w_f)


def _lambda_full(lmb, lam_init):
    a = jnp.sum(lmb[0:1, :] * lmb[1:2, :], axis=-1, keepdims=True)
    b = jnp.sum(lmb[2:3, :] * lmb[3:4, :], axis=-1, keepdims=True)
    return jnp.exp(a) - jnp.exp(b) + lam_init


def _diff_head(q, ks, vs, lam, g, lam_init):
    vas = [jnp.concatenate([v.astype(BF16), jnp.ones(v.shape, BF16)], axis=1) for v in vs]
    lq = q.shape[0]
    first_map = (lax.broadcasted_iota(jnp.int32, (1, 2 * DA_HALF), 1) < DA_HALF).astype(BF16)
    qb = q.astype(BF16)
    q_stack = jnp.concatenate([qb * first_map, qb * (1 - first_map)], axis=0)
    parts = [_dot_nt(q_stack, k.astype(BF16)) for k in ks]
    mx = parts[0].max(axis=-1, keepdims=True)
    for p in parts[1:]:
        mx = jnp.maximum(mx, p.max(axis=-1, keepdims=True))
    oa = _dot(jnp.exp(parts[0] - mx).astype(BF16), vas[0])
    for p, va in zip(parts[1:], vas[1:]):
        oa = oa + _dot(jnp.exp(p - mx).astype(BF16), va)
    o = oa[:, :DA_VDIM] * (1.0 / oa[:, DA_VDIM:DA_VDIM + 1])
    a = o[:lq] - lam * o[lq:]
    ms = jnp.mean(a * a, axis=-1, keepdims=True)
    return a * lax.rsqrt(ms + EPS) * g * (1.0 - lam_init)


def _rope(x, cos, sin):
    lane = lax.broadcasted_iota(jnp.int32, x.shape, 1)
    first = ((lane >> 4) & 1) == 0
    rot = jnp.where(first, -pltpu.roll(x, 128 - DA_HALF // 4, 1), pltpu.roll(x, DA_HALF // 4, 1))
    return x * cos + rot * sin


def _attn_lat_body(lam_init, tq, q_ref, k_ref, v_ref, kc_ref, vc_ref, cos_ref, sin_ref, lmb_ref, g_ref, o_ref):
    lam = _lambda_full(lmb_ref[...], lam_init)
    scale = DA_HALF ** -0.5
    k = _rope(k_ref[...].astype(F32), cos_ref[...], sin_ref[...])
    for j in range(q_ref.shape[0] // tq):
        rows = slice(j * tq, (j + 1) * tq)
        q = _rope(q_ref[rows, :].astype(F32), cos_ref[rows, :], sin_ref[rows, :]) * scale
        o = _diff_head(q, [k, kc_ref[...]], [v_ref[...], vc_ref[...]], lam, g_ref[...], lam_init)
        o_ref[rows, :] = o.astype(o_ref.dtype)


def _attn_lat(proj, row0, n_seq, seq_len, cache_k, cache_v, layer, cos, sin, lmb, g, lam_init):
    past = cache_k.shape[2]
    kb0 = row0 // seq_len
    head_cols = lambda c0: pl.BlockSpec((seq_len, DA_VDIM), lambda b, h: (kb0 + b, c0 // DA_VDIM + h))
    cache_spec = pl.BlockSpec((None, None, past, DA_VDIM), lambda b, h: (b, layer, 0, h))
    table = pl.BlockSpec((seq_len, DA_VDIM), lambda b, h: (0, 0))
    return pl.pallas_call(
        functools.partial(_attn_lat_body, lam_init, 512),
        grid=(n_seq, DA_HEADS),
        in_specs=[
            head_cols(COL_Q), head_cols(COL_K), head_cols(COL_V),
            cache_spec, cache_spec, table, table,
            pl.BlockSpec((4, DA_HALF), lambda b, h: (0, 0)),
            pl.BlockSpec((1, DA_VDIM), lambda b, h: (0, 0)),
        ],
        out_specs=pl.BlockSpec((seq_len, DA_VDIM), lambda b, h: (b, h)),
        out_shape=jax.ShapeDtypeStruct((n_seq * seq_len, DA_WIDTH), BF16),
        compiler_params=_params(2),
        name="attn_lat",
    )(proj, proj, proj, cache_k, cache_v, cos, sin, lmb, g)


def _rope_tables(n_tokens):
    rows = n_tokens // GRID_W
    row = np.repeat(np.arange(rows, dtype=np.float64), GRID_W)
    col = np.tile(np.arange(GRID_W, dtype=np.float64), rows)
    axis_dim = DA_HALF // 2
    inv_freq = ROPE_THETA ** (-np.arange(0, axis_dim, 2, dtype=np.float64) / axis_dim)
    ang_r = row[:, None] * inv_freq[None, :]
    ang_c = col[:, None] * inv_freq[None, :]
    ang = np.concatenate([ang_r, ang_r, ang_c, ang_c] * 2, axis=-1)
    return jnp.asarray(np.cos(ang).astype(np.float32)), jnp.asarray(np.sin(ang).astype(np.float32))


def _hgrn_tables():
    gl = CHUNKS_PER_GROUP * CHUNK
    t = np.arange(gl)
    same = (t[:, None] // CHUNK) == (t[None, :] // CHUNK)
    fwd = same & (t[None, :] <= t[:, None])
    bwd = same & (t[None, :] >= t[:, None])
    f = np.arange(HG_WIDTH)
    bd = (f[:, None] // HG_DK) == (f[None, :] // HG_DK)
    tri = jnp.asarray(np.stack([fwd, bwd]).astype(np.float32)).astype(BF16)
    causal = jnp.asarray(np.stack([fwd, bwd]).astype(np.float32))
    return tri, causal, jnp.asarray(bd.astype(np.float32))


def _hgrn2_body(n_seq, groups_per_seq, has_s0, *refs):
    (hq_ref, hi_ref, hf_ref, hb_ref, hg_ref, lb_ref, ng_ref, tri_ref, causal_ref, bd_ref) = refs[:10]
    if has_s0:
        s0_ref, o_ref, oi_scr, qe_scr, u_scr, st_scr, dec_scr = refs[10:]
        so_ref = None
    else:
        o_ref, so_ref, oi_scr, qe_scr, u_scr, st_scr, dec_scr = refs[10:]
        s0_ref = None
    c = CHUNK
    w = HG_WIDTH
    gc = CHUNKS_PER_GROUP
    gl = gc * c
    n_groups = n_seq * groups_per_seq
    chunks_per_seq = groups_per_seq * gc
    lane = lax.broadcasted_iota(jnp.int32, (1, w), 1)
    dk_bits = HG_DK.bit_length() - 1
    head_masks_b = [((lane >> dk_bits) == h).astype(BF16) for h in range(HG_HEADS)]
    bd = bd_ref[...] > 0.5
    edge_rows = (c - 1, 0)
    fp_refs = (hf_ref, hb_ref)

    def per_chunk_row(x, row):
        return jnp.concatenate([jnp.broadcast_to(x[k * c + row:k * c + row + 1, :], (c, w)) for k in range(gc)],
                               axis=0)

    def group_terms(gi):
        rows = pl.ds(gi * gl if isinstance(gi, int) else pl.multiple_of(gi * gl, gl), gl)
        hq = hq_ref[rows, :]
        q = _silu_t(hq, HG_DK ** -0.5)
        v = hi_ref[rows, :].astype(BF16)
        v_stack = jnp.concatenate([v * hm for hm in head_masks_b], axis=0)
        for d in range(2):
            lb = lb_ref[d:d + 1, :]
            fp = fp_refs[d][rows, :]
            lsig = jnp.minimum(fp, 0.0) - jnp.log(1.0 + jnp.exp(-jnp.abs(fp)))
            la = jnp.log(lb)
            lbb = jnp.log(1.0 - lb) + lsig
            logf = jnp.maximum(la, lbb) + jnp.log(1.0 + jnp.exp(-jnp.abs(la - lbb)))
            kk = (0.5 * (1.0 - lb)) * (1.0 - jnp.tanh(0.5 * fp))
            g_hi, g_lo = _split(logf, 2)
            cum = _dot(tri_ref[d], g_hi) + _dot(tri_ref[d], g_lo)
            total = per_chunk_row(cum, edge_rows[d])
            ref = per_chunk_row(cum, c // 2)
            qc = (q * jnp.exp(cum - ref)).astype(BF16)
            kc = (kk * jnp.exp(ref - cum)).astype(BF16)
            ke = (kk * jnp.exp(total - cum)).astype(BF16)
            qe_scr[d, rows, :] = (q * jnp.exp(cum)).astype(BF16)
            kc_stack = jnp.concatenate([kc * hm for hm in head_masks_b], axis=0)
            a = _dot_nt(qc, kc_stack)
            keep = causal_ref[d] > 0.5
            a = jnp.concatenate([jnp.where(keep, a[:, h * gl:(h + 1) * gl], 0.0) for h in range(HG_HEADS)],
                                axis=1).astype(BF16)
            oi_scr[d, rows, :] = _dot(a, v_stack)
            for k in range(gc):
                ck = slice(k * c, (k + 1) * c)
                i = gi * gc + k
                dec_scr[d, pl.ds(i, 1), :] = jnp.exp(cum[k * c + edge_rows[d]:k * c + edge_rows[d] + 1, :])
                u_scr[d, i] = jnp.where(bd, _dot_tn(v[ck, :], ke[ck, :]), 0.0)

    if n_groups <= 4:
        for gi in range(n_groups):
            group_terms(gi)
    else:
        def terms_step(gi, carry):
            group_terms(gi)
            return carry

        lax.fori_loop(0, n_groups, terms_step, 0)

    slab = 32
    for s in range(n_seq):
        first = s * chunks_per_seq
        for d in range(2):
            order = range(chunks_per_seq) if d == 0 else range(chunks_per_seq - 1, -1, -1)
            for r in range(w // slab):
                rs = slice(r * slab, (r + 1) * slab)
                st = s0_ref[s, d, rs, :] if has_s0 else jnp.zeros((slab, w), F32)
                for j in order:
                    i = first + j
                    st_scr[d, i, rs, :] = st.astype(BF16)
                    st = st * dec_scr[d, i:i + 1, :] + u_scr[d, i, rs, :]
                if so_ref is not None:
                    h = (r * slab) // HG_DK
                    off = (r * slab) % HG_DK
                    so_ref[s, d, h, off:off + slab, :] = st[:, h * HG_DK:(h + 1) * HG_DK]

    ones_bd = bd.astype(BF16)

    def finish(gi):
        rows = pl.ds(gi * gl if isinstance(gi, int) else pl.multiple_of(gi * gl, gl), gl)
        inter = []
        for k in range(gc):
            i = gi * gc + k
            ck = pl.ds(gi * gl + k * c if isinstance(gi, int) else pl.multiple_of(gi * gl + k * c, c), c)
            inter.append(_dot_nt(qe_scr[0, ck, :], st_scr[0, i]) + _dot_nt(qe_scr[1, ck, :], st_scr[1, i]))
        o = oi_scr[0, rows, :] + oi_scr[1, rows, :] + jnp.concatenate(inter, axis=0)
        sq = _split(o * o, 2)
        ms = (_dot(sq[0], ones_bd) + _dot(sq[1], ones_bd)) * (1.0 / HG_DK)
        y = o * lax.rsqrt(ms + EPS) * ng_ref[...]
        hg = hg_ref[rows, :]
        o_ref[rows, :] = (y * _silu_t(hg)).astype(o_ref.dtype)

    if n_groups <= 4:
        for gi in range(n_groups):
            finish(gi)
    else:
        def finish_step(gi, carry):
            finish(gi)
            return carry

        lax.fori_loop(0, n_groups, finish_step, 0)


def _hgrn_scratch(rows):
    n_chunks = rows // CHUNK
    return [pltpu.VMEM((2, rows, HG_WIDTH), F32),
            pltpu.VMEM((2, rows, HG_WIDTH), BF16),
            pltpu.VMEM((2, n_chunks, HG_WIDTH, HG_WIDTH), F32),
            pltpu.VMEM((2, n_chunks, HG_WIDTH, HG_WIDTH), BF16),
            pltpu.VMEM((2, max(n_chunks, 8), HG_WIDTH), F32)]


def _hgrn2(proj_h, row0, n_seq, seq_len, seqs_per_step, lb, ng, s0t):
    gl = CHUNKS_PER_GROUP * CHUNK
    groups_per_seq = seq_len // gl
    rows = seqs_per_step * seq_len
    blk0 = row0 // rows
    tri, causal, bd = _hgrn_tables()
    col = lambda c0: pl.BlockSpec((rows, HG_WIDTH), lambda i: (blk0 + i, c0 // HG_WIDTH))
    const = lambda shape: pl.BlockSpec(shape, lambda i: (0,) * len(shape))
    in_specs = [col(COL_HQ), col(COL_HI), col(COL_HF), col(COL_HB), col(COL_HG),
                const((2, HG_WIDTH)), const((1, HG_WIDTH)),
                const((2, gl, gl)), const((2, gl, gl)), const((HG_WIDTH, HG_WIDTH))]
    args = [proj_h] * 5 + [lb, ng, tri, causal, bd]
    out_specs = [pl.BlockSpec((rows, HG_WIDTH), lambda i: (i, 0))]
    out_shape = [jax.ShapeDtypeStruct((n_seq * seq_len, HG_WIDTH), BF16)]
    if s0t is not None:
        in_specs.append(pl.BlockSpec((seqs_per_step, 2, HG_WIDTH, HG_WIDTH), lambda i: (i, 0, 0, 0)))
        args.append(s0t)
    else:
        out_specs.append(pl.BlockSpec((seqs_per_step, 2, HG_HEADS, HG_DK, HG_DK), lambda i: (i, 0, 0, 0, 0)))
        out_shape.append(jax.ShapeDtypeStruct((n_seq, 2, HG_HEADS, HG_DK, HG_DK), F32))
    return pl.pallas_call(
        functools.partial(_hgrn2_body, seqs_per_step, groups_per_seq, s0t is not None),
        grid=(n_seq // seqs_per_step,),
        in_specs=in_specs,
        out_specs=out_specs,
        out_shape=out_shape,
        scratch_shapes=_hgrn_scratch(rows),
        compiler_params=_params(1),
        name=f"hgrn_{seq_len}",
    )(*args)


def _ctx_mixers_body(n_seq, seq_len, lam_init, *refs):
    (q_ref, k_ref, v_ref, u_ref, hq_ref, hi_ref, hf_ref, hb_ref, hg_ref,
     cl_ref, sl_ref, cc_ref, sc_ref, wf_ref, lmb_ref, ag_ref,
     lb_ref, ng_ref, tri_ref, causal_ref, bd_ref,
     fn_ref, a_ref, ho_ref, so_ref) = refs[:25]
    scratch = refs[25:]

    zab = _channel_dft(u_ref[...].astype(BF16), cc_ref[...], sc_ref[...])
    for s in range(n_seq):
        rs = slice(s * seq_len, (s + 1) * seq_len)
        fn_ref[rs, :] = _fourier_rows(zab[rs, :], cl_ref[...], sl_ref[...], wf_ref[...]).astype(fn_ref.dtype)

    lam = _lambda_full(lmb_ref[...], lam_init)
    scale = DA_HALF ** -0.5
    for s in range(n_seq):
        rs = slice(s * seq_len, (s + 1) * seq_len)
        for h in range(DA_HEADS):
            sl = slice(h * DA_VDIM, (h + 1) * DA_VDIM)
            o = _diff_head(q_ref[rs, sl] * scale, [k_ref[rs, sl]], [v_ref[rs, sl]], lam, ag_ref[...], lam_init)
            a_ref[rs, sl] = o.astype(a_ref.dtype)

    _hgrn2_body(n_seq, seq_len // (CHUNKS_PER_GROUP * CHUNK), False,
                hq_ref, hi_ref, hf_ref, hb_ref, hg_ref, lb_ref, ng_ref, tri_ref, causal_ref, bd_ref,
                ho_ref, so_ref, *scratch)


def _ctx_mixers(proj_a, proj_h, n_seq, seq_len, seqs_per_step, w_f, lmb, ag, lam_init, lb, ng):
    rows = seqs_per_step * seq_len
    gl = CHUNKS_PER_GROUP * CHUNK
    cl, sl = _dft_tables(seq_len, seq_len)
    cc, sc = _dft_tables(FN_WIDTH, FN_GROUP_DIM)
    tri, causal, bd = _hgrn_tables()
    col = lambda c0, width: pl.BlockSpec((rows, width), lambda i: (i, c0 // width))
    const = lambda shape: pl.BlockSpec(shape, lambda i: (0,) * len(shape))
    out_rows = lambda width: pl.BlockSpec((rows, width), lambda i: (i, 0))
    n_tok = n_seq * seq_len
    return pl.pallas_call(
        functools.partial(_ctx_mixers_body, seqs_per_step, seq_len, lam_init),
        grid=(n_seq // seqs_per_step,),
        in_specs=[col(COL_Q, DA_WIDTH), col(COL_K, DA_WIDTH), col(COL_V, DA_WIDTH), col(COL_FN, FN_WIDTH),
                  col(COL_HQ, HG_WIDTH), col(COL_HI, HG_WIDTH), col(COL_HF, HG_WIDTH), col(COL_HB, HG_WIDTH),
                  col(COL_HG, HG_WIDTH),
                  const((seq_len, seq_len)), const((seq_len, seq_len)),
                  const((FN_WIDTH, FN_WIDTH)), const((FN_WIDTH, FN_WIDTH)), const((FN_WIDTH, FN_WIDTH)),
                  const((4, DA_HALF)), const((1, DA_VDIM)),
                  const((2, HG_WIDTH)), const((1, HG_WIDTH)),
                  const((2, gl, gl)), const((2, gl, gl)), const((HG_WIDTH, HG_WIDTH))],
        out_specs=[out_rows(FN_WIDTH), out_rows(DA_WIDTH), out_rows(HG_WIDTH),
                   pl.BlockSpec((seqs_per_step, 2, HG_HEADS, HG_DK, HG_DK), lambda i: (i, 0, 0, 0, 0))],
        out_shape=[jax.ShapeDtypeStruct((n_tok, FN_WIDTH), BF16),
                   jax.ShapeDtypeStruct((n_tok, DA_WIDTH), BF16),
                   jax.ShapeDtypeStruct((n_tok, HG_WIDTH), BF16),
                   jax.ShapeDtypeStruct((n_seq, 2, HG_HEADS, HG_DK, HG_DK), F32)],
        scratch_shapes=_hgrn_scratch(rows),
        compiler_params=_params(1),
        name="ctx_mixers",
    )(*([proj_a] * 4 + [proj_h] * 5), cl, sl, cc, sc, w_f, lmb, ag, lb, ng, tri, causal, bd)


def _route_gates(lt, bias):
    per = N_EXPERTS // N_GROUPS
    tt = lt.shape[1]
    neg = -jnp.inf
    assert per == N_GROUPS == 8
    gi = lax.broadcasted_iota(jnp.int32, (N_GROUPS, tt), 0).astype(F32)
    s_j, b_j = [], []
    for j in range(per):
        s = _sigmoid(lt[j * N_GROUPS:(j + 1) * N_GROUPS, :])
        s_j.append(s)
        b_j.append(s + bias[j * N_GROUPS:(j + 1) * N_GROUPS, :])
    m1 = functools.reduce(jnp.maximum, b_j)
    i1 = functools.reduce(jnp.minimum, [jnp.where(b_j[j] == m1, float(j), float(per)) for j in range(per)])
    m2 = functools.reduce(jnp.maximum, [jnp.where(i1 == float(j), neg, b_j[j]) for j in range(per)])
    gs = m1 + m2
    gsel = jnp.zeros((N_GROUPS, tt), jnp.bool_)
    for _ in range(TOPK_GROUPS):
        m = gs.max(axis=0, keepdims=True)
        idx = jnp.where(gs == m, gi, float(N_GROUPS)).min(axis=0, keepdims=True)
        hit = gi == idx
        gsel = gsel | hit
        gs = jnp.where(hit, neg, gs)
    x_j = [jnp.where(gsel, b_j[j], neg) for j in range(per)]
    e_j = [gi * per + j for j in range(per)]
    sel_j = [jnp.zeros((N_GROUPS, tt), jnp.bool_) for _ in range(per)]
    for _ in range(TOP_K):
        m = functools.reduce(jnp.maximum, x_j).max(axis=0, keepdims=True)
        idx = functools.reduce(jnp.minimum, [jnp.where(x_j[j] == m, e_j[j], float(N_EXPERTS))
                                             for j in range(per)]).min(axis=0, keepdims=True)
        for j in range(per):
            hit = e_j[j] == idx
            sel_j[j] = sel_j[j] | hit
            x_j[j] = jnp.where(hit, neg, x_j[j])
    w_j = [jnp.where(sel_j[j], s_j[j], 0.0) for j in range(per)]
    denom = functools.reduce(lambda a, b: a + b, w_j).sum(axis=0, keepdims=True)
    gates_t = jnp.concatenate([w / denom * ROUTED_SCALE for w in w_j], axis=0)
    r_io = lax.broadcasted_iota(jnp.int32, (N_EXPERTS, N_EXPERTS), 0)
    e_io = lax.broadcasted_iota(jnp.int32, (N_EXPERTS, N_EXPERTS), 1)
    eye = (e_io == (r_io & (N_GROUPS - 1)) * per + (r_io >> 3)).astype(BF16)
    p = _split(gates_t, 3)
    return _dot_tn(p[0], eye) + _dot_tn(p[1], eye) + _dot_tn(p[2], eye)


def _outproj_body(n_x, n_ctx_tiles, *refs):
    fn_refs, a_refs, hg_refs = refs[0:2], refs[2:4], refs[4:6]
    x_refs = refs[6:6 + n_x]
    mod_ref, g_ref, w_ref, wr_ref, rb_ref, xo_ref, h_ref, gate_ref = refs[6 + n_x:]
    is_ctx = pl.program_id(0) < n_ctx_tiles
    mix = _dot(jnp.concatenate([_pick(is_ctx, fn_refs), _pick(is_ctx, a_refs), _pick(is_ctx, hg_refs)], axis=1),
               w_ref[...])
    x = _pick(is_ctx, x_refs) + mod_ref[2:3, :] * mix
    xo_ref[...] = x
    h = _modnorm(x, g_ref[...], mod_ref[3:4, :], mod_ref[4:5, :])
    h_ref[...] = h.astype(BF16)
    h_hi, h_lo = _split(h, 2)
    w_hi, w_lo = _split(wr_ref[...], 2)
    logits_t = _dot_nt(w_hi, h_hi) + _dot_nt(w_lo, h_hi) + _dot_nt(w_hi, h_lo)
    gate_ref[...] = _route_gates(logits_t, rb_ref[...])


def _outproj(fn, a, hg, xs, mod, g, w_out, w_router_t, router_bias, n_ctx_tok, lat_len):
    t = sum(x.shape[0] for x in xs)
    tm = TOKEN_TILE
    n_ctx_tiles = n_ctx_tok // tm
    row = _mod_row(n_ctx_tiles, lat_len // tm)
    rows = lambda width: pl.BlockSpec((tm, width), lambda i: (i, 0))
    parts = lambda n, width: _row_specs(n, tm, width, n_ctx_tiles)
    return pl.pallas_call(
        functools.partial(_outproj_body, len(xs), n_ctx_tiles),
        grid=(t // tm,),
        in_specs=parts(2, FN_WIDTH) + parts(2, DA_WIDTH) + parts(2, HG_WIDTH) + parts(len(xs), D_MODEL) + [
            pl.BlockSpec((None, 6, D_MODEL), lambda i: (row(i), 0, 0)),
            pl.BlockSpec((1, D_MODEL), lambda i: (0, 0)),
            pl.BlockSpec((D_MODEL, D_MODEL), lambda i: (0, 0)),
            pl.BlockSpec((N_EXPERTS, D_MODEL), lambda i: (0, 0)),
            pl.BlockSpec((N_EXPERTS, 1), lambda i: (0, 0)),
        ],
        out_specs=[rows(D_MODEL), rows(D_MODEL), rows(N_EXPERTS)],
        out_shape=[jax.ShapeDtypeStruct((t, D_MODEL), F32),
                   jax.ShapeDtypeStruct((t, D_MODEL), BF16),
                   jax.ShapeDtypeStruct((t, N_EXPERTS), F32)],
        compiler_params=_params(1),
        name="outproj",
    )(*fn, *a, *hg, *xs, mod, g, w_out, w_router_t, router_bias)


def _moe_body(final_norm, n_ctx_tiles, h_ref, gate_ref, wg_ref, wu_ref, wd_ref, sg_ref, su_ref, sd_ref,
              x_ref, mod_ref, fg_ref, *out_and_scratch):
    acc_ref = out_and_scratch[-1]
    o_refs = out_and_scratch[:-1]
    is_ctx = pl.program_id(0) < n_ctx_tiles
    j = pl.program_id(1)
    h = h_ref[...]

    def act_of(wg, wu, gate):
        gu = _dot(h, jnp.concatenate([wg.astype(BF16), wu.astype(BF16)], axis=1))
        a = _silu_t(gu[:, :D_EXPERT]) * gu[:, D_EXPERT:]
        return a if gate is None else a * gate

    @pl.when(j == 0)
    def _():
        acc_ref[...] = _dot(act_of(sg_ref[...], su_ref[...], None).astype(BF16), sd_ref[...].astype(BF16))

    gates = gate_ref[...]
    expert_of_lane = lax.broadcasted_iota(jnp.int32, gates.shape, 1)

    def gate_col(p):
        e = j * EXPERTS_PER_STEP + p
        return jnp.sum(jnp.where(expert_of_lane == e, gates, 0.0), axis=1, keepdims=True)

    acts = [act_of(wg_ref[p], wu_ref[p], gate_col(p)).astype(BF16) for p in range(EXPERTS_PER_STEP)]
    wd = wd_ref[...].astype(BF16).reshape(EXPERTS_PER_STEP * D_EXPERT, D_MODEL)
    acc_ref[...] += _dot(jnp.concatenate(acts, axis=1), wd)

    @pl.when(j == pl.num_programs(1) - 1)
    def _():
        x = x_ref[...] + mod_ref[5:6, :] * acc_ref[...]
        if not final_norm:
            o_refs[0][...] = x
        else:
            ms = jnp.mean(x * x, axis=-1, keepdims=True)
            y = x * lax.rsqrt(ms + EPS) * fg_ref[...]

            @pl.when(is_ctx)
            def _():
                o_refs[0][...] = y

            @pl.when(jnp.logical_not(is_ctx))
            def _():
                o_refs[1][...] = y


def _moe(h, gates, w_gate, w_up, w_down2, ws_gate, ws_up, ws_down, layer, x, mod, final_g, final_norm,
         n_ctx_tok, lat_len):
    t = x.shape[0]
    tm = MOE_TILE
    eps_ = EXPERTS_PER_STEP
    n_ctx_tiles = n_ctx_tok // tm
    row = _mod_row(n_ctx_tiles, lat_len // tm)
    once = pl.Buffered(1)
    if final_norm:
        last_ctx = n_ctx_tiles - 1
        out_specs = [pl.BlockSpec((tm, D_MODEL), lambda i, j: (jnp.minimum(i, last_ctx), 0), pipeline_mode=once),
                     pl.BlockSpec((tm, D_MODEL), lambda i, j: (jnp.maximum(i - n_ctx_tiles, 0), 0),
                                  pipeline_mode=once)]
        out_shape = [jax.ShapeDtypeStruct((n_ctx_tok, D_MODEL), F32),
                     jax.ShapeDtypeStruct((t - n_ctx_tok, D_MODEL), F32)]
    else:
        out_specs = [pl.BlockSpec((tm, D_MODEL), lambda i, j: (i, 0), pipeline_mode=once)]
        out_shape = [jax.ShapeDtypeStruct((t, D_MODEL), F32)]
    return pl.pallas_call(
        functools.partial(_moe_body, final_norm, n_ctx_tiles),
        grid=(t // tm, N_EXPERTS // eps_),
        in_specs=[
            pl.BlockSpec((tm, D_MODEL), lambda i, j: (i, 0)),
            pl.BlockSpec((tm, N_EXPERTS), lambda i, j: (i, 0)),
            pl.BlockSpec((None, eps_, D_MODEL, D_EXPERT), lambda i, j: (layer, j, 0, 0)),
            pl.BlockSpec((None, eps_, D_MODEL, D_EXPERT), lambda i, j: (layer, j, 0, 0)),
            pl.BlockSpec((None, eps_ // 2, 2 * D_EXPERT, D_MODEL), lambda i, j: (layer, j, 0, 0)),
            pl.BlockSpec((None, D_MODEL, D_EXPERT), lambda i, j: (layer, 0, 0), pipeline_mode=once),
            pl.BlockSpec((None, D_MODEL, D_EXPERT), lambda i, j: (layer, 0, 0), pipeline_mode=once),
            pl.BlockSpec((None, D_EXPERT, D_MODEL), lambda i, j: (layer, 0, 0), pipeline_mode=once),
            pl.BlockSpec((tm, D_MODEL), lambda i, j: (i, 0), pipeline_mode=once),
            pl.BlockSpec((None, 6, D_MODEL), lambda i, j: (row(i), 0, 0)),
            pl.BlockSpec((1, D_MODEL), lambda i, j: (0, 0)),
        ],
        out_specs=out_specs,
        out_shape=out_shape,
        scratch_shapes=[pltpu.VMEM((tm, D_MODEL), F32)],
        compiler_params=_params(2),
        name="moe",
    )(h, gates, w_gate, w_up, w_down2, ws_gate, ws_up, ws_down, x, mod, final_g)


def _block_diag_t(s):
    eye = jnp.eye(HG_HEADS, dtype=s.dtype)
    out = jnp.einsum('...hkv,hg->...hvgk', s, eye)
    return out.reshape(s.shape[:-3] + (HG_WIDTH, HG_WIDTH))


def kernel(x_prompt, x_sample, cache_k, cache_v, state_hgrn, c, c_ctx, w_ada, b_ada, norm_g, w_in,
           w_fourier, lambdas, attn_norm_g, lower_bounds, hg_norm_g, w_out, w_router, router_bias,
           w_gate, w_up, w_down, ws_gate, ws_up, ws_down, final_g):
    n_ctx, ctx_len, _ = x_prompt.shape
    n_lat, lat_len, _ = x_sample.shape
    n_ctx_tok = n_ctx * ctx_len
    past = cache_k.shape[2]

    xs = (x_prompt.reshape(n_ctx_tok, D_MODEL), x_sample.reshape(n_lat * lat_len, D_MODEL))

    c8 = jnp.zeros((8, D_MODEL), F32).at[0].set(c_ctx).at[1:1 + n_lat].set(c)
    mods = _ada_mods(c8, w_ada, b_ada).reshape(DEPTH, 8, 6, D_MODEL)

    cs = jnp.cumsum(jax.nn.softmax(lower_bounds.astype(F32), axis=0), axis=0)
    lbs = cs - cs[0:1]

    cos, sin = _rope_tables(lat_len)
    cache_k4 = cache_k.reshape(n_lat, DEPTH, past, DA_WIDTH)
    cache_v4 = cache_v.reshape(n_lat, DEPTH, past, DA_WIDTH)
    s0t = _block_diag_t(state_hgrn.astype(F32))

    qkv_end = FN_WIDTH + 3 * DA_WIDTH
    w_in_b = jnp.concatenate([w_in[:, :, FN_WIDTH:qkv_end], w_in[:, :, :FN_WIDTH], w_in[:, :, qkv_end:]],
                             axis=-1).astype(BF16)
    w_f_b = w_fourier.astype(BF16)
    w_out_b = w_out.astype(BF16)
    per = N_EXPERTS // N_GROUPS
    w_router_t = (jnp.swapaxes(w_router, 1, 2).reshape(DEPTH, N_GROUPS, per, D_MODEL)
                  .swapaxes(1, 2).reshape(DEPTH, N_EXPERTS, D_MODEL))
    bias_mm = router_bias.reshape(DEPTH, N_GROUPS, per).swapaxes(1, 2).reshape(DEPTH, N_EXPERTS, 1)
    w_down2 = w_down.reshape(DEPTH, N_EXPERTS // 2, 2 * D_EXPERT, D_MODEL)
    ng = jnp.tile(hg_norm_g, (1, HG_HEADS))

    caches = []
    new_s = []
    for l in range(DEPTH):
        lam_init = 0.8 - 0.6 * math.exp(-0.3 * l)
        mod = mods[l]
        last = l == DEPTH - 1
        proj_a, proj_h, k_l, v_l = _inproj(xs, mod, norm_g[l, 0:1], w_in_b[l], tuple(caches) if last else (),
                                           n_ctx, ctx_len, lat_len)
        caches = [k_l, v_l] if last else caches + [k_l, v_l]

        ag = attn_norm_g[l].reshape(1, DA_VDIM)
        fn_ctx, a_ctx, hg_ctx, st_ctx = _ctx_mixers(proj_a, proj_h, n_ctx, ctx_len, 2, w_f_b[l], lambdas[l], ag,
                                                    lam_init, lbs[l], ng[l:l + 1])
        fn_lat = _fourier(proj_a, n_ctx_tok, n_lat, lat_len, w_f_b[l])
        a_lat = _attn_lat(proj_a, n_ctx_tok, n_lat, lat_len, cache_k4, cache_v4, l, cos, sin, lambdas[l], ag,
                          lam_init)
        (hg_lat,) = _hgrn2(proj_h, n_ctx_tok, n_lat, lat_len, 1, lbs[l], ng[l:l + 1], s0t[:, l])
        new_s.append(jnp.swapaxes(st_ctx, -1, -2))

        x, h2, gates = _outproj((fn_ctx, fn_lat), (a_ctx, a_lat), (hg_ctx, hg_lat), xs, mod,
                                norm_g[l, 1:2], w_out_b[l], w_router_t[l], bias_mm[l], n_ctx_tok, lat_len)
        xs = _moe(h2, gates, w_gate, w_up, w_down2, ws_gate, ws_up, ws_down, l, x, mod,
                  final_g.reshape(1, D_MODEL), l == DEPTH - 1, n_ctx_tok, lat_len)

    y_prompt = xs[0].reshape(n_ctx, ctx_len, D_MODEL)
    y_sample = xs[1].reshape(n_lat, lat_len, D_MODEL)
    new_k = caches[0].reshape(n_ctx, DEPTH, ctx_len, DA_HEADS, DA_VDIM)
    new_v = caches[1].reshape(n_ctx, DEPTH, ctx_len, DA_HEADS, DA_VDIM)
    return (y_prompt, y_sample, new_k, new_v, jnp.stack(new_s, axis=1))
```
